```python
import math
import jax, jax.numpy as jnp
from jax import lax
import numpy as np

D_MODEL = 2048
BATCH = 8
SEQ = 2048
DEPTH = 1
DEC_BATCH = 32
DEC_SEQ = 8
PAST_LEN = 16384
PAGE_SIZE = 128

HEAD_DIM = 128
DILATED_GROUPS = ((128, 1), (512, 4), (2048, 16))
N_GROUPS = 3
H_A = 4
N_HEADS_A = N_GROUPS * H_A
A_WIDTH = H_A * HEAD_DIM
A_QKV_WIDTH = N_HEADS_A * HEAD_DIM
Q_BLOCK = 128
N_BUCKETS = 32
REL_MAX_DIST = 2048
H_B = 4
DK_B = 64
DV_B = 128
GLA_RANK = 16
GLA_TAU = 16.0
GLA_CHUNK = 64
N_EXPERTS = 32
TOP_K = 4
D_FF = 2048
SWIGLU_LIMIT = 7.0
SWIGLU_ALPHA = 1.702
MOE_BLOCK = 128
LN_EPS = 1e-5
RMS_EPS = 1e-6
DEEPNORM_ALPHA = (2.0 * DEPTH) ** 0.25
DEEPNORM_BETA = (8.0 * DEPTH) ** -0.25

PROJ_SPLITS = (A_QKV_WIDTH, A_QKV_WIDTH, A_QKV_WIDTH, H_B * DK_B, H_B * DK_B, H_B * DV_B, H_B * DV_B,
               GLA_RANK, D_MODEL, D_MODEL)
PROJ_COLS = sum(PROJ_SPLITS)

kernel_name = 'dilated_gla_moe_hybrid_step'


def _layer_norm(x, g, b):
    x32 = x.astype(jnp.float32)
    mu = jnp.mean(x32, axis=-1, keepdims=True)
    var = jnp.mean(jnp.square(x32 - mu), axis=-1, keepdims=True)
    return ((x32 - mu) * lax.rsqrt(var + LN_EPS) * g.astype(jnp.float32) + b.astype(jnp.float32)).astype(x.dtype)


def _t5_bucket(dist):
    max_exact = N_BUCKETS // 2
    d = np.maximum(dist, 1).astype(np.float32)
    large = max_exact + (np.log(d / max_exact) / np.log(REL_MAX_DIST / max_exact)
                         * (N_BUCKETS - max_exact)).astype(np.int32)
    large = np.minimum(large, N_BUCKETS - 1)
    return np.where(dist < max_exact, dist, large).astype(np.int32)


def _group_biases(rel_bias):
    out = []
    for g, (window, dil) in enumerate(DILATED_GROUPS):
        bucket = _t5_bucket(dil * np.arange(window // dil + 1))
        out.append(rel_bias[bucket][:, g * H_A:(g + 1) * H_A].astype(jnp.float32))
    return out


def _project(x, w_in, gla_w_up, gla_b):
    B, S, _ = x.shape
    proj = jnp.einsum('bsd,de->bse', x, w_in)
    offs = np.cumsum(PROJ_SPLITS)[:-1].tolist()
    qa, ka, va, qb, kb, vb, rb, lr, ga, gb = jnp.split(proj, offs, axis=-1)
    shp_a = (B, S, N_GROUPS, H_A, HEAD_DIM)
    qa, ka, va = qa.reshape(shp_a), ka.reshape(shp_a), va.reshape(shp_a)
    qb = qb.reshape(B, S, H_B, DK_B) * (DK_B ** -0.5)
    kb = kb.reshape(B, S, H_B, DK_B)
    vb = vb.reshape(B, S, H_B, DV_B)
    z = (jnp.einsum('bsr,rk->bsk', lr, gla_w_up) + gla_b).astype(jnp.float32)
    log_a = (jax.nn.log_sigmoid(z) / GLA_TAU).reshape(B, S, H_B, DK_B)
    return (qa, ka, va), (qb, kb, vb, log_a, rb), (ga, gb)


def _dilated_group(q, k_ctx, v_ctx, q_idx, dil, n_keys, bias):
    idx = q_idx[:, None] - dil * jnp.arange(n_keys)[None, :]
    valid = idx >= 0
    idx = jnp.maximum(idx, 0)
    kg = k_ctx[:, idx]
    vg = v_ctx[:, idx]
    s = jnp.einsum('bqhd,bqjhd->bqhj', q, kg).astype(jnp.float32) * (HEAD_DIM ** -0.5) + bias.T[None, None]
    s = jnp.where(valid[None, :, None, :], s, jnp.finfo(jnp.float32).min)
    m = jnp.max(s, axis=-1, keepdims=True)
    p = jnp.exp(s - m)
    l = jnp.sum(p, axis=-1, keepdims=True)
    o = jnp.einsum('bqhj,bqjhd->bqhd', (p / l).astype(vg.dtype), vg).astype(jnp.float32)
    lse = (m + jnp.log(l))[..., 0]
    return o, lse


def _dilated_mix(q, ctxs, q_idxs, biases):
    outs, lses = [], []
    for g, (window, dil) in enumerate(DILATED_GROUPS):
        o, lse = _dilated_group(q[:, :, g], ctxs[g][0], ctxs[g][1], q_idxs[g], dil, window // dil + 1, biases[g])
        outs.append(o)
        lses.append(lse)
    w = jax.nn.softmax(jnp.stack(lses), axis=0)
    return jnp.sum(w[..., None] * jnp.stack(outs), axis=0).astype(q.dtype)


def _mixer_a_prompt(qa, ka, va, biases):
    B, S = qa.shape[:2]
    nb = S // Q_BLOCK
    ctxs = [(ka[:, :, g], va[:, :, g]) for g in range(N_GROUPS)]
    q_blocks = qa.reshape(B, nb, Q_BLOCK, N_GROUPS, H_A, HEAD_DIM).swapaxes(0, 1)

    def body(args):
        i, q_blk = args
        q_idx = i * Q_BLOCK + jnp.arange(Q_BLOCK)
        return _dilated_mix(q_blk, ctxs, [q_idx] * N_GROUPS, biases)

    o = lax.map(body, (jnp.arange(nb), q_blocks))
    o = o.swapaxes(0, 1).reshape(B, S, A_WIDTH)
    bufs = []
    for g, (window, _) in enumerate(DILATED_GROUPS):
        keep = min(window, S)
        bufs.append(jnp.stack([ka[:, S - keep:, g], va[:, S - keep:, g]], axis=2))
    return o, bufs


def _mixer_a_sample(qa, ka, va, caches, biases):
    B, S = qa.shape[:2]
    ctxs, q_idxs, bufs = [], [], []
    for g, (window, _) in enumerate(DILATED_GROUPS):
        buf = caches[g]
        L = buf.shape[1]
        kv_new = jnp.stack([ka[:, :, g], va[:, :, g]], axis=2).astype(buf.dtype)
        ctx = jnp.concatenate([buf, kv_new], axis=1)
        ctxs.append((ctx[:, :, 0], ctx[:, :, 1]))
        q_idxs.append(L + jnp.arange(S))
        keep = min(window, L + S)
        bufs.append(ctx[:, L + S - keep:])
    o = _dilated_mix(qa, ctxs, q_idxs, biases).reshape(B, S, A_WIDTH)
    return o, bufs


def _gla(q, k, v, log_a, s0):
    B, S = q.shape[:2]
    C = math.gcd(S, GLA_CHUNK)
    n = S // C

    def to_chunks(t):
        return t.astype(jnp.float32).reshape(B, n, C, H_B, -1).transpose(1, 0, 3, 2, 4)

    mask = jnp.tril(jnp.ones((C, C), dtype=bool))

    def step(state, inp):
        qc, kc, vc, gc = inp
        b = jnp.cumsum(gc, axis=2)
        o_inter = jnp.einsum('bhtk,bhkv->bhtv', qc * jnp.exp(b), state)
        diff = b[:, :, :, None, :] - b[:, :, None, :, :]
        dec = jnp.exp(jnp.where(mask[:, :, None], diff, -jnp.inf))
        attn = jnp.einsum('bhtk,bhsk,bhtsk->bhts', qc, kc, dec)
        o_intra = jnp.einsum('bhts,bhsv->bhtv', attn, vc)
        b_last = b[:, :, -1:, :]
        state = jnp.exp(b_last[:, :, 0, :])[..., None] * state + \
            jnp.einsum('bhsk,bhsv->bhkv', kc * jnp.exp(b_last - b), vc)
        return state, o_inter + o_intra

    s_T, o = lax.scan(step, s0.astype(jnp.float32), (to_chunks(q), to_chunks(k), to_chunks(v), to_chunks(log_a)))
    o = o.transpose(1, 0, 3, 2, 4).reshape(B, S, H_B, DV_B)
    return o, s_T


def _gla_out(o, rb, gla_norm_g):
    B, S = o.shape[:2]
    o = o * lax.rsqrt(jnp.mean(jnp.square(o), axis=-1, keepdims=True) + RMS_EPS) * gla_norm_g.astype(jnp.float32)
    return (o.reshape(B, S, H_B * DV_B) * jax.nn.silu(rb.astype(jnp.float32))).astype(rb.dtype)


def _moe(x, router_w, router_b, w_gate_up, b_gate_up, w_down, b_down):
    T, D = x.shape
    logits = (x @ router_w + router_b).astype(jnp.float32)
    top_v, top_i = lax.top_k(logits, TOP_K)
    gate = jax.nn.softmax(top_v, axis=-1)
    n_assign = T * TOP_K
    expert = top_i.reshape(n_assign).astype(jnp.int32)
    token = jnp.repeat(jnp.arange(T, dtype=jnp.int32), TOP_K)
    weight = gate.reshape(n_assign)
    onehot = jax.nn.one_hot(expert, N_EXPERTS, dtype=jnp.int32)
    counts = jnp.sum(onehot, axis=0)
    rank = jnp.take_along_axis(jnp.cumsum(onehot, axis=0), expert[:, None], axis=1)[:, 0] - 1
    blocks_per_expert = (counts + MOE_BLOCK - 1) // MOE_BLOCK
    block_end = jnp.cumsum(blocks_per_expert)
    block_start = block_end - blocks_per_expert
    dest = block_start[expert] * MOE_BLOCK + rank
    n_blocks = -(-n_assign // MOE_BLOCK) + N_EXPERTS
    n_rows = n_blocks * MOE_BLOCK
    row_token = jnp.full((n_rows,), T, jnp.int32).at[dest].set(token)
    row_weight = jnp.zeros((n_rows,), jnp.float32).at[dest].set(weight)
    block_expert = jnp.minimum(jnp.searchsorted(block_end, jnp.arange(n_blocks), side='right'), N_EXPERTS - 1)
    x_pad = jnp.concatenate([x, jnp.zeros((1, D), x.dtype)], axis=0)

    def expert_block(args):
        rows, e = args
        xb = x_pad[rows]
        gu = (xb @ w_gate_up[e] + b_gate_up[e]).astype(jnp.float32)
        g = jnp.minimum(gu[:, :D_FF], SWIGLU_LIMIT)
        u = jnp.clip(gu[:, D_FF:], -SWIGLU_LIMIT, SWIGLU_LIMIT)
        h = (u + 1.0) * g * jax.nn.sigmoid(SWIGLU_ALPHA * g)
        return h.astype(x.dtype) @ w_down[e] + b_down[e]

    out = lax.map(expert_block, (row_token.reshape(n_blocks, MOE_BLOCK), block_expert))
    out = out.reshape(n_rows, D).astype(jnp.float32) * row_weight[:, None]
    return jax.ops.segment_sum(out, row_token, num_segments=T + 1)[:T].astype(x.dtype)


def _merge_and_channel_mix(x, o_a, o_b, ga, gb, w_pa, w_pb, w_o, ln1_g, ln1_b,
                           router_w, router_b, w_gate_up, b_gate_up, w_down, b_down, ln2_g, ln2_b):
    branch = jax.nn.sigmoid(ga) * (o_a @ w_pa) + jax.nn.sigmoid(gb) * (o_b @ w_pb)
    y = branch @ w_o
    x1 = _layer_norm(DEEPNORM_ALPHA * x + y, ln1_g, ln1_b)
    B, S, D = x1.shape
    m = _moe(x1.reshape(B * S, D), router_w, router_b, w_gate_up, b_gate_up, w_down, b_down).reshape(B, S, D)
    return _layer_norm(DEEPNORM_ALPHA * x1 + m, ln2_g, ln2_b)


def setup_inputs(seed: int = 0) -> dict:
    key = jax.random.key(seed)
    ks = jax.random.split(key, 26)
    f32 = jnp.float32

    def nrm(k, shape, scale):
        return jax.random.normal(k, shape, f32) * scale

    lens = [min(w, PAST_LEN) for w, _ in DILATED_GROUPS]
    return {
        'x_prompt': nrm(ks[0], (BATCH, SEQ, D_MODEL), 1.0),
        'x_sample': nrm(ks[1], (DEC_BATCH, DEC_SEQ, D_MODEL), 1.0),
        'cache_a1_kv': nrm(ks[2], (DEPTH, DEC_BATCH, lens[0], 2, H_A, HEAD_DIM), 1.0),
        'cache_a2_kv': nrm(ks[3], (DEPTH, DEC_BATCH, lens[1], 2, H_A, HEAD_DIM), 1.0),
        'cache_a3_kv': nrm(ks[4], (DEPTH, DEC_BATCH, lens[2], 2, H_A, HEAD_DIM), 1.0),
        'state_b_s': nrm(ks[5], (DEPTH, DEC_BATCH, H_B, DK_B, DV_B), 1.0),
        'w_in': nrm(ks[6], (DEPTH, D_MODEL, PROJ_COLS), D_MODEL ** -0.5),
        'rel_bias': nrm(ks[7], (N_BUCKETS, N_HEADS_A), 0.5),
        'gla_w_up': nrm(ks[8], (DEPTH, GLA_RANK, H_B * DK_B), GLA_RANK ** -0.5),
        'gla_b': nrm(ks[9], (DEPTH, H_B * DK_B), 0.1),
        'gla_norm_g': 1.0 + nrm(ks[10], (DEPTH, DV_B), 0.02),
        'w_pa': nrm(ks[11], (DEPTH, A_WIDTH, D_MODEL), A_WIDTH ** -0.5),
        'w_pb': nrm(ks[12], (DEPTH, H_B * DV_B, D_MODEL), (H_B * DV_B) ** -0.5),
        'w_o': nrm(ks[13], (DEPTH, D_MODEL, D_MODEL), D_MODEL ** -0.5 * DEEPNORM_BETA),
        'ln1_g': 1.0 + nrm(ks[14], (DEPTH, D_MODEL), 0.02),
        'ln1_b': nrm(ks[15], (DEPTH, D_MODEL), 0.02),
        'router_w': nrm(ks[16], (DEPTH, D_MODEL, N_EXPERTS), D_MODEL ** -0.5),
        'router_b': nrm(ks[17], (DEPTH, N_EXPERTS), 0.01),
        'w_gate_up': nrm(ks[18], (DEPTH, N_EXPERTS, D_MODEL, 2 * D_FF), D_MODEL ** -0.5),
        'b_gate_up': nrm(ks[19], (DEPTH, N_EXPERTS, 2 * D_FF), 0.02),
        'w_down': nrm(ks[20], (DEPTH, N_EXPERTS, D_FF, D_MODEL), D_FF ** -0.5 * DEEPNORM_BETA),
        'b_down': nrm(ks[21], (DEPTH, N_EXPERTS, D_MODEL), 0.02),
        'ln2_g': 1.0 + nrm(ks[22], (DEPTH, D_MODEL), 0.02),
        'ln2_b': nrm(ks[23], (DEPTH, D_MODEL), 0.02),
    }


def reference(x_prompt, x_sample, cache_a1_kv, cache_a2_kv, cache_a3_kv, state_b_s,
              w_in, rel_bias, gla_w_up, gla_b, gla_norm_g, w_pa, w_pb, w_o, ln1_g, ln1_b,
              router_w, router_b, w_gate_up, b_gate_up, w_down, b_down, ln2_g, ln2_b):
    biases = _group_biases(rel_bias)
    yp, ys = x_prompt, x_sample
    pa1, pa2, pa3, pst, sa1, sa2, sa3, sst = [], [], [], [], [], [], [], []
    for l in range(DEPTH):
        (qa, ka, va), (qb, kb, vb, la, rb), (ga, gb) = _project(yp, w_in[l], gla_w_up[l], gla_b[l])
        o_a, bufs_p = _mixer_a_prompt(qa, ka, va, biases)
        s0 = jnp.zeros((yp.shape[0], H_B, DK_B, DV_B), jnp.float32)
        o_b, s_p = _gla(qb, kb, vb, la, s0)
        o_b = _gla_out(o_b, rb, gla_norm_g[l])
        yp = _merge_and_channel_mix(yp, o_a, o_b, ga, gb, w_pa[l], w_pb[l], w_o[l], ln1_g[l], ln1_b[l],
                                    router_w[l], router_b[l], w_gate_up[l], b_gate_up[l], w_down[l], b_down[l],
                                    ln2_g[l], ln2_b[l])
        pa1.append(bufs_p[0]); pa2.append(bufs_p[1]); pa3.append(bufs_p[2]); pst.append(s_p)
        (qa, ka, va), (qb, kb, vb, la, rb), (ga, gb) = _project(ys, w_in[l], gla_w_up[l], gla_b[l])
        o_a, bufs_s = _mixer_a_sample(qa, ka, va, (cache_a1_kv[l], cache_a2_kv[l], cache_a3_kv[l]), biases)
        o_b, s_s = _gla(qb, kb, vb, la, state_b_s[l])
        o_b = _gla_out(o_b, rb, gla_norm_g[l])
        ys = _merge_and_channel_mix(ys, o_a, o_b, ga, gb, w_pa[l], w_pb[l], w_o[l], ln1_g[l], ln1_b[l],
                                    router_w[l], router_b[l], w_gate_up[l], b_gate_up[l], w_down[l], b_down[l],
                                    ln2_g[l], ln2_b[l])
        sa1.append(bufs_s[0]); sa2.append(bufs_s[1]); sa3.append(bufs_s[2]); sst.append(s_s.astype(state_b_s.dtype))
    return (yp, ys, jnp.stack(pa1), jnp.stack(pa2), jnp.stack(pa3), jnp.stack(pst),
            jnp.stack(sa1), jnp.stack(sa2), jnp.stack(sa3), jnp.stack(sst))
```

```python
import functools

import numpy as np
import jax
import jax.numpy as jnp
from jax import lax
from jax.experimental import pallas as pl
from jax.experimental.pallas import tpu as pltpu
from jax.experimental.pallas import tpu_sc as plsc

F32 = jnp.float32
BF16 = jnp.bfloat16

D_MODEL = 2048
HEAD_DIM = 128
DILATED_GROUPS = ((128, 1), (512, 4), (2048, 16))
N_GROUPS = 3
H_A = 4
A_WIDTH = H_A * HEAD_DIM
A_QKV_WIDTH = N_GROUPS * A_WIDTH
Q_BLOCK = 128
N_BUCKETS = 32
REL_MAX_DIST = 2048
H_B = 4
DK_B = 64
DV_B = 128
GLA_RANK = 16
GLA_TAU = 16.0
GLA_CHUNK = 64
GLA_SUB = 16
GLA_EXP_CLAMP = 80.0
N_EXPERTS = 32
TOP_K = 4
D_FF = 2048
SWIGLU_LIMIT = 7.0
SWIGLU_ALPHA = 1.702
LN_EPS = 1e-5
RMS_EPS = 1e-6
DEPTH = 1
DEEPNORM_ALPHA = (2.0 * DEPTH) ** 0.25
ATT_SCALE = HEAD_DIM ** -0.5
NEG = float(np.finfo(np.float32).min)

VMEM_LIMIT = 56 * 1024 * 1024
LANE = 128

PA_W = 3 * A_QKV_WIDTH
PB_LR = H_B * DK_B * 2 + H_B * DV_B * 2
PB_W = PB_LR + LANE
PG_W = 2 * D_MODEL

MOE_BM = 1024
MOE_SB = 512
MOE_TF = 256
SC_CHUNK = 16


def _cparams(sem):
    return pltpu.CompilerParams(dimension_semantics=sem, vmem_limit_bytes=VMEM_LIMIT)


def _dot(a, b):
    return jnp.dot(a, b, preferred_element_type=F32)


def _dot_nt(a, b):
    return lax.dot_general(a, b, (((1,), (1,)), ((), ())), preferred_element_type=F32)


def _dot_tn(a, b):
    return lax.dot_general(a, b, (((0,), (0,)), ((), ())), preferred_element_type=F32)


def _proj_kernel(x_ref, w_ref, o_ref, xb_ref):
    @pl.when(pl.program_id(1) == 0)
    def _():
        xb_ref[...] = x_ref[...].astype(BF16)

    o_ref[...] = _dot(xb_ref[...], w_ref[...])


def _project(x, w, tm, tn, name):
    T, D = x.shape
    N = w.shape[1]
    return pl.pallas_call(
        _proj_kernel,
        grid=(T // tm, N // tn),
        in_specs=[pl.BlockSpec((tm, D), lambda i, j: (i, 0)), pl.BlockSpec((D, tn), lambda i, j: (0, j))],
        out_specs=pl.BlockSpec((tm, tn), lambda i, j: (i, j)),
        out_shape=jax.ShapeDtypeStruct((T, N), F32),
        scratch_shapes=[pltpu.VMEM((tm, D), BF16)],
        compiler_params=_cparams(("parallel", "arbitrary")),
        name=name,
    )(x, w)


def _t5_bucket(dist):
    max_exact = N_BUCKETS // 2
    d = np.maximum(dist, 1).astype(np.float32)
    large = max_exact + (np.log(d / max_exact) / np.log(REL_MAX_DIST / max_exact) * (N_BUCKETS - max_exact)).astype(np.int32)
    large = np.minimum(large, N_BUCKETS - 1)
    return np.where(dist < max_exact, dist, large).astype(np.int32)


def _group_biases(rel_bias):
    out = []
    for g, (window, dil) in enumerate(DILATED_GROUPS):
        bucket = _t5_bucket(dil * np.arange(window // dil + 1))
        out.append(rel_bias[bucket][:, g * H_A:(g + 1) * H_A].astype(F32))
    return out


def _prompt_table(bias_g):
    qi = np.arange(Q_BLOCK)[:, None]
    kj = np.arange(2 * Q_BLOCK)[None, :]
    j = Q_BLOCK + qi - kj
    valid = (j >= 0) & (j <= Q_BLOCK)
    tab = bias_g.T[:, np.clip(j, 0, Q_BLOCK)]
    return jnp.where(valid[None], tab, NEG)


def _sample_tables(biases, dec_seq):
    m = np.arange(Q_BLOCK)
    tabc, combo_base = [], []
    for g, (_, dil) in enumerate(DILATED_GROUPS):
        combo_base.append(len(tabc))
        for fl in range((dec_seq - 1) // dil + 1):
            j = Q_BLOCK + fl - m
            valid = j <= Q_BLOCK
            col = jnp.where(valid[None], biases[g].T[:, np.clip(j, 0, Q_BLOCK)], NEG)
            tabc.append(jnp.broadcast_to(col[:, :, None], (H_A, Q_BLOCK, LANE)))
    tabn = []
    s = np.arange(dec_seq)[:, None]
    sp = np.arange(dec_seq)[None, :]
    for g, (_, dil) in enumerate(DILATED_GROUPS):
        diff = s - sp
        valid = (diff >= 0) & (diff % dil == 0)
        t = jnp.where(valid[None], biases[g].T[:, np.clip(diff // dil, 0, Q_BLOCK)], NEG)
        tabn.append(jnp.broadcast_to(t[..., None], (H_A, dec_seq, dec_seq, LANE)))
    return jnp.stack(tabc), jnp.stack(tabn), tuple(combo_base)


def _attn_prompt_kernel(q_ref, kc_ref, kp_ref, vc_ref, vp_ref, tab_ref, o_ref, lse_ref):
    has_prev = pl.program_id(2) > 0
    for h in range(H_A):
        sl = slice(h * HEAD_DIM, (h + 1) * HEAD_DIM)
        q = q_ref[:, sl].astype(BF16)
        sc = _dot_nt(q, kc_ref[:, sl].astype(BF16)) * ATT_SCALE + tab_ref[h, :, Q_BLOCK:]
        sp = _dot_nt(q, kp_ref[:, sl].astype(BF16)) * ATT_SCALE + tab_ref[h, :, :Q_BLOCK]
        sp = jnp.where(has_prev, sp, NEG)
        m = jnp.maximum(jnp.max(sc, axis=-1, keepdims=True), jnp.max(sp, axis=-1, keepdims=True))
        pc = jnp.exp(sc - m)
        pp = jnp.exp(sp - m)
        l = jnp.sum(pc, axis=-1, keepdims=True) + jnp.sum(pp, axis=-1, keepdims=True)
        inv = 1.0 / l
        o = _dot((pc * inv).astype(BF16), vc_ref[:, sl].astype(BF16)) + _dot((pp * inv).astype(BF16), vp_ref[:, sl].astype(BF16))
        o_ref[:, sl] = o
        lse_ref[:, sl] = jnp.broadcast_to(m + jnp.log(l), (Q_BLOCK, HEAD_DIM))


def _attn_prompt_group(pa, table, g, batch, seq):
    _, dil = DILATED_GROUPS[g]
    sub = seq // dil
    nqb = sub // Q_BLOCK
    wblk = PA_W // A_WIDTH
    pv = pa.reshape(batch, sub, dil * PA_W)

    def spec(off, prev):
        if prev:
            return pl.BlockSpec((None, Q_BLOCK, A_WIDTH), lambda b, r, i: (b, jnp.maximum(i - 1, 0), r * wblk + off + g))
        return pl.BlockSpec((None, Q_BLOCK, A_WIDTH), lambda b, r, i: (b, i, r * wblk + off + g))

    out_spec = pl.BlockSpec((None, Q_BLOCK, A_WIDTH), lambda b, r, i: (b, i, r))
    o, lse = pl.pallas_call(
        _attn_prompt_kernel,
        grid=(batch, dil, nqb),
        in_specs=[spec(0, False), spec(N_GROUPS, False), spec(N_GROUPS, True), spec(2 * N_GROUPS, False),
                  spec(2 * N_GROUPS, True), pl.BlockSpec((H_A, Q_BLOCK, 2 * Q_BLOCK), lambda b, r, i: (0, 0, 0))],
        out_specs=[out_spec, out_spec],
        out_shape=[jax.ShapeDtypeStruct((batch, sub, dil * A_WIDTH), F32)] * 2,
        compiler_params=_cparams(("parallel", "parallel", "arbitrary")),
        name=f"attn_prompt_g{g}",
    )(pv, pv, pv, pv, pv, table)
    return o.reshape(batch * seq, A_WIDTH), lse.reshape(batch * seq, A_WIDTH)


def _attn_sample_kernel(qkv_ref, c1_ref, c2_ref, c3_ref, tabc_ref, tabn_ref, o_ref, *, dec_seq, combo_base):
    caches = (c1_ref, c2_ref, c3_ref)
    kv_w = 2 * A_WIDTH
    for s in range(dec_seq):
        for h in range(H_A):
            outs, lses = [], []
            for g, (_, dil) in enumerate(DILATED_GROUPS):
                rho, fl = s % dil, s // dil
                col = g * A_WIDTH + h * HEAD_DIM
                q = qkv_ref[s:s + 1, col:col + HEAD_DIM]
                kn = qkv_ref[:, A_QKV_WIDTH + col:A_QKV_WIDTH + col + HEAD_DIM]
                vn = qkv_ref[:, 2 * A_QKV_WIDTH + col:2 * A_QKV_WIDTH + col + HEAD_DIM]
                c0 = rho * kv_w + h * HEAD_DIM
                kc = caches[g][:, c0:c0 + HEAD_DIM]
                vc = caches[g][:, c0 + A_WIDTH:c0 + A_WIDTH + HEAD_DIM]
                sc = jnp.sum(kc * q, axis=-1, keepdims=True) * ATT_SCALE + tabc_ref[combo_base[g] + fl, h]
                sn = jnp.sum(kn * q, axis=-1, keepdims=True) * ATT_SCALE + tabn_ref[g, h, s]
                m = jnp.maximum(jnp.max(sc, axis=0, keepdims=True), jnp.max(sn, axis=0, keepdims=True))
                pc = jnp.exp(sc - m)
                pn = jnp.exp(sn - m)
                l = jnp.sum(pc, axis=0, keepdims=True) + jnp.sum(pn, axis=0, keepdims=True)
                o = (jnp.sum(pc * vc, axis=0, keepdims=True) + jnp.sum(pn * vn, axis=0, keepdims=True)) / l
                outs.append(o)
                lses.append(m + jnp.log(l))
            mm = jnp.maximum(jnp.maximum(lses[0], lses[1]), lses[2])
            ws = [jnp.exp(x - mm) for x in lses]
            tot = ws[0] + ws[1] + ws[2]
            o_ref[s:s + 1, h * HEAD_DIM:(h + 1) * HEAD_DIM] = (ws[0] * outs[0] + ws[1] * outs[1] + ws[2] * outs[2]) / tot


def _attn_sample(pa, caches, tabc, tabn, combo_base, batch, dec_seq):
    kv_w = 2 * A_WIDTH
    views, specs = [], []
    for g, (window, dil) in enumerate(DILATED_GROUPS):
        assert caches[g].shape[1] == window and dec_seq <= Q_BLOCK
        views.append(caches[g].reshape(batch, Q_BLOCK, dil * kv_w))
        used = min(dil, dec_seq)
        specs.append(pl.BlockSpec((None, Q_BLOCK, used * kv_w), lambda b: (b, 0, 0)))
    qkv = pa.reshape(batch, dec_seq, PA_W)
    out = pl.pallas_call(
        functools.partial(_attn_sample_kernel, dec_seq=dec_seq, combo_base=combo_base),
        grid=(batch,),
        in_specs=[pl.BlockSpec((None, dec_seq, PA_W), lambda b: (b, 0, 0))] + specs + [
            pl.BlockSpec(tabc.shape, lambda b: (0, 0, 0, 0)), pl.BlockSpec(tabn.shape, lambda b: (0, 0, 0, 0, 0))],
        out_specs=pl.BlockSpec((None, dec_seq, A_WIDTH), lambda b: (b, 0, 0)),
        out_shape=jax.ShapeDtypeStruct((batch, dec_seq, A_WIDTH), F32),
        compiler_params=_cparams(("parallel",)),
        name="attn_sample",
    )(qkv, *views, tabc, tabn)
    return out.reshape(batch * dec_seq, A_WIDTH)


def _split3(x):
    hi = x.astype(BF16)
    r = x - hi.astype(F32)
    mid = r.astype(BF16)
    lo = (r - mid.astype(F32)).astype(BF16)
    return hi, mid, lo


def _gla_kernel(p_ref, wup_ref, gb_ref, ng_ref, s0_ref, o_ref, st_ref, *, chunk, tb):
    @pl.when(pl.program_id(1) == 0)
    def _():
        st_ref[...] = s0_ref[...]

    sub = min(GLA_SUB, chunk)
    kq = H_B * DK_B

    def rb(x):
        xb = x.astype(BF16)
        return xb if chunk >= 16 else xb.astype(F32)

    row = lax.broadcasted_iota(jnp.int32, (chunk, chunk), 0)
    colm = lax.broadcasted_iota(jnp.int32, (chunk, chunk), 1)
    tri = rb(jnp.where(row >= colm, 1.0, 0.0))
    for c in range(tb // chunk):
        rows = slice(c * chunk, (c + 1) * chunk)
        z = _dot(rb(p_ref[rows, PB_LR:PB_W]), rb(wup_ref[...])) + gb_ref[...]
        la = -(jnp.maximum(-z, 0.0) + jnp.log1p(jnp.exp(-jnp.abs(z)))) * (1.0 / GLA_TAU)
        b = functools.reduce(lambda u, w: u + w, [_dot(tri, rb(t)) for t in _split3(la)])
        blast = b[chunk - 1:chunk, :]
        q = p_ref[rows, 0:kq] * (DK_B ** -0.5)
        k = p_ref[rows, kq:2 * kq]
        qin = rb(q * jnp.exp(b))
        kst = rb(k * jnp.exp(blast - b))
        for h in range(H_B):
            ks = slice(h * DK_B, (h + 1) * DK_B)
            vs = slice(h * DV_B, (h + 1) * DV_B)
            st = st_ref[h]
            vb = rb(p_ref[rows, 2 * kq + h * DV_B:2 * kq + (h + 1) * DV_B])
            o_inter = _dot_nt(qin[:, ks], rb(st))
            parts = []
            for blk in range(chunk // sub):
                r0 = blk * sub
                n = r0 + sub
                ref_b = b[r0 - 1:r0, ks] if blk > 0 else jnp.zeros((1, DK_B), F32)
                qi = rb(q[r0:n, ks] * jnp.exp(b[r0:n, ks] - ref_b))
                ki = rb(k[0:n, ks] * jnp.exp(jnp.minimum(ref_b - b[0:n, ks], GLA_EXP_CLAMP)))
                a = _dot_nt(qi, ki)
                ti = lax.broadcasted_iota(jnp.int32, (sub, n), 0) + r0
                si = lax.broadcasted_iota(jnp.int32, (sub, n), 1)
                a = jnp.where(si <= ti, a, 0.0)
                parts.append(_dot(rb(a), vb[0:n]))
            o = o_inter + (jnp.concatenate(parts, axis=0) if len(parts) > 1 else parts[0])
            st_ref[h] = st * jnp.exp(blast[:, ks]) + _dot_tn(vb, kst[:, ks])
            on = o * lax.rsqrt(jnp.mean(o * o, axis=-1, keepdims=True) + RMS_EPS) * ng_ref[...]
            rg = p_ref[rows, 2 * kq + H_B * DV_B + h * DV_B:2 * kq + H_B * DV_B + (h + 1) * DV_B]
            o_ref[rows, vs] = (on * (rg * jax.nn.sigmoid(rg))).astype(BF16)


def _gla(pb, wup, gb, ng, s0t, batch, seq, chunk, tb):
    p3 = pb.reshape(batch, seq, PB_W)
    o, st = pl.pallas_call(
        functools.partial(_gla_kernel, chunk=chunk, tb=tb),
        grid=(batch, seq // tb),
        in_specs=[pl.BlockSpec((None, tb, PB_W), lambda b, i: (b, i, 0)),
                  pl.BlockSpec(wup.shape, lambda b, i: (0, 0)),
                  pl.BlockSpec(gb.shape, lambda b, i: (0, 0)),
                  pl.BlockSpec(ng.shape, lambda b, i: (0, 0)),
                  pl.BlockSpec((None, H_B, DV_B, DK_B), lambda b, i: (b, 0, 0, 0))],
        out_specs=[pl.BlockSpec((None, tb, H_B * DV_B), lambda b, i: (b, i, 0)),
                   pl.BlockSpec((None, H_B, DV_B, DK_B), lambda b, i: (b, 0, 0, 0))],
        out_shape=[jax.ShapeDtypeStruct((batch, seq, H_B * DV_B), BF16),
                   jax.ShapeDtypeStruct((batch, H_B, DV_B, DK_B), F32)],
        compiler_params=_cparams(("parallel", "arbitrary")),
        name=f"gla_c{chunk}",
    )(p3, wup, gb, ng, s0t)
    return o.reshape(batch * seq, H_B * DV_B), st


def _layer_norm(u, g, b):
    mu = jnp.mean(u, axis=-1, keepdims=True)
    d = u - mu
    var = jnp.mean(d * d, axis=-1, keepdims=True)
    return d * lax.rsqrt(var + LN_EPS) * g + b


def _merge_kernel(*refs, n_groups):
    x_ref = refs[0]
    oa_refs = refs[1:1 + 2 * n_groups] if n_groups > 1 else refs[1:2]
    rest = refs[1 + (2 * n_groups if n_groups > 1 else 1):]
    (ob_ref, pg_a_ref, pg_b_ref, wpa_ref, wpb_ref, wo_ref, g1_ref, b1_ref, rwh_ref, rwl_ref, rb_ref,
     x1_ref, ti_ref, gt_ref) = rest
    if n_groups > 1:
        os_ = [r[...] for r in oa_refs[:n_groups]]
        ls = [r[...] for r in oa_refs[n_groups:]]
        mm = functools.reduce(jnp.maximum, ls)
        ws = [jnp.exp(x - mm) for x in ls]
        oa = sum(w * o for w, o in zip(ws, os_)) / sum(ws)
    else:
        oa = oa_refs[0][...]
    ya = _dot(oa.astype(BF16), wpa_ref[...])
    yb = _dot(ob_ref[...], wpb_ref[...])
    branch = jax.nn.sigmoid(pg_a_ref[...]) * ya + jax.nn.sigmoid(pg_b_ref[...]) * yb
    y = _dot(branch.astype(BF16), wo_ref[...])
    x1 = _layer_norm(DEEPNORM_ALPHA * x_ref[...] + y, g1_ref[...], b1_ref[...])
    x1_ref[...] = x1
    xh = x1.astype(BF16)
    xl = (x1 - xh.astype(F32)).astype(BF16)
    logits = _dot(xh, rwh_ref[...]) + _dot(xl, rwh_ref[...]) + _dot(xh, rwl_ref[...]) + rb_ref[...]
    lane = lax.broadcasted_iota(jnp.int32, logits.shape, 1)
    vals = logits
    top_v, top_i = [], []
    for _ in range(TOP_K):
        m = jnp.max(vals, axis=-1, keepdims=True)
        ik = jnp.min(jnp.where(vals == m, lane, LANE), axis=-1, keepdims=True)
        vals = jnp.where(lane == ik, -jnp.inf, vals)
        top_v.append(m)
        top_i.append(ik)
    es = [jnp.exp(v - top_v[0]) for v in top_v]
    tot = functools.reduce(lambda a, b: a + b, es)
    ti_ref[...] = jnp.concatenate(top_i, axis=1)
    gt_ref[...] = jnp.concatenate([e / tot for e in es], axis=1)


def _merge(x, oas, ob, pg, wpa, wpb, wo, g1, b1, rwh, rwl, rbp, tm):
    T = x.shape[0]
    n_groups = len(oas) // 2 if len(oas) > 1 else 1

    def row(w):
        return pl.BlockSpec((tm, w), lambda i: (i, 0))

    def const(a):
        return pl.BlockSpec(a.shape, lambda i: (0,) * a.ndim, pipeline_mode=pl.Buffered(1))

    in_specs = ([row(D_MODEL)] + [row(A_WIDTH)] * len(oas) + [row(H_B * DV_B), row(D_MODEL),
                pl.BlockSpec((tm, D_MODEL), lambda i: (i, 1))] + [const(a) for a in (wpa, wpb, wo, g1, b1, rwh, rwl, rbp)])
    return pl.pallas_call(
        functools.partial(_merge_kernel, n_groups=n_groups),
        grid=(T // tm,),
        in_specs=in_specs,
        out_specs=[row(D_MODEL), row(TOP_K), row(TOP_K)],
        out_shape=[jax.ShapeDtypeStruct((T, D_MODEL), F32), jax.ShapeDtypeStruct((T, TOP_K), jnp.int32),
                   jax.ShapeDtypeStruct((T, TOP_K), F32)],
        compiler_params=_cparams(("parallel",)),
        name=f"merge_g{n_groups}",
    )(x, *oas, ob, pg, pg, wpa, wpb, wo, g1, b1, rwh, rwl, rbp)


def _sc_gather(table, idx):
    info = plsc.get_sparse_core_info()
    n_workers = info.num_cores * info.num_subcores
    n, width = idx.shape[0], table.shape[1]
    per_worker = n // n_workers
    assert per_worker * n_workers == n and per_worker % SC_CHUNK == 0
    mesh = plsc.VectorSubcoreMesh(core_axis_name="c", subcore_axis_name="s")

    @functools.partial(
        pl.kernel, mesh=mesh,
        out_type=jax.ShapeDtypeStruct((n, width), table.dtype),
        scratch_types=[pltpu.VMEM((SC_CHUNK,), jnp.int32), pltpu.VMEM((SC_CHUNK, width), table.dtype),
                       pltpu.SemaphoreType.DMA],
    )
    def gather(table_hbm, idx_hbm, out_hbm, idx_v, rows_v, sem):
        wid = lax.axis_index("s") * info.num_cores + lax.axis_index("c")
        base = wid * per_worker

        @pl.loop(0, per_worker // SC_CHUNK)
        def _(c):
            off = pl.multiple_of(base + c * SC_CHUNK, SC_CHUNK)
            pltpu.sync_copy(idx_hbm.at[pl.ds(off, SC_CHUNK)], idx_v)
            pltpu.async_copy(table_hbm.at[idx_v], rows_v, sem).wait()
            pltpu.sync_copy(rows_v, out_hbm.at[pl.ds(off, SC_CHUNK)])

    return gather(table, idx)


def _expert_kernel(be_ref, rows_ref, nu_ref, xs_ref, wg_ref, wu_ref, bg_ref, bu_ref, wd_ref, bd_ref, o_ref,
                   xb_ref, wgb_ref, wub_ref, wdb_ref):
    del be_ref, nu_ref
    i = pl.program_id(0)
    f = pl.program_id(1)
    nrows = rows_ref[i]

    @pl.when(nrows > 0)
    def _():
        @pl.when(f == 0)
        def _():
            xb_ref[...] = xs_ref[...].astype(BF16)

        wgb_ref[...] = wg_ref[...].astype(BF16)
        wub_ref[...] = wu_ref[...].astype(BF16)
        wdb_ref[...] = wd_ref[...].astype(BF16)

    for sb in range(MOE_BM // MOE_SB):
        r = slice(sb * MOE_SB, (sb + 1) * MOE_SB)

        @pl.when(sb * MOE_SB < nrows)
        def _():
            x = xb_ref[r, :]
            g = jnp.minimum(_dot(x, wgb_ref[...]) + bg_ref[...], SWIGLU_LIMIT)
            u = jnp.clip(_dot(x, wub_ref[...]) + bu_ref[...], -SWIGLU_LIMIT, SWIGLU_LIMIT)
            hid = (u + 1.0) * g * jax.nn.sigmoid(SWIGLU_ALPHA * g)
            y = _dot(hid.astype(BF16), wdb_ref[...])

            @pl.when(f == 0)
            def _():
                o_ref[r, :] = y + bd_ref[...]

            @pl.when(f > 0)
            def _():
                o_ref[r, :] += y

        @pl.when(jnp.logical_and(sb * MOE_SB >= nrows, f == 0))
        def _():
            o_ref[r, :] = jnp.zeros((MOE_SB, D_MODEL), F32)


def _experts(xs, block_expert, block_rows, n_used, w_gate_up, b_gate_up, w_down, b_down):
    nb = xs.shape[0] // MOE_BM
    nf = D_FF // MOE_TF

    def fsel(i, f, nu):
        return jnp.where(i < nu[0], f, nf - 1)

    def isel(i, nu):
        return jnp.minimum(i, nu[0] - 1)

    grid_spec = pltpu.PrefetchScalarGridSpec(
        num_scalar_prefetch=3,
        grid=(nb, nf),
        in_specs=[
            pl.BlockSpec((MOE_BM, D_MODEL), lambda i, f, be, rw, nu: (isel(i, nu), 0), pipeline_mode=pl.Buffered(1)),
            pl.BlockSpec((None, D_MODEL, MOE_TF), lambda i, f, be, rw, nu: (be[i], 0, fsel(i, f, nu))),
            pl.BlockSpec((None, D_MODEL, MOE_TF), lambda i, f, be, rw, nu: (be[i], 0, nf + fsel(i, f, nu))),
            pl.BlockSpec((None, 1, MOE_TF), lambda i, f, be, rw, nu: (be[i], 0, fsel(i, f, nu))),
            pl.BlockSpec((None, 1, MOE_TF), lambda i, f, be, rw, nu: (be[i], 0, nf + fsel(i, f, nu))),
            pl.BlockSpec((None, MOE_TF, D_MODEL), lambda i, f, be, rw, nu: (be[i], fsel(i, f, nu), 0)),
            pl.BlockSpec((None, 1, D_MODEL), lambda i, f, be, rw, nu: (be[i], 0, 0)),
        ],
        out_specs=pl.BlockSpec((MOE_BM, D_MODEL), lambda i, f, be, rw, nu: (i, 0)),
        scratch_shapes=[pltpu.VMEM((MOE_BM, D_MODEL), BF16), pltpu.VMEM((D_MODEL, MOE_TF), BF16),
                        pltpu.VMEM((D_MODEL, MOE_TF), BF16), pltpu.VMEM((MOE_TF, D_MODEL), BF16)],
    )
    bgu = b_gate_up.reshape(N_EXPERTS, 1, 2 * D_FF)
    bd = b_down.reshape(N_EXPERTS, 1, D_MODEL)
    return pl.pallas_call(
        _expert_kernel,
        grid_spec=grid_spec,
        out_shape=jax.ShapeDtypeStruct(xs.shape, F32),
        compiler_params=_cparams(("arbitrary", "arbitrary")),
        name="moe_experts",
    )(block_expert, block_rows, n_used, xs, w_gate_up, w_gate_up, bgu, bgu, w_down, bd)


def _route(top_i, n_blocks):
    T = top_i.shape[0]
    expert = top_i.reshape(T * TOP_K)
    onehot = (expert[:, None] == jnp.arange(N_EXPERTS, dtype=jnp.int32)[None, :]).astype(jnp.int32)
    csum = jnp.cumsum(onehot, axis=0)
    counts = csum[-1]
    rank = jnp.take_along_axis(csum, expert[:, None], axis=1)[:, 0] - 1
    bpe = (counts + MOE_BM - 1) // MOE_BM
    bend = jnp.cumsum(bpe)
    bstart = bend - bpe
    dest = (bstart[expert] * MOE_BM + rank).astype(jnp.int32)
    token = jnp.repeat(jnp.arange(T, dtype=jnp.int32), TOP_K)
    row_token = jnp.zeros((n_blocks * MOE_BM,), jnp.int32).at[dest].set(token)
    n_used = bend[-1]
    blk = jnp.arange(n_blocks, dtype=jnp.int32)
    be = jnp.minimum(jnp.searchsorted(bend, jnp.minimum(blk, n_used - 1), side="right"), N_EXPERTS - 1).astype(jnp.int32)
    rows = jnp.clip(counts[be] - (blk - bstart[be]) * MOE_BM, 0, MOE_BM)
    rows = jnp.where(blk < n_used, rows, 0).astype(jnp.int32)
    return dest, row_token, be, rows, n_used.reshape(1).astype(jnp.int32)


def _combine_kernel(x1_ref, ge_ref, gt_ref, g2_ref, b2_ref, o_ref):
    gt = gt_ref[...]
    m = gt[:, 0:1] * ge_ref[:, 0:D_MODEL]
    for k in range(1, TOP_K):
        m = m + gt[:, k:k + 1] * ge_ref[:, k * D_MODEL:(k + 1) * D_MODEL]
    o_ref[...] = _layer_norm(DEEPNORM_ALPHA * x1_ref[...] + m, g2_ref[...], b2_ref[...])


def _combine(x1, ge, gate, g2, b2, row0, n_rows, tm):
    b0 = row0 // tm

    def row(w):
        return pl.BlockSpec((tm, w), lambda i: (b0 + i, 0))

    return pl.pallas_call(
        _combine_kernel,
        grid=(n_rows // tm,),
        in_specs=[row(D_MODEL), row(TOP_K * D_MODEL), row(TOP_K),
                  pl.BlockSpec(g2.shape, lambda i: (0, 0)), pl.BlockSpec(b2.shape, lambda i: (0, 0))],
        out_specs=pl.BlockSpec((tm, D_MODEL), lambda i: (i, 0)),
        out_shape=jax.ShapeDtypeStruct((n_rows, D_MODEL), F32),
        compiler_params=_cparams(("parallel",)),
        name="moe_combine",
    )(x1, ge, gate, g2, b2)


def _layer(xp, xs, caches, state, w_in, rel_bias, gla_w_up, gla_b, gla_norm_g, w_pa, w_pb, w_o, ln1_g, ln1_b,
           router_w, router_b, w_gate_up, b_gate_up, w_down, b_down, ln2_g, ln2_b):
    batch, seq, _ = xp.shape
    dbatch, dseq, _ = xs.shape
    tp, ts = batch * seq, dbatch * dseq
    xp2, xs2 = xp.reshape(tp, D_MODEL), xs.reshape(ts, D_MODEL)

    o_b0, o_lr, o_g = PA_W, PA_W + PB_LR, PA_W + PB_LR + GLA_RANK
    w_a = w_in[:, :PA_W].astype(BF16)
    w_b = jnp.concatenate([w_in[:, o_b0:o_g], jnp.zeros((D_MODEL, LANE - GLA_RANK), F32)], axis=1).astype(BF16)
    w_g = w_in[:, o_g:].astype(BF16)
    wup = jnp.concatenate([gla_w_up, jnp.zeros((LANE - GLA_RANK, H_B * DK_B), F32)], axis=0).astype(BF16)
    gb = gla_b.reshape(1, H_B * DK_B)
    ng = gla_norm_g.reshape(1, DV_B)
    wpa, wpb, wo = w_pa.astype(BF16), w_pb.astype(BF16), w_o.astype(BF16)
    g1, b1 = ln1_g.reshape(1, D_MODEL), ln1_b.reshape(1, D_MODEL)
    g2, b2 = ln2_g.reshape(1, D_MODEL), ln2_b.reshape(1, D_MODEL)
    rw = jnp.concatenate([router_w, jnp.zeros((D_MODEL, LANE - N_EXPERTS), F32)], axis=1)
    rwh = rw.astype(BF16)
    rwl = (rw - rwh.astype(F32)).astype(BF16)
    rbp = jnp.concatenate([router_b, jnp.full((LANE - N_EXPERTS,), NEG, F32)]).reshape(1, LANE)
    biases = _group_biases(rel_bias)

    pa_p = _project(xp2, w_a, 512, A_QKV_WIDTH, "proj_a_prompt")
    pb_p = _project(xp2, w_b, 512, PB_W, "proj_b_prompt")
    pg_p = _project(xp2, w_g, 512, 1024, "proj_g_prompt")
    oas, lses = [], []
    for g in range(N_GROUPS):
        o, lse = _attn_prompt_group(pa_p, _prompt_table(biases[g]), g, batch, seq)
        oas.append(o)
        lses.append(lse)
    ob_p, st_p = _gla(pb_p, wup, gb, ng, jnp.zeros((batch, H_B, DV_B, DK_B), F32), batch, seq, GLA_CHUNK, 256)
    x1_p, ti_p, gt_p = _merge(xp2, oas + lses, ob_p, pg_p, wpa, wpb, wo, g1, b1, rwh, rwl, rbp, 256)

    pa_s = _project(xs2, w_a, ts, A_QKV_WIDTH, "proj_a_sample")
    pb_s = _project(xs2, w_b, ts, PB_W, "proj_b_sample")
    pg_s = _project(xs2, w_g, ts, 1024, "proj_g_sample")
    tabc, tabn, combo_base = _sample_tables(biases, dseq)
    oa_s = _attn_sample(pa_s, caches, tabc, tabn, combo_base, dbatch, dseq)
    chunk_s = int(np.gcd(dseq, GLA_CHUNK))
    ob_s, st_s = _gla(pb_s, wup, gb, ng, jnp.swapaxes(state, -1, -2), dbatch, dseq, chunk_s, dseq)
    x1_s, ti_s, gt_s = _merge(xs2, [oa_s], ob_s, pg_s, wpa, wpb, wo, g1, b1, rwh, rwl, rbp, ts)

    x1 = jnp.concatenate([x1_p, x1_s], axis=0)
    top_i = jnp.concatenate([ti_p, ti_s], axis=0)
    gate = jnp.concatenate([gt_p, gt_s], axis=0)
    t_all = tp + ts
    n_blocks = -(-(t_all * TOP_K) // MOE_BM) + N_EXPERTS
    dest, row_token, be, rows, n_used = _route(top_i, n_blocks)
    xsorted = _sc_gather(x1, row_token)
    eo = _experts(xsorted, be, rows, n_used, w_gate_up, b_gate_up, w_down, b_down)
    ge = _sc_gather(eo, dest).reshape(t_all, TOP_K * D_MODEL)
    y_p = _combine(x1, ge, gate, g2, b2, 0, tp, 256)
    y_s = _combine(x1, ge, gate, g2, b2, tp, ts, 256)

    kp = pa_p[:, A_QKV_WIDTH:2 * A_QKV_WIDTH].reshape(batch, seq, N_GROUPS, H_A, HEAD_DIM)
    vp = pa_p[:, 2 * A_QKV_WIDTH:].reshape(batch, seq, N_GROUPS, H_A, HEAD_DIM)
    ks = pa_s[:, A_QKV_WIDTH:2 * A_QKV_WIDTH].reshape(dbatch, dseq, N_GROUPS, H_A, HEAD_DIM)
    vs = pa_s[:, 2 * A_QKV_WIDTH:].reshape(dbatch, dseq, N_GROUPS, H_A, HEAD_DIM)
    bufs_p, bufs_s = [], []
    for g, (window, _) in enumerate(DILATED_GROUPS):
        keep = min(window, seq)
        bufs_p.append(jnp.stack([kp[:, seq - keep:, g], vp[:, seq - keep:, g]], axis=2))
        ctx = jnp.concatenate([caches[g], jnp.stack([ks[:, :, g], vs[:, :, g]], axis=2)], axis=1)
        keep = min(window, ctx.shape[1])
        bufs_s.append(ctx[:, ctx.shape[1] - keep:])
    return (y_p.reshape(batch, seq, D_MODEL), y_s.reshape(dbatch, dseq, D_MODEL), bufs_p, jnp.swapaxes(st_p, -1, -2),
            bufs_s, jnp.swapaxes(st_s, -1, -2))


def kernel(x_prompt, x_sample, cache_a1_kv, cache_a2_kv, cache_a3_kv, state_b_s, w_in, rel_bias, gla_w_up, gla_b,
           gla_norm_g, w_pa, w_pb, w_o, ln1_g, ln1_b, router_w, router_b, w_gate_up, b_gate_up, w_down, b_down,
           ln2_g, ln2_b):
    assert w_in.shape[0] == DEPTH
    yp, ys, bufs_p, st_p, bufs_s, st_s = _layer(
        x_prompt, x_sample, (cache_a1_kv[0], cache_a2_kv[0], cache_a3_kv[0]), state_b_s[0], w_in[0], rel_bias,
        gla_w_up[0], gla_b[0], gla_norm_g[0], w_pa[0], w_pb[0], w_o[0], ln1_g[0], ln1_b[0], router_w[0], router_b[0],
        w_gate_up[0], b_gate_up[0], w_down[0], b_down[0], ln2_g[0], ln2_b[0])
    return (yp, ys, bufs_p[0][None], bufs_p[1][None], bufs_p[2][None], st_p[None],
            bufs_s[0][None], bufs_s[1][None], bufs_s[2][None], st_s[None].astype(state_b_s.dtype))
```

```python
import functools

import numpy as np
import jax
import jax.numpy as jnp
from jax import lax
from jax.experimental import pallas as pl
from jax.experimental.pallas import tpu as pltpu
from jax.experimental.pallas import tpu_sc as plsc

F32 = jnp.float32
BF16 = jnp.bfloat16

D_MODEL = 2048
HEAD_DIM = 128
DILATED_GROUPS = ((128, 1), (512, 4), (2048, 16))
N_GROUPS = 3
H_A = 4
A_WIDTH = H_A * HEAD_DIM
A_QKV_WIDTH = N_GROUPS * A_WIDTH
Q_BLOCK = 128
N_BUCKETS = 32
REL_MAX_DIST = 2048
H_B = 4
DK_B = 64
DV_B = 128
GLA_RANK = 16
GLA_TAU = 16.0
GLA_CHUNK = 64
GLA_SUB = 16
GLA_EXP_CLAMP = 80.0
N_EXPERTS = 32
TOP_K = 4
D_FF = 2048
SWIGLU_LIMIT = 7.0
SWIGLU_ALPHA = 1.702
LN_EPS = 1e-5
RMS_EPS = 1e-6
DEPTH = 1
DEEPNORM_ALPHA = (2.0 * DEPTH) ** 0.25
ATT_SCALE = HEAD_DIM ** -0.5
NEG = float(np.finfo(np.float32).min)

VMEM_LIMIT = 56 * 1024 * 1024
LANE = 128

PA_W = 3 * A_QKV_WIDTH
PB_LR = H_B * DK_B * 2 + H_B * DV_B * 2
PB_W = PB_LR + LANE
PG_W = 2 * D_MODEL

MOE_BM = 1024
MOE_SB = 512
MOE_TF = 512
MOE_NF = D_FF // MOE_TF
MOE_TN = 512
SC_CHUNK_BYTES = 128 * 1024


def _cparams(sem):
    return pltpu.CompilerParams(dimension_semantics=sem, vmem_limit_bytes=VMEM_LIMIT)


def _dot(a, b):
    return jnp.dot(a, b, preferred_element_type=F32)


def _dot_nt(a, b):
    return lax.dot_general(a, b, (((1,), (1,)), ((), ())), preferred_element_type=F32)


def _dot_tn(a, b):
    return lax.dot_general(a, b, (((0,), (0,)), ((), ())), preferred_element_type=F32)


def _proj_kernel(x_ref, w_ref, o_ref, xb_ref):
    @pl.when(pl.program_id(1) == 0)
    def _():
        xb_ref[...] = x_ref[...].astype(BF16)

    o_ref[...] = _dot(xb_ref[...], w_ref[...])


def _project(x, w, tm, tn, name):
    T, D = x.shape
    N = w.shape[1]
    return pl.pallas_call(
        _proj_kernel,
        grid=(T // tm, N // tn),
        in_specs=[pl.BlockSpec((tm, D), lambda i, j: (i, 0)), pl.BlockSpec((D, tn), lambda i, j: (0, j))],
        out_specs=pl.BlockSpec((tm, tn), lambda i, j: (i, j)),
        out_shape=jax.ShapeDtypeStruct((T, N), F32),
        scratch_shapes=[pltpu.VMEM((tm, D), BF16)],
        compiler_params=_cparams(("parallel", "arbitrary")),
        name=name,
    )(x, w)


def _t5_bucket(dist):
    max_exact = N_BUCKETS // 2
    d = np.maximum(dist, 1).astype(np.float32)
    large = max_exact + (np.log(d / max_exact) / np.log(REL_MAX_DIST / max_exact) * (N_BUCKETS - max_exact)).astype(np.int32)
    large = np.minimum(large, N_BUCKETS - 1)
    return np.where(dist < max_exact, dist, large).astype(np.int32)


def _group_biases(rel_bias):
    out = []
    for g, (window, dil) in enumerate(DILATED_GROUPS):
        bucket = _t5_bucket(dil * np.arange(window // dil + 1))
        out.append(rel_bias[bucket][:, g * H_A:(g + 1) * H_A].astype(F32))
    return out


def _prompt_table(bias_g):
    qi = np.arange(Q_BLOCK)[:, None]
    kj = np.arange(2 * Q_BLOCK)[None, :]
    j = Q_BLOCK + qi - kj
    valid = (j >= 0) & (j <= Q_BLOCK)
    tab = bias_g.T[:, np.clip(j, 0, Q_BLOCK)]
    return jnp.where(valid[None], tab, NEG)


def _sample_tables(biases, dec_seq):
    m = np.arange(Q_BLOCK)
    tabc, combo_base = [], []
    for g, (_, dil) in enumerate(DILATED_GROUPS):
        combo_base.append(len(tabc))
        for fl in range((dec_seq - 1) // dil + 1):
            j = Q_BLOCK + fl - m
            valid = j <= Q_BLOCK
            col = jnp.where(valid[None], biases[g].T[:, np.clip(j, 0, Q_BLOCK)], NEG)
            tabc.append(jnp.broadcast_to(col[:, :, None], (H_A, Q_BLOCK, LANE)))
    tabn = []
    s = np.arange(dec_seq)[:, None]
    sp = np.arange(dec_seq)[None, :]
    for g, (_, dil) in enumerate(DILATED_GROUPS):
        diff = s - sp
        valid = (diff >= 0) & (diff % dil == 0)
        t = jnp.where(valid[None], biases[g].T[:, np.clip(diff // dil, 0, Q_BLOCK)], NEG)
        tabn.append(jnp.broadcast_to(t[..., None], (H_A, dec_seq, dec_seq, LANE)))
    return jnp.stack(tabc), jnp.stack(tabn), tuple(combo_base)


def _attn_prompt_kernel(q_ref, kc_ref, kp_ref, vc_ref, vp_ref, tab_ref, o_ref, lse_ref):
    has_prev = pl.program_id(2) > 0
    for h in range(H_A):
        sl = slice(h * HEAD_DIM, (h + 1) * HEAD_DIM)
        q = q_ref[:, sl].astype(BF16)
        sc = _dot_nt(q, kc_ref[:, sl].astype(BF16)) * ATT_SCALE + tab_ref[h, :, Q_BLOCK:]
        sp = _dot_nt(q, kp_ref[:, sl].astype(BF16)) * ATT_SCALE + tab_ref[h, :, :Q_BLOCK]
        sp = jnp.where(has_prev, sp, NEG)
        m = jnp.maximum(jnp.max(sc, axis=-1, keepdims=True), jnp.max(sp, axis=-1, keepdims=True))
        pc = jnp.exp(sc - m)
        pp = jnp.exp(sp - m)
        l = jnp.sum(pc, axis=-1, keepdims=True) + jnp.sum(pp, axis=-1, keepdims=True)
        inv = 1.0 / l
        o = _dot((pc * inv).astype(BF16), vc_ref[:, sl].astype(BF16)) + _dot((pp * inv).astype(BF16), vp_ref[:, sl].astype(BF16))
        o_ref[:, sl] = o
        lse_ref[:, sl] = jnp.broadcast_to(m + jnp.log(l), (Q_BLOCK, HEAD_DIM))


def _attn_prompt_strided_kernel(*refs, dil, with_prev):
    if with_prev:
        q_ref, kc_ref, kp_ref, vc_ref, vp_ref, tab_ref, o_ref, lse_ref = refs
    else:
        q_ref, kc_ref, vc_ref, tab_ref, o_ref, lse_ref = refs
    has_prev = pl.program_id(1) > 0

    def body(r, carry):
        idx = pl.ds(r, Q_BLOCK, stride=dil)
        q = q_ref[idx, :].astype(BF16)
        sc = _dot_nt(q, kc_ref[idx, :].astype(BF16)) * ATT_SCALE + tab_ref[:, Q_BLOCK:]
        m = jnp.max(sc, axis=-1, keepdims=True)
        if with_prev:
            sp = _dot_nt(q, kp_ref[idx, :].astype(BF16)) * ATT_SCALE + tab_ref[:, :Q_BLOCK]
            sp = jnp.where(has_prev, sp, NEG)
            m = jnp.maximum(m, jnp.max(sp, axis=-1, keepdims=True))
        pc = jnp.exp(sc - m)
        l = jnp.sum(pc, axis=-1, keepdims=True)
        if with_prev:
            pp = jnp.exp(sp - m)
            l = l + jnp.sum(pp, axis=-1, keepdims=True)
        inv = 1.0 / l
        o = _dot((pc * inv).astype(BF16), vc_ref[idx, :].astype(BF16))
        if with_prev:
            o = o + _dot((pp * inv).astype(BF16), vp_ref[idx, :].astype(BF16))
        o_ref[idx, :] = o
        lse_ref[idx, :] = jnp.broadcast_to(m + jnp.log(l), (Q_BLOCK, HEAD_DIM))
        return carry

    lax.fori_loop(0, dil, body, 0, unroll=min(dil, 4))


def _attn_prompt_strided(pa, table, g, batch, seq):
    _, dil = DILATED_GROUPS[g]
    rows = dil * Q_BLOCK
    nblk = seq // rows
    with_prev = nblk > 1
    hcols = A_QKV_WIDTH // HEAD_DIM

    def spec(sec, prev):
        if prev:
            return pl.BlockSpec((rows, HEAD_DIM), lambda b, i, h: (b * nblk + jnp.maximum(i - 1, 0), sec * hcols + g * H_A + h))
        return pl.BlockSpec((rows, HEAD_DIM), lambda b, i, h: (b * nblk + i, sec * hcols + g * H_A + h))

    in_specs = [spec(0, False), spec(1, False)] + ([spec(1, True)] if with_prev else []) + [spec(2, False)] + (
        [spec(2, True)] if with_prev else []) + [pl.BlockSpec((None, Q_BLOCK, 2 * Q_BLOCK), lambda b, i, h: (h, 0, 0))]
    out_spec = pl.BlockSpec((rows, HEAD_DIM), lambda b, i, h: (b * nblk + i, h))
    return pl.pallas_call(
        functools.partial(_attn_prompt_strided_kernel, dil=dil, with_prev=with_prev),
        grid=(batch, nblk, H_A),
        in_specs=in_specs,
        out_specs=[out_spec, out_spec],
        out_shape=[jax.ShapeDtypeStruct((batch * seq, A_WIDTH), F32)] * 2,
        compiler_params=_cparams(("parallel", "arbitrary", "arbitrary")),
        name=f"attn_prompt_g{g}",
    )(*([pa] * (len(in_specs) - 1)), table)


def _attn_prompt_group(pa, table, g, batch, seq):
    _, dil = DILATED_GROUPS[g]
    if dil > 1:
        return _attn_prompt_strided(pa, table, g, batch, seq)
    sub = seq // dil
    nqb = sub // Q_BLOCK
    wblk = PA_W // A_WIDTH
    pv = pa.reshape(batch, sub, dil * PA_W)

    def spec(off, prev):
        if prev:
            return pl.BlockSpec((None, Q_BLOCK, A_WIDTH), lambda b, r, i: (b, jnp.maximum(i - 1, 0), r * wblk + off + g))
        return pl.BlockSpec((None, Q_BLOCK, A_WIDTH), lambda b, r, i: (b, i, r * wblk + off + g))

    out_spec = pl.BlockSpec((None, Q_BLOCK, A_WIDTH), lambda b, r, i: (b, i, r))
    o, lse = pl.pallas_call(
        _attn_prompt_kernel,
        grid=(batch, dil, nqb),
        in_specs=[spec(0, False), spec(N_GROUPS, False), spec(N_GROUPS, True), spec(2 * N_GROUPS, False),
                  spec(2 * N_GROUPS, True), pl.BlockSpec((H_A, Q_BLOCK, 2 * Q_BLOCK), lambda b, r, i: (0, 0, 0))],
        out_specs=[out_spec, out_spec],
        out_shape=[jax.ShapeDtypeStruct((batch, sub, dil * A_WIDTH), F32)] * 2,
        compiler_params=_cparams(("parallel", "parallel", "arbitrary")),
        name=f"attn_prompt_g{g}",
    )(pv, pv, pv, pv, pv, table)
    return o.reshape(batch * seq, A_WIDTH), lse.reshape(batch * seq, A_WIDTH)


def _attn_sample_kernel(qkv_ref, c1_ref, c2_ref, c3_ref, tabc_ref, tabn_ref, o_ref, *, dec_seq, combo_base):
    caches = (c1_ref, c2_ref, c3_ref)
    kv_w = 2 * A_WIDTH
    for s in range(dec_seq):
        for h in range(H_A):
            outs, lses = [], []
            for g, (_, dil) in enumerate(DILATED_GROUPS):
                rho, fl = s % dil, s // dil
                col = g * A_WIDTH + h * HEAD_DIM
                q = qkv_ref[s:s + 1, col:col + HEAD_DIM]
                kn = qkv_ref[:, A_QKV_WIDTH + col:A_QKV_WIDTH + col + HEAD_DIM]
                vn = qkv_ref[:, 2 * A_QKV_WIDTH + col:2 * A_QKV_WIDTH + col + HEAD_DIM]
                c0 = rho * kv_w + h * HEAD_DIM
                kc = caches[g][:, c0:c0 + HEAD_DIM]
                vc = caches[g][:, c0 + A_WIDTH:c0 + A_WIDTH + HEAD_DIM]
                sc = jnp.sum(kc * q, axis=-1, keepdims=True) * ATT_SCALE + tabc_ref[combo_base[g] + fl, h]
                sn = jnp.sum(kn * q, axis=-1, keepdims=True) * ATT_SCALE + tabn_ref[g, h, s]
                m = jnp.maximum(jnp.max(sc, axis=0, keepdims=True), jnp.max(sn, axis=0, keepdims=True))
                pc = jnp.exp(sc - m)
                pn = jnp.exp(sn - m)
                l = jnp.sum(pc, axis=0, keepdims=True) + jnp.sum(pn, axis=0, keepdims=True)
                o = (jnp.sum(pc * vc, axis=0, keepdims=True) + jnp.sum(pn * vn, axis=0, keepdims=True)) / l
                outs.append(o)
                lses.append(m + jnp.log(l))
            mm = jnp.maximum(jnp.maximum(lses[0], lses[1]), lses[2])
            ws = [jnp.exp(x - mm) for x in lses]
            tot = ws[0] + ws[1] + ws[2]
            o_ref[s:s + 1, h * HEAD_DIM:(h + 1) * HEAD_DIM] = (ws[0] * outs[0] + ws[1] * outs[1] + ws[2] * outs[2]) / tot


def _attn_sample(pa, caches, tabc, tabn, combo_base, batch, dec_seq):
    kv_w = 2 * A_WIDTH
    views, specs = [], []
    for g, (window, dil) in enumerate(DILATED_GROUPS):
        assert caches[g].shape[1] == window and dec_seq <= Q_BLOCK
        views.append(caches[g].reshape(batch, Q_BLOCK, dil * kv_w))
        used = min(dil, dec_seq)
        specs.append(pl.BlockSpec((None, Q_BLOCK, used * kv_w), lambda b: (b, 0, 0)))
    qkv = pa.reshape(batch, dec_seq, PA_W)
    out = pl.pallas_call(
        functools.partial(_attn_sample_kernel, dec_seq=dec_seq, combo_base=combo_base),
        grid=(batch,),
        in_specs=[pl.BlockSpec((None, dec_seq, PA_W), lambda b: (b, 0, 0))] + specs + [
            pl.BlockSpec(tabc.shape, lambda b: (0, 0, 0, 0)), pl.BlockSpec(tabn.shape, lambda b: (0, 0, 0, 0, 0))],
        out_specs=pl.BlockSpec((None, dec_seq, A_WIDTH), lambda b: (b, 0, 0)),
        out_shape=jax.ShapeDtypeStruct((batch, dec_seq, A_WIDTH), F32),
        compiler_params=_cparams(("parallel",)),
        name="attn_sample",
    )(qkv, *views, tabc, tabn)
    return out.reshape(batch * dec_seq, A_WIDTH)


def _split3(x):
    hi = x.astype(BF16)
    r = x - hi.astype(F32)
    mid = r.astype(BF16)
    lo = (r - mid.astype(F32)).astype(BF16)
    return hi, mid, lo


def _gla_kernel(p_ref, wup_ref, gb_ref, ng_ref, s0_ref, o_ref, st_ref, *, chunk, tb):
    @pl.when(pl.program_id(1) == 0)
    def _():
        st_ref[...] = s0_ref[...]

    sub = min(GLA_SUB, chunk)
    kq = H_B * DK_B

    def rb(x):
        xb = x.astype(BF16)
        return xb if chunk >= 16 else xb.astype(F32)

    row = lax.broadcasted_iota(jnp.int32, (chunk, chunk), 0)
    colm = lax.broadcasted_iota(jnp.int32, (chunk, chunk), 1)
    tri = rb(jnp.where(row >= colm, 1.0, 0.0))
    for c in range(tb // chunk):
        rows = slice(c * chunk, (c + 1) * chunk)
        z = _dot(rb(p_ref[rows, PB_LR:PB_W]), rb(wup_ref[...])) + gb_ref[...]
        la = -(jnp.maximum(-z, 0.0) + jnp.log1p(jnp.exp(-jnp.abs(z)))) * (1.0 / GLA_TAU)
        b = functools.reduce(lambda u, w: u + w, [_dot(tri, rb(t)) for t in _split3(la)])
        blast = b[chunk - 1:chunk, :]
        q = p_ref[rows, 0:kq] * (DK_B ** -0.5)
        k = p_ref[rows, kq:2 * kq]
        qin = rb(q * jnp.exp(b))
        kst = rb(k * jnp.exp(blast - b))
        for h in range(H_B):
            ks = slice(h * DK_B, (h + 1) * DK_B)
            vs = slice(h * DV_B, (h + 1) * DV_B)
            st = st_ref[h]
            vb = rb(p_ref[rows, 2 * kq + h * DV_B:2 * kq + (h + 1) * DV_B])
            o_inter = _dot_nt(qin[:, ks], rb(st))
            parts = []
            for blk in range(chunk // sub):
                r0 = blk * sub
                n = r0 + sub
                ref_b = b[r0 - 1:r0, ks] if blk > 0 else jnp.zeros((1, DK_B), F32)
                qi = rb(q[r0:n, ks] * jnp.exp(b[r0:n, ks] - ref_b))
                ki = rb(k[0:n, ks] * jnp.exp(jnp.minimum(ref_b - b[0:n, ks], GLA_EXP_CLAMP)))
                a = _dot_nt(qi, ki)
                ti = lax.broadcasted_iota(jnp.int32, (sub, n), 0) + r0
                si = lax.broadcasted_iota(jnp.int32, (sub, n), 1)
                a = jnp.where(si <= ti, a, 0.0)
                parts.append(_dot(rb(a), vb[0:n]))
            o = o_inter + (jnp.concatenate(parts, axis=0) if len(parts) > 1 else parts[0])
            st_ref[h] = st * jnp.exp(blast[:, ks]) + _dot_tn(vb, kst[:, ks])
            on = o * lax.rsqrt(jnp.mean(o * o, axis=-1, keepdims=True) + RMS_EPS) * ng_ref[...]
            rg = p_ref[rows, 2 * kq + H_B * DV_B + h * DV_B:2 * kq + H_B * DV_B + (h + 1) * DV_B]
            o_ref[rows, vs] = (on * (rg * jax.nn.sigmoid(rg))).astype(BF16)


def _gla(pb, wup, gb, ng, s0t, batch, seq, chunk, tb):
    p3 = pb.reshape(batch, seq, PB_W)
    o, st = pl.pallas_call(
        functools.partial(_gla_kernel, chunk=chunk, tb=tb),
        grid=(batch, seq // tb),
        in_specs=[pl.BlockSpec((None, tb, PB_W), lambda b, i: (b, i, 0)),
                  pl.BlockSpec(wup.shape, lambda b, i: (0, 0)),
                  pl.BlockSpec(gb.shape, lambda b, i: (0, 0)),
                  pl.BlockSpec(ng.shape, lambda b, i: (0, 0)),
                  pl.BlockSpec((None, H_B, DV_B, DK_B), lambda b, i: (b, 0, 0, 0))],
        out_specs=[pl.BlockSpec((None, tb, H_B * DV_B), lambda b, i: (b, i, 0)),
                   pl.BlockSpec((None, H_B, DV_B, DK_B), lambda b, i: (b, 0, 0, 0))],
        out_shape=[jax.ShapeDtypeStruct((batch, seq, H_B * DV_B), BF16),
                   jax.ShapeDtypeStruct((batch, H_B, DV_B, DK_B), F32)],
        compiler_params=_cparams(("parallel", "arbitrary")),
        name=f"gla_c{chunk}",
    )(p3, wup, gb, ng, s0t)
    return o.reshape(batch * seq, H_B * DV_B), st


def _layer_norm(u, g, b):
    mu = jnp.mean(u, axis=-1, keepdims=True)
    d = u - mu
    var = jnp.mean(d * d, axis=-1, keepdims=True)
    return d * lax.rsqrt(var + LN_EPS) * g + b


def _merge_kernel(*refs, n_groups):
    x_ref = refs[0]
    oa_refs = refs[1:1 + 2 * n_groups] if n_groups > 1 else refs[1:2]
    rest = refs[1 + (2 * n_groups if n_groups > 1 else 1):]
    (ob_ref, pg_a_ref, pg_b_ref, wpa_ref, wpb_ref, wo_ref, g1_ref, b1_ref, rwh_ref, rwl_ref, rb_ref,
     x1_ref, x1p_ref, ti_ref, gt_ref) = rest
    if n_groups > 1:
        os_ = [r[...] for r in oa_refs[:n_groups]]
        ls = [r[...] for r in oa_refs[n_groups:]]
        mm = functools.reduce(jnp.maximum, ls)
        ws = [jnp.exp(x - mm) for x in ls]
        oa = sum(w * o for w, o in zip(ws, os_)) / sum(ws)
    else:
        oa = oa_refs[0][...]
    ya = _dot(oa.astype(BF16), wpa_ref[...])
    yb = _dot(ob_ref[...], wpb_ref[...])
    branch = jax.nn.sigmoid(pg_a_ref[...]) * ya + jax.nn.sigmoid(pg_b_ref[...]) * yb
    y = _dot(branch.astype(BF16), wo_ref[...])
    x1 = _layer_norm(DEEPNORM_ALPHA * x_ref[...] + y, g1_ref[...], b1_ref[...])
    x1_ref[...] = x1
    xh = x1.astype(BF16)
    xhf = xh.astype(F32)
    bits = lax.bitcast_convert_type(xhf, jnp.int32)
    half = D_MODEL // 2
    x1p_ref[...] = lax.shift_right_logical(bits[:, :half], 16) | bits[:, half:]
    xl = (x1 - xhf).astype(BF16)
    logits = _dot(xh, rwh_ref[...]) + _dot(xl, rwh_ref[...]) + _dot(xh, rwl_ref[...]) + rb_ref[...]
    lane = lax.broadcasted_iota(jnp.int32, logits.shape, 1)
    vals = logits
    top_v, top_i = [], []
    for _ in range(TOP_K):
        m = jnp.max(vals, axis=-1, keepdims=True)
        ik = jnp.min(jnp.where(vals == m, lane, LANE), axis=-1, keepdims=True)
        vals = jnp.where(lane == ik, -jnp.inf, vals)
        top_v.append(m)
        top_i.append(ik)
    es = [jnp.exp(v - top_v[0]) for v in top_v]
    tot = functools.reduce(lambda a, b: a + b, es)
    ti_ref[...] = jnp.concatenate(top_i, axis=1)
    gt_ref[...] = jnp.concatenate([e / tot for e in es], axis=1)


def _merge(x, oas, ob, pg, wpa, wpb, wo, g1, b1, rwh, rwl, rbp, tm):
    T = x.shape[0]
    n_groups = len(oas) // 2 if len(oas) > 1 else 1

    def row(w):
        return pl.BlockSpec((tm, w), lambda i: (i, 0))

    def const(a):
        return pl.BlockSpec(a.shape, lambda i: (0,) * a.ndim, pipeline_mode=pl.Buffered(1))

    in_specs = ([row(D_MODEL)] + [row(A_WIDTH)] * len(oas) + [row(H_B * DV_B), row(D_MODEL),
                pl.BlockSpec((tm, D_MODEL), lambda i: (i, 1))] + [const(a) for a in (wpa, wpb, wo, g1, b1, rwh, rwl, rbp)])
    return pl.pallas_call(
        functools.partial(_merge_kernel, n_groups=n_groups),
        grid=(T // tm,),
        in_specs=in_specs,
        out_specs=[row(D_MODEL), row(D_MODEL // 2), row(TOP_K), row(TOP_K)],
        out_shape=[jax.ShapeDtypeStruct((T, D_MODEL), F32), jax.ShapeDtypeStruct((T, D_MODEL // 2), jnp.int32),
                   jax.ShapeDtypeStruct((T, TOP_K), jnp.int32), jax.ShapeDtypeStruct((T, TOP_K), F32)],
        compiler_params=_cparams(("parallel",)),
        name=f"merge_g{n_groups}",
    )(x, *oas, ob, pg, pg, wpa, wpb, wo, g1, b1, rwh, rwl, rbp)


def _sc_gather(table, idx):
    info = plsc.get_sparse_core_info()
    n_workers = info.num_cores * info.num_subcores
    n, width = idx.shape[0], table.shape[1]
    per_worker = n // n_workers
    chunk = SC_CHUNK_BYTES // (width * table.dtype.itemsize)
    assert per_worker * n_workers == n and per_worker % chunk == 0 and chunk % 8 == 0
    mesh = plsc.VectorSubcoreMesh(core_axis_name="c", subcore_axis_name="s")

    @functools.partial(
        pl.kernel, mesh=mesh,
        out_type=jax.ShapeDtypeStruct((n, width), table.dtype),
        scratch_types=[pltpu.VMEM((chunk,), jnp.int32), pltpu.VMEM((chunk, width), table.dtype),
                       pltpu.SemaphoreType.DMA],
    )
    def gather(table_hbm, idx_hbm, out_hbm, idx_v, rows_v, sem):
        wid = lax.axis_index("s") * info.num_cores + lax.axis_index("c")
        base = wid * per_worker

        @pl.loop(0, per_worker // chunk)
        def _(c):
            off = pl.multiple_of(base + c * chunk, chunk)
            pltpu.sync_copy(idx_hbm.at[pl.ds(off, chunk)], idx_v)
            pltpu.async_copy(table_hbm.at[idx_v], rows_v, sem).wait()
            pltpu.sync_copy(rows_v, out_hbm.at[pl.ds(off, chunk)])

    return gather(table, idx)


def _expert_kernel(be_ref, rows_ref, nu_ref, xs_ref, wg_ref, wu_ref, bg_ref, bu_ref, wd_ref, bd_ref, o_ref,
                   xb_ref, hid_ref):
    del be_ref, nu_ref
    i = pl.program_id(0)
    p = pl.program_id(1)
    nrows = rows_ref[i]
    half = D_MODEL // 2

    @pl.when(jnp.logical_and(p == 0, nrows > 0))
    def _():
        packed = xs_ref[...]
        lo = lax.bitcast_convert_type(lax.shift_left(packed, 16), F32)
        hi = lax.bitcast_convert_type(packed & jnp.int32(-65536), F32)
        xb_ref[:, :half] = lo.astype(BF16)
        xb_ref[:, half:] = hi.astype(BF16)

    for sb in range(MOE_BM // MOE_SB):
        r = slice(sb * MOE_SB, (sb + 1) * MOE_SB)
        live = sb * MOE_SB < nrows

        @pl.when(jnp.logical_and(p < MOE_NF, live))
        def _():
            x = xb_ref[r, :]
            g = jnp.minimum(_dot(x, wg_ref[...].astype(BF16)) + bg_ref[...], SWIGLU_LIMIT)
            u = jnp.clip(_dot(x, wu_ref[...].astype(BF16)) + bu_ref[...], -SWIGLU_LIMIT, SWIGLU_LIMIT)
            hid_ref[p, r, :] = ((u + 1.0) * g * jax.nn.sigmoid(SWIGLU_ALPHA * g)).astype(BF16)

        @pl.when(jnp.logical_and(p >= MOE_NF, live))
        def _():
            y = bd_ref[...]
            for f in range(MOE_NF):
                y = y + _dot(hid_ref[f, r, :], wd_ref[f * MOE_TF:(f + 1) * MOE_TF, :].astype(BF16))
            o_ref[r, :] = y

        @pl.when(jnp.logical_and(p >= MOE_NF, jnp.logical_not(live)))
        def _():
            o_ref[r, :] = jnp.zeros((MOE_SB, MOE_TN), F32)


def _experts(xs, block_expert, block_rows, n_used, w_gate_up, b_gate_up, w_down, b_down):
    nb = xs.shape[0] // MOE_BM
    nf, nn = MOE_NF, D_MODEL // MOE_TN

    def fsel(i, p, nu):
        return jnp.where(i < nu[0], jnp.minimum(p, nf - 1), nf - 1)

    def nsel(i, p, nu):
        return jnp.where(i < nu[0], jnp.maximum(p - nf, 0), nn - 1)

    def isel(i, nu):
        return jnp.minimum(i, nu[0] - 1)

    grid_spec = pltpu.PrefetchScalarGridSpec(
        num_scalar_prefetch=3,
        grid=(nb, nf + nn),
        in_specs=[
            pl.BlockSpec((MOE_BM, D_MODEL // 2), lambda i, p, be, rw, nu: (isel(i, nu), 0)),
            pl.BlockSpec((None, D_MODEL, MOE_TF), lambda i, p, be, rw, nu: (be[i], 0, fsel(i, p, nu))),
            pl.BlockSpec((None, D_MODEL, MOE_TF), lambda i, p, be, rw, nu: (be[i], 0, nf + fsel(i, p, nu))),
            pl.BlockSpec((None, 1, MOE_TF), lambda i, p, be, rw, nu: (be[i], 0, fsel(i, p, nu))),
            pl.BlockSpec((None, 1, MOE_TF), lambda i, p, be, rw, nu: (be[i], 0, nf + fsel(i, p, nu))),
            pl.BlockSpec((None, D_FF, MOE_TN), lambda i, p, be, rw, nu: (be[i], 0, nsel(i, p, nu))),
            pl.BlockSpec((None, 1, MOE_TN), lambda i, p, be, rw, nu: (be[i], 0, nsel(i, p, nu))),
        ],
        out_specs=pl.BlockSpec((MOE_BM, MOE_TN), lambda i, p, be, rw, nu: (i, jnp.maximum(p - nf, 0))),
        scratch_shapes=[pltpu.VMEM((MOE_BM, D_MODEL), BF16), pltpu.VMEM((nf, MOE_BM, MOE_TF), BF16)],
    )
    bgu = b_gate_up.reshape(N_EXPERTS, 1, 2 * D_FF)
    bd = b_down.reshape(N_EXPERTS, 1, D_MODEL)
    return pl.pallas_call(
        _expert_kernel,
        grid_spec=grid_spec,
        out_shape=jax.ShapeDtypeStruct((xs.shape[0], D_MODEL), F32),
        compiler_params=_cparams(("arbitrary", "arbitrary")),
        name="moe_experts",
    )(block_expert, block_rows, n_used, xs, w_gate_up, w_gate_up, bgu, bgu, w_down, bd)


def _route(top_i, n_blocks):
    T = top_i.shape[0]
    expert = top_i.reshape(T * TOP_K)
    onehot = (expert[:, None] == jnp.arange(N_EXPERTS, dtype=jnp.int32)[None, :]).astype(jnp.int32)
    csum = jnp.cumsum(onehot, axis=0)
    counts = csum[-1]
    rank = jnp.take_along_axis(csum, expert[:, None], axis=1)[:, 0] - 1
    bpe = (counts + MOE_BM - 1) // MOE_BM
    bend = jnp.cumsum(bpe)
    bstart = bend - bpe
    dest = (bstart[expert] * MOE_BM + rank).astype(jnp.int32)
    token = jnp.repeat(jnp.arange(T, dtype=jnp.int32), TOP_K)
    row_token = (jnp.arange(n_blocks * MOE_BM, dtype=jnp.int32) % T).at[dest].set(token)
    n_used = bend[-1]
    blk = jnp.arange(n_blocks, dtype=jnp.int32)
    be = jnp.minimum(jnp.searchsorted(bend, jnp.minimum(blk, n_used - 1), side="right"), N_EXPERTS - 1).astype(jnp.int32)
    rows = jnp.clip(counts[be] - (blk - bstart[be]) * MOE_BM, 0, MOE_BM)
    rows = jnp.where(blk < n_used, rows, 0).astype(jnp.int32)
    return dest, row_token, be, rows, n_used.reshape(1).astype(jnp.int32)


def _combine_kernel(x1_ref, ge_ref, gt_ref, g2_ref, b2_ref, o_ref):
    gt = gt_ref[...]
    m = gt[:, 0:1] * ge_ref[0]
    for k in range(1, TOP_K):
        m = m + gt[:, k:k + 1] * ge_ref[k]
    o_ref[...] = _layer_norm(DEEPNORM_ALPHA * x1_ref[...] + m, g2_ref[...], b2_ref[...])


def _combine(x1, ge, gate, g2, b2, row0, tm):
    n_rows = x1.shape[0]
    b0 = row0 // tm
    return pl.pallas_call(
        _combine_kernel,
        grid=(n_rows // tm,),
        in_specs=[pl.BlockSpec((tm, D_MODEL), lambda i: (i, 0)),
                  pl.BlockSpec((TOP_K, tm, D_MODEL), lambda i: (0, b0 + i, 0)),
                  pl.BlockSpec((tm, TOP_K), lambda i: (b0 + i, 0)),
                  pl.BlockSpec(g2.shape, lambda i: (0, 0)), pl.BlockSpec(b2.shape, lambda i: (0, 0))],
        out_specs=pl.BlockSpec((tm, D_MODEL), lambda i: (i, 0)),
        out_shape=jax.ShapeDtypeStruct((n_rows, D_MODEL), F32),
        compiler_params=_cparams(("parallel",)),
        name="moe_combine",
    )(x1, ge, gate, g2, b2)


def _layer(xp, xs, caches, state, w_in, rel_bias, gla_w_up, gla_b, gla_norm_g, w_pa, w_pb, w_o, ln1_g, ln1_b,
           router_w, router_b, w_gate_up, b_gate_up, w_down, b_down, ln2_g, ln2_b):
    batch, seq, _ = xp.shape
    dbatch, dseq, _ = xs.shape
    tp, ts = batch * seq, dbatch * dseq
    xp2, xs2 = xp.reshape(tp, D_MODEL), xs.reshape(ts, D_MODEL)

    o_b0, o_lr, o_g = PA_W, PA_W + PB_LR, PA_W + PB_LR + GLA_RANK
    w_a = w_in[:, :PA_W].astype(BF16)
    w_b = jnp.concatenate([w_in[:, o_b0:o_g], jnp.zeros((D_MODEL, LANE - GLA_RANK), F32)], axis=1).astype(BF16)
    w_g = w_in[:, o_g:].astype(BF16)
    wup = jnp.concatenate([gla_w_up, jnp.zeros((LANE - GLA_RANK, H_B * DK_B), F32)], axis=0).astype(BF16)
    gb = gla_b.reshape(1, H_B * DK_B)
    ng = gla_norm_g.reshape(1, DV_B)
    wpa, wpb, wo = w_pa.astype(BF16), w_pb.astype(BF16), w_o.astype(BF16)
    g1, b1 = ln1_g.reshape(1, D_MODEL), ln1_b.reshape(1, D_MODEL)
    g2, b2 = ln2_g.reshape(1, D_MODEL), ln2_b.reshape(1, D_MODEL)
    rw = jnp.concatenate([router_w, jnp.zeros((D_MODEL, LANE - N_EXPERTS), F32)], axis=1)
    rwh = rw.astype(BF16)
    rwl = (rw - rwh.astype(F32)).astype(BF16)
    rbp = jnp.concatenate([router_b, jnp.full((LANE - N_EXPERTS,), NEG, F32)]).reshape(1, LANE)
    biases = _group_biases(rel_bias)

    pa_p = _project(xp2, w_a, 512, A_QKV_WIDTH, "proj_a_prompt")
    pb_p = _project(xp2, w_b, 512, PB_W, "proj_b_prompt")
    pg_p = _project(xp2, w_g, 512, 1024, "proj_g_prompt")
    oas, lses = [], []
    for g in range(N_GROUPS):
        o, lse = _attn_prompt_group(pa_p, _prompt_table(biases[g]), g, batch, seq)
        oas.append(o)
        lses.append(lse)
    ob_p, st_p = _gla(pb_p, wup, gb, ng, jnp.zeros((batch, H_B, DV_B, DK_B), F32), batch, seq, GLA_CHUNK, 256)
    x1_p, x1p_p, ti_p, gt_p = _merge(xp2, oas + lses, ob_p, pg_p, wpa, wpb, wo, g1, b1, rwh, rwl, rbp, 256)

    pa_s = _project(xs2, w_a, ts, A_QKV_WIDTH, "proj_a_sample")
    pb_s = _project(xs2, w_b, ts, PB_W, "proj_b_sample")
    pg_s = _project(xs2, w_g, ts, 1024, "proj_g_sample")
    tabc, tabn, combo_base = _sample_tables(biases, dseq)
    oa_s = _attn_sample(pa_s, caches, tabc, tabn, combo_base, dbatch, dseq)
    chunk_s = int(np.gcd(dseq, GLA_CHUNK))
    ob_s, st_s = _gla(pb_s, wup, gb, ng, jnp.swapaxes(state, -1, -2), dbatch, dseq, chunk_s, dseq)
    x1_s, x1p_s, ti_s, gt_s = _merge(xs2, [oa_s], ob_s, pg_s, wpa, wpb, wo, g1, b1, rwh, rwl, rbp, ts)

    x1p = jnp.concatenate([x1p_p, x1p_s], axis=0)
    top_i = jnp.concatenate([ti_p, ti_s], axis=0)
    gate = jnp.concatenate([gt_p, gt_s], axis=0)
    t_all = tp + ts
    n_blocks = -(-(t_all * TOP_K) // MOE_BM) + N_EXPERTS
    dest, row_token, be, rows, n_used = _route(top_i, n_blocks)
    xsorted = _sc_gather(x1p, row_token)
    eo = _experts(xsorted, be, rows, n_used, w_gate_up, b_gate_up, w_down, b_down)
    dest_kmajor = dest.reshape(t_all, TOP_K).T.reshape(TOP_K * t_all)
    ge = _sc_gather(eo, dest_kmajor).reshape(TOP_K, t_all, D_MODEL)
    y_p = _combine(x1_p, ge, gate, g2, b2, 0, 256)
    y_s = _combine(x1_s, ge, gate, g2, b2, tp, 256)

    kp = pa_p[:, A_QKV_WIDTH:2 * A_QKV_WIDTH].reshape(batch, seq, N_GROUPS, H_A, HEAD_DIM)
    vp = pa_p[:, 2 * A_QKV_WIDTH:].reshape(batch, seq, N_GROUPS, H_A, HEAD_DIM)
    ks = pa_s[:, A_QKV_WIDTH:2 * A_QKV_WIDTH].reshape(dbatch, dseq, N_GROUPS, H_A, HEAD_DIM)
    vs = pa_s[:, 2 * A_QKV_WIDTH:].reshape(dbatch, dseq, N_GROUPS, H_A, HEAD_DIM)
    bufs_p, bufs_s = [], []
    for g, (window, _) in enumerate(DILATED_GROUPS):
        keep = min(window, seq)
        bufs_p.append(jnp.stack([kp[:, seq - keep:, g], vp[:, seq - keep:, g]], axis=2))
        ctx = jnp.concatenate([caches[g], jnp.stack([ks[:, :, g], vs[:, :, g]], axis=2)], axis=1)
        keep = min(window, ctx.shape[1])
        bufs_s.append(ctx[:, ctx.shape[1] - keep:])
    return (y_p.reshape(batch, seq, D_MODEL), y_s.reshape(dbatch, dseq, D_MODEL), bufs_p, jnp.swapaxes(st_p, -1, -2),
            bufs_s, jnp.swapaxes(st_s, -1, -2))


def kernel(x_prompt, x_sample, cache_a1_kv, cache_a2_kv, cache_a3_kv, state_b_s, w_in, rel_bias, gla_w_up, gla_b,
           gla_norm_g, w_pa, w_pb, w_o, ln1_g, ln1_b, router_w, router_b, w_gate_up, b_gate_up, w_down, b_down,
           ln2_g, ln2_b):
    assert w_in.shape[0] == DEPTH
    yp, ys, bufs_p, st_p, bufs_s, st_s = _layer(
        x_prompt, x_sample, (cache_a1_kv[0], cache_a2_kv[0], cache_a3_kv[0]), state_b_s[0], w_in[0], rel_bias,
        gla_w_up[0], gla_b[0], gla_norm_g[0], w_pa[0], w_pb[0], w_o[0], ln1_g[0], ln1_b[0], router_w[0], router_b[0],
        w_gate_up[0], b_gate_up[0], w_down[0], b_down[0], ln2_g[0], ln2_b[0])
    return (yp, ys, bufs_p[0][None], bufs_p[1][None], bufs_p[2][None], st_p[None],
            bufs_s[0][None], bufs_s[1][None], bufs_s[2][None], st_s[None].astype(state_b_s.dtype))
```

```python
import functools

import numpy as np
import jax
import jax.numpy as jnp
from jax import lax
from jax.experimental import pallas as pl
from jax.experimental.pallas import tpu as pltpu
from jax.experimental.pallas import tpu_sc as plsc

F32 = jnp.float32
BF16 = jnp.bfloat16

D_MODEL = 2048
HEAD_DIM = 128
DILATED_GROUPS = ((128, 1), (512, 4), (2048, 16))
N_GROUPS = 3
H_A = 4
A_WIDTH = H_A * HEAD_DIM
A_QKV_WIDTH = N_GROUPS * A_WIDTH
Q_BLOCK = 128
N_BUCKETS = 32
REL_MAX_DIST = 2048
H_B = 4
DK_B = 64
DV_B = 128
GLA_RANK = 16
GLA_TAU = 16.0
GLA_CHUNK = 64
GLA_SUB = 16
GLA_EXP_CLAMP = 80.0
N_EXPERTS = 32
TOP_K = 4
D_FF = 2048
SWIGLU_LIMIT = 7.0
SWIGLU_ALPHA = 1.702
LN_EPS = 1e-5
RMS_EPS = 1e-6
DEPTH = 1
DEEPNORM_ALPHA = (2.0 * DEPTH) ** 0.25
ATT_SCALE = HEAD_DIM ** -0.5
NEG = float(np.finfo(np.float32).min)

VMEM_LIMIT = 56 * 1024 * 1024
LANE = 128

PA_W = 3 * A_QKV_WIDTH
PB_LR = H_B * DK_B * 2 + H_B * DV_B * 2
PB_W = PB_LR + LANE
PG_W = 2 * D_MODEL

MOE_BM = 1024
MOE_SB = 512
MOE_TF = 512
MOE_NF = D_FF // MOE_TF
MOE_TN = 512
SC_CHUNK_BYTES = 128 * 1024


def _cparams(sem):
    return pltpu.CompilerParams(dimension_semantics=sem, vmem_limit_bytes=VMEM_LIMIT)


def _dot(a, b):
    return jnp.dot(a, b, preferred_element_type=F32)


def _dot_nt(a, b):
    return lax.dot_general(a, b, (((1,), (1,)), ((), ())), preferred_element_type=F32)


def _dot_tn(a, b):
    return lax.dot_general(a, b, (((0,), (0,)), ((), ())), preferred_element_type=F32)


def _proj_kernel(x_ref, w_ref, o_ref, xb_ref):
    @pl.when(pl.program_id(1) == 0)
    def _():
        xb_ref[...] = x_ref[...].astype(BF16)

    o_ref[...] = _dot(xb_ref[...], w_ref[...])


def _project(x, w, tm, tn, name):
    T, D = x.shape
    N = w.shape[1]
    return pl.pallas_call(
        _proj_kernel,
        grid=(T // tm, N // tn),
        in_specs=[pl.BlockSpec((tm, D), lambda i, j: (i, 0)), pl.BlockSpec((D, tn), lambda i, j: (0, j))],
        out_specs=pl.BlockSpec((tm, tn), lambda i, j: (i, j)),
        out_shape=jax.ShapeDtypeStruct((T, N), F32),
        scratch_shapes=[pltpu.VMEM((tm, D), BF16)],
        compiler_params=_cparams(("parallel", "arbitrary")),
        name=name,
    )(x, w)


def _t5_bucket(dist):
    max_exact = N_BUCKETS // 2
    d = np.maximum(dist, 1).astype(np.float32)
    large = max_exact + (np.log(d / max_exact) / np.log(REL_MAX_DIST / max_exact) * (N_BUCKETS - max_exact)).astype(np.int32)
    large = np.minimum(large, N_BUCKETS - 1)
    return np.where(dist < max_exact, dist, large).astype(np.int32)


def _bias_lookup(rel_bias, g, j, valid):
    _, dil = DILATED_GROUPS[g]
    bucket = _t5_bucket(dil * np.clip(j, 0, Q_BLOCK))
    onehot = bucket[None] == np.arange(N_BUCKETS).reshape((N_BUCKETS,) + (1,) * j.ndim)
    rb = rel_bias[:, g * H_A:(g + 1) * H_A].astype(F32).T.reshape((H_A, N_BUCKETS) + (1,) * j.ndim)
    vals = jnp.sum(jnp.where(onehot[None], rb, 0.0), axis=1)
    return jnp.where(valid[None], vals, NEG)


def _prompt_table(rel_bias, g):
    qi = np.arange(Q_BLOCK)[:, None]
    kj = np.arange(2 * Q_BLOCK)[None, :]
    j = Q_BLOCK + qi - kj
    return _bias_lookup(rel_bias, g, j, (j >= 0) & (j <= Q_BLOCK))


def _sample_tables(rel_bias, dec_seq):
    m = np.arange(Q_BLOCK)
    tabc, combo_base = [], []
    for g, (_, dil) in enumerate(DILATED_GROUPS):
        combo_base.append(len(tabc))
        for fl in range((dec_seq - 1) // dil + 1):
            j = Q_BLOCK + fl - m
            col = _bias_lookup(rel_bias, g, j, j <= Q_BLOCK).T
            col = jnp.concatenate([col, jnp.zeros_like(col)], axis=1)
            tabc.append(jnp.broadcast_to(col[:, :, None], (Q_BLOCK, 2 * H_A, LANE)))
    tabn = []
    s = np.arange(dec_seq)[:, None]
    sp = np.arange(dec_seq)[None, :]
    for g, (_, dil) in enumerate(DILATED_GROUPS):
        diff = s - sp
        t = _bias_lookup(rel_bias, g, diff // dil, (diff >= 0) & (diff % dil == 0))
        t = jnp.transpose(t, (1, 2, 0))
        t = jnp.concatenate([t, jnp.zeros_like(t)], axis=2)
        tabn.append(jnp.broadcast_to(t[..., None], (dec_seq, dec_seq, 2 * H_A, LANE)))
    return jnp.stack(tabc), jnp.stack(tabn), tuple(combo_base)


def _attn_prompt_kernel(q_ref, kc_ref, kp_ref, vc_ref, vp_ref, tab_ref, o_ref, lse_ref):
    has_prev = pl.program_id(2) > 0
    for h in range(H_A):
        sl = slice(h * HEAD_DIM, (h + 1) * HEAD_DIM)
        q = q_ref[:, sl].astype(BF16)
        sc = _dot_nt(q, kc_ref[:, sl].astype(BF16)) * ATT_SCALE + tab_ref[h, :, Q_BLOCK:]
        sp = _dot_nt(q, kp_ref[:, sl].astype(BF16)) * ATT_SCALE + tab_ref[h, :, :Q_BLOCK]
        sp = jnp.where(has_prev, sp, NEG)
        m = jnp.maximum(jnp.max(sc, axis=-1, keepdims=True), jnp.max(sp, axis=-1, keepdims=True))
        pc = jnp.exp(sc - m)
        pp = jnp.exp(sp - m)
        l = jnp.sum(pc, axis=-1, keepdims=True) + jnp.sum(pp, axis=-1, keepdims=True)
        inv = 1.0 / l
        o = _dot((pc * inv).astype(BF16), vc_ref[:, sl].astype(BF16)) + _dot((pp * inv).astype(BF16), vp_ref[:, sl].astype(BF16))
        o_ref[:, sl] = o
        lse_ref[:, sl] = jnp.broadcast_to(m + jnp.log(l), (Q_BLOCK, HEAD_DIM))


def _attn_prompt_strided_kernel(*refs, dil, with_prev):
    if with_prev:
        q_ref, kc_ref, kp_ref, vc_ref, vp_ref, tab_ref, o_ref, lse_ref = refs
    else:
        q_ref, kc_ref, vc_ref, tab_ref, o_ref, lse_ref = refs
    has_prev = pl.program_id(1) > 0

    def body(r, carry):
        idx = pl.ds(r, Q_BLOCK, stride=dil)
        q = q_ref[idx, :].astype(BF16)
        sc = _dot_nt(q, kc_ref[idx, :].astype(BF16)) * ATT_SCALE + tab_ref[:, Q_BLOCK:]
        m = jnp.max(sc, axis=-1, keepdims=True)
        if with_prev:
            sp = _dot_nt(q, kp_ref[idx, :].astype(BF16)) * ATT_SCALE + tab_ref[:, :Q_BLOCK]
            sp = jnp.where(has_prev, sp, NEG)
            m = jnp.maximum(m, jnp.max(sp, axis=-1, keepdims=True))
        pc = jnp.exp(sc - m)
        l = jnp.sum(pc, axis=-1, keepdims=True)
        if with_prev:
            pp = jnp.exp(sp - m)
            l = l + jnp.sum(pp, axis=-1, keepdims=True)
        inv = 1.0 / l
        o = _dot((pc * inv).astype(BF16), vc_ref[idx, :].astype(BF16))
        if with_prev:
            o = o + _dot((pp * inv).astype(BF16), vp_ref[idx, :].astype(BF16))
        o_ref[idx, :] = o
        lse_ref[idx, :] = jnp.broadcast_to(m + jnp.log(l), (Q_BLOCK, HEAD_DIM))
        return carry

    lax.fori_loop(0, dil, body, 0, unroll=min(dil, 4))


def _attn_prompt_strided(pa, table, g, batch, seq):
    _, dil = DILATED_GROUPS[g]
    rows = dil * Q_BLOCK
    nblk = seq // rows
    with_prev = nblk > 1
    hcols = A_QKV_WIDTH // HEAD_DIM

    def spec(sec, prev):
        if prev:
            return pl.BlockSpec((rows, HEAD_DIM), lambda b, i, h: (b * nblk + jnp.maximum(i - 1, 0), sec * hcols + g * H_A + h))
        return pl.BlockSpec((rows, HEAD_DIM), lambda b, i, h: (b * nblk + i, sec * hcols + g * H_A + h))

    in_specs = [spec(0, False), spec(1, False)] + ([spec(1, True)] if with_prev else []) + [spec(2, False)] + (
        [spec(2, True)] if with_prev else []) + [pl.BlockSpec((None, Q_BLOCK, 2 * Q_BLOCK), lambda b, i, h: (h, 0, 0))]
    out_spec = pl.BlockSpec((rows, HEAD_DIM), lambda b, i, h: (b * nblk + i, h))
    return pl.pallas_call(
        functools.partial(_attn_prompt_strided_kernel, dil=dil, with_prev=with_prev),
        grid=(batch, nblk, H_A),
        in_specs=in_specs,
        out_specs=[out_spec, out_spec],
        out_shape=[jax.ShapeDtypeStruct((batch * seq, A_WIDTH), F32)] * 2,
        compiler_params=_cparams(("parallel", "arbitrary", "arbitrary")),
        name=f"attn_prompt_g{g}",
    )(*([pa] * (len(in_specs) - 1)), table)


def _attn_prompt_group(pa, table, g, batch, seq):
    _, dil = DILATED_GROUPS[g]
    if dil > 1:
        return _attn_prompt_strided(pa, table, g, batch, seq)
    sub = seq // dil
    nqb = sub // Q_BLOCK
    wblk = PA_W // A_WIDTH
    pv = pa.reshape(batch, sub, dil * PA_W)

    def spec(off, prev):
        if prev:
            return pl.BlockSpec((None, Q_BLOCK, A_WIDTH), lambda b, r, i: (b, jnp.maximum(i - 1, 0), r * wblk + off + g))
        return pl.BlockSpec((None, Q_BLOCK, A_WIDTH), lambda b, r, i: (b, i, r * wblk + off + g))

    out_spec = pl.BlockSpec((None, Q_BLOCK, A_WIDTH), lambda b, r, i: (b, i, r))
    o, lse = pl.pallas_call(
        _attn_prompt_kernel,
        grid=(batch, dil, nqb),
        in_specs=[spec(0, False), spec(N_GROUPS, False), spec(N_GROUPS, True), spec(2 * N_GROUPS, False),
                  spec(2 * N_GROUPS, True), pl.BlockSpec((H_A, Q_BLOCK, 2 * Q_BLOCK), lambda b, r, i: (0, 0, 0))],
        out_specs=[out_spec, out_spec],
        out_shape=[jax.ShapeDtypeStruct((batch, sub, dil * A_WIDTH), F32)] * 2,
        compiler_params=_cparams(("parallel", "parallel", "arbitrary")),
        name=f"attn_prompt_g{g}",
    )(pv, pv, pv, pv, pv, table)
    return o.reshape(batch * seq, A_WIDTH), lse.reshape(batch * seq, A_WIDTH)


KV_ROWS = 2 * H_A


def _kv_pack_kernel(k_ref, v_ref, o_ref):
    parts = [k_ref[:, h * HEAD_DIM:(h + 1) * HEAD_DIM] for h in range(H_A)]
    parts += [v_ref[:, h * HEAD_DIM:(h + 1) * HEAD_DIM] for h in range(H_A)]
    o_ref[...] = jnp.stack(parts, axis=1)


def _kv_pack(pa, g, batch, seq, keep, tm):
    nblk, blk0, per_b = keep // tm, (seq - keep) // tm, seq // tm
    wblk = A_QKV_WIDTH // A_WIDTH
    out = pl.pallas_call(
        _kv_pack_kernel,
        grid=(batch, nblk),
        in_specs=[pl.BlockSpec((tm, A_WIDTH), lambda b, i: (b * per_b + blk0 + i, wblk + g)),
                  pl.BlockSpec((tm, A_WIDTH), lambda b, i: (b * per_b + blk0 + i, 2 * wblk + g))],
        out_specs=pl.BlockSpec((tm, KV_ROWS, HEAD_DIM), lambda b, i: (b * nblk + i, 0, 0)),
        out_shape=jax.ShapeDtypeStruct((batch * keep, KV_ROWS, HEAD_DIM), F32),
        compiler_params=_cparams(("parallel", "parallel")),
        name=f"kv_pack_g{g}_{keep}",
    )(pa, pa)
    return out.reshape(batch, keep, KV_ROWS, HEAD_DIM)


def _attn_sample_kernel(qkv_ref, n1_ref, n2_ref, n3_ref, c1_ref, c2_ref, c3_ref, tabc_ref, tabn_ref, o_ref, *,
                        dec_seq, combo_base):
    caches = (c1_ref, c2_ref, c3_ref)
    news = (n1_ref, n2_ref, n3_ref)
    zeros = jnp.zeros((H_A, HEAD_DIM), F32)
    for s in range(dec_seq):
        outs, lses = [], []
        for g, (_, dil) in enumerate(DILATED_GROUPS):
            rho, fl = s % dil, s // dil
            qm = jnp.concatenate([qkv_ref[s:s + 1, g * A_WIDTH + h * HEAD_DIM:g * A_WIDTH + (h + 1) * HEAD_DIM]
                                  for h in range(H_A)] + [zeros], axis=0)
            kc = caches[g][:, rho]
            kn = news[g][...]
            sc = jnp.sum(kc * qm[None], axis=-1, keepdims=True) * ATT_SCALE + tabc_ref[combo_base[g] + fl]
            sn = jnp.sum(kn * qm[None], axis=-1, keepdims=True) * ATT_SCALE + tabn_ref[g, s]
            m = jnp.maximum(jnp.max(sc, axis=0), jnp.max(sn, axis=0))
            pc = jnp.exp(sc - m[None])
            pn = jnp.exp(sn - m[None])
            l = jnp.sum(pc, axis=0) + jnp.sum(pn, axis=0)
            acc = jnp.sum(pltpu.roll(pc, H_A, 1) * kc, axis=0) + jnp.sum(pltpu.roll(pn, H_A, 1) * kn, axis=0)
            outs.append(acc / pltpu.roll(l, H_A, 0))
            lses.append(pltpu.roll(m + jnp.log(l), H_A, 0))
        mm = jnp.maximum(jnp.maximum(lses[0], lses[1]), lses[2])
        ws = [jnp.exp(x - mm) for x in lses]
        o_ref[s] = (ws[0] * outs[0] + ws[1] * outs[1] + ws[2] * outs[2]) / (ws[0] + ws[1] + ws[2])


def _attn_sample(pa, new_rows, caches, tabc, tabn, combo_base, batch, dec_seq):
    views, specs = [], []
    for g, (window, dil) in enumerate(DILATED_GROUPS):
        assert caches[g].shape[1] == window and dec_seq <= Q_BLOCK
        views.append(caches[g].reshape(batch, Q_BLOCK, dil, KV_ROWS, HEAD_DIM))
        used = min(dil, dec_seq)
        specs.append(pl.BlockSpec((None, Q_BLOCK, used, KV_ROWS, HEAD_DIM), lambda b: (b, 0, 0, 0, 0)))
    qkv = pa.reshape(batch, dec_seq, PA_W)
    new_spec = pl.BlockSpec((None, dec_seq, KV_ROWS, HEAD_DIM), lambda b: (b, 0, 0, 0))
    out = pl.pallas_call(
        functools.partial(_attn_sample_kernel, dec_seq=dec_seq, combo_base=combo_base),
        grid=(batch,),
        in_specs=[pl.BlockSpec((None, dec_seq, PA_W), lambda b: (b, 0, 0))] + [new_spec] * N_GROUPS + specs + [
            pl.BlockSpec(tabc.shape, lambda b: (0, 0, 0, 0)), pl.BlockSpec(tabn.shape, lambda b: (0, 0, 0, 0, 0))],
        out_specs=pl.BlockSpec((None, dec_seq, KV_ROWS, HEAD_DIM), lambda b: (b, 0, 0, 0)),
        out_shape=jax.ShapeDtypeStruct((batch, dec_seq, KV_ROWS, HEAD_DIM), F32),
        compiler_params=_cparams(("parallel",)),
        name="attn_sample",
    )(qkv, *new_rows, *views, tabc, tabn)
    return out[:, :, H_A:, :].reshape(batch * dec_seq, A_WIDTH)


def _sc_cache_shift(caches, drop):
    info = plsc.get_sparse_core_info()
    n_workers = info.num_cores * info.num_subcores
    batch = caches[0].shape[0]
    assert batch % n_workers == 0
    mesh = plsc.VectorSubcoreMesh(core_axis_name="c", subcore_axis_name="s")

    @functools.partial(pl.kernel, mesh=mesh, out_type=[jax.ShapeDtypeStruct(c.shape, c.dtype) for c in caches])
    def shift(*refs):
        srcs, dsts = refs[:len(caches)], refs[len(caches):]
        wid = lax.axis_index("s") * info.num_cores + lax.axis_index("c")
        for j in range(batch // n_workers):
            b = wid * (batch // n_workers) + j
            for src, dst in zip(srcs, dsts):
                keep = src.shape[1] - drop
                pltpu.sync_copy(src.at[b, pl.ds(drop, keep)], dst.at[b, pl.ds(0, keep)])

    return shift(*caches)


def _split3(x):
    hi = x.astype(BF16)
    r = x - hi.astype(F32)
    mid = r.astype(BF16)
    lo = (r - mid.astype(F32)).astype(BF16)
    return hi, mid, lo


def _gla_kernel(p_ref, wup_ref, gb_ref, ng_ref, s0_ref, o_ref, st_ref, *, chunk, tb):
    @pl.when(pl.program_id(1) == 0)
    def _():
        st_ref[...] = s0_ref[...]

    sub = min(GLA_SUB, chunk)
    kq = H_B * DK_B

    def rb(x):
        xb = x.astype(BF16)
        return xb if chunk >= 16 else xb.astype(F32)

    row = lax.broadcasted_iota(jnp.int32, (chunk, chunk), 0)
    colm = lax.broadcasted_iota(jnp.int32, (chunk, chunk), 1)
    tri = rb(jnp.where(row >= colm, 1.0, 0.0))
    for c in range(tb // chunk):
        rows = slice(c * chunk, (c + 1) * chunk)
        z = _dot(rb(p_ref[rows, PB_LR:PB_W]), rb(wup_ref[...])) + gb_ref[...]
        la = -(jnp.maximum(-z, 0.0) + jnp.log1p(jnp.exp(-jnp.abs(z)))) * (1.0 / GLA_TAU)
        b = functools.reduce(lambda u, w: u + w, [_dot(tri, rb(t)) for t in _split3(la)])
        blast = b[chunk - 1:chunk, :]
        q = p_ref[rows, 0:kq] * (DK_B ** -0.5)
        k = p_ref[rows, kq:2 * kq]
        qin = rb(q * jnp.exp(b))
        kst = rb(k * jnp.exp(blast - b))
        for h in range(H_B):
            ks = slice(h * DK_B, (h + 1) * DK_B)
            vs = slice(h * DV_B, (h + 1) * DV_B)
            st = st_ref[h]
            vb = rb(p_ref[rows, 2 * kq + h * DV_B:2 * kq + (h + 1) * DV_B])
            o_inter = _dot_nt(qin[:, ks], rb(st))
            parts = []
            for blk in range(chunk // sub):
                r0 = blk * sub
                n = r0 + sub
                ref_b = b[r0 - 1:r0, ks] if blk > 0 else jnp.zeros((1, DK_B), F32)
                qi = rb(q[r0:n, ks] * jnp.exp(b[r0:n, ks] - ref_b))
                ki = rb(k[0:n, ks] * jnp.exp(jnp.minimum(ref_b - b[0:n, ks], GLA_EXP_CLAMP)))
                a = _dot_nt(qi, ki)
                ti = lax.broadcasted_iota(jnp.int32, (sub, n), 0) + r0
                si = lax.broadcasted_iota(jnp.int32, (sub, n), 1)
                a = jnp.where(si <= ti, a, 0.0)
                parts.append(_dot(rb(a), vb[0:n]))
            o = o_inter + (jnp.concatenate(parts, axis=0) if len(parts) > 1 else parts[0])
            st_ref[h] = st * jnp.exp(blast[:, ks]) + _dot_tn(vb, kst[:, ks])
            on = o * lax.rsqrt(jnp.mean(o * o, axis=-1, keepdims=True) + RMS_EPS) * ng_ref[...]
            rg = p_ref[rows, 2 * kq + H_B * DV_B + h * DV_B:2 * kq + H_B * DV_B + (h + 1) * DV_B]
            o_ref[rows, vs] = (on * (rg * jax.nn.sigmoid(rg))).astype(BF16)


def _gla(pb, wup, gb, ng, s0t, batch, seq, chunk, tb):
    p3 = pb.reshape(batch, seq, PB_W)
    o, st = pl.pallas_call(
        functools.partial(_gla_kernel, chunk=chunk, tb=tb),
        grid=(batch, seq // tb),
        in_specs=[pl.BlockSpec((None, tb, PB_W), lambda b, i: (b, i, 0)),
                  pl.BlockSpec(wup.shape, lambda b, i: (0, 0)),
                  pl.BlockSpec(gb.shape, lambda b, i: (0, 0)),
                  pl.BlockSpec(ng.shape, lambda b, i: (0, 0)),
                  pl.BlockSpec((None, H_B, DV_B, DK_B), lambda b, i: (b, 0, 0, 0))],
        out_specs=[pl.BlockSpec((None, tb, H_B * DV_B), lambda b, i: (b, i, 0)),
                   pl.BlockSpec((None, H_B, DV_B, DK_B), lambda b, i: (b, 0, 0, 0))],
        out_shape=[jax.ShapeDtypeStruct((batch, seq, H_B * DV_B), BF16),
                   jax.ShapeDtypeStruct((batch, H_B, DV_B, DK_B), F32)],
        compiler_params=_cparams(("parallel", "arbitrary")),
        name=f"gla_c{chunk}",
    )(p3, wup, gb, ng, s0t)
    return o.reshape(batch * seq, H_B * DV_B), st


def _layer_norm(u, g, b):
    mu = jnp.mean(u, axis=-1, keepdims=True)
    d = u - mu
    var = jnp.mean(d * d, axis=-1, keepdims=True)
    return d * lax.rsqrt(var + LN_EPS) * g + b


def _merge_kernel(*refs, n_groups):
    x_ref = refs[0]
    oa_refs = refs[1:1 + 2 * n_groups] if n_groups > 1 else refs[1:2]
    rest = refs[1 + (2 * n_groups if n_groups > 1 else 1):]
    (ob_ref, pg_a_ref, pg_b_ref, wpa_ref, wpb_ref, wo_ref, g1_ref, b1_ref, rwh_ref, rwl_ref, rb_ref,
     x1_ref, x1p_ref, ti_ref, gt_ref) = rest
    if n_groups > 1:
        os_ = [r[...] for r in oa_refs[:n_groups]]
        ls = [r[...] for r in oa_refs[n_groups:]]
        mm = functools.reduce(jnp.maximum, ls)
        ws = [jnp.exp(x - mm) for x in ls]
        oa = sum(w * o for w, o in zip(ws, os_)) / sum(ws)
    else:
        oa = oa_refs[0][...]
    ya = _dot(oa.astype(BF16), wpa_ref[...])
    yb = _dot(ob_ref[...], wpb_ref[...])
    branch = jax.nn.sigmoid(pg_a_ref[...]) * ya + jax.nn.sigmoid(pg_b_ref[...]) * yb
    y = _dot(branch.astype(BF16), wo_ref[...])
    x1 = _layer_norm(DEEPNORM_ALPHA * x_ref[...] + y, g1_ref[...], b1_ref[...])
    x1_ref[...] = x1
    xh = x1.astype(BF16)
    xhf = xh.astype(F32)
    bits = lax.bitcast_convert_type(xhf, jnp.int32)
    half = D_MODEL // 2
    x1p_ref[...] = lax.shift_right_logical(bits[:, :half], 16) | bits[:, half:]
    xl = (x1 - xhf).astype(BF16)
    logits = _dot(xh, rwh_ref[...]) + _dot(xl, rwh_ref[...]) + _dot(xh, rwl_ref[...]) + rb_ref[...]
    lane = lax.broadcasted_iota(jnp.int32, logits.shape, 1)
    vals = logits
    top_v, top_i = [], []
    for _ in range(TOP_K):
        m = jnp.max(vals, axis=-1, keepdims=True)
        ik = jnp.min(jnp.where(vals == m, lane, LANE), axis=-1, keepdims=True)
        vals = jnp.where(lane == ik, -jnp.inf, vals)
        top_v.append(m)
        top_i.append(ik)
    es = [jnp.exp(v - top_v[0]) for v in top_v]
    tot = functools.reduce(lambda a, b: a + b, es)
    ti_ref[...] = jnp.concatenate(top_i, axis=1)
    gt_ref[...] = jnp.concatenate([e / tot for e in es], axis=1)


def _merge(x, oas, ob, pg, wpa, wpb, wo, g1, b1, rwh, rwl, rbp, tm):
    T = x.shape[0]
    n_groups = len(oas) // 2 if len(oas) > 1 else 1

    def row(w):
        return pl.BlockSpec((tm, w), lambda i: (i, 0))

    def const(a):
        return pl.BlockSpec(a.shape, lambda i: (0,) * a.ndim, pipeline_mode=pl.Buffered(1))

    in_specs = ([row(D_MODEL)] + [row(A_WIDTH)] * len(oas) + [row(H_B * DV_B), row(D_MODEL),
                pl.BlockSpec((tm, D_MODEL), lambda i: (i, 1))] + [const(a) for a in (wpa, wpb, wo, g1, b1, rwh, rwl, rbp)])
    return pl.pallas_call(
        functools.partial(_merge_kernel, n_groups=n_groups),
        grid=(T // tm,),
        in_specs=in_specs,
        out_specs=[row(D_MODEL), row(D_MODEL // 2), row(TOP_K), row(TOP_K)],
        out_shape=[jax.ShapeDtypeStruct((T, D_MODEL), F32), jax.ShapeDtypeStruct((T, D_MODEL // 2), jnp.int32),
                   jax.ShapeDtypeStruct((T, TOP_K), jnp.int32), jax.ShapeDtypeStruct((T, TOP_K), F32)],
        compiler_params=_cparams(("parallel",)),
        name=f"merge_g{n_groups}",
    )(x, *oas, ob, pg, pg, wpa, wpb, wo, g1, b1, rwh, rwl, rbp)


def _sc_gather(table, idx):
    info = plsc.get_sparse_core_info()
    n_workers = info.num_cores * info.num_subcores
    n, width = idx.shape[0], table.shape[1]
    per_worker = n // n_workers
    chunk = SC_CHUNK_BYTES // (width * table.dtype.itemsize)
    assert per_worker * n_workers == n and per_worker % chunk == 0 and chunk % 8 == 0
    mesh = plsc.VectorSubcoreMesh(core_axis_name="c", subcore_axis_name="s")

    @functools.partial(
        pl.kernel, mesh=mesh,
        out_type=jax.ShapeDtypeStruct((n, width), table.dtype),
        scratch_types=[pltpu.VMEM((chunk,), jnp.int32), pltpu.VMEM((chunk, width), table.dtype),
                       pltpu.SemaphoreType.DMA],
    )
    def gather(table_hbm, idx_hbm, out_hbm, idx_v, rows_v, sem):
        wid = lax.axis_index("s") * info.num_cores + lax.axis_index("c")
        base = wid * per_worker

        @pl.loop(0, per_worker // chunk)
        def _(c):
            off = pl.multiple_of(base + c * chunk, chunk)
            pltpu.sync_copy(idx_hbm.at[pl.ds(off, chunk)], idx_v)
            pltpu.async_copy(table_hbm.at[idx_v], rows_v, sem).wait()
            pltpu.sync_copy(rows_v, out_hbm.at[pl.ds(off, chunk)])

    return gather(table, idx)


def _expert_kernel(be_ref, rows_ref, nu_ref, xs_ref, wg_ref, wu_ref, bg_ref, bu_ref, wd_ref, bd_ref, o_ref,
                   xb_ref, hid_ref):
    del be_ref, nu_ref
    i = pl.program_id(0)
    p = pl.program_id(1)
    nrows = rows_ref[i]
    half = D_MODEL // 2

    @pl.when(jnp.logical_and(p == 0, nrows > 0))
    def _():
        packed = xs_ref[...]
        lo = lax.bitcast_convert_type(lax.shift_left(packed, 16), F32)
        hi = lax.bitcast_convert_type(packed & jnp.int32(-65536), F32)
        xb_ref[:, :half] = lo.astype(BF16)
        xb_ref[:, half:] = hi.astype(BF16)

    for sb in range(MOE_BM // MOE_SB):
        r = slice(sb * MOE_SB, (sb + 1) * MOE_SB)
        live = sb * MOE_SB < nrows

        @pl.when(jnp.logical_and(p < MOE_NF, live))
        def _():
            x = xb_ref[r, :]
            g = jnp.minimum(_dot(x, wg_ref[...].astype(BF16)) + bg_ref[...], SWIGLU_LIMIT)
            u = jnp.clip(_dot(x, wu_ref[...].astype(BF16)) + bu_ref[...], -SWIGLU_LIMIT, SWIGLU_LIMIT)
            hid_ref[p, r, :] = ((u + 1.0) * g * jax.nn.sigmoid(SWIGLU_ALPHA * g)).astype(BF16)

        @pl.when(jnp.logical_and(p >= MOE_NF, live))
        def _():
            y = bd_ref[...]
            for f in range(MOE_NF):
                y = y + _dot(hid_ref[f, r, :], wd_ref[f * MOE_TF:(f + 1) * MOE_TF, :].astype(BF16))
            o_ref[r, :] = y

        @pl.when(jnp.logical_and(p >= MOE_NF, jnp.logical_not(live)))
        def _():
            o_ref[r, :] = jnp.zeros((MOE_SB, MOE_TN), F32)


def _experts(xs, block_expert, block_rows, n_used, w_gate_up, b_gate_up, w_down, b_down):
    nb = xs.shape[0] // MOE_BM
    nf, nn = MOE_NF, D_MODEL // MOE_TN

    def fsel(i, p, nu):
        return jnp.where(i < nu[0], jnp.minimum(p, nf - 1), nf - 1)

    def down_map(i, p, be, rw, nu):
        parked = p < nf - 1
        e = jnp.where(parked, be[jnp.maximum(i - 1, 0)], be[i])
        n_live = jnp.where(parked, jnp.where(i > 0, nn - 1, 0), jnp.maximum(p - nf, 0))
        return (e, 0, jnp.where(i < nu[0], n_live, nn - 1))

    def xs_map(i, p, be, rw, nu):
        return (jnp.minimum(jnp.where(p > 0, i + 1, i), nu[0] - 1), 0)

    grid_spec = pltpu.PrefetchScalarGridSpec(
        num_scalar_prefetch=3,
        grid=(nb, nf + nn),
        in_specs=[
            pl.BlockSpec((MOE_BM, D_MODEL // 2), xs_map),
            pl.BlockSpec((None, D_MODEL, MOE_TF), lambda i, p, be, rw, nu: (be[i], 0, fsel(i, p, nu))),
            pl.BlockSpec((None, D_MODEL, MOE_TF), lambda i, p, be, rw, nu: (be[i], 0, nf + fsel(i, p, nu))),
            pl.BlockSpec((None, 1, MOE_TF), lambda i, p, be, rw, nu: (be[i], 0, fsel(i, p, nu))),
            pl.BlockSpec((None, 1, MOE_TF), lambda i, p, be, rw, nu: (be[i], 0, nf + fsel(i, p, nu))),
            pl.BlockSpec((None, D_FF, MOE_TN), down_map),
            pl.BlockSpec((None, 1, MOE_TN), down_map),
        ],
        out_specs=pl.BlockSpec((MOE_BM, MOE_TN), lambda i, p, be, rw, nu: (i, jnp.maximum(p - nf, 0))),
        scratch_shapes=[pltpu.VMEM((MOE_BM, D_MODEL), BF16), pltpu.VMEM((nf, MOE_BM, MOE_TF), BF16)],
    )
    bgu = b_gate_up.reshape(N_EXPERTS, 1, 2 * D_FF)
    bd = b_down.reshape(N_EXPERTS, 1, D_MODEL)
    return pl.pallas_call(
        _expert_kernel,
        grid_spec=grid_spec,
        out_shape=jax.ShapeDtypeStruct((xs.shape[0], D_MODEL), F32),
        compiler_params=_cparams(("arbitrary", "arbitrary")),
        name="moe_experts",
    )(block_expert, block_rows, n_used, xs, w_gate_up, w_gate_up, bgu, bgu, w_down, bd)


def _route(top_i, n_blocks):
    T = top_i.shape[0]
    expert = top_i.reshape(T * TOP_K)
    onehot = (expert[:, None] == jnp.arange(N_EXPERTS, dtype=jnp.int32)[None, :]).astype(jnp.int32)
    csum = jnp.cumsum(onehot, axis=0)
    counts = csum[-1]
    rank = jnp.take_along_axis(csum, expert[:, None], axis=1)[:, 0] - 1
    bpe = (counts + MOE_BM - 1) // MOE_BM
    bend = jnp.cumsum(bpe)
    bstart = bend - bpe
    dest = (bstart[expert] * MOE_BM + rank).astype(jnp.int32)
    token = jnp.repeat(jnp.arange(T, dtype=jnp.int32), TOP_K)
    row_token = (jnp.arange(n_blocks * MOE_BM, dtype=jnp.int32) % T).at[dest].set(token)
    n_used = bend[-1]
    blk = jnp.arange(n_blocks, dtype=jnp.int32)
    be = jnp.minimum(jnp.searchsorted(bend, jnp.minimum(blk, n_used - 1), side="right"), N_EXPERTS - 1).astype(jnp.int32)
    rows = jnp.clip(counts[be] - (blk - bstart[be]) * MOE_BM, 0, MOE_BM)
    rows = jnp.where(blk < n_used, rows, 0).astype(jnp.int32)
    return dest, row_token, be, rows, n_used.reshape(1).astype(jnp.int32)


def _combine_kernel(x1_ref, ge_ref, gt_ref, g2_ref, b2_ref, o_ref):
    gt = gt_ref[...]
    m = gt[:, 0:1] * ge_ref[0]
    for k in range(1, TOP_K):
        m = m + gt[:, k:k + 1] * ge_ref[k]
    o_ref[...] = _layer_norm(DEEPNORM_ALPHA * x1_ref[...] + m, g2_ref[...], b2_ref[...])


def _combine(x1, ge, gate, g2, b2, row0, tm):
    n_rows = x1.shape[0]
    b0 = row0 // tm
    return pl.pallas_call(
        _combine_kernel,
        grid=(n_rows // tm,),
        in_specs=[pl.BlockSpec((tm, D_MODEL), lambda i: (i, 0)),
                  pl.BlockSpec((TOP_K, tm, D_MODEL), lambda i: (0, b0 + i, 0)),
                  pl.BlockSpec((tm, TOP_K), lambda i: (b0 + i, 0)),
                  pl.BlockSpec(g2.shape, lambda i: (0, 0)), pl.BlockSpec(b2.shape, lambda i: (0, 0))],
        out_specs=pl.BlockSpec((tm, D_MODEL), lambda i: (i, 0)),
        out_shape=jax.ShapeDtypeStruct((n_rows, D_MODEL), F32),
        compiler_params=_cparams(("parallel",)),
        name="moe_combine",
    )(x1, ge, gate, g2, b2)


def _layer(xp, xs, caches, state, w_in, rel_bias, gla_w_up, gla_b, gla_norm_g, w_pa, w_pb, w_o, ln1_g, ln1_b,
           router_w, router_b, w_gate_up, b_gate_up, w_down, b_down, ln2_g, ln2_b):
    batch, seq, _ = xp.shape
    dbatch, dseq, _ = xs.shape
    tp, ts = batch * seq, dbatch * dseq
    xp2, xs2 = xp.reshape(tp, D_MODEL), xs.reshape(ts, D_MODEL)

    o_b0, o_lr, o_g = PA_W, PA_W + PB_LR, PA_W + PB_LR + GLA_RANK
    w_a = w_in[:, :PA_W].astype(BF16)
    w_b = jnp.concatenate([w_in[:, o_b0:o_g], jnp.zeros((D_MODEL, LANE - GLA_RANK), F32)], axis=1).astype(BF16)
    w_g = w_in[:, o_g:].astype(BF16)
    wup = jnp.concatenate([gla_w_up, jnp.zeros((LANE - GLA_RANK, H_B * DK_B), F32)], axis=0).astype(BF16)
    gb = gla_b.reshape(1, H_B * DK_B)
    ng = gla_norm_g.reshape(1, DV_B)
    wpa, wpb, wo = w_pa.astype(BF16), w_pb.astype(BF16), w_o.astype(BF16)
    g1, b1 = ln1_g.reshape(1, D_MODEL), ln1_b.reshape(1, D_MODEL)
    g2, b2 = ln2_g.reshape(1, D_MODEL), ln2_b.reshape(1, D_MODEL)
    rw = jnp.concatenate([router_w, jnp.zeros((D_MODEL, LANE - N_EXPERTS), F32)], axis=1)
    rwh = rw.astype(BF16)
    rwl = (rw - rwh.astype(F32)).astype(BF16)
    rbp = jnp.concatenate([router_b, jnp.full((LANE - N_EXPERTS,), NEG, F32)]).reshape(1, LANE)
    caches8 = [c.reshape(dbatch, c.shape[1], KV_ROWS, HEAD_DIM) for c in caches]
    shifted = _sc_cache_shift(caches8, dseq)

    pa_p = _project(xp2, w_a, 512, A_QKV_WIDTH, "proj_a_prompt")
    pb_p = _project(xp2, w_b, 512, PB_W, "proj_b_prompt")
    pg_p = _project(xp2, w_g, 512, 1024, "proj_g_prompt")
    oas, lses = [], []
    for g in range(N_GROUPS):
        o, lse = _attn_prompt_group(pa_p, _prompt_table(rel_bias, g), g, batch, seq)
        oas.append(o)
        lses.append(lse)
    ob_p, st_p = _gla(pb_p, wup, gb, ng, jnp.zeros((batch, H_B, DV_B, DK_B), F32), batch, seq, GLA_CHUNK, 256)
    x1_p, x1p_p, ti_p, gt_p = _merge(xp2, oas + lses, ob_p, pg_p, wpa, wpb, wo, g1, b1, rwh, rwl, rbp, 256)

    pa_s = _project(xs2, w_a, ts, A_QKV_WIDTH, "proj_a_sample")
    pb_s = _project(xs2, w_b, ts, PB_W, "proj_b_sample")
    pg_s = _project(xs2, w_g, ts, 1024, "proj_g_sample")
    tabc, tabn, combo_base = _sample_tables(rel_bias, dseq)
    new_rows = [_kv_pack(pa_s, g, dbatch, dseq, dseq, dseq) for g in range(N_GROUPS)]
    oa_s = _attn_sample(pa_s, new_rows, caches8, tabc, tabn, combo_base, dbatch, dseq)
    chunk_s = int(np.gcd(dseq, GLA_CHUNK))
    ob_s, st_s = _gla(pb_s, wup, gb, ng, jnp.swapaxes(state, -1, -2), dbatch, dseq, chunk_s, dseq)
    x1_s, x1p_s, ti_s, gt_s = _merge(xs2, [oa_s], ob_s, pg_s, wpa, wpb, wo, g1, b1, rwh, rwl, rbp, ts)

    x1p = jnp.concatenate([x1p_p, x1p_s], axis=0)
    top_i = jnp.concatenate([ti_p, ti_s], axis=0)
    gate = jnp.concatenate([gt_p, gt_s], axis=0)
    t_all = tp + ts
    n_blocks = -(-(t_all * TOP_K) // MOE_BM) + N_EXPERTS
    dest, row_token, be, rows, n_used = _route(top_i, n_blocks)
    xsorted = _sc_gather(x1p, row_token)
    eo = _experts(xsorted, be, rows, n_used, w_gate_up, b_gate_up, w_down, b_down)
    dest_kmajor = dest.reshape(t_all, TOP_K).T.reshape(TOP_K * t_all)
    ge = _sc_gather(eo, dest_kmajor).reshape(TOP_K, t_all, D_MODEL)
    y_p = _combine(x1_p, ge, gate, g2, b2, 0, 256)
    y_s = _combine(x1_s, ge, gate, g2, b2, tp, 256)

    bufs_p, bufs_s = [], []
    for g, (window, _) in enumerate(DILATED_GROUPS):
        keep = min(window, seq)
        bufs_p.append(_kv_pack(pa_p, g, batch, seq, keep, Q_BLOCK).reshape(batch, keep, 2, H_A, HEAD_DIM))
        clen = caches[g].shape[1]
        assert clen == window and dseq <= clen
        buf = lax.dynamic_update_slice(shifted[g], new_rows[g], (0, clen - dseq, 0, 0))
        bufs_s.append(buf.reshape(dbatch, clen, 2, H_A, HEAD_DIM))
    return (y_p.reshape(batch, seq, D_MODEL), y_s.reshape(dbatch, dseq, D_MODEL), bufs_p, jnp.swapaxes(st_p, -1, -2),
            bufs_s, jnp.swapaxes(st_s, -1, -2))


def kernel(x_prompt, x_sample, cache_a1_kv, cache_a2_kv, cache_a3_kv, state_b_s, w_in, rel_bias, gla_w_up, gla_b,
           gla_norm_g, w_pa, w_pb, w_o, ln1_g, ln1_b, router_w, router_b, w_gate_up, b_gate_up, w_down, b_down,
           ln2_g, ln2_b):
    assert w_in.shape[0] == DEPTH
    yp, ys, bufs_p, st_p, bufs_s, st_s = _layer(
        x_prompt, x_sample, (cache_a1_kv[0], cache_a2_kv[0], cache_a3_kv[0]), state_b_s[0], w_in[0], rel_bias,
        gla_w_up[0], gla_b[0], gla_norm_g[0], w_pa[0], w_pb[0], w_o[0], ln1_g[0], ln1_b[0], router_w[0], router_b[0],
        w_gate_up[0], b_gate_up[0], w_down[0], b_down[0], ln2_g[0], ln2_b[0])
    return (yp, ys, bufs_p[0][None], bufs_p[1][None], bufs_p[2][None], st_p[None],
            bufs_s[0][None], bufs_s[1][None], bufs_s[2][None], st_s[None].astype(state_b_s.dtype))
```

```python
import functools

import numpy as np
import jax
import jax.numpy as jnp
from jax import lax
from jax.experimental import pallas as pl
from jax.experimental.pallas import tpu as pltpu
from jax.experimental.pallas import tpu_sc as plsc

F32 = jnp.float32
BF16 = jnp.bfloat16

D_MODEL = 2048
HEAD_DIM = 128
DILATED_GROUPS = ((128, 1), (512, 4), (2048, 16))
N_GROUPS = 3
H_A = 4
A_WIDTH = H_A * HEAD_DIM
A_QKV_WIDTH = N_GROUPS * A_WIDTH
Q_BLOCK = 128
N_BUCKETS = 32
REL_MAX_DIST = 2048
H_B = 4
DK_B = 64
DV_B = 128
GLA_RANK = 16
GLA_TAU = 16.0
GLA_CHUNK = 64
GLA_SUB = 16
GLA_EXP_CLAMP = 80.0
N_EXPERTS = 32
TOP_K = 4
D_FF = 2048
SWIGLU_LIMIT = 7.0
SWIGLU_ALPHA = 1.702
LN_EPS = 1e-5
RMS_EPS = 1e-6
DEPTH = 1
DEEPNORM_ALPHA = (2.0 * DEPTH) ** 0.25
ATT_SCALE = HEAD_DIM ** -0.5
NEG = float(np.finfo(np.float32).min)

VMEM_LIMIT = 56 * 1024 * 1024
LANE = 128

PA_W = 3 * A_QKV_WIDTH
PB_LR = H_B * DK_B * 2 + H_B * DV_B * 2
PB_W = PB_LR + LANE
PG_W = 2 * D_MODEL

MOE_BM = 1024
MOE_SB = 512
MOE_TF = 512
MOE_NF = D_FF // MOE_TF
MOE_TN = 512
SC_CHUNK_BYTES = 128 * 1024


def _cparams(sem):
    return pltpu.CompilerParams(dimension_semantics=sem, vmem_limit_bytes=VMEM_LIMIT)


def _dot(a, b):
    return jnp.dot(a, b, preferred_element_type=F32)


def _dot_nt(a, b):
    return lax.dot_general(a, b, (((1,), (1,)), ((), ())), preferred_element_type=F32)


def _dot_tn(a, b):
    return lax.dot_general(a, b, (((0,), (0,)), ((), ())), preferred_element_type=F32)


def _proj_kernel(x_ref, w_ref, o_ref, xb_ref):
    @pl.when(pl.program_id(1) == 0)
    def _():
        xb_ref[...] = x_ref[...].astype(BF16)

    o_ref[...] = _dot(xb_ref[...], w_ref[...])


def _project(x, w, tm, tn, name):
    T, D = x.shape
    N = w.shape[1]
    return pl.pallas_call(
        _proj_kernel,
        grid=(T // tm, N // tn),
        in_specs=[pl.BlockSpec((tm, D), lambda i, j: (i, 0)), pl.BlockSpec((D, tn), lambda i, j: (0, j))],
        out_specs=pl.BlockSpec((tm, tn), lambda i, j: (i, j)),
        out_shape=jax.ShapeDtypeStruct((T, N), F32),
        scratch_shapes=[pltpu.VMEM((tm, D), BF16)],
        compiler_params=_cparams(("parallel", "arbitrary")),
        name=name,
    )(x, w)


def _t5_bucket(dist):
    max_exact = N_BUCKETS // 2
    d = np.maximum(dist, 1).astype(np.float32)
    large = max_exact + (np.log(d / max_exact) / np.log(REL_MAX_DIST / max_exact) * (N_BUCKETS - max_exact)).astype(np.int32)
    large = np.minimum(large, N_BUCKETS - 1)
    return np.where(dist < max_exact, dist, large).astype(np.int32)


def _bias_lookup(rel_bias, g, j, valid):
    _, dil = DILATED_GROUPS[g]
    bucket = _t5_bucket(dil * np.clip(j, 0, Q_BLOCK))
    onehot = bucket[None] == np.arange(N_BUCKETS).reshape((N_BUCKETS,) + (1,) * j.ndim)
    rb = rel_bias[:, g * H_A:(g + 1) * H_A].astype(F32).T.reshape((H_A, N_BUCKETS) + (1,) * j.ndim)
    vals = jnp.sum(jnp.where(onehot[None], rb, 0.0), axis=1)
    return jnp.where(valid[None], vals, NEG)


def _prompt_table(rel_bias, g):
    qi = np.arange(Q_BLOCK)[:, None]
    kj = np.arange(2 * Q_BLOCK)[None, :]
    j = Q_BLOCK + qi - kj
    return _bias_lookup(rel_bias, g, j, (j >= 0) & (j <= Q_BLOCK))


def _sample_tables(rel_bias, dec_seq):
    m = np.arange(Q_BLOCK)
    tabc, combo_base = [], []
    for g, (_, dil) in enumerate(DILATED_GROUPS):
        combo_base.append(len(tabc))
        for fl in range((dec_seq - 1) // dil + 1):
            j = Q_BLOCK + fl - m
            col = _bias_lookup(rel_bias, g, j, j <= Q_BLOCK).T
            col = jnp.concatenate([col, jnp.zeros_like(col)], axis=1)
            tabc.append(jnp.broadcast_to(col[:, :, None], (Q_BLOCK, 2 * H_A, LANE)))
    tabn = []
    s = np.arange(dec_seq)[:, None]
    sp = np.arange(dec_seq)[None, :]
    for g, (_, dil) in enumerate(DILATED_GROUPS):
        diff = s - sp
        t = _bias_lookup(rel_bias, g, diff // dil, (diff >= 0) & (diff % dil == 0))
        t = jnp.transpose(t, (1, 2, 0))
        t = jnp.concatenate([t, jnp.zeros_like(t)], axis=2)
        tabn.append(jnp.broadcast_to(t[..., None], (dec_seq, dec_seq, 2 * H_A, LANE)))
    return jnp.stack(tabc), jnp.stack(tabn), tuple(combo_base)


def _attn_prompt_kernel(q_ref, kc_ref, kp_ref, vc_ref, vp_ref, tab_ref, o_ref, lse_ref):
    has_prev = pl.program_id(2) > 0
    for h in range(H_A):
        sl = slice(h * HEAD_DIM, (h + 1) * HEAD_DIM)
        q = q_ref[:, sl].astype(BF16)
        sc = _dot_nt(q, kc_ref[:, sl].astype(BF16)) * ATT_SCALE + tab_ref[h, :, Q_BLOCK:]
        sp = _dot_nt(q, kp_ref[:, sl].astype(BF16)) * ATT_SCALE + tab_ref[h, :, :Q_BLOCK]
        sp = jnp.where(has_prev, sp, NEG)
        m = jnp.maximum(jnp.max(sc, axis=-1, keepdims=True), jnp.max(sp, axis=-1, keepdims=True))
        pc = jnp.exp(sc - m)
        pp = jnp.exp(sp - m)
        l = jnp.sum(pc, axis=-1, keepdims=True) + jnp.sum(pp, axis=-1, keepdims=True)
        inv = 1.0 / l
        o = _dot((pc * inv).astype(BF16), vc_ref[:, sl].astype(BF16)) + _dot((pp * inv).astype(BF16), vp_ref[:, sl].astype(BF16))
        o_ref[:, sl] = o
        lse_ref[:, sl] = jnp.broadcast_to(m + jnp.log(l), (Q_BLOCK, HEAD_DIM))


def _attn_prompt_strided_kernel(*refs, dil, with_prev):
    if with_prev:
        q_ref, kc_ref, kp_ref, vc_ref, vp_ref, tab_ref, o_ref, lse_ref = refs
    else:
        q_ref, kc_ref, vc_ref, tab_ref, o_ref, lse_ref = refs
    has_prev = pl.program_id(1) > 0

    def body(r, carry):
        idx = pl.ds(r, Q_BLOCK, stride=dil)
        q = q_ref[idx, :].astype(BF16)
        sc = _dot_nt(q, kc_ref[idx, :].astype(BF16)) * ATT_SCALE + tab_ref[:, Q_BLOCK:]
        m = jnp.max(sc, axis=-1, keepdims=True)
        if with_prev:
            sp = _dot_nt(q, kp_ref[idx, :].astype(BF16)) * ATT_SCALE + tab_ref[:, :Q_BLOCK]
            sp = jnp.where(has_prev, sp, NEG)
            m = jnp.maximum(m, jnp.max(sp, axis=-1, keepdims=True))
        pc = jnp.exp(sc - m)
        l = jnp.sum(pc, axis=-1, keepdims=True)
        if with_prev:
            pp = jnp.exp(sp - m)
            l = l + jnp.sum(pp, axis=-1, keepdims=True)
        inv = 1.0 / l
        o = _dot((pc * inv).astype(BF16), vc_ref[idx, :].astype(BF16))
        if with_prev:
            o = o + _dot((pp * inv).astype(BF16), vp_ref[idx, :].astype(BF16))
        o_ref[idx, :] = o
        lse_ref[idx, :] = jnp.broadcast_to(m + jnp.log(l), (Q_BLOCK, HEAD_DIM))
        return carry

    lax.fori_loop(0, dil, body, 0, unroll=min(dil, 4))


def _attn_prompt_strided(pa, table, g, batch, seq):
    _, dil = DILATED_GROUPS[g]
    rows = dil * Q_BLOCK
    nblk = seq // rows
    with_prev = nblk > 1
    hcols = A_QKV_WIDTH // HEAD_DIM

    def spec(sec, prev):
        if prev:
            return pl.BlockSpec((rows, HEAD_DIM), lambda b, i, h: (b * nblk + jnp.maximum(i - 1, 0), sec * hcols + g * H_A + h))
        return pl.BlockSpec((rows, HEAD_DIM), lambda b, i, h: (b * nblk + i, sec * hcols + g * H_A + h))

    in_specs = [spec(0, False), spec(1, False)] + ([spec(1, True)] if with_prev else []) + [spec(2, False)] + (
        [spec(2, True)] if with_prev else []) + [pl.BlockSpec((None, Q_BLOCK, 2 * Q_BLOCK), lambda b, i, h: (h, 0, 0))]
    out_spec = pl.BlockSpec((rows, HEAD_DIM), lambda b, i, h: (b * nblk + i, h))
    return pl.pallas_call(
        functools.partial(_attn_prompt_strided_kernel, dil=dil, with_prev=with_prev),
        grid=(batch, nblk, H_A),
        in_specs=in_specs,
        out_specs=[out_spec, out_spec],
        out_shape=[jax.ShapeDtypeStruct((batch * seq, A_WIDTH), F32)] * 2,
        compiler_params=_cparams(("parallel", "arbitrary", "arbitrary")),
        name=f"attn_prompt_g{g}",
    )(*([pa] * (len(in_specs) - 1)), table)


def _attn_prompt_group(pa, table, g, batch, seq):
    _, dil = DILATED_GROUPS[g]
    if dil > 1:
        return _attn_prompt_strided(pa, table, g, batch, seq)
    sub = seq // dil
    nqb = sub // Q_BLOCK
    wblk = PA_W // A_WIDTH
    pv = pa.reshape(batch, sub, dil * PA_W)

    def spec(off, prev):
        if prev:
            return pl.BlockSpec((None, Q_BLOCK, A_WIDTH), lambda b, r, i: (b, jnp.maximum(i - 1, 0), r * wblk + off + g))
        return pl.BlockSpec((None, Q_BLOCK, A_WIDTH), lambda b, r, i: (b, i, r * wblk + off + g))

    out_spec = pl.BlockSpec((None, Q_BLOCK, A_WIDTH), lambda b, r, i: (b, i, r))
    o, lse = pl.pallas_call(
        _attn_prompt_kernel,
        grid=(batch, dil, nqb),
        in_specs=[spec(0, False), spec(N_GROUPS, False), spec(N_GROUPS, True), spec(2 * N_GROUPS, False),
                  spec(2 * N_GROUPS, True), pl.BlockSpec((H_A, Q_BLOCK, 2 * Q_BLOCK), lambda b, r, i: (0, 0, 0))],
        out_specs=[out_spec, out_spec],
        out_shape=[jax.ShapeDtypeStruct((batch, sub, dil * A_WIDTH), F32)] * 2,
        compiler_params=_cparams(("parallel", "parallel", "arbitrary")),
        name=f"attn_prompt_g{g}",
    )(pv, pv, pv, pv, pv, table)
    return o.reshape(batch * seq, A_WIDTH), lse.reshape(batch * seq, A_WIDTH)


KV_ROWS = 2 * H_A


def _kv_pack_kernel(k_ref, v_ref, o_ref):
    parts = [k_ref[:, h * HEAD_DIM:(h + 1) * HEAD_DIM] for h in range(H_A)]
    parts += [v_ref[:, h * HEAD_DIM:(h + 1) * HEAD_DIM] for h in range(H_A)]
    o_ref[...] = jnp.stack(parts, axis=1)


def _kv_pack(pa, g, batch, seq, keep, tm):
    nblk, blk0, per_b = keep // tm, (seq - keep) // tm, seq // tm
    wblk = A_QKV_WIDTH // A_WIDTH
    out = pl.pallas_call(
        _kv_pack_kernel,
        grid=(batch, nblk),
        in_specs=[pl.BlockSpec((tm, A_WIDTH), lambda b, i: (b * per_b + blk0 + i, wblk + g)),
                  pl.BlockSpec((tm, A_WIDTH), lambda b, i: (b * per_b + blk0 + i, 2 * wblk + g))],
        out_specs=pl.BlockSpec((tm, KV_ROWS, HEAD_DIM), lambda b, i: (b * nblk + i, 0, 0)),
        out_shape=jax.ShapeDtypeStruct((batch * keep, KV_ROWS, HEAD_DIM), F32),
        compiler_params=_cparams(("parallel", "parallel")),
        name=f"kv_pack_g{g}_{keep}",
    )(pa, pa)
    return out.reshape(batch, keep, KV_ROWS, HEAD_DIM)


def _attn_sample_kernel(qkv_ref, n1_ref, n2_ref, n3_ref, c1_ref, c2_ref, c3_ref, tabc_ref, tabn_ref, o_ref, *,
                        dec_seq, combo_base):
    caches = (c1_ref, c2_ref, c3_ref)
    news = (n1_ref, n2_ref, n3_ref)
    zeros = jnp.zeros((H_A, HEAD_DIM), F32)
    for s in range(dec_seq):
        outs, lses = [], []
        for g, (_, dil) in enumerate(DILATED_GROUPS):
            rho, fl = s % dil, s // dil
            qm = jnp.concatenate([qkv_ref[s:s + 1, g * A_WIDTH + h * HEAD_DIM:g * A_WIDTH + (h + 1) * HEAD_DIM]
                                  for h in range(H_A)] + [zeros], axis=0)
            kc = caches[g][:, rho]
            kn = news[g][...]
            sc = jnp.sum(kc * qm[None], axis=-1, keepdims=True) * ATT_SCALE + tabc_ref[combo_base[g] + fl]
            sn = jnp.sum(kn * qm[None], axis=-1, keepdims=True) * ATT_SCALE + tabn_ref[g, s]
            m = jnp.maximum(jnp.max(sc, axis=0), jnp.max(sn, axis=0))
            pc = jnp.exp(sc - m[None])
            pn = jnp.exp(sn - m[None])
            l = jnp.sum(pc, axis=0) + jnp.sum(pn, axis=0)
            acc = jnp.sum(pltpu.roll(pc, H_A, 1) * kc, axis=0) + jnp.sum(pltpu.roll(pn, H_A, 1) * kn, axis=0)
            outs.append(acc / pltpu.roll(l, H_A, 0))
            lses.append(pltpu.roll(m + jnp.log(l), H_A, 0))
        mm = jnp.maximum(jnp.maximum(lses[0], lses[1]), lses[2])
        ws = [jnp.exp(x - mm) for x in lses]
        o_ref[s] = (ws[0] * outs[0] + ws[1] * outs[1] + ws[2] * outs[2]) / (ws[0] + ws[1] + ws[2])


def _attn_sample(pa, new_rows, caches, tabc, tabn, combo_base, batch, dec_seq):
    views, specs = [], []
    for g, (window, dil) in enumerate(DILATED_GROUPS):
        assert caches[g].shape[1] == window and dec_seq <= Q_BLOCK
        views.append(caches[g].reshape(batch, Q_BLOCK, dil, KV_ROWS, HEAD_DIM))
        used = min(dil, dec_seq)
        specs.append(pl.BlockSpec((None, Q_BLOCK, used, KV_ROWS, HEAD_DIM), lambda b: (b, 0, 0, 0, 0)))
    qkv = pa.reshape(batch, dec_seq, PA_W)
    new_spec = pl.BlockSpec((None, dec_seq, KV_ROWS, HEAD_DIM), lambda b: (b, 0, 0, 0))
    out = pl.pallas_call(
        functools.partial(_attn_sample_kernel, dec_seq=dec_seq, combo_base=combo_base),
        grid=(batch,),
        in_specs=[pl.BlockSpec((None, dec_seq, PA_W), lambda b: (b, 0, 0))] + [new_spec] * N_GROUPS + specs + [
            pl.BlockSpec(tabc.shape, lambda b: (0, 0, 0, 0)), pl.BlockSpec(tabn.shape, lambda b: (0, 0, 0, 0, 0))],
        out_specs=pl.BlockSpec((None, dec_seq, KV_ROWS, HEAD_DIM), lambda b: (b, 0, 0, 0)),
        out_shape=jax.ShapeDtypeStruct((batch, dec_seq, KV_ROWS, HEAD_DIM), F32),
        compiler_params=_cparams(("parallel",)),
        name="attn_sample",
    )(qkv, *new_rows, *views, tabc, tabn)
    return out[:, :, H_A:, :].reshape(batch * dec_seq, A_WIDTH)


def _sc_cache_shift(caches, drop):
    info = plsc.get_sparse_core_info()
    n_workers = info.num_cores * info.num_subcores
    batch = caches[0].shape[0]
    assert batch % n_workers == 0
    mesh = plsc.VectorSubcoreMesh(core_axis_name="c", subcore_axis_name="s")

    row_bytes = KV_ROWS * HEAD_DIM * 4
    chunks = []
    for c in caches:
        keep = c.shape[1] - drop
        ch = max(d for d in range(1, SC_CHUNK_BYTES // row_bytes + 1) if keep % d == 0)
        chunks.append(ch)
    buf_rows = max(chunks)

    @functools.partial(pl.kernel, mesh=mesh, out_type=[jax.ShapeDtypeStruct(c.shape, c.dtype) for c in caches],
                       scratch_types=[pltpu.VMEM((buf_rows, KV_ROWS, HEAD_DIM), caches[0].dtype)])
    def shift(*refs):
        srcs, dsts, buf = refs[:len(caches)], refs[len(caches):2 * len(caches)], refs[-1]
        wid = lax.axis_index("s") * info.num_cores + lax.axis_index("c")
        for j in range(batch // n_workers):
            b = wid * (batch // n_workers) + j
            for src, dst, ch in zip(srcs, dsts, chunks):
                stage = buf.at[pl.ds(0, ch)]

                @pl.loop(0, (src.shape[1] - drop) // ch)
                def _(i):
                    pltpu.sync_copy(src.at[b, pl.ds(drop + i * ch, ch)], stage)
                    pltpu.sync_copy(stage, dst.at[b, pl.ds(i * ch, ch)])

    return shift(*caches)


def _split3(x):
    hi = x.astype(BF16)
    r = x - hi.astype(F32)
    mid = r.astype(BF16)
    lo = (r - mid.astype(F32)).astype(BF16)
    return hi, mid, lo


def _gla_kernel(p_ref, wup_ref, gb_ref, ng_ref, s0_ref, o_ref, st_ref, *, chunk, tb):
    @pl.when(pl.program_id(1) == 0)
    def _():
        st_ref[...] = s0_ref[...]

    sub = min(GLA_SUB, chunk)
    kq = H_B * DK_B

    def rb(x):
        xb = x.astype(BF16)
        return xb if chunk >= 16 else xb.astype(F32)

    row = lax.broadcasted_iota(jnp.int32, (chunk, chunk), 0)
    colm = lax.broadcasted_iota(jnp.int32, (chunk, chunk), 1)
    tri = rb(jnp.where(row >= colm, 1.0, 0.0))
    for c in range(tb // chunk):
        rows = slice(c * chunk, (c + 1) * chunk)
        z = _dot(rb(p_ref[rows, PB_LR:PB_W]), rb(wup_ref[...])) + gb_ref[...]
        la = -(jnp.maximum(-z, 0.0) + jnp.log1p(jnp.exp(-jnp.abs(z)))) * (1.0 / GLA_TAU)
        b = functools.reduce(lambda u, w: u + w, [_dot(tri, rb(t)) for t in _split3(la)])
        blast = b[chunk - 1:chunk, :]
        q = p_ref[rows, 0:kq] * (DK_B ** -0.5)
        k = p_ref[rows, kq:2 * kq]
        qin = rb(q * jnp.exp(b))
        kst = rb(k * jnp.exp(blast - b))
        for h in range(H_B):
            ks = slice(h * DK_B, (h + 1) * DK_B)
            vs = slice(h * DV_B, (h + 1) * DV_B)
            st = st_ref[h]
            vb = rb(p_ref[rows, 2 * kq + h * DV_B:2 * kq + (h + 1) * DV_B])
            o_inter = _dot_nt(qin[:, ks], rb(st))
            parts = []
            for blk in range(chunk // sub):
                r0 = blk * sub
                n = r0 + sub
                ref_b = b[r0 - 1:r0, ks] if blk > 0 else jnp.zeros((1, DK_B), F32)
                qi = rb(q[r0:n, ks] * jnp.exp(b[r0:n, ks] - ref_b))
                ki = rb(k[0:n, ks] * jnp.exp(jnp.minimum(ref_b - b[0:n, ks], GLA_EXP_CLAMP)))
                a = _dot_nt(qi, ki)
                ti = lax.broadcasted_iota(jnp.int32, (sub, n), 0) + r0
                si = lax.broadcasted_iota(jnp.int32, (sub, n), 1)
                a = jnp.where(si <= ti, a, 0.0)
                parts.append(_dot(rb(a), vb[0:n]))
            o = o_inter + (jnp.concatenate(parts, axis=0) if len(parts) > 1 else parts[0])
            st_ref[h] = st * jnp.exp(blast[:, ks]) + _dot_tn(vb, kst[:, ks])
            on = o * lax.rsqrt(jnp.mean(o * o, axis=-1, keepdims=True) + RMS_EPS) * ng_ref[...]
            rg = p_ref[rows, 2 * kq + H_B * DV_B + h * DV_B:2 * kq + H_B * DV_B + (h + 1) * DV_B]
            o_ref[rows, vs] = (on * (rg * jax.nn.sigmoid(rg))).astype(BF16)


def _gla(pb, wup, gb, ng, s0t, batch, seq, chunk, tb):
    p3 = pb.reshape(batch, seq, PB_W)
    o, st = pl.pallas_call(
        functools.partial(_gla_kernel, chunk=chunk, tb=tb),
        grid=(batch, seq // tb),
        in_specs=[pl.BlockSpec((None, tb, PB_W), lambda b, i: (b, i, 0)),
                  pl.BlockSpec(wup.shape, lambda b, i: (0, 0)),
                  pl.BlockSpec(gb.shape, lambda b, i: (0, 0)),
                  pl.BlockSpec(ng.shape, lambda b, i: (0, 0)),
                  pl.BlockSpec((None, H_B, DV_B, DK_B), lambda b, i: (b, 0, 0, 0))],
        out_specs=[pl.BlockSpec((None, tb, H_B * DV_B), lambda b, i: (b, i, 0)),
                   pl.BlockSpec((None, H_B, DV_B, DK_B), lambda b, i: (b, 0, 0, 0))],
        out_shape=[jax.ShapeDtypeStruct((batch, seq, H_B * DV_B), BF16),
                   jax.ShapeDtypeStruct((batch, H_B, DV_B, DK_B), F32)],
        compiler_params=_cparams(("parallel", "arbitrary")),
        name=f"gla_c{chunk}",
    )(p3, wup, gb, ng, s0t)
    return o.reshape(batch * seq, H_B * DV_B), st


def _layer_norm(u, g, b):
    mu = jnp.mean(u, axis=-1, keepdims=True)
    d = u - mu
    var = jnp.mean(d * d, axis=-1, keepdims=True)
    return d * lax.rsqrt(var + LN_EPS) * g + b


def _merge_kernel(*refs, n_groups):
    x_ref = refs[0]
    oa_refs = refs[1:1 + 2 * n_groups] if n_groups > 1 else refs[1:2]
    rest = refs[1 + (2 * n_groups if n_groups > 1 else 1):]
    (ob_ref, pg_a_ref, pg_b_ref, wpa_ref, wpb_ref, wo_ref, g1_ref, b1_ref, rwh_ref, rwl_ref, rb_ref,
     x1_ref, x1p_ref, ti_ref, gt_ref) = rest
    if n_groups > 1:
        os_ = [r[...] for r in oa_refs[:n_groups]]
        ls = [r[...] for r in oa_refs[n_groups:]]
        mm = functools.reduce(jnp.maximum, ls)
        ws = [jnp.exp(x - mm) for x in ls]
        oa = sum(w * o for w, o in zip(ws, os_)) / sum(ws)
    else:
        oa = oa_refs[0][...]
    ya = _dot(oa.astype(BF16), wpa_ref[...])
    yb = _dot(ob_ref[...], wpb_ref[...])
    branch = jax.nn.sigmoid(pg_a_ref[...]) * ya + jax.nn.sigmoid(pg_b_ref[...]) * yb
    y = _dot(branch.astype(BF16), wo_ref[...])
    x1 = _layer_norm(DEEPNORM_ALPHA * x_ref[...] + y, g1_ref[...], b1_ref[...])
    x1_ref[...] = x1
    xh = x1.astype(BF16)
    xhf = xh.astype(F32)
    bits = lax.bitcast_convert_type(xhf, jnp.int32)
    half = D_MODEL // 2
    x1p_ref[...] = lax.shift_right_logical(bits[:, :half], 16) | bits[:, half:]
    xl = (x1 - xhf).astype(BF16)
    logits = _dot(xh, rwh_ref[...]) + _dot(xl, rwh_ref[...]) + _dot(xh, rwl_ref[...]) + rb_ref[...]
    lane = lax.broadcasted_iota(jnp.int32, logits.shape, 1)
    vals = logits
    top_v, top_i = [], []
    for _ in range(TOP_K):
        m = jnp.max(vals, axis=-1, keepdims=True)
        ik = jnp.min(jnp.where(vals == m, lane, LANE), axis=-1, keepdims=True)
        vals = jnp.where(lane == ik, -jnp.inf, vals)
        top_v.append(m)
        top_i.append(ik)
    es = [jnp.exp(v - top_v[0]) for v in top_v]
    tot = functools.reduce(lambda a, b: a + b, es)
    ti_ref[...] = jnp.concatenate(top_i, axis=1)
    gt_ref[...] = jnp.concatenate([e / tot for e in es], axis=1)


def _merge(x, oas, ob, pg, wpa, wpb, wo, g1, b1, rwh, rwl, rbp, tm):
    T = x.shape[0]
    n_groups = len(oas) // 2 if len(oas) > 1 else 1

    def row(w):
        return pl.BlockSpec((tm, w), lambda i: (i, 0))

    def const(a):
        return pl.BlockSpec(a.shape, lambda i: (0,) * a.ndim, pipeline_mode=pl.Buffered(1))

    in_specs = ([row(D_MODEL)] + [row(A_WIDTH)] * len(oas) + [row(H_B * DV_B), row(D_MODEL),
                pl.BlockSpec((tm, D_MODEL), lambda i: (i, 1))] + [const(a) for a in (wpa, wpb, wo, g1, b1, rwh, rwl, rbp)])
    return pl.pallas_call(
        functools.partial(_merge_kernel, n_groups=n_groups),
        grid=(T // tm,),
        in_specs=in_specs,
        out_specs=[row(D_MODEL), row(D_MODEL // 2), row(TOP_K), row(TOP_K)],
        out_shape=[jax.ShapeDtypeStruct((T, D_MODEL), F32), jax.ShapeDtypeStruct((T, D_MODEL // 2), jnp.int32),
                   jax.ShapeDtypeStruct((T, TOP_K), jnp.int32), jax.ShapeDtypeStruct((T, TOP_K), F32)],
        compiler_params=_cparams(("parallel",)),
        name=f"merge_g{n_groups}",
    )(x, *oas, ob, pg, pg, wpa, wpb, wo, g1, b1, rwh, rwl, rbp)


def _sc_gather(table, idx):
    info = plsc.get_sparse_core_info()
    n_workers = info.num_cores * info.num_subcores
    n, width = idx.shape[0], table.shape[1]
    per_worker = n // n_workers
    chunk = SC_CHUNK_BYTES // (width * table.dtype.itemsize)
    assert per_worker * n_workers == n and per_worker % chunk == 0 and chunk % 8 == 0
    mesh = plsc.VectorSubcoreMesh(core_axis_name="c", subcore_axis_name="s")

    @functools.partial(
        pl.kernel, mesh=mesh,
        out_type=jax.ShapeDtypeStruct((n, width), table.dtype),
        scratch_types=[pltpu.VMEM((chunk,), jnp.int32), pltpu.VMEM((chunk, width), table.dtype),
                       pltpu.SemaphoreType.DMA],
    )
    def gather(table_hbm, idx_hbm, out_hbm, idx_v, rows_v, sem):
        wid = lax.axis_index("s") * info.num_cores + lax.axis_index("c")
        base = wid * per_worker

        @pl.loop(0, per_worker // chunk)
        def _(c):
            off = pl.multiple_of(base + c * chunk, chunk)
            pltpu.sync_copy(idx_hbm.at[pl.ds(off, chunk)], idx_v)
            pltpu.async_copy(table_hbm.at[idx_v], rows_v, sem).wait()
            pltpu.sync_copy(rows_v, out_hbm.at[pl.ds(off, chunk)])

    return gather(table, idx)


def _expert_kernel(be_ref, rows_ref, nu_ref, xs_ref, wg_ref, wu_ref, bg_ref, bu_ref, wd_ref, bd_ref, o_ref,
                   xb_ref, hid_ref):
    del be_ref, nu_ref
    i = pl.program_id(0)
    p = pl.program_id(1)
    nrows = rows_ref[i]
    half = D_MODEL // 2

    @pl.when(jnp.logical_and(p == 0, nrows > 0))
    def _():
        packed = xs_ref[...]
        lo = lax.bitcast_convert_type(lax.shift_left(packed, 16), F32)
        hi = lax.bitcast_convert_type(packed & jnp.int32(-65536), F32)
        xb_ref[:, :half] = lo.astype(BF16)
        xb_ref[:, half:] = hi.astype(BF16)

    for sb in range(MOE_BM // MOE_SB):
        r = slice(sb * MOE_SB, (sb + 1) * MOE_SB)
        live = sb * MOE_SB < nrows

        @pl.when(jnp.logical_and(p < MOE_NF, live))
        def _():
            x = xb_ref[r, :]
            g = jnp.minimum(_dot(x, wg_ref[...].astype(BF16)) + bg_ref[...], SWIGLU_LIMIT)
            u = jnp.clip(_dot(x, wu_ref[...].astype(BF16)) + bu_ref[...], -SWIGLU_LIMIT, SWIGLU_LIMIT)
            hid_ref[p, r, :] = ((u + 1.0) * g * jax.nn.sigmoid(SWIGLU_ALPHA * g)).astype(BF16)

        @pl.when(jnp.logical_and(p >= MOE_NF, live))
        def _():
            y = bd_ref[...]
            for f in range(MOE_NF):
                y = y + _dot(hid_ref[f, r, :], wd_ref[f * MOE_TF:(f + 1) * MOE_TF, :].astype(BF16))
            o_ref[r, :] = y

        @pl.when(jnp.logical_and(p >= MOE_NF, jnp.logical_not(live)))
        def _():
            o_ref[r, :] = jnp.zeros((MOE_SB, MOE_TN), F32)


def _experts(xs, block_expert, block_rows, n_used, w_gate_up, b_gate_up, w_down, b_down):
    nb = xs.shape[0] // MOE_BM
    nf, nn = MOE_NF, D_MODEL // MOE_TN

    def fsel(i, p, nu):
        return jnp.where(i < nu[0], jnp.minimum(p, nf - 1), nf - 1)

    def down_map(i, p, be, rw, nu):
        parked = p < nf - 1
        e = jnp.where(parked, be[jnp.maximum(i - 1, 0)], be[i])
        n_live = jnp.where(parked, jnp.where(i > 0, nn - 1, 0), jnp.maximum(p - nf, 0))
        return (e, 0, jnp.where(i < nu[0], n_live, nn - 1))

    def xs_map(i, p, be, rw, nu):
        return (jnp.minimum(jnp.where(p > 0, i + 1, i), nu[0] - 1), 0)

    grid_spec = pltpu.PrefetchScalarGridSpec(
        num_scalar_prefetch=3,
        grid=(nb, nf + nn),
        in_specs=[
            pl.BlockSpec((MOE_BM, D_MODEL // 2), xs_map),
            pl.BlockSpec((None, D_MODEL, MOE_TF), lambda i, p, be, rw, nu: (be[i], 0, fsel(i, p, nu))),
            pl.BlockSpec((None, D_MODEL, MOE_TF), lambda i, p, be, rw, nu: (be[i], 0, nf + fsel(i, p, nu))),
            pl.BlockSpec((None, 1, MOE_TF), lambda i, p, be, rw, nu: (be[i], 0, fsel(i, p, nu))),
            pl.BlockSpec((None, 1, MOE_TF), lambda i, p, be, rw, nu: (be[i], 0, nf + fsel(i, p, nu))),
            pl.BlockSpec((None, D_FF, MOE_TN), down_map),
            pl.BlockSpec((None, 1, MOE_TN), down_map),
        ],
        out_specs=pl.BlockSpec((MOE_BM, MOE_TN), lambda i, p, be, rw, nu: (i, jnp.maximum(p - nf, 0))),
        scratch_shapes=[pltpu.VMEM((MOE_BM, D_MODEL), BF16), pltpu.VMEM((nf, MOE_BM, MOE_TF), BF16)],
    )
    bgu = b_gate_up.reshape(N_EXPERTS, 1, 2 * D_FF)
    bd = b_down.reshape(N_EXPERTS, 1, D_MODEL)
    return pl.pallas_call(
        _expert_kernel,
        grid_spec=grid_spec,
        out_shape=jax.ShapeDtypeStruct((xs.shape[0], D_MODEL), F32),
        compiler_params=_cparams(("arbitrary", "arbitrary")),
        name="moe_experts",
    )(block_expert, block_rows, n_used, xs, w_gate_up, w_gate_up, bgu, bgu, w_down, bd)


def _route(top_i, n_blocks):
    T = top_i.shape[0]
    expert = top_i.reshape(T * TOP_K)
    onehot = (expert[:, None] == jnp.arange(N_EXPERTS, dtype=jnp.int32)[None, :]).astype(jnp.int32)
    csum = jnp.cumsum(onehot, axis=0)
    counts = csum[-1]
    rank = jnp.take_along_axis(csum, expert[:, None], axis=1)[:, 0] - 1
    bpe = (counts + MOE_BM - 1) // MOE_BM
    bend = jnp.cumsum(bpe)
    bstart = bend - bpe
    dest = (bstart[expert] * MOE_BM + rank).astype(jnp.int32)
    token = jnp.repeat(jnp.arange(T, dtype=jnp.int32), TOP_K)
    row_token = (jnp.arange(n_blocks * MOE_BM, dtype=jnp.int32) % T).at[dest].set(token)
    n_used = bend[-1]
    blk = jnp.arange(n_blocks, dtype=jnp.int32)
    be = jnp.minimum(jnp.searchsorted(bend, jnp.minimum(blk, n_used - 1), side="right"), N_EXPERTS - 1).astype(jnp.int32)
    rows = jnp.clip(counts[be] - (blk - bstart[be]) * MOE_BM, 0, MOE_BM)
    rows = jnp.where(blk < n_used, rows, 0).astype(jnp.int32)
    return dest, row_token, be, rows, n_used.reshape(1).astype(jnp.int32)


def _combine_kernel(x1_ref, ge_ref, gt_ref, g2_ref, b2_ref, o_ref):
    gt = gt_ref[...]
    m = gt[:, 0:1] * ge_ref[0]
    for k in range(1, TOP_K):
        m = m + gt[:, k:k + 1] * ge_ref[k]
    o_ref[...] = _layer_norm(DEEPNORM_ALPHA * x1_ref[...] + m, g2_ref[...], b2_ref[...])


def _combine(x1, ge, gate, g2, b2, row0, tm):
    n_rows = x1.shape[0]
    b0 = row0 // tm
    return pl.pallas_call(
        _combine_kernel,
        grid=(n_rows // tm,),
        in_specs=[pl.BlockSpec((tm, D_MODEL), lambda i: (i, 0)),
                  pl.BlockSpec((TOP_K, tm, D_MODEL), lambda i: (0, b0 + i, 0)),
                  pl.BlockSpec((tm, TOP_K), lambda i: (b0 + i, 0)),
                  pl.BlockSpec(g2.shape, lambda i: (0, 0)), pl.BlockSpec(b2.shape, lambda i: (0, 0))],
        out_specs=pl.BlockSpec((tm, D_MODEL), lambda i: (i, 0)),
        out_shape=jax.ShapeDtypeStruct((n_rows, D_MODEL), F32),
        compiler_params=_cparams(("parallel",)),
        name="moe_combine",
    )(x1, ge, gate, g2, b2)


def _layer(xp, xs, caches, state, w_in, rel_bias, gla_w_up, gla_b, gla_norm_g, w_pa, w_pb, w_o, ln1_g, ln1_b,
           router_w, router_b, w_gate_up, b_gate_up, w_down, b_down, ln2_g, ln2_b):
    batch, seq, _ = xp.shape
    dbatch, dseq, _ = xs.shape
    tp, ts = batch * seq, dbatch * dseq
    xp2, xs2 = xp.reshape(tp, D_MODEL), xs.reshape(ts, D_MODEL)

    o_b0, o_lr, o_g = PA_W, PA_W + PB_LR, PA_W + PB_LR + GLA_RANK
    w_a = w_in[:, :PA_W].astype(BF16)
    w_b = jnp.concatenate([w_in[:, o_b0:o_g], jnp.zeros((D_MODEL, LANE - GLA_RANK), F32)], axis=1).astype(BF16)
    w_g = w_in[:, o_g:].astype(BF16)
    wup = jnp.concatenate([gla_w_up, jnp.zeros((LANE - GLA_RANK, H_B * DK_B), F32)], axis=0).astype(BF16)
    gb = gla_b.reshape(1, H_B * DK_B)
    ng = gla_norm_g.reshape(1, DV_B)
    wpa, wpb, wo = w_pa.astype(BF16), w_pb.astype(BF16), w_o.astype(BF16)
    g1, b1 = ln1_g.reshape(1, D_MODEL), ln1_b.reshape(1, D_MODEL)
    g2, b2 = ln2_g.reshape(1, D_MODEL), ln2_b.reshape(1, D_MODEL)
    rw = jnp.concatenate([router_w, jnp.zeros((D_MODEL, LANE - N_EXPERTS), F32)], axis=1)
    rwh = rw.astype(BF16)
    rwl = (rw - rwh.astype(F32)).astype(BF16)
    rbp = jnp.concatenate([router_b, jnp.full((LANE - N_EXPERTS,), NEG, F32)]).reshape(1, LANE)
    caches8 = [c.reshape(dbatch, c.shape[1], KV_ROWS, HEAD_DIM) for c in caches]

    pa_p = _project(xp2, w_a, 512, A_QKV_WIDTH, "proj_a_prompt")
    pb_p = _project(xp2, w_b, 512, PB_W, "proj_b_prompt")
    pg_p = _project(xp2, w_g, 512, 1024, "proj_g_prompt")
    oas, lses = [], []
    for g in range(N_GROUPS):
        o, lse = _attn_prompt_group(pa_p, _prompt_table(rel_bias, g), g, batch, seq)
        oas.append(o)
        lses.append(lse)
    ob_p, st_p = _gla(pb_p, wup, gb, ng, jnp.zeros((batch, H_B, DV_B, DK_B), F32), batch, seq, GLA_CHUNK, 256)
    x1_p, x1p_p, ti_p, gt_p = _merge(xp2, oas + lses, ob_p, pg_p, wpa, wpb, wo, g1, b1, rwh, rwl, rbp, 256)

    pa_s = _project(xs2, w_a, ts, A_QKV_WIDTH, "proj_a_sample")
    pb_s = _project(xs2, w_b, ts, PB_W, "proj_b_sample")
    pg_s = _project(xs2, w_g, ts, 1024, "proj_g_sample")
    tabc, tabn, combo_base = _sample_tables(rel_bias, dseq)
    new_rows = [_kv_pack(pa_s, g, dbatch, dseq, dseq, dseq) for g in range(N_GROUPS)]
    oa_s = _attn_sample(pa_s, new_rows, caches8, tabc, tabn, combo_base, dbatch, dseq)
    chunk_s = int(np.gcd(dseq, GLA_CHUNK))
    ob_s, st_s = _gla(pb_s, wup, gb, ng, jnp.swapaxes(state, -1, -2), dbatch, dseq, chunk_s, dseq)
    x1_s, x1p_s, ti_s, gt_s = _merge(xs2, [oa_s], ob_s, pg_s, wpa, wpb, wo, g1, b1, rwh, rwl, rbp, ts)

    x1p = jnp.concatenate([x1p_p, x1p_s], axis=0)
    top_i = jnp.concatenate([ti_p, ti_s], axis=0)
    gate = jnp.concatenate([gt_p, gt_s], axis=0)
    t_all = tp + ts
    n_blocks = -(-(t_all * TOP_K) // MOE_BM) + N_EXPERTS
    dest, row_token, be, rows, n_used = _route(top_i, n_blocks)
    xsorted = _sc_gather(x1p, row_token)
    eo = _experts(xsorted, be, rows, n_used, w_gate_up, b_gate_up, w_down, b_down)
    dest_kmajor = dest.reshape(t_all, TOP_K).T.reshape(TOP_K * t_all)
    ge = _sc_gather(eo, dest_kmajor).reshape(TOP_K, t_all, D_MODEL)
    y_p = _combine(x1_p, ge, gate, g2, b2, 0, 256)
    y_s = _combine(x1_s, ge, gate, g2, b2, tp, 256)

    shifted = _sc_cache_shift(caches8, dseq)
    bufs_p, bufs_s = [], []
    for g, (window, _) in enumerate(DILATED_GROUPS):
        keep = min(window, seq)
        bufs_p.append(_kv_pack(pa_p, g, batch, seq, keep, Q_BLOCK).reshape(batch, keep, 2, H_A, HEAD_DIM))
        clen = caches[g].shape[1]
        assert clen == window and dseq <= clen
        buf = lax.dynamic_update_slice(shifted[g], new_rows[g], (0, clen - dseq, 0, 0))
        bufs_s.append(buf.reshape(dbatch, clen, 2, H_A, HEAD_DIM))
    return (y_p.reshape(batch, seq, D_MODEL), y_s.reshape(dbatch, dseq, D_MODEL), bufs_p, jnp.swapaxes(st_p, -1, -2),
            bufs_s, jnp.swapaxes(st_s, -1, -2))


def kernel(x_prompt, x_sample, cache_a1_kv, cache_a2_kv, cache_a3_kv, state_b_s, w_in, rel_bias, gla_w_up, gla_b,
           gla_norm_g, w_pa, w_pb, w_o, ln1_g, ln1_b, router_w, router_b, w_gate_up, b_gate_up, w_down, b_down,
           ln2_g, ln2_b):
    assert w_in.shape[0] == DEPTH
    yp, ys, bufs_p, st_p, bufs_s, st_s = _layer(
        x_prompt, x_sample, (cache_a1_kv[0], cache_a2_kv[0], cache_a3_kv[0]), state_b_s[0], w_in[0], rel_bias,
        gla_w_up[0], gla_b[0], gla_norm_g[0], w_pa[0], w_pb[0], w_o[0], ln1_g[0], ln1_b[0], router_w[0], router_b[0],
        w_gate_up[0], b_gate_up[0], w_down[0], b_down[0], ln2_g[0], ln2_b[0])
    return (yp, ys, bufs_p[0][None], bufs_p[1][None], bufs_p[2][None], st_p[None],
            bufs_s[0][None], bufs_s[1][None], bufs_s[2][None], st_s[None].astype(state_b_s.dtype))
```

```python
import functools

import numpy as np
import jax
import jax.numpy as jnp
from jax import lax
from jax.experimental import pallas as pl
from jax.experimental.pallas import tpu as pltpu
from jax.experimental.pallas import tpu_sc as plsc

F32 = jnp.float32
BF16 = jnp.bfloat16

D_MODEL = 2048
HEAD_DIM = 128
DILATED_GROUPS = ((128, 1), (512, 4), (2048, 16))
N_GROUPS = 3
H_A = 4
A_WIDTH = H_A * HEAD_DIM
A_QKV_WIDTH = N_GROUPS * A_WIDTH
Q_BLOCK = 128
N_BUCKETS = 32
REL_MAX_DIST = 2048
H_B = 4
DK_B = 64
DV_B = 128
GLA_RANK = 16
GLA_TAU = 16.0
GLA_CHUNK = 64
GLA_SUB = 16
GLA_EXP_CLAMP = 80.0
N_EXPERTS = 32
TOP_K = 4
D_FF = 2048
SWIGLU_LIMIT = 7.0
SWIGLU_ALPHA = 1.702
LN_EPS = 1e-5
RMS_EPS = 1e-6
DEPTH = 1
DEEPNORM_ALPHA = (2.0 * DEPTH) ** 0.25
ATT_SCALE = HEAD_DIM ** -0.5
NEG = float(np.finfo(np.float32).min)

VMEM_LIMIT = 56 * 1024 * 1024
LANE = 128

PA_W = 3 * A_QKV_WIDTH
PB_LR = H_B * DK_B * 2 + H_B * DV_B * 2
PB_W = PB_LR + LANE
PG_W = 2 * D_MODEL

MOE_BM = 1280
MOE_SB = 320
MOE_TF = 512
MOE_NF = D_FF // MOE_TF
MOE_TN = 512
SC_CHUNK_BYTES = 128 * 1024
SC_SCATTER_BYTES = 160 * 1024


def _cparams(sem):
    return pltpu.CompilerParams(dimension_semantics=sem, vmem_limit_bytes=VMEM_LIMIT)


def _dot(a, b):
    return jnp.dot(a, b, preferred_element_type=F32)


def _dot_nt(a, b):
    return lax.dot_general(a, b, (((1,), (1,)), ((), ())), preferred_element_type=F32)


def _dot_tn(a, b):
    return lax.dot_general(a, b, (((0,), (0,)), ((), ())), preferred_element_type=F32)


def _proj_kernel(x_ref, w_ref, o_ref, xb_ref):
    @pl.when(pl.program_id(1) == 0)
    def _():
        xb_ref[...] = x_ref[...].astype(BF16)

    o_ref[...] = _dot(xb_ref[...], w_ref[...])


def _project(x, w, tm, tn, name):
    T, D = x.shape
    N = w.shape[1]
    return pl.pallas_call(
        _proj_kernel,
        grid=(T // tm, N // tn),
        in_specs=[pl.BlockSpec((tm, D), lambda i, j: (i, 0)), pl.BlockSpec((D, tn), lambda i, j: (0, j))],
        out_specs=pl.BlockSpec((tm, tn), lambda i, j: (i, j)),
        out_shape=jax.ShapeDtypeStruct((T, N), F32),
        scratch_shapes=[pltpu.VMEM((tm, D), BF16)],
        compiler_params=_cparams(("parallel", "arbitrary")),
        name=name,
    )(x, w)


def _t5_bucket(dist):
    max_exact = N_BUCKETS // 2
    d = np.maximum(dist, 1).astype(np.float32)
    large = max_exact + (np.log(d / max_exact) / np.log(REL_MAX_DIST / max_exact) * (N_BUCKETS - max_exact)).astype(np.int32)
    large = np.minimum(large, N_BUCKETS - 1)
    return np.where(dist < max_exact, dist, large).astype(np.int32)


def _bias_lookup(rel_bias, g, j, valid):
    _, dil = DILATED_GROUPS[g]
    bucket = _t5_bucket(dil * np.clip(j, 0, Q_BLOCK))
    onehot = bucket[None] == np.arange(N_BUCKETS).reshape((N_BUCKETS,) + (1,) * j.ndim)
    rb = rel_bias[:, g * H_A:(g + 1) * H_A].astype(F32).T.reshape((H_A, N_BUCKETS) + (1,) * j.ndim)
    vals = jnp.sum(jnp.where(onehot[None], rb, 0.0), axis=1)
    return jnp.where(valid[None], vals, NEG)


def _prompt_table(rel_bias, g):
    qi = np.arange(Q_BLOCK)[:, None]
    kj = np.arange(2 * Q_BLOCK)[None, :]
    j = Q_BLOCK + qi - kj
    return _bias_lookup(rel_bias, g, j, (j >= 0) & (j <= Q_BLOCK))


def _sample_tables(rel_bias, dec_seq):
    m = np.arange(Q_BLOCK)
    tabc, combo_base = [], []
    for g, (_, dil) in enumerate(DILATED_GROUPS):
        combo_base.append(len(tabc))
        for fl in range((dec_seq - 1) // dil + 1):
            j = Q_BLOCK + fl - m
            col = _bias_lookup(rel_bias, g, j, j <= Q_BLOCK).T
            col = jnp.concatenate([col, jnp.zeros_like(col)], axis=1)
            tabc.append(jnp.broadcast_to(col[:, :, None], (Q_BLOCK, 2 * H_A, LANE)))
    tabn = []
    s = np.arange(dec_seq)[:, None]
    sp = np.arange(dec_seq)[None, :]
    for g, (_, dil) in enumerate(DILATED_GROUPS):
        diff = s - sp
        t = _bias_lookup(rel_bias, g, diff // dil, (diff >= 0) & (diff % dil == 0))
        t = jnp.transpose(t, (1, 2, 0))
        t = jnp.concatenate([t, jnp.zeros_like(t)], axis=2)
        tabn.append(jnp.broadcast_to(t[..., None], (dec_seq, dec_seq, 2 * H_A, LANE)))
    return jnp.stack(tabc), jnp.stack(tabn), tuple(combo_base)


def _attn_prompt_kernel(q_ref, kc_ref, kp_ref, vc_ref, vp_ref, tab_ref, o_ref, lse_ref):
    has_prev = pl.program_id(2) > 0
    for h in range(H_A):
        sl = slice(h * HEAD_DIM, (h + 1) * HEAD_DIM)
        q = q_ref[:, sl].astype(BF16)
        sc = _dot_nt(q, kc_ref[:, sl].astype(BF16)) * ATT_SCALE + tab_ref[h, :, Q_BLOCK:]
        sp = _dot_nt(q, kp_ref[:, sl].astype(BF16)) * ATT_SCALE + tab_ref[h, :, :Q_BLOCK]
        sp = jnp.where(has_prev, sp, NEG)
        m = jnp.maximum(jnp.max(sc, axis=-1, keepdims=True), jnp.max(sp, axis=-1, keepdims=True))
        pc = jnp.exp(sc - m)
        pp = jnp.exp(sp - m)
        l = jnp.sum(pc, axis=-1, keepdims=True) + jnp.sum(pp, axis=-1, keepdims=True)
        inv = 1.0 / l
        o = _dot((pc * inv).astype(BF16), vc_ref[:, sl].astype(BF16)) + _dot((pp * inv).astype(BF16), vp_ref[:, sl].astype(BF16))
        o_ref[:, sl] = o
        lse_ref[:, sl] = jnp.broadcast_to(m + jnp.log(l), (Q_BLOCK, HEAD_DIM))


def _attn_prompt_strided_kernel(*refs, dil, with_prev):
    if with_prev:
        q_ref, kc_ref, kp_ref, vc_ref, vp_ref, tab_ref, o_ref, lse_ref = refs
    else:
        q_ref, kc_ref, vc_ref, tab_ref, o_ref, lse_ref = refs
    has_prev = pl.program_id(1) > 0

    def body(r, carry):
        idx = pl.ds(r, Q_BLOCK, stride=dil)
        q = q_ref[idx, :].astype(BF16)
        sc = _dot_nt(q, kc_ref[idx, :].astype(BF16)) * ATT_SCALE + tab_ref[:, Q_BLOCK:]
        m = jnp.max(sc, axis=-1, keepdims=True)
        if with_prev:
            sp = _dot_nt(q, kp_ref[idx, :].astype(BF16)) * ATT_SCALE + tab_ref[:, :Q_BLOCK]
            sp = jnp.where(has_prev, sp, NEG)
            m = jnp.maximum(m, jnp.max(sp, axis=-1, keepdims=True))
        pc = jnp.exp(sc - m)
        l = jnp.sum(pc, axis=-1, keepdims=True)
        if with_prev:
            pp = jnp.exp(sp - m)
            l = l + jnp.sum(pp, axis=-1, keepdims=True)
        inv = 1.0 / l
        o = _dot((pc * inv).astype(BF16), vc_ref[idx, :].astype(BF16))
        if with_prev:
            o = o + _dot((pp * inv).astype(BF16), vp_ref[idx, :].astype(BF16))
        o_ref[idx, :] = o
        lse_ref[idx, :] = jnp.broadcast_to(m + jnp.log(l), (Q_BLOCK, HEAD_DIM))
        return carry

    lax.fori_loop(0, dil, body, 0, unroll=min(dil, 4))


def _attn_prompt_strided(pa, table, g, batch, seq):
    _, dil = DILATED_GROUPS[g]
    rows = dil * Q_BLOCK
    nblk = seq // rows
    with_prev = nblk > 1
    hcols = A_QKV_WIDTH // HEAD_DIM

    def spec(sec, prev):
        if prev:
            return pl.BlockSpec((rows, HEAD_DIM), lambda b, i, h: (b * nblk + jnp.maximum(i - 1, 0), sec * hcols + g * H_A + h))
        return pl.BlockSpec((rows, HEAD_DIM), lambda b, i, h: (b * nblk + i, sec * hcols + g * H_A + h))

    in_specs = [spec(0, False), spec(1, False)] + ([spec(1, True)] if with_prev else []) + [spec(2, False)] + (
        [spec(2, True)] if with_prev else []) + [pl.BlockSpec((None, Q_BLOCK, 2 * Q_BLOCK), lambda b, i, h: (h, 0, 0))]
    out_spec = pl.BlockSpec((rows, HEAD_DIM), lambda b, i, h: (b * nblk + i, h))
    return pl.pallas_call(
        functools.partial(_attn_prompt_strided_kernel, dil=dil, with_prev=with_prev),
        grid=(batch, nblk, H_A),
        in_specs=in_specs,
        out_specs=[out_spec, out_spec],
        out_shape=[jax.ShapeDtypeStruct((batch * seq, A_WIDTH), F32)] * 2,
        compiler_params=_cparams(("parallel", "arbitrary", "arbitrary")),
        name=f"attn_prompt_g{g}",
    )(*([pa] * (len(in_specs) - 1)), table)


def _attn_prompt_group(pa, table, g, batch, seq):
    _, dil = DILATED_GROUPS[g]
    if dil > 1:
        return _attn_prompt_strided(pa, table, g, batch, seq)
    sub = seq // dil
    nqb = sub // Q_BLOCK
    wblk = PA_W // A_WIDTH
    pv = pa.reshape(batch, sub, dil * PA_W)

    def spec(off, prev):
        if prev:
            return pl.BlockSpec((None, Q_BLOCK, A_WIDTH), lambda b, r, i: (b, jnp.maximum(i - 1, 0), r * wblk + off + g))
        return pl.BlockSpec((None, Q_BLOCK, A_WIDTH), lambda b, r, i: (b, i, r * wblk + off + g))

    out_spec = pl.BlockSpec((None, Q_BLOCK, A_WIDTH), lambda b, r, i: (b, i, r))
    o, lse = pl.pallas_call(
        _attn_prompt_kernel,
        grid=(batch, dil, nqb),
        in_specs=[spec(0, False), spec(N_GROUPS, False), spec(N_GROUPS, True), spec(2 * N_GROUPS, False),
                  spec(2 * N_GROUPS, True), pl.BlockSpec((H_A, Q_BLOCK, 2 * Q_BLOCK), lambda b, r, i: (0, 0, 0))],
        out_specs=[out_spec, out_spec],
        out_shape=[jax.ShapeDtypeStruct((batch, sub, dil * A_WIDTH), F32)] * 2,
        compiler_params=_cparams(("parallel", "parallel", "arbitrary")),
        name=f"attn_prompt_g{g}",
    )(pv, pv, pv, pv, pv, table)
    return o.reshape(batch * seq, A_WIDTH), lse.reshape(batch * seq, A_WIDTH)


KV_ROWS = 2 * H_A


def _kv_pack_kernel(k_ref, v_ref, o_ref):
    parts = [k_ref[:, h * HEAD_DIM:(h + 1) * HEAD_DIM] for h in range(H_A)]
    parts += [v_ref[:, h * HEAD_DIM:(h + 1) * HEAD_DIM] for h in range(H_A)]
    o_ref[...] = jnp.stack(parts, axis=1)


def _kv_pack(pa, g, batch, seq, keep, tm):
    nblk, blk0, per_b = keep // tm, (seq - keep) // tm, seq // tm
    wblk = A_QKV_WIDTH // A_WIDTH
    out = pl.pallas_call(
        _kv_pack_kernel,
        grid=(batch, nblk),
        in_specs=[pl.BlockSpec((tm, A_WIDTH), lambda b, i: (b * per_b + blk0 + i, wblk + g)),
                  pl.BlockSpec((tm, A_WIDTH), lambda b, i: (b * per_b + blk0 + i, 2 * wblk + g))],
        out_specs=pl.BlockSpec((tm, KV_ROWS, HEAD_DIM), lambda b, i: (b * nblk + i, 0, 0)),
        out_shape=jax.ShapeDtypeStruct((batch * keep, KV_ROWS, HEAD_DIM), F32),
        compiler_params=_cparams(("parallel", "parallel")),
        name=f"kv_pack_g{g}_{keep}",
    )(pa, pa)
    return out.reshape(batch, keep, KV_ROWS, HEAD_DIM)


def _attn_sample_kernel(qkv_ref, n1_ref, n2_ref, n3_ref, c1_ref, c2_ref, c3_ref, tabc_ref, tabn_ref, o_ref, *,
                        dec_seq, combo_base):
    caches = (c1_ref, c2_ref, c3_ref)
    news = (n1_ref, n2_ref, n3_ref)
    zeros = jnp.zeros((H_A, HEAD_DIM), F32)
    for s in range(dec_seq):
        outs, lses = [], []
        for g, (_, dil) in enumerate(DILATED_GROUPS):
            rho, fl = s % dil, s // dil
            qm = jnp.concatenate([qkv_ref[s:s + 1, g * A_WIDTH + h * HEAD_DIM:g * A_WIDTH + (h + 1) * HEAD_DIM]
                                  for h in range(H_A)] + [zeros], axis=0)
            kc = caches[g][:, rho]
            kn = news[g][...]
            sc = jnp.sum(kc * qm[None], axis=-1, keepdims=True) * ATT_SCALE + tabc_ref[combo_base[g] + fl]
            sn = jnp.sum(kn * qm[None], axis=-1, keepdims=True) * ATT_SCALE + tabn_ref[g, s]
            m = jnp.maximum(jnp.max(sc, axis=0), jnp.max(sn, axis=0))
            pc = jnp.exp(sc - m[None])
            pn = jnp.exp(sn - m[None])
            l = jnp.sum(pc, axis=0) + jnp.sum(pn, axis=0)
            acc = jnp.sum(pltpu.roll(pc, H_A, 1) * kc, axis=0) + jnp.sum(pltpu.roll(pn, H_A, 1) * kn, axis=0)
            outs.append(acc / pltpu.roll(l, H_A, 0))
            lses.append(pltpu.roll(m + jnp.log(l), H_A, 0))
        mm = jnp.maximum(jnp.maximum(lses[0], lses[1]), lses[2])
        ws = [jnp.exp(x - mm) for x in lses]
        o_ref[s] = (ws[0] * outs[0] + ws[1] * outs[1] + ws[2] * outs[2]) / (ws[0] + ws[1] + ws[2])


def _attn_sample(pa, new_rows, caches, tabc, tabn, combo_base, batch, dec_seq):
    views, specs = [], []
    for g, (window, dil) in enumerate(DILATED_GROUPS):
        assert caches[g].shape[1] == window and dec_seq <= Q_BLOCK
        views.append(caches[g].reshape(batch, Q_BLOCK, dil, KV_ROWS, HEAD_DIM))
        used = min(dil, dec_seq)
        specs.append(pl.BlockSpec((None, Q_BLOCK, used, KV_ROWS, HEAD_DIM), lambda b: (b, 0, 0, 0, 0)))
    qkv = pa.reshape(batch, dec_seq, PA_W)
    new_spec = pl.BlockSpec((None, dec_seq, KV_ROWS, HEAD_DIM), lambda b: (b, 0, 0, 0))
    out = pl.pallas_call(
        functools.partial(_attn_sample_kernel, dec_seq=dec_seq, combo_base=combo_base),
        grid=(batch,),
        in_specs=[pl.BlockSpec((None, dec_seq, PA_W), lambda b: (b, 0, 0))] + [new_spec] * N_GROUPS + specs + [
            pl.BlockSpec(tabc.shape, lambda b: (0, 0, 0, 0)), pl.BlockSpec(tabn.shape, lambda b: (0, 0, 0, 0, 0))],
        out_specs=pl.BlockSpec((None, dec_seq, KV_ROWS, HEAD_DIM), lambda b: (b, 0, 0, 0)),
        out_shape=jax.ShapeDtypeStruct((batch, dec_seq, KV_ROWS, HEAD_DIM), F32),
        compiler_params=_cparams(("parallel",)),
        name="attn_sample",
    )(qkv, *new_rows, *views, tabc, tabn)
    return out[:, :, H_A:, :].reshape(batch * dec_seq, A_WIDTH)


def _sc_cache_shift(caches, drop):
    info = plsc.get_sparse_core_info()
    n_workers = info.num_cores * info.num_subcores
    batch = caches[0].shape[0]
    assert batch % n_workers == 0
    mesh = plsc.VectorSubcoreMesh(core_axis_name="c", subcore_axis_name="s")

    row_bytes = KV_ROWS * HEAD_DIM * 4
    chunks = []
    for c in caches:
        keep = c.shape[1] - drop
        ch = max(d for d in range(1, SC_CHUNK_BYTES // row_bytes + 1) if keep % d == 0)
        chunks.append(ch)
    buf_rows = max(chunks)

    @functools.partial(pl.kernel, mesh=mesh, out_type=[jax.ShapeDtypeStruct(c.shape, c.dtype) for c in caches],
                       scratch_types=[pltpu.VMEM((buf_rows, KV_ROWS, HEAD_DIM), caches[0].dtype)])
    def shift(*refs):
        srcs, dsts, buf = refs[:len(caches)], refs[len(caches):2 * len(caches)], refs[-1]
        wid = lax.axis_index("s") * info.num_cores + lax.axis_index("c")
        for j in range(batch // n_workers):
            b = wid * (batch // n_workers) + j
            for src, dst, ch in zip(srcs, dsts, chunks):
                stage = buf.at[pl.ds(0, ch)]

                @pl.loop(0, (src.shape[1] - drop) // ch)
                def _(i):
                    pltpu.sync_copy(src.at[b, pl.ds(drop + i * ch, ch)], stage)
                    pltpu.sync_copy(stage, dst.at[b, pl.ds(i * ch, ch)])

    return shift(*caches)


def _split3(x):
    hi = x.astype(BF16)
    r = x - hi.astype(F32)
    mid = r.astype(BF16)
    lo = (r - mid.astype(F32)).astype(BF16)
    return hi, mid, lo


def _gla_kernel(p_ref, wup_ref, gb_ref, ng_ref, s0_ref, o_ref, st_ref, *, chunk, tb):
    @pl.when(pl.program_id(1) == 0)
    def _():
        st_ref[...] = s0_ref[...]

    sub = min(GLA_SUB, chunk)
    kq = H_B * DK_B

    def rb(x):
        xb = x.astype(BF16)
        return xb if chunk >= 16 else xb.astype(F32)

    row = lax.broadcasted_iota(jnp.int32, (chunk, chunk), 0)
    colm = lax.broadcasted_iota(jnp.int32, (chunk, chunk), 1)
    tri = rb(jnp.where(row >= colm, 1.0, 0.0))
    for c in range(tb // chunk):
        rows = slice(c * chunk, (c + 1) * chunk)
        z = _dot(rb(p_ref[rows, PB_LR:PB_W]), rb(wup_ref[...])) + gb_ref[...]
        la = -(jnp.maximum(-z, 0.0) + jnp.log1p(jnp.exp(-jnp.abs(z)))) * (1.0 / GLA_TAU)
        b = functools.reduce(lambda u, w: u + w, [_dot(tri, rb(t)) for t in _split3(la)])
        blast = b[chunk - 1:chunk, :]
        q = p_ref[rows, 0:kq] * (DK_B ** -0.5)
        k = p_ref[rows, kq:2 * kq]
        qin = rb(q * jnp.exp(b))
        kst = rb(k * jnp.exp(blast - b))
        for h in range(H_B):
            ks = slice(h * DK_B, (h + 1) * DK_B)
            vs = slice(h * DV_B, (h + 1) * DV_B)
            st = st_ref[h]
            vb = rb(p_ref[rows, 2 * kq + h * DV_B:2 * kq + (h + 1) * DV_B])
            o_inter = _dot_nt(qin[:, ks], rb(st))
            parts = []
            for blk in range(chunk // sub):
                r0 = blk * sub
                n = r0 + sub
                ref_b = b[r0 - 1:r0, ks] if blk > 0 else jnp.zeros((1, DK_B), F32)
                qi = rb(q[r0:n, ks] * jnp.exp(b[r0:n, ks] - ref_b))
                ki = rb(k[0:n, ks] * jnp.exp(jnp.minimum(ref_b - b[0:n, ks], GLA_EXP_CLAMP)))
                a = _dot_nt(qi, ki)
                ti = lax.broadcasted_iota(jnp.int32, (sub, n), 0) + r0
                si = lax.broadcasted_iota(jnp.int32, (sub, n), 1)
                a = jnp.where(si <= ti, a, 0.0)
                parts.append(_dot(rb(a), vb[0:n]))
            o = o_inter + (jnp.concatenate(parts, axis=0) if len(parts) > 1 else parts[0])
            st_ref[h] = st * jnp.exp(blast[:, ks]) + _dot_tn(vb, kst[:, ks])
            on = o * lax.rsqrt(jnp.mean(o * o, axis=-1, keepdims=True) + RMS_EPS) * ng_ref[...]
            rg = p_ref[rows, 2 * kq + H_B * DV_B + h * DV_B:2 * kq + H_B * DV_B + (h + 1) * DV_B]
            o_ref[rows, vs] = (on * (rg * jax.nn.sigmoid(rg))).astype(BF16)


def _gla(pb, wup, gb, ng, s0t, batch, seq, chunk, tb):
    p3 = pb.reshape(batch, seq, PB_W)
    o, st = pl.pallas_call(
        functools.partial(_gla_kernel, chunk=chunk, tb=tb),
        grid=(batch, seq // tb),
        in_specs=[pl.BlockSpec((None, tb, PB_W), lambda b, i: (b, i, 0)),
                  pl.BlockSpec(wup.shape, lambda b, i: (0, 0)),
                  pl.BlockSpec(gb.shape, lambda b, i: (0, 0)),
                  pl.BlockSpec(ng.shape, lambda b, i: (0, 0)),
                  pl.BlockSpec((None, H_B, DV_B, DK_B), lambda b, i: (b, 0, 0, 0))],
        out_specs=[pl.BlockSpec((None, tb, H_B * DV_B), lambda b, i: (b, i, 0)),
                   pl.BlockSpec((None, H_B, DV_B, DK_B), lambda b, i: (b, 0, 0, 0))],
        out_shape=[jax.ShapeDtypeStruct((batch, seq, H_B * DV_B), BF16),
                   jax.ShapeDtypeStruct((batch, H_B, DV_B, DK_B), F32)],
        compiler_params=_cparams(("parallel", "arbitrary")),
        name=f"gla_c{chunk}",
    )(p3, wup, gb, ng, s0t)
    return o.reshape(batch * seq, H_B * DV_B), st


def _layer_norm(u, g, b):
    mu = jnp.mean(u, axis=-1, keepdims=True)
    d = u - mu
    var = jnp.mean(d * d, axis=-1, keepdims=True)
    return d * lax.rsqrt(var + LN_EPS) * g + b


def _merge_kernel(*refs, n_groups):
    x_ref = refs[0]
    oa_refs = refs[1:1 + 2 * n_groups] if n_groups > 1 else refs[1:2]
    rest = refs[1 + (2 * n_groups if n_groups > 1 else 1):]
    (ob_ref, pg_a_ref, pg_b_ref, wpa_ref, wpb_ref, wo_ref, g1_ref, b1_ref, rwh_ref, rwl_ref, rb_ref,
     x1_ref, x1p_ref, ti_ref, gt_ref) = rest
    if n_groups > 1:
        os_ = [r[...] for r in oa_refs[:n_groups]]
        ls = [r[...] for r in oa_refs[n_groups:]]
        mm = functools.reduce(jnp.maximum, ls)
        ws = [jnp.exp(x - mm) for x in ls]
        oa = sum(w * o for w, o in zip(ws, os_)) / sum(ws)
    else:
        oa = oa_refs[0][...]
    ya = _dot(oa.astype(BF16), wpa_ref[...])
    yb = _dot(ob_ref[...], wpb_ref[...])
    branch = jax.nn.sigmoid(pg_a_ref[...]) * ya + jax.nn.sigmoid(pg_b_ref[...]) * yb
    y = _dot(branch.astype(BF16), wo_ref[...])
    x1 = _layer_norm(DEEPNORM_ALPHA * x_ref[...] + y, g1_ref[...], b1_ref[...])
    x1_ref[...] = x1
    xh = x1.astype(BF16)
    xhf = xh.astype(F32)
    bits = lax.bitcast_convert_type(xhf, jnp.int32)
    half = D_MODEL // 2
    x1p_ref[...] = lax.shift_right_logical(bits[:, :half], 16) | bits[:, half:]
    xl = (x1 - xhf).astype(BF16)
    logits = _dot(xh, rwh_ref[...]) + _dot(xl, rwh_ref[...]) + _dot(xh, rwl_ref[...]) + rb_ref[...]
    lane = lax.broadcasted_iota(jnp.int32, logits.shape, 1)
    vals = logits
    top_v, top_i = [], []
    for _ in range(TOP_K):
        m = jnp.max(vals, axis=-1, keepdims=True)
        ik = jnp.min(jnp.where(vals == m, lane, LANE), axis=-1, keepdims=True)
        vals = jnp.where(lane == ik, -jnp.inf, vals)
        top_v.append(m)
        top_i.append(ik)
    es = [jnp.exp(v - top_v[0]) for v in top_v]
    tot = functools.reduce(lambda a, b: a + b, es)
    ti_ref[...] = jnp.concatenate(top_i, axis=1)
    gt_ref[...] = jnp.concatenate([e / tot for e in es], axis=1)


def _merge(x, oas, ob, pg, wpa, wpb, wo, g1, b1, rwh, rwl, rbp, tm):
    T = x.shape[0]
    n_groups = len(oas) // 2 if len(oas) > 1 else 1

    def row(w):
        return pl.BlockSpec((tm, w), lambda i: (i, 0))

    def const(a):
        return pl.BlockSpec(a.shape, lambda i: (0,) * a.ndim, pipeline_mode=pl.Buffered(1))

    in_specs = ([row(D_MODEL)] + [row(A_WIDTH)] * len(oas) + [row(H_B * DV_B), row(D_MODEL),
                pl.BlockSpec((tm, D_MODEL), lambda i: (i, 1))] + [const(a) for a in (wpa, wpb, wo, g1, b1, rwh, rwl, rbp)])
    return pl.pallas_call(
        functools.partial(_merge_kernel, n_groups=n_groups),
        grid=(T // tm,),
        in_specs=in_specs,
        out_specs=[row(D_MODEL), row(D_MODEL // 2), row(TOP_K), row(TOP_K)],
        out_shape=[jax.ShapeDtypeStruct((T, D_MODEL), F32), jax.ShapeDtypeStruct((T, D_MODEL // 2), jnp.int32),
                   jax.ShapeDtypeStruct((T, TOP_K), jnp.int32), jax.ShapeDtypeStruct((T, TOP_K), F32)],
        compiler_params=_cparams(("parallel",)),
        name=f"merge_g{n_groups}",
    )(x, *oas, ob, pg, pg, wpa, wpb, wo, g1, b1, rwh, rwl, rbp)


def _sc_gather(table, idx):
    info = plsc.get_sparse_core_info()
    n_workers = info.num_cores * info.num_subcores
    n, width = idx.shape[0], table.shape[1]
    per_worker = n // n_workers
    chunk = SC_CHUNK_BYTES // (width * table.dtype.itemsize)
    assert per_worker * n_workers == n and per_worker % chunk == 0 and chunk % 8 == 0
    mesh = plsc.VectorSubcoreMesh(core_axis_name="c", subcore_axis_name="s")

    @functools.partial(
        pl.kernel, mesh=mesh,
        out_type=jax.ShapeDtypeStruct((n, width), table.dtype),
        scratch_types=[pltpu.VMEM((chunk,), jnp.int32), pltpu.VMEM((chunk, width), table.dtype),
                       pltpu.SemaphoreType.DMA],
    )
    def gather(table_hbm, idx_hbm, out_hbm, idx_v, rows_v, sem):
        wid = lax.axis_index("s") * info.num_cores + lax.axis_index("c")
        base = wid * per_worker

        @pl.loop(0, per_worker // chunk)
        def _(c):
            off = pl.multiple_of(base + c * chunk, chunk)
            pltpu.sync_copy(idx_hbm.at[pl.ds(off, chunk)], idx_v)
            pltpu.async_copy(table_hbm.at[idx_v], rows_v, sem).wait()
            pltpu.sync_copy(rows_v, out_hbm.at[pl.ds(off, chunk)])

    return gather(table, idx)


def _sc_scatter_rows(table, idx, n_out):
    info = plsc.get_sparse_core_info()
    n_workers = info.num_cores * info.num_subcores
    n_idx, n = idx.shape
    width = table.shape[1]
    per_worker = n // n_workers
    max_rows = SC_SCATTER_BYTES // (width * table.dtype.itemsize)
    chunk = max(d for d in range(8, max_rows + 1, 8) if per_worker % d == 0)
    assert per_worker * n_workers == n and table.shape[0] == n
    mesh = plsc.VectorSubcoreMesh(core_axis_name="c", subcore_axis_name="s")

    @functools.partial(
        pl.kernel, mesh=mesh,
        out_type=jax.ShapeDtypeStruct((n_out, width), table.dtype),
        scratch_types=[pltpu.VMEM((chunk,), jnp.int32)] * n_idx + [pltpu.VMEM((chunk, width), table.dtype)],
    )
    def scatter(table_hbm, idx_hbm, out_hbm, *scratch):
        idx_vs, rows_v = scratch[:n_idx], scratch[n_idx]
        wid = lax.axis_index("s") * info.num_cores + lax.axis_index("c")
        base = wid * per_worker

        @pl.loop(0, per_worker // chunk)
        def _(c):
            off = pl.multiple_of(base + c * chunk, 8)
            pltpu.sync_copy(table_hbm.at[pl.ds(off, chunk)], rows_v)
            for k in range(n_idx):
                pltpu.sync_copy(idx_hbm.at[pl.ds(pl.multiple_of(k * n + off, 8), chunk)], idx_vs[k])
            for k in range(n_idx):
                pltpu.sync_copy(rows_v, out_hbm.at[idx_vs[k]])

    return scatter(table, idx.reshape(n_idx * n))


def _expert_kernel(be_ref, rows_ref, nu_ref, xs_ref, wg_ref, wu_ref, bg_ref, bu_ref, wd_ref, bd_ref, o_ref,
                   xb_ref, hid_ref):
    del be_ref, nu_ref
    i = pl.program_id(0)
    p = pl.program_id(1)
    nrows = rows_ref[i]
    half = D_MODEL // 2

    @pl.when(jnp.logical_and(p == 0, nrows > 0))
    def _():
        rid = lax.broadcasted_iota(jnp.int32, (MOE_BM, half), 0)
        packed = jnp.where(rid < nrows, xs_ref[...], 0)
        lo = lax.bitcast_convert_type(lax.shift_left(packed, 16), F32)
        hi = lax.bitcast_convert_type(packed & jnp.int32(-65536), F32)
        xb_ref[:, :half] = lo.astype(BF16)
        xb_ref[:, half:] = hi.astype(BF16)

    for sb in range(MOE_BM // MOE_SB):
        r = slice(sb * MOE_SB, (sb + 1) * MOE_SB)
        live = sb * MOE_SB < nrows

        @pl.when(jnp.logical_and(p < MOE_NF, live))
        def _():
            x = xb_ref[r, :]
            g = jnp.minimum(_dot(x, wg_ref[...].astype(BF16)) + bg_ref[...], SWIGLU_LIMIT)
            u = jnp.clip(_dot(x, wu_ref[...].astype(BF16)) + bu_ref[...], -SWIGLU_LIMIT, SWIGLU_LIMIT)
            hid_ref[p, r, :] = ((u + 1.0) * g * jax.nn.sigmoid(SWIGLU_ALPHA * g)).astype(BF16)

        @pl.when(jnp.logical_and(p >= MOE_NF, live))
        def _():
            y = bd_ref[...]
            for f in range(MOE_NF):
                y = y + _dot(hid_ref[f, r, :], wd_ref[f * MOE_TF:(f + 1) * MOE_TF, :].astype(BF16))
            o_ref[r, :] = y

        @pl.when(jnp.logical_and(p >= MOE_NF, jnp.logical_not(live)))
        def _():
            o_ref[r, :] = jnp.zeros((MOE_SB, MOE_TN), F32)


def _experts(xs, block_expert, block_rows, n_used, w_gate_up, b_gate_up, w_down, b_down):
    nb = xs.shape[0] // MOE_BM
    nf, nn = MOE_NF, D_MODEL // MOE_TN

    def fsel(i, p, nu):
        return jnp.where(i < nu[0], jnp.minimum(p, nf - 1), nf - 1)

    def down_map(i, p, be, rw, nu):
        parked = p < nf - 1
        e = jnp.where(parked, be[jnp.maximum(i - 1, 0)], be[i])
        n_live = jnp.where(parked, jnp.where(i > 0, nn - 1, 0), jnp.maximum(p - nf, 0))
        return (e, 0, jnp.where(i < nu[0], n_live, nn - 1))

    def xs_map(i, p, be, rw, nu):
        return (jnp.minimum(jnp.where(p > 0, i + 1, i), nu[0] - 1), 0)

    grid_spec = pltpu.PrefetchScalarGridSpec(
        num_scalar_prefetch=3,
        grid=(nb, nf + nn),
        in_specs=[
            pl.BlockSpec((MOE_BM, D_MODEL // 2), xs_map),
            pl.BlockSpec((None, D_MODEL, MOE_TF), lambda i, p, be, rw, nu: (be[i], 0, fsel(i, p, nu))),
            pl.BlockSpec((None, D_MODEL, MOE_TF), lambda i, p, be, rw, nu: (be[i], 0, nf + fsel(i, p, nu))),
            pl.BlockSpec((None, 1, MOE_TF), lambda i, p, be, rw, nu: (be[i], 0, fsel(i, p, nu))),
            pl.BlockSpec((None, 1, MOE_TF), lambda i, p, be, rw, nu: (be[i], 0, nf + fsel(i, p, nu))),
            pl.BlockSpec((None, D_FF, MOE_TN), down_map),
            pl.BlockSpec((None, 1, MOE_TN), down_map),
        ],
        out_specs=pl.BlockSpec((MOE_BM, MOE_TN), lambda i, p, be, rw, nu: (i, jnp.maximum(p - nf, 0))),
        scratch_shapes=[pltpu.VMEM((MOE_BM, D_MODEL), BF16), pltpu.VMEM((nf, MOE_BM, MOE_TF), BF16)],
    )
    bgu = b_gate_up.reshape(N_EXPERTS, 1, 2 * D_FF)
    bd = b_down.reshape(N_EXPERTS, 1, D_MODEL)
    return pl.pallas_call(
        _expert_kernel,
        grid_spec=grid_spec,
        out_shape=jax.ShapeDtypeStruct((xs.shape[0], D_MODEL), F32),
        compiler_params=_cparams(("arbitrary", "arbitrary")),
        name="moe_experts",
    )(block_expert, block_rows, n_used, xs, w_gate_up, w_gate_up, bgu, bgu, w_down, bd)


def _route(top_i, n_blocks):
    T = top_i.shape[0]
    expert = top_i.reshape(T * TOP_K)
    onehot = (expert[:, None] == jnp.arange(N_EXPERTS, dtype=jnp.int32)[None, :]).astype(jnp.int32)
    csum = jnp.cumsum(onehot, axis=0)
    counts = csum[-1]
    rank = jnp.take_along_axis(csum, expert[:, None], axis=1)[:, 0] - 1
    bpe = (counts + MOE_BM - 1) // MOE_BM
    bend = jnp.cumsum(bpe)
    bstart = bend - bpe
    dest = (bstart[expert] * MOE_BM + rank).astype(jnp.int32)
    n_used = bend[-1]
    blk = jnp.arange(n_blocks, dtype=jnp.int32)
    be = jnp.minimum(jnp.searchsorted(bend, jnp.minimum(blk, n_used - 1), side="right"), N_EXPERTS - 1).astype(jnp.int32)
    rows = jnp.clip(counts[be] - (blk - bstart[be]) * MOE_BM, 0, MOE_BM)
    rows = jnp.where(blk < n_used, rows, 0).astype(jnp.int32)
    return dest.reshape(T, TOP_K).T, be, rows, n_used.reshape(1).astype(jnp.int32)


def _combine_kernel(x1_ref, ge_ref, gt_ref, g2_ref, b2_ref, o_ref):
    gt = gt_ref[...]
    m = gt[:, 0:1] * ge_ref[0]
    for k in range(1, TOP_K):
        m = m + gt[:, k:k + 1] * ge_ref[k]
    o_ref[...] = _layer_norm(DEEPNORM_ALPHA * x1_ref[...] + m, g2_ref[...], b2_ref[...])


def _combine(x1, ge, gate, g2, b2, row0, tm):
    n_rows = x1.shape[0]
    b0 = row0 // tm
    return pl.pallas_call(
        _combine_kernel,
        grid=(n_rows // tm,),
        in_specs=[pl.BlockSpec((tm, D_MODEL), lambda i: (i, 0)),
                  pl.BlockSpec((TOP_K, tm, D_MODEL), lambda i: (0, b0 + i, 0)),
                  pl.BlockSpec((tm, TOP_K), lambda i: (b0 + i, 0)),
                  pl.BlockSpec(g2.shape, lambda i: (0, 0)), pl.BlockSpec(b2.shape, lambda i: (0, 0))],
        out_specs=pl.BlockSpec((tm, D_MODEL), lambda i: (i, 0)),
        out_shape=jax.ShapeDtypeStruct((n_rows, D_MODEL), F32),
        compiler_params=_cparams(("parallel",)),
        name="moe_combine",
    )(x1, ge, gate, g2, b2)


def _layer(xp, xs, caches, state, w_in, rel_bias, gla_w_up, gla_b, gla_norm_g, w_pa, w_pb, w_o, ln1_g, ln1_b,
           router_w, router_b, w_gate_up, b_gate_up, w_down, b_down, ln2_g, ln2_b):
    batch, seq, _ = xp.shape
    dbatch, dseq, _ = xs.shape
    tp, ts = batch * seq, dbatch * dseq
    xp2, xs2 = xp.reshape(tp, D_MODEL), xs.reshape(ts, D_MODEL)

    o_b0, o_lr, o_g = PA_W, PA_W + PB_LR, PA_W + PB_LR + GLA_RANK
    w_a = w_in[:, :PA_W].astype(BF16)
    w_b = jnp.concatenate([w_in[:, o_b0:o_g], jnp.zeros((D_MODEL, LANE - GLA_RANK), F32)], axis=1).astype(BF16)
    w_g = w_in[:, o_g:].astype(BF16)
    wup = jnp.concatenate([gla_w_up, jnp.zeros((LANE - GLA_RANK, H_B * DK_B), F32)], axis=0).astype(BF16)
    gb = gla_b.reshape(1, H_B * DK_B)
    ng = gla_norm_g.reshape(1, DV_B)
    wpa, wpb, wo = w_pa.astype(BF16), w_pb.astype(BF16), w_o.astype(BF16)
    g1, b1 = ln1_g.reshape(1, D_MODEL), ln1_b.reshape(1, D_MODEL)
    g2, b2 = ln2_g.reshape(1, D_MODEL), ln2_b.reshape(1, D_MODEL)
    rw = jnp.concatenate([router_w, jnp.zeros((D_MODEL, LANE - N_EXPERTS), F32)], axis=1)
    rwh = rw.astype(BF16)
    rwl = (rw - rwh.astype(F32)).astype(BF16)
    rbp = jnp.concatenate([router_b, jnp.full((LANE - N_EXPERTS,), NEG, F32)]).reshape(1, LANE)
    caches8 = [c.reshape(dbatch, c.shape[1], KV_ROWS, HEAD_DIM) for c in caches]

    pa_p = _project(xp2, w_a, 512, A_QKV_WIDTH, "proj_a_prompt")
    pb_p = _project(xp2, w_b, 512, PB_W, "proj_b_prompt")
    pg_p = _project(xp2, w_g, 512, 1024, "proj_g_prompt")
    oas, lses = [], []
    for g in range(N_GROUPS):
        o, lse = _attn_prompt_group(pa_p, _prompt_table(rel_bias, g), g, batch, seq)
        oas.append(o)
        lses.append(lse)
    ob_p, st_p = _gla(pb_p, wup, gb, ng, jnp.zeros((batch, H_B, DV_B, DK_B), F32), batch, seq, GLA_CHUNK, 256)
    x1_p, x1p_p, ti_p, gt_p = _merge(xp2, oas + lses, ob_p, pg_p, wpa, wpb, wo, g1, b1, rwh, rwl, rbp, 256)

    pa_s = _project(xs2, w_a, ts, A_QKV_WIDTH, "proj_a_sample")
    pb_s = _project(xs2, w_b, ts, PB_W, "proj_b_sample")
    pg_s = _project(xs2, w_g, ts, 1024, "proj_g_sample")
    tabc, tabn, combo_base = _sample_tables(rel_bias, dseq)
    new_rows = [_kv_pack(pa_s, g, dbatch, dseq, dseq, dseq) for g in range(N_GROUPS)]
    oa_s = _attn_sample(pa_s, new_rows, caches8, tabc, tabn, combo_base, dbatch, dseq)
    chunk_s = int(np.gcd(dseq, GLA_CHUNK))
    ob_s, st_s = _gla(pb_s, wup, gb, ng, jnp.swapaxes(state, -1, -2), dbatch, dseq, chunk_s, dseq)
    x1_s, x1p_s, ti_s, gt_s = _merge(xs2, [oa_s], ob_s, pg_s, wpa, wpb, wo, g1, b1, rwh, rwl, rbp, ts)

    x1p = jnp.concatenate([x1p_p, x1p_s], axis=0)
    top_i = jnp.concatenate([ti_p, ti_s], axis=0)
    gate = jnp.concatenate([gt_p, gt_s], axis=0)
    t_all = tp + ts
    n_blocks = -(-(t_all * TOP_K) // MOE_BM) + N_EXPERTS
    dest, be, rows, n_used = _route(top_i, n_blocks)
    xsorted = _sc_scatter_rows(x1p, dest, n_blocks * MOE_BM)
    eo = _experts(xsorted, be, rows, n_used, w_gate_up, b_gate_up, w_down, b_down)
    ge = _sc_gather(eo, dest.reshape(TOP_K * t_all)).reshape(TOP_K, t_all, D_MODEL)
    y_p = _combine(x1_p, ge, gate, g2, b2, 0, 256)
    y_s = _combine(x1_s, ge, gate, g2, b2, tp, 256)

    shifted = _sc_cache_shift(caches8, dseq)
    bufs_p, bufs_s = [], []
    for g, (window, _) in enumerate(DILATED_GROUPS):
        keep = min(window, seq)
        bufs_p.append(_kv_pack(pa_p, g, batch, seq, keep, Q_BLOCK).reshape(batch, keep, 2, H_A, HEAD_DIM))
        clen = caches[g].shape[1]
        assert clen == window and dseq <= clen
        buf = lax.dynamic_update_slice(shifted[g], new_rows[g], (0, clen - dseq, 0, 0))
        bufs_s.append(buf.reshape(dbatch, clen, 2, H_A, HEAD_DIM))
    return (y_p.reshape(batch, seq, D_MODEL), y_s.reshape(dbatch, dseq, D_MODEL), bufs_p, jnp.swapaxes(st_p, -1, -2),
            bufs_s, jnp.swapaxes(st_s, -1, -2))


def kernel(x_prompt, x_sample, cache_a1_kv, cache_a2_kv, cache_a3_kv, state_b_s, w_in, rel_bias, gla_w_up, gla_b,
           gla_norm_g, w_pa, w_pb, w_o, ln1_g, ln1_b, router_w, router_b, w_gate_up, b_gate_up, w_down, b_down,
           ln2_g, ln2_b):
    assert w_in.shape[0] == DEPTH
    yp, ys, bufs_p, st_p, bufs_s, st_s = _layer(
        x_prompt, x_sample, (cache_a1_kv[0], cache_a2_kv[0], cache_a3_kv[0]), state_b_s[0], w_in[0], rel_bias,
        gla_w_up[0], gla_b[0], gla_norm_g[0], w_pa[0], w_pb[0], w_o[0], ln1_g[0], ln1_b[0], router_w[0], router_b[0],
        w_gate_up[0], b_gate_up[0], w_down[0], b_down[0], ln2_g[0], ln2_b[0])
    return (yp, ys, bufs_p[0][None], bufs_p[1][None], bufs_p[2][None], st_p[None],
            bufs_s[0][None], bufs_s[1][None], bufs_s[2][None], st_s[None].astype(state_b_s.dtype))
```

```python
import functools

import numpy as np
import jax
import jax.numpy as jnp
from jax import lax
from jax.experimental import pallas as pl
from jax.experimental.pallas import tpu as pltpu
from jax.experimental.pallas import tpu_sc as plsc

F32 = jnp.float32
BF16 = jnp.bfloat16

D_MODEL = 2048
HEAD_DIM = 128
DILATED_GROUPS = ((128, 1), (512, 4), (2048, 16))
N_GROUPS = 3
H_A = 4
A_WIDTH = H_A * HEAD_DIM
A_QKV_WIDTH = N_GROUPS * A_WIDTH
Q_BLOCK = 128
N_BUCKETS = 32
REL_MAX_DIST = 2048
H_B = 4
DK_B = 64
DV_B = 128
GLA_RANK = 16
GLA_TAU = 16.0
GLA_CHUNK = 64
GLA_SUB = 16
GLA_EXP_CLAMP = 80.0
N_EXPERTS = 32
TOP_K = 4
D_FF = 2048
SWIGLU_LIMIT = 7.0
SWIGLU_ALPHA = 1.702
LN_EPS = 1e-5
RMS_EPS = 1e-6
DEPTH = 1
DEEPNORM_ALPHA = (2.0 * DEPTH) ** 0.25
ATT_SCALE = HEAD_DIM ** -0.5
NEG = float(np.finfo(np.float32).min)

VMEM_LIMIT = 56 * 1024 * 1024
LANE = 128

PA_W = 3 * A_QKV_WIDTH
PB_LR = H_B * DK_B * 2 + H_B * DV_B * 2
PB_W = PB_LR + LANE
PG_W = 2 * D_MODEL

PROJ_TM = 1024

MOE_BM = 2560
MOE_SB = 320
MOE_TF = 512
MOE_NF = D_FF // MOE_TF
MOE_TN = 256
SC_CHUNK_BYTES = 128 * 1024
SC_SCATTER_BYTES = 160 * 1024


def _cparams(sem):
    return pltpu.CompilerParams(dimension_semantics=sem, vmem_limit_bytes=VMEM_LIMIT)


def _dot(a, b):
    return jnp.dot(a, b, preferred_element_type=F32)


def _dot_nt(a, b):
    return lax.dot_general(a, b, (((1,), (1,)), ((), ())), preferred_element_type=F32)


def _dot_tn(a, b):
    return lax.dot_general(a, b, (((0,), (0,)), ((), ())), preferred_element_type=F32)


def _proj_kernel(x_ref, w_ref, o_ref, xb_ref):
    @pl.when(pl.program_id(1) == 0)
    def _():
        xb_ref[...] = x_ref[...].astype(BF16)

    o_ref[...] = _dot(xb_ref[...], w_ref[...])


def _project(x, w, tm, tn, name):
    T, D = x.shape
    N = w.shape[1]
    return pl.pallas_call(
        _proj_kernel,
        grid=(T // tm, N // tn),
        in_specs=[pl.BlockSpec((tm, D), lambda i, j: (i, 0)), pl.BlockSpec((D, tn), lambda i, j: (0, j))],
        out_specs=pl.BlockSpec((tm, tn), lambda i, j: (i, j)),
        out_shape=jax.ShapeDtypeStruct((T, N), F32),
        scratch_shapes=[pltpu.VMEM((tm, D), BF16)],
        compiler_params=_cparams(("parallel", "arbitrary")),
        name=name,
    )(x, w)


def _t5_bucket(dist):
    max_exact = N_BUCKETS // 2
    d = np.maximum(dist, 1).astype(np.float32)
    large = max_exact + (np.log(d / max_exact) / np.log(REL_MAX_DIST / max_exact) * (N_BUCKETS - max_exact)).astype(np.int32)
    large = np.minimum(large, N_BUCKETS - 1)
    return np.where(dist < max_exact, dist, large).astype(np.int32)


def _bias_lookup(rel_bias, g, j, valid):
    _, dil = DILATED_GROUPS[g]
    bucket = _t5_bucket(dil * np.clip(j, 0, Q_BLOCK))
    onehot = bucket[None] == np.arange(N_BUCKETS).reshape((N_BUCKETS,) + (1,) * j.ndim)
    rb = rel_bias[:, g * H_A:(g + 1) * H_A].astype(F32).T.reshape((H_A, N_BUCKETS) + (1,) * j.ndim)
    vals = jnp.sum(jnp.where(onehot[None], rb, 0.0), axis=1)
    return jnp.where(valid[None], vals, NEG)


def _prompt_table(rel_bias, g):
    qi = np.arange(Q_BLOCK)[:, None]
    kj = np.arange(2 * Q_BLOCK)[None, :]
    j = Q_BLOCK + qi - kj
    return _bias_lookup(rel_bias, g, j, (j >= 0) & (j <= Q_BLOCK))


def _sample_tables(rel_bias, dec_seq):
    m = np.arange(Q_BLOCK)
    tabc, combo_base = [], []
    for g, (_, dil) in enumerate(DILATED_GROUPS):
        combo_base.append(len(tabc))
        for fl in range((dec_seq - 1) // dil + 1):
            j = Q_BLOCK + fl - m
            col = _bias_lookup(rel_bias, g, j, j <= Q_BLOCK).T
            col = jnp.concatenate([col, jnp.zeros_like(col)], axis=1)
            tabc.append(jnp.broadcast_to(col[:, :, None], (Q_BLOCK, 2 * H_A, LANE)))
    tabn = []
    s = np.arange(dec_seq)[:, None]
    sp = np.arange(dec_seq)[None, :]
    for g, (_, dil) in enumerate(DILATED_GROUPS):
        diff = s - sp
        t = _bias_lookup(rel_bias, g, diff // dil, (diff >= 0) & (diff % dil == 0))
        t = jnp.transpose(t, (1, 2, 0))
        t = jnp.concatenate([t, jnp.zeros_like(t)], axis=2)
        tabn.append(jnp.broadcast_to(t[..., None], (dec_seq, dec_seq, 2 * H_A, LANE)))
    return jnp.stack(tabc), jnp.stack(tabn), tuple(combo_base)


def _attn_prompt_kernel(q_ref, kc_ref, kp_ref, vc_ref, vp_ref, tab_ref, o_ref, lse_ref):
    has_prev = pl.program_id(2) > 0
    for h in range(H_A):
        sl = slice(h * HEAD_DIM, (h + 1) * HEAD_DIM)
        q = q_ref[:, sl].astype(BF16)
        sc = _dot_nt(q, kc_ref[:, sl].astype(BF16)) * ATT_SCALE + tab_ref[h, :, Q_BLOCK:]
        sp = _dot_nt(q, kp_ref[:, sl].astype(BF16)) * ATT_SCALE + tab_ref[h, :, :Q_BLOCK]
        sp = jnp.where(has_prev, sp, NEG)
        m = jnp.maximum(jnp.max(sc, axis=-1, keepdims=True), jnp.max(sp, axis=-1, keepdims=True))
        pc = jnp.exp(sc - m)
        pp = jnp.exp(sp - m)
        l = jnp.sum(pc, axis=-1, keepdims=True) + jnp.sum(pp, axis=-1, keepdims=True)
        inv = 1.0 / l
        o = _dot((pc * inv).astype(BF16), vc_ref[:, sl].astype(BF16)) + _dot((pp * inv).astype(BF16), vp_ref[:, sl].astype(BF16))
        o_ref[:, sl] = o
        lse_ref[:, sl] = jnp.broadcast_to(m + jnp.log(l), (Q_BLOCK, HEAD_DIM))


def _attn_prompt_strided_kernel(*refs, dil, with_prev):
    if with_prev:
        q_ref, kc_ref, kp_ref, vc_ref, vp_ref, tab_ref, o_ref, lse_ref = refs
    else:
        q_ref, kc_ref, vc_ref, tab_ref, o_ref, lse_ref = refs
    has_prev = pl.program_id(1) > 0

    def body(r, carry):
        idx = pl.ds(r, Q_BLOCK, stride=dil)
        q = q_ref[idx, :].astype(BF16)
        sc = _dot_nt(q, kc_ref[idx, :].astype(BF16)) * ATT_SCALE + tab_ref[:, Q_BLOCK:]
        m = jnp.max(sc, axis=-1, keepdims=True)
        if with_prev:
            sp = _dot_nt(q, kp_ref[idx, :].astype(BF16)) * ATT_SCALE + tab_ref[:, :Q_BLOCK]
            sp = jnp.where(has_prev, sp, NEG)
            m = jnp.maximum(m, jnp.max(sp, axis=-1, keepdims=True))
        pc = jnp.exp(sc - m)
        l = jnp.sum(pc, axis=-1, keepdims=True)
        if with_prev:
            pp = jnp.exp(sp - m)
            l = l + jnp.sum(pp, axis=-1, keepdims=True)
        inv = 1.0 / l
        o = _dot((pc * inv).astype(BF16), vc_ref[idx, :].astype(BF16))
        if with_prev:
            o = o + _dot((pp * inv).astype(BF16), vp_ref[idx, :].astype(BF16))
        o_ref[idx, :] = o
        lse_ref[idx, :] = jnp.broadcast_to(m + jnp.log(l), (Q_BLOCK, HEAD_DIM))
        return carry

    lax.fori_loop(0, dil, body, 0, unroll=min(dil, 4))


def _attn_prompt_strided(pa, table, g, batch, seq):
    _, dil = DILATED_GROUPS[g]
    rows = dil * Q_BLOCK
    nblk = seq // rows
    with_prev = nblk > 1
    hcols = A_QKV_WIDTH // HEAD_DIM

    def spec(sec, prev):
        if prev:
            return pl.BlockSpec((rows, HEAD_DIM), lambda b, i, h: (b * nblk + jnp.maximum(i - 1, 0), sec * hcols + g * H_A + h))
        return pl.BlockSpec((rows, HEAD_DIM), lambda b, i, h: (b * nblk + i, sec * hcols + g * H_A + h))

    in_specs = [spec(0, False), spec(1, False)] + ([spec(1, True)] if with_prev else []) + [spec(2, False)] + (
        [spec(2, True)] if with_prev else []) + [pl.BlockSpec((None, Q_BLOCK, 2 * Q_BLOCK), lambda b, i, h: (h, 0, 0))]
    out_spec = pl.BlockSpec((rows, HEAD_DIM), lambda b, i, h: (b * nblk + i, h))
    return pl.pallas_call(
        functools.partial(_attn_prompt_strided_kernel, dil=dil, with_prev=with_prev),
        grid=(batch, nblk, H_A),
        in_specs=in_specs,
        out_specs=[out_spec, out_spec],
        out_shape=[jax.ShapeDtypeStruct((batch * seq, A_WIDTH), F32)] * 2,
        compiler_params=_cparams(("parallel", "arbitrary", "arbitrary")),
        name=f"attn_prompt_g{g}",
    )(*([pa] * (len(in_specs) - 1)), table)


def _attn_prompt_group(pa, table, g, batch, seq):
    _, dil = DILATED_GROUPS[g]
    if dil > 1:
        return _attn_prompt_strided(pa, table, g, batch, seq)
    sub = seq // dil
    nqb = sub // Q_BLOCK
    wblk = PA_W // A_WIDTH
    pv = pa.reshape(batch, sub, dil * PA_W)

    def spec(off, prev):
        if prev:
            return pl.BlockSpec((None, Q_BLOCK, A_WIDTH), lambda b, r, i: (b, jnp.maximum(i - 1, 0), r * wblk + off + g))
        return pl.BlockSpec((None, Q_BLOCK, A_WIDTH), lambda b, r, i: (b, i, r * wblk + off + g))

    out_spec = pl.BlockSpec((None, Q_BLOCK, A_WIDTH), lambda b, r, i: (b, i, r))
    o, lse = pl.pallas_call(
        _attn_prompt_kernel,
        grid=(batch, dil, nqb),
        in_specs=[spec(0, False), spec(N_GROUPS, False), spec(N_GROUPS, True), spec(2 * N_GROUPS, False),
                  spec(2 * N_GROUPS, True), pl.BlockSpec((H_A, Q_BLOCK, 2 * Q_BLOCK), lambda b, r, i: (0, 0, 0))],
        out_specs=[out_spec, out_spec],
        out_shape=[jax.ShapeDtypeStruct((batch, sub, dil * A_WIDTH), F32)] * 2,
        compiler_params=_cparams(("parallel", "parallel", "arbitrary")),
        name=f"attn_prompt_g{g}",
    )(pv, pv, pv, pv, pv, table)
    return o.reshape(batch * seq, A_WIDTH), lse.reshape(batch * seq, A_WIDTH)


KV_ROWS = 2 * H_A


def _kv_pack_kernel(k_ref, v_ref, o_ref):
    parts = [k_ref[:, h * HEAD_DIM:(h + 1) * HEAD_DIM] for h in range(H_A)]
    parts += [v_ref[:, h * HEAD_DIM:(h + 1) * HEAD_DIM] for h in range(H_A)]
    o_ref[...] = jnp.stack(parts, axis=1)


def _kv_pack(pa, g, batch, seq, keep, tm):
    nblk, blk0, per_b = keep // tm, (seq - keep) // tm, seq // tm
    wblk = A_QKV_WIDTH // A_WIDTH
    out = pl.pallas_call(
        _kv_pack_kernel,
        grid=(batch, nblk),
        in_specs=[pl.BlockSpec((tm, A_WIDTH), lambda b, i: (b * per_b + blk0 + i, wblk + g)),
                  pl.BlockSpec((tm, A_WIDTH), lambda b, i: (b * per_b + blk0 + i, 2 * wblk + g))],
        out_specs=pl.BlockSpec((tm, KV_ROWS, HEAD_DIM), lambda b, i: (b * nblk + i, 0, 0)),
        out_shape=jax.ShapeDtypeStruct((batch * keep, KV_ROWS, HEAD_DIM), F32),
        compiler_params=_cparams(("parallel", "parallel")),
        name=f"kv_pack_g{g}_{keep}",
    )(pa, pa)
    return out.reshape(batch, keep, KV_ROWS, HEAD_DIM)


def _attn_sample_kernel(qkv_ref, n1_ref, n2_ref, n3_ref, c1_ref, c2_ref, c3_ref, tabc_ref, tabn_ref, o_ref, *,
                        dec_seq, combo_base):
    caches = (c1_ref, c2_ref, c3_ref)
    news = (n1_ref, n2_ref, n3_ref)
    zeros = jnp.zeros((H_A, HEAD_DIM), F32)
    for s in range(dec_seq):
        outs, lses = [], []
        for g, (_, dil) in enumerate(DILATED_GROUPS):
            rho, fl = s % dil, s // dil
            qm = jnp.concatenate([qkv_ref[s:s + 1, g * A_WIDTH + h * HEAD_DIM:g * A_WIDTH + (h + 1) * HEAD_DIM]
                                  for h in range(H_A)] + [zeros], axis=0)
            kc = caches[g][:, rho]
            kn = news[g][...]
            sc = jnp.sum(kc * qm[None], axis=-1, keepdims=True) * ATT_SCALE + tabc_ref[combo_base[g] + fl]
            sn = jnp.sum(kn * qm[None], axis=-1, keepdims=True) * ATT_SCALE + tabn_ref[g, s]
            m = jnp.maximum(jnp.max(sc, axis=0), jnp.max(sn, axis=0))
            pc = jnp.exp(sc - m[None])
            pn = jnp.exp(sn - m[None])
            l = jnp.sum(pc, axis=0) + jnp.sum(pn, axis=0)
            acc = jnp.sum(pltpu.roll(pc, H_A, 1) * kc, axis=0) + jnp.sum(pltpu.roll(pn, H_A, 1) * kn, axis=0)
            outs.append(acc / pltpu.roll(l, H_A, 0))
            lses.append(pltpu.roll(m + jnp.log(l), H_A, 0))
        mm = jnp.maximum(jnp.maximum(lses[0], lses[1]), lses[2])
        ws = [jnp.exp(x - mm) for x in lses]
        o_ref[s] = (ws[0] * outs[0] + ws[1] * outs[1] + ws[2] * outs[2]) / (ws[0] + ws[1] + ws[2])


def _attn_sample(pa, new_rows, caches, tabc, tabn, combo_base, batch, dec_seq):
    views, specs = [], []
    for g, (window, dil) in enumerate(DILATED_GROUPS):
        assert caches[g].shape[1] == window and dec_seq <= Q_BLOCK
        views.append(caches[g].reshape(batch, Q_BLOCK, dil, KV_ROWS, HEAD_DIM))
        used = min(dil, dec_seq)
        specs.append(pl.BlockSpec((None, Q_BLOCK, used, KV_ROWS, HEAD_DIM), lambda b: (b, 0, 0, 0, 0)))
    qkv = pa.reshape(batch, dec_seq, PA_W)
    new_spec = pl.BlockSpec((None, dec_seq, KV_ROWS, HEAD_DIM), lambda b: (b, 0, 0, 0))
    out = pl.pallas_call(
        functools.partial(_attn_sample_kernel, dec_seq=dec_seq, combo_base=combo_base),
        grid=(batch,),
        in_specs=[pl.BlockSpec((None, dec_seq, PA_W), lambda b: (b, 0, 0))] + [new_spec] * N_GROUPS + specs + [
            pl.BlockSpec(tabc.shape, lambda b: (0, 0, 0, 0)), pl.BlockSpec(tabn.shape, lambda b: (0, 0, 0, 0, 0))],
        out_specs=pl.BlockSpec((None, dec_seq, KV_ROWS, HEAD_DIM), lambda b: (b, 0, 0, 0)),
        out_shape=jax.ShapeDtypeStruct((batch, dec_seq, KV_ROWS, HEAD_DIM), F32),
        compiler_params=_cparams(("parallel",)),
        name="attn_sample",
    )(qkv, *new_rows, *views, tabc, tabn)
    return out[:, :, H_A:, :].reshape(batch * dec_seq, A_WIDTH)


def _sc_cache_shift(caches, drop):
    info = plsc.get_sparse_core_info()
    n_workers = info.num_cores * info.num_subcores
    batch = caches[0].shape[0]
    assert batch % n_workers == 0
    mesh = plsc.VectorSubcoreMesh(core_axis_name="c", subcore_axis_name="s")

    row_bytes = KV_ROWS * HEAD_DIM * 4
    chunks = []
    for c in caches:
        keep = c.shape[1] - drop
        ch = max(d for d in range(1, SC_CHUNK_BYTES // row_bytes + 1) if keep % d == 0)
        chunks.append(ch)
    buf_rows = max(chunks)

    @functools.partial(pl.kernel, mesh=mesh, out_type=[jax.ShapeDtypeStruct(c.shape, c.dtype) for c in caches],
                       scratch_types=[pltpu.VMEM((buf_rows, KV_ROWS, HEAD_DIM), caches[0].dtype)])
    def shift(*refs):
        srcs, dsts, buf = refs[:len(caches)], refs[len(caches):2 * len(caches)], refs[-1]
        wid = lax.axis_index("s") * info.num_cores + lax.axis_index("c")
        for j in range(batch // n_workers):
            b = wid * (batch // n_workers) + j
            for src, dst, ch in zip(srcs, dsts, chunks):
                stage = buf.at[pl.ds(0, ch)]

                @pl.loop(0, (src.shape[1] - drop) // ch)
                def _(i):
                    pltpu.sync_copy(src.at[b, pl.ds(drop + i * ch, ch)], stage)
                    pltpu.sync_copy(stage, dst.at[b, pl.ds(i * ch, ch)])

    return shift(*caches)


def _split3(x):
    hi = x.astype(BF16)
    r = x - hi.astype(F32)
    mid = r.astype(BF16)
    lo = (r - mid.astype(F32)).astype(BF16)
    return hi, mid, lo


def _gla_kernel(p_ref, wup_ref, gb_ref, ng_ref, s0_ref, o_ref, st_ref, *, chunk, tb):
    @pl.when(pl.program_id(1) == 0)
    def _():
        st_ref[...] = s0_ref[...]

    sub = min(GLA_SUB, chunk)
    kq = H_B * DK_B

    def rb(x):
        xb = x.astype(BF16)
        return xb if chunk >= 16 else xb.astype(F32)

    row = lax.broadcasted_iota(jnp.int32, (chunk, chunk), 0)
    colm = lax.broadcasted_iota(jnp.int32, (chunk, chunk), 1)
    tri = rb(jnp.where(row >= colm, 1.0, 0.0))
    for c in range(tb // chunk):
        rows = slice(c * chunk, (c + 1) * chunk)
        z = _dot(rb(p_ref[rows, PB_LR:PB_W]), rb(wup_ref[...])) + gb_ref[...]
        la = -(jnp.maximum(-z, 0.0) + jnp.log1p(jnp.exp(-jnp.abs(z)))) * (1.0 / GLA_TAU)
        b = functools.reduce(lambda u, w: u + w, [_dot(tri, rb(t)) for t in _split3(la)])
        blast = b[chunk - 1:chunk, :]
        q = p_ref[rows, 0:kq] * (DK_B ** -0.5)
        k = p_ref[rows, kq:2 * kq]
        qin = rb(q * jnp.exp(b))
        kst = rb(k * jnp.exp(blast - b))
        for h in range(H_B):
            ks = slice(h * DK_B, (h + 1) * DK_B)
            vs = slice(h * DV_B, (h + 1) * DV_B)
            st = st_ref[h]
            vb = rb(p_ref[rows, 2 * kq + h * DV_B:2 * kq + (h + 1) * DV_B])
            o_inter = _dot_nt(qin[:, ks], rb(st))
            parts = []
            for blk in range(chunk // sub):
                r0 = blk * sub
                n = r0 + sub
                ref_b = b[r0 - 1:r0, ks] if blk > 0 else jnp.zeros((1, DK_B), F32)
                qi = rb(q[r0:n, ks] * jnp.exp(b[r0:n, ks] - ref_b))
                ki = rb(k[0:n, ks] * jnp.exp(jnp.minimum(ref_b - b[0:n, ks], GLA_EXP_CLAMP)))
                a = _dot_nt(qi, ki)
                ti = lax.broadcasted_iota(jnp.int32, (sub, n), 0) + r0
                si = lax.broadcasted_iota(jnp.int32, (sub, n), 1)
                a = jnp.where(si <= ti, a, 0.0)
                parts.append(_dot(rb(a), vb[0:n]))
            o = o_inter + (jnp.concatenate(parts, axis=0) if len(parts) > 1 else parts[0])
            st_ref[h] = st * jnp.exp(blast[:, ks]) + _dot_tn(vb, kst[:, ks])
            on = o * lax.rsqrt(jnp.mean(o * o, axis=-1, keepdims=True) + RMS_EPS) * ng_ref[...]
            rg = p_ref[rows, 2 * kq + H_B * DV_B + h * DV_B:2 * kq + H_B * DV_B + (h + 1) * DV_B]
            o_ref[rows, vs] = (on * (rg * jax.nn.sigmoid(rg))).astype(BF16)


def _gla(pb, wup, gb, ng, s0t, batch, seq, chunk, tb):
    p3 = pb.reshape(batch, seq, PB_W)
    o, st = pl.pallas_call(
        functools.partial(_gla_kernel, chunk=chunk, tb=tb),
        grid=(batch, seq // tb),
        in_specs=[pl.BlockSpec((None, tb, PB_W), lambda b, i: (b, i, 0)),
                  pl.BlockSpec(wup.shape, lambda b, i: (0, 0)),
                  pl.BlockSpec(gb.shape, lambda b, i: (0, 0)),
                  pl.BlockSpec(ng.shape, lambda b, i: (0, 0)),
                  pl.BlockSpec((None, H_B, DV_B, DK_B), lambda b, i: (b, 0, 0, 0))],
        out_specs=[pl.BlockSpec((None, tb, H_B * DV_B), lambda b, i: (b, i, 0)),
                   pl.BlockSpec((None, H_B, DV_B, DK_B), lambda b, i: (b, 0, 0, 0))],
        out_shape=[jax.ShapeDtypeStruct((batch, seq, H_B * DV_B), BF16),
                   jax.ShapeDtypeStruct((batch, H_B, DV_B, DK_B), F32)],
        compiler_params=_cparams(("parallel", "arbitrary")),
        name=f"gla_c{chunk}",
    )(p3, wup, gb, ng, s0t)
    return o.reshape(batch * seq, H_B * DV_B), st


def _layer_norm(u, g, b):
    mu = jnp.mean(u, axis=-1, keepdims=True)
    d = u - mu
    var = jnp.mean(d * d, axis=-1, keepdims=True)
    return d * lax.rsqrt(var + LN_EPS) * g + b


def _merge_kernel(*refs, n_groups):
    x_ref = refs[0]
    oa_refs = refs[1:1 + 2 * n_groups] if n_groups > 1 else refs[1:2]
    rest = refs[1 + (2 * n_groups if n_groups > 1 else 1):]
    (ob_ref, pg_a_ref, pg_b_ref, wpa_ref, wpb_ref, wo_ref, g1_ref, b1_ref, rwh_ref, rwl_ref, rb_ref,
     x1_ref, x1p_ref, ti_ref, gt_ref) = rest
    if n_groups > 1:
        os_ = [r[...] for r in oa_refs[:n_groups]]
        ls = [r[...] for r in oa_refs[n_groups:]]
        mm = functools.reduce(jnp.maximum, ls)
        ws = [jnp.exp(x - mm) for x in ls]
        oa = sum(w * o for w, o in zip(ws, os_)) / sum(ws)
    else:
        oa = oa_refs[0][...]
    ya = _dot(oa.astype(BF16), wpa_ref[...])
    yb = _dot(ob_ref[...], wpb_ref[...])
    branch = jax.nn.sigmoid(pg_a_ref[...]) * ya + jax.nn.sigmoid(pg_b_ref[...]) * yb
    y = _dot(branch.astype(BF16), wo_ref[...])
    x1 = _layer_norm(DEEPNORM_ALPHA * x_ref[...] + y, g1_ref[...], b1_ref[...])
    x1_ref[...] = x1
    xh = x1.astype(BF16)
    xhf = xh.astype(F32)
    bits = lax.bitcast_convert_type(xhf, jnp.int32)
    half = D_MODEL // 2
    x1p_ref[...] = lax.shift_right_logical(bits[:, :half], 16) | bits[:, half:]
    xl = (x1 - xhf).astype(BF16)
    logits = _dot(xh, rwh_ref[...]) + _dot(xl, rwh_ref[...]) + _dot(xh, rwl_ref[...]) + rb_ref[...]
    lane = lax.broadcasted_iota(jnp.int32, logits.shape, 1)
    vals = logits
    top_v, top_i = [], []
    for _ in range(TOP_K):
        m = jnp.max(vals, axis=-1, keepdims=True)
        ik = jnp.min(jnp.where(vals == m, lane, LANE), axis=-1, keepdims=True)
        vals = jnp.where(lane == ik, -jnp.inf, vals)
        top_v.append(m)
        top_i.append(ik)
    es = [jnp.exp(v - top_v[0]) for v in top_v]
    tot = functools.reduce(lambda a, b: a + b, es)
    ti_ref[...] = jnp.concatenate(top_i, axis=1)
    gt_ref[...] = jnp.concatenate([e / tot for e in es], axis=1)


def _merge(x, oas, ob, pg, wpa, wpb, wo, g1, b1, rwh, rwl, rbp, tm):
    T = x.shape[0]
    n_groups = len(oas) // 2 if len(oas) > 1 else 1

    def row(w):
        return pl.BlockSpec((tm, w), lambda i: (i, 0))

    def const(a):
        return pl.BlockSpec(a.shape, lambda i: (0,) * a.ndim, pipeline_mode=pl.Buffered(1))

    in_specs = ([row(D_MODEL)] + [row(A_WIDTH)] * len(oas) + [row(H_B * DV_B), row(D_MODEL),
                pl.BlockSpec((tm, D_MODEL), lambda i: (i, 1))] + [const(a) for a in (wpa, wpb, wo, g1, b1, rwh, rwl, rbp)])
    return pl.pallas_call(
        functools.partial(_merge_kernel, n_groups=n_groups),
        grid=(T // tm,),
        in_specs=in_specs,
        out_specs=[row(D_MODEL), row(D_MODEL // 2), row(TOP_K), row(TOP_K)],
        out_shape=[jax.ShapeDtypeStruct((T, D_MODEL), F32), jax.ShapeDtypeStruct((T, D_MODEL // 2), jnp.int32),
                   jax.ShapeDtypeStruct((T, TOP_K), jnp.int32), jax.ShapeDtypeStruct((T, TOP_K), F32)],
        compiler_params=_cparams(("parallel",)),
        name=f"merge_g{n_groups}",
    )(x, *oas, ob, pg, pg, wpa, wpb, wo, g1, b1, rwh, rwl, rbp)


def _sc_gather(table, idx):
    info = plsc.get_sparse_core_info()
    n_workers = info.num_cores * info.num_subcores
    n, width = idx.shape[0], table.shape[1]
    per_worker = n // n_workers
    chunk = SC_CHUNK_BYTES // (width * table.dtype.itemsize)
    assert per_worker * n_workers == n and per_worker % chunk == 0 and chunk % 8 == 0
    mesh = plsc.VectorSubcoreMesh(core_axis_name="c", subcore_axis_name="s")

    @functools.partial(
        pl.kernel, mesh=mesh,
        out_type=jax.ShapeDtypeStruct((n, width), table.dtype),
        scratch_types=[pltpu.VMEM((chunk,), jnp.int32), pltpu.VMEM((chunk, width), table.dtype),
                       pltpu.SemaphoreType.DMA],
    )
    def gather(table_hbm, idx_hbm, out_hbm, idx_v, rows_v, sem):
        wid = lax.axis_index("s") * info.num_cores + lax.axis_index("c")
        base = wid * per_worker

        @pl.loop(0, per_worker // chunk)
        def _(c):
            off = pl.multiple_of(base + c * chunk, chunk)
            pltpu.sync_copy(idx_hbm.at[pl.ds(off, chunk)], idx_v)
            pltpu.async_copy(table_hbm.at[idx_v], rows_v, sem).wait()
            pltpu.sync_copy(rows_v, out_hbm.at[pl.ds(off, chunk)])

    return gather(table, idx)


def _sc_scatter_rows(table, idx, n_out):
    info = plsc.get_sparse_core_info()
    n_workers = info.num_cores * info.num_subcores
    n_idx, n = idx.shape
    width = table.shape[1]
    per_worker = n // n_workers
    max_rows = SC_SCATTER_BYTES // (width * table.dtype.itemsize)
    chunk = max(d for d in range(8, max_rows + 1, 8) if per_worker % d == 0)
    assert per_worker * n_workers == n and table.shape[0] == n
    mesh = plsc.VectorSubcoreMesh(core_axis_name="c", subcore_axis_name="s")

    @functools.partial(
        pl.kernel, mesh=mesh,
        out_type=jax.ShapeDtypeStruct((n_out, width), table.dtype),
        scratch_types=[pltpu.VMEM((chunk,), jnp.int32)] * n_idx + [pltpu.VMEM((chunk, width), table.dtype)],
    )
    def scatter(table_hbm, idx_hbm, out_hbm, *scratch):
        idx_vs, rows_v = scratch[:n_idx], scratch[n_idx]
        wid = lax.axis_index("s") * info.num_cores + lax.axis_index("c")
        base = wid * per_worker

        @pl.loop(0, per_worker // chunk)
        def _(c):
            off = pl.multiple_of(base + c * chunk, 8)
            pltpu.sync_copy(table_hbm.at[pl.ds(off, chunk)], rows_v)
            for k in range(n_idx):
                pltpu.sync_copy(idx_hbm.at[pl.ds(pl.multiple_of(k * n + off, 8), chunk)], idx_vs[k])
            for k in range(n_idx):
                pltpu.sync_copy(rows_v, out_hbm.at[idx_vs[k]])

    return scatter(table, idx.reshape(n_idx * n))


def _expert_kernel(be_ref, rows_ref, nu_ref, xs_ref, wg_ref, wu_ref, bg_ref, bu_ref, wd_ref, bd_ref, o_ref, hid_ref):
    del be_ref, nu_ref
    i = pl.program_id(0)
    p = pl.program_id(1)
    nrows = rows_ref[i]
    n_live = (nrows + (MOE_SB - 1)) // MOE_SB
    half = D_MODEL // 2

    @pl.when(p < MOE_NF)
    def _():
        def body(s, carry):
            r0 = pl.multiple_of(s * MOE_SB, MOE_SB)
            rid = r0 + lax.broadcasted_iota(jnp.int32, (MOE_SB, half), 0)
            packed = jnp.where(rid < nrows, xs_ref[pl.ds(r0, MOE_SB), :], 0)
            lo = lax.bitcast_convert_type(lax.shift_left(packed, 16), F32)
            hi = lax.bitcast_convert_type(packed & jnp.int32(-65536), F32)
            x = jnp.concatenate([lo.astype(BF16), hi.astype(BF16)], axis=1)
            g = jnp.minimum(_dot(x, wg_ref[...].astype(BF16)) + bg_ref[...], SWIGLU_LIMIT)
            u = jnp.clip(_dot(x, wu_ref[...].astype(BF16)) + bu_ref[...], -SWIGLU_LIMIT, SWIGLU_LIMIT)
            hid_ref[p, pl.ds(r0, MOE_SB), :] = ((u + 1.0) * g * jax.nn.sigmoid(SWIGLU_ALPHA * g)).astype(BF16)
            return carry

        lax.fori_loop(0, n_live, body, 0)

    @pl.when(p >= MOE_NF)
    def _():
        def body(s, carry):
            r0 = pl.multiple_of(s * MOE_SB, MOE_SB)
            y = bd_ref[...]
            for f in range(MOE_NF):
                y = y + _dot(hid_ref[f, pl.ds(r0, MOE_SB), :], wd_ref[f * MOE_TF:(f + 1) * MOE_TF, :].astype(BF16))
            o_ref[pl.ds(r0, MOE_SB), :] = y
            return carry

        def zero_body(s, carry):
            o_ref[pl.ds(pl.multiple_of(s * MOE_SB, MOE_SB), MOE_SB), :] = jnp.zeros((MOE_SB, MOE_TN), F32)
            return carry

        lax.fori_loop(0, n_live, body, 0)
        lax.fori_loop(n_live, MOE_BM // MOE_SB, zero_body, 0)


def _experts(xs, block_expert, block_rows, n_used, w_gate_up, b_gate_up, w_down, b_down):
    nf, nn = MOE_NF, D_MODEL // MOE_TN

    def gate_map(col0, lead):
        def index_map(i, p, be, rw, nu):
            ahead = p >= nf + nn - lead
            e = jnp.where(ahead, be[jnp.minimum(i + 1, nu[0] - 1)], be[i])
            return (e, 0, col0 + jnp.where(ahead, 0, jnp.minimum(p, nf - 1)))
        return index_map

    def down_map(i, p, be, rw, nu):
        parked = p < nf
        e = jnp.where(parked, be[jnp.maximum(i - 1, 0)], be[i])
        return (e, 0, jnp.where(parked, jnp.where(i > 0, nn - 1, 0), p - nf))

    grid_spec = pltpu.PrefetchScalarGridSpec(
        num_scalar_prefetch=3,
        grid=(n_used[0], nf + nn),
        in_specs=[
            pl.BlockSpec((MOE_BM, D_MODEL // 2), lambda i, p, be, rw, nu: (i, 0), pipeline_mode=pl.Buffered(1)),
            pl.BlockSpec((None, D_MODEL, MOE_TF), gate_map(0, nn // 2)),
            pl.BlockSpec((None, D_MODEL, MOE_TF), gate_map(nf, nn // 4)),
            pl.BlockSpec((None, 1, MOE_TF), gate_map(0, nn // 2)),
            pl.BlockSpec((None, 1, MOE_TF), gate_map(nf, nn // 4)),
            pl.BlockSpec((None, D_FF, MOE_TN), down_map),
            pl.BlockSpec((None, 1, MOE_TN), down_map),
        ],
        out_specs=pl.BlockSpec((MOE_BM, MOE_TN), lambda i, p, be, rw, nu: (i, jnp.maximum(p - nf, 0))),
        scratch_shapes=[pltpu.VMEM((nf, MOE_BM, MOE_TF), BF16)],
    )
    bgu = b_gate_up.reshape(N_EXPERTS, 1, 2 * D_FF)
    bd = b_down.reshape(N_EXPERTS, 1, D_MODEL)
    return pl.pallas_call(
        _expert_kernel,
        grid_spec=grid_spec,
        out_shape=jax.ShapeDtypeStruct((xs.shape[0], D_MODEL), F32),
        compiler_params=_cparams(("arbitrary", "arbitrary")),
        name="moe_experts",
    )(block_expert, block_rows, n_used, xs, w_gate_up, w_gate_up, bgu, bgu, w_down, bd)


def _route(top_i, n_blocks):
    T = top_i.shape[0]
    expert = top_i.reshape(T * TOP_K)
    onehot = (expert[:, None] == jnp.arange(N_EXPERTS, dtype=jnp.int32)[None, :]).astype(jnp.int32)
    csum = jnp.cumsum(onehot, axis=0)
    counts = csum[-1]
    rank = jnp.take_along_axis(csum, expert[:, None], axis=1)[:, 0] - 1
    bpe = (counts + MOE_BM - 1) // MOE_BM
    bend = jnp.cumsum(bpe)
    bstart = bend - bpe
    dest = (bstart[expert] * MOE_BM + rank).astype(jnp.int32)
    n_used = bend[-1]
    blk = jnp.arange(n_blocks, dtype=jnp.int32)
    be = jnp.minimum(jnp.searchsorted(bend, jnp.minimum(blk, n_used - 1), side="right"), N_EXPERTS - 1).astype(jnp.int32)
    rows = jnp.clip(counts[be] - (blk - bstart[be]) * MOE_BM, 0, MOE_BM)
    rows = jnp.where(blk < n_used, rows, 0).astype(jnp.int32)
    return dest.reshape(T, TOP_K).T, be, rows, n_used.reshape(1).astype(jnp.int32)


def _combine_kernel(x1_ref, ge_ref, gt_ref, g2_ref, b2_ref, o_ref):
    gt = gt_ref[...]
    m = gt[:, 0:1] * ge_ref[0]
    for k in range(1, TOP_K):
        m = m + gt[:, k:k + 1] * ge_ref[k]
    o_ref[...] = _layer_norm(DEEPNORM_ALPHA * x1_ref[...] + m, g2_ref[...], b2_ref[...])


def _combine(x1, ge, gate, g2, b2, row0, tm):
    n_rows = x1.shape[0]
    b0 = row0 // tm
    return pl.pallas_call(
        _combine_kernel,
        grid=(n_rows // tm,),
        in_specs=[pl.BlockSpec((tm, D_MODEL), lambda i: (i, 0)),
                  pl.BlockSpec((TOP_K, tm, D_MODEL), lambda i: (0, b0 + i, 0)),
                  pl.BlockSpec((tm, TOP_K), lambda i: (b0 + i, 0)),
                  pl.BlockSpec(g2.shape, lambda i: (0, 0)), pl.BlockSpec(b2.shape, lambda i: (0, 0))],
        out_specs=pl.BlockSpec((tm, D_MODEL), lambda i: (i, 0)),
        out_shape=jax.ShapeDtypeStruct((n_rows, D_MODEL), F32),
        compiler_params=_cparams(("parallel",)),
        name="moe_combine",
    )(x1, ge, gate, g2, b2)


def _layer(xp, xs, caches, state, w_in, rel_bias, gla_w_up, gla_b, gla_norm_g, w_pa, w_pb, w_o, ln1_g, ln1_b,
           router_w, router_b, w_gate_up, b_gate_up, w_down, b_down, ln2_g, ln2_b):
    batch, seq, _ = xp.shape
    dbatch, dseq, _ = xs.shape
    tp, ts = batch * seq, dbatch * dseq
    xp2, xs2 = xp.reshape(tp, D_MODEL), xs.reshape(ts, D_MODEL)

    o_b0, o_lr, o_g = PA_W, PA_W + PB_LR, PA_W + PB_LR + GLA_RANK
    w_a = w_in[:, :PA_W].astype(BF16)
    w_b = jnp.concatenate([w_in[:, o_b0:o_g], jnp.zeros((D_MODEL, LANE - GLA_RANK), F32)], axis=1).astype(BF16)
    w_g = w_in[:, o_g:].astype(BF16)
    wup = jnp.concatenate([gla_w_up, jnp.zeros((LANE - GLA_RANK, H_B * DK_B), F32)], axis=0).astype(BF16)
    gb = gla_b.reshape(1, H_B * DK_B)
    ng = gla_norm_g.reshape(1, DV_B)
    wpa, wpb, wo = w_pa.astype(BF16), w_pb.astype(BF16), w_o.astype(BF16)
    g1, b1 = ln1_g.reshape(1, D_MODEL), ln1_b.reshape(1, D_MODEL)
    g2, b2 = ln2_g.reshape(1, D_MODEL), ln2_b.reshape(1, D_MODEL)
    rw = jnp.concatenate([router_w, jnp.zeros((D_MODEL, LANE - N_EXPERTS), F32)], axis=1)
    rwh = rw.astype(BF16)
    rwl = (rw - rwh.astype(F32)).astype(BF16)
    rbp = jnp.concatenate([router_b, jnp.full((LANE - N_EXPERTS,), NEG, F32)]).reshape(1, LANE)
    caches8 = [c.reshape(dbatch, c.shape[1], KV_ROWS, HEAD_DIM) for c in caches]

    pa_p = _project(xp2, w_a, PROJ_TM, A_QKV_WIDTH, "proj_a_prompt")
    pb_p = _project(xp2, w_b, PROJ_TM, PB_W, "proj_b_prompt")
    pg_p = _project(xp2, w_g, PROJ_TM, 1024, "proj_g_prompt")
    oas, lses = [], []
    for g in range(N_GROUPS):
        o, lse = _attn_prompt_group(pa_p, _prompt_table(rel_bias, g), g, batch, seq)
        oas.append(o)
        lses.append(lse)
    ob_p, st_p = _gla(pb_p, wup, gb, ng, jnp.zeros((batch, H_B, DV_B, DK_B), F32), batch, seq, GLA_CHUNK, 256)
    x1_p, x1p_p, ti_p, gt_p = _merge(xp2, oas + lses, ob_p, pg_p, wpa, wpb, wo, g1, b1, rwh, rwl, rbp, 256)

    pa_s = _project(xs2, w_a, ts, A_QKV_WIDTH, "proj_a_sample")
    pb_s = _project(xs2, w_b, ts, PB_W, "proj_b_sample")
    pg_s = _project(xs2, w_g, ts, 1024, "proj_g_sample")
    tabc, tabn, combo_base = _sample_tables(rel_bias, dseq)
    new_rows = [_kv_pack(pa_s, g, dbatch, dseq, dseq, dseq) for g in range(N_GROUPS)]
    oa_s = _attn_sample(pa_s, new_rows, caches8, tabc, tabn, combo_base, dbatch, dseq)
    chunk_s = int(np.gcd(dseq, GLA_CHUNK))
    ob_s, st_s = _gla(pb_s, wup, gb, ng, jnp.swapaxes(state, -1, -2), dbatch, dseq, chunk_s, dseq)
    x1_s, x1p_s, ti_s, gt_s = _merge(xs2, [oa_s], ob_s, pg_s, wpa, wpb, wo, g1, b1, rwh, rwl, rbp, ts)

    x1p = jnp.concatenate([x1p_p, x1p_s], axis=0)
    top_i = jnp.concatenate([ti_p, ti_s], axis=0)
    gate = jnp.concatenate([gt_p, gt_s], axis=0)
    t_all = tp + ts
    n_blocks = -(-(t_all * TOP_K) // MOE_BM) + N_EXPERTS
    dest, be, rows, n_used = _route(top_i, n_blocks)
    xsorted = _sc_scatter_rows(x1p, dest, n_blocks * MOE_BM)
    eo = _experts(xsorted, be, rows, n_used, w_gate_up, b_gate_up, w_down, b_down)
    ge = _sc_gather(eo, dest.reshape(TOP_K * t_all)).reshape(TOP_K, t_all, D_MODEL)
    y_p = _combine(x1_p, ge, gate, g2, b2, 0, 256)
    y_s = _combine(x1_s, ge, gate, g2, b2, tp, 256)

    shifted = _sc_cache_shift(caches8, dseq)
    bufs_p, bufs_s = [], []
    for g, (window, _) in enumerate(DILATED_GROUPS):
        keep = min(window, seq)
        bufs_p.append(_kv_pack(pa_p, g, batch, seq, keep, Q_BLOCK).reshape(batch, keep, 2, H_A, HEAD_DIM))
        clen = caches[g].shape[1]
        assert clen == window and dseq <= clen
        buf = lax.dynamic_update_slice(shifted[g], new_rows[g], (0, clen - dseq, 0, 0))
        bufs_s.append(buf.reshape(dbatch, clen, 2, H_A, HEAD_DIM))
    return (y_p.reshape(batch, seq, D_MODEL), y_s.reshape(dbatch, dseq, D_MODEL), bufs_p, jnp.swapaxes(st_p, -1, -2),
            bufs_s, jnp.swapaxes(st_s, -1, -2))


def kernel(x_prompt, x_sample, cache_a1_kv, cache_a2_kv, cache_a3_kv, state_b_s, w_in, rel_bias, gla_w_up, gla_b,
           gla_norm_g, w_pa, w_pb, w_o, ln1_g, ln1_b, router_w, router_b, w_gate_up, b_gate_up, w_down, b_down,
           ln2_g, ln2_b):
    assert w_in.shape[0] == DEPTH
    yp, ys, bufs_p, st_p, bufs_s, st_s = _layer(
        x_prompt, x_sample, (cache_a1_kv[0], cache_a2_kv[0], cache_a3_kv[0]), state_b_s[0], w_in[0], rel_bias,
        gla_w_up[0], gla_b[0], gla_norm_g[0], w_pa[0], w_pb[0], w_o[0], ln1_g[0], ln1_b[0], router_w[0], router_b[0],
        w_gate_up[0], b_gate_up[0], w_down[0], b_down[0], ln2_g[0], ln2_b[0])
    return (yp, ys, bufs_p[0][None], bufs_p[1][None], bufs_p[2][None], st_p[None],
            bufs_s[0][None], bufs_s[1][None], bufs_s[2][None], st_s[None].astype(state_b_s.dtype))
```

```python
import functools

import numpy as np
import jax
import jax.numpy as jnp
from jax import lax
from jax.experimental import pallas as pl
from jax.experimental.pallas import tpu as pltpu
from jax.experimental.pallas import tpu_sc as plsc

F32 = jnp.float32
BF16 = jnp.bfloat16

D_MODEL = 2048
HEAD_DIM = 128
DILATED_GROUPS = ((128, 1), (512, 4), (2048, 16))
N_GROUPS = 3
H_A = 4
A_WIDTH = H_A * HEAD_DIM
A_QKV_WIDTH = N_GROUPS * A_WIDTH
Q_BLOCK = 128
N_BUCKETS = 32
REL_MAX_DIST = 2048
H_B = 4
DK_B = 64
DV_B = 128
GLA_RANK = 16
GLA_TAU = 16.0
GLA_CHUNK = 64
GLA_SUB = 16
GLA_EXP_CLAMP = 80.0
N_EXPERTS = 32
TOP_K = 4
D_FF = 2048
SWIGLU_LIMIT = 7.0
SWIGLU_ALPHA = 1.702
LN_EPS = 1e-5
RMS_EPS = 1e-6
DEPTH = 1
DEEPNORM_ALPHA = (2.0 * DEPTH) ** 0.25
ATT_SCALE = HEAD_DIM ** -0.5
NEG = float(np.finfo(np.float32).min)

VMEM_LIMIT = 56 * 1024 * 1024
LANE = 128

PA_W = 3 * A_QKV_WIDTH
PB_LR = H_B * DK_B * 2 + H_B * DV_B * 2
PB_W = PB_LR + LANE
PG_W = 2 * D_MODEL

PROJ_TM = 1024

MOE_BM = 2560
MOE_SB = 320
MOE_TF = 512
MOE_NF = D_FF // MOE_TF
MOE_TN = 256
SC_CHUNK_BYTES = 128 * 1024
SC_SCATTER_BYTES = 160 * 1024


def _cparams(sem):
    return pltpu.CompilerParams(dimension_semantics=sem, vmem_limit_bytes=VMEM_LIMIT)


def _dot(a, b):
    return jnp.dot(a, b, preferred_element_type=F32)


def _dot_nt(a, b):
    return lax.dot_general(a, b, (((1,), (1,)), ((), ())), preferred_element_type=F32)


def _dot_tn(a, b):
    return lax.dot_general(a, b, (((0,), (0,)), ((), ())), preferred_element_type=F32)


def _proj_kernel(x_ref, w_ref, o_ref, xb_ref):
    @pl.when(pl.program_id(1) == 0)
    def _():
        xb_ref[...] = x_ref[...].astype(BF16)

    o_ref[...] = _dot(xb_ref[...], w_ref[...])


def _proj_heads_kernel(x_ref, w_ref, o_ref, xb_ref):
    @pl.when(pl.program_id(1) == 0)
    def _():
        xb_ref[...] = x_ref[...].astype(BF16)

    acc = _dot(xb_ref[...], w_ref[...])
    for c in range(o_ref.shape[0]):
        o_ref[c] = acc[:, c * HEAD_DIM:(c + 1) * HEAD_DIM]


def _project(x, w, tm, tn, name, head_major=False):
    T, D = x.shape
    N = w.shape[1]
    if head_major:
        nh = tn // HEAD_DIM
        out_spec = pl.BlockSpec((nh, tm, HEAD_DIM), lambda i, j: (j, i, 0))
        out_shape = jax.ShapeDtypeStruct((N // HEAD_DIM, T, HEAD_DIM), F32)
    else:
        out_spec = pl.BlockSpec((tm, tn), lambda i, j: (i, j))
        out_shape = jax.ShapeDtypeStruct((T, N), F32)
    return pl.pallas_call(
        _proj_heads_kernel if head_major else _proj_kernel,
        grid=(T // tm, N // tn),
        in_specs=[pl.BlockSpec((tm, D), lambda i, j: (i, 0)), pl.BlockSpec((D, tn), lambda i, j: (0, j))],
        out_specs=out_spec,
        out_shape=out_shape,
        scratch_shapes=[pltpu.VMEM((tm, D), BF16)],
        compiler_params=_cparams(("parallel", "arbitrary")),
        name=name,
    )(x, w)


def _t5_bucket(dist):
    max_exact = N_BUCKETS // 2
    d = np.maximum(dist, 1).astype(np.float32)
    large = max_exact + (np.log(d / max_exact) / np.log(REL_MAX_DIST / max_exact) * (N_BUCKETS - max_exact)).astype(np.int32)
    large = np.minimum(large, N_BUCKETS - 1)
    return np.where(dist < max_exact, dist, large).astype(np.int32)


def _bias_lookup(rel_bias, g, j, valid):
    _, dil = DILATED_GROUPS[g]
    bucket = _t5_bucket(dil * np.clip(j, 0, Q_BLOCK))
    onehot = bucket[None] == np.arange(N_BUCKETS).reshape((N_BUCKETS,) + (1,) * j.ndim)
    rb = rel_bias[:, g * H_A:(g + 1) * H_A].astype(F32).T.reshape((H_A, N_BUCKETS) + (1,) * j.ndim)
    vals = jnp.sum(jnp.where(onehot[None], rb, 0.0), axis=1)
    return jnp.where(valid[None], vals, NEG)


def _prompt_table(rel_bias, g):
    qi = np.arange(Q_BLOCK)[:, None]
    kj = np.arange(2 * Q_BLOCK)[None, :]
    j = Q_BLOCK + qi - kj
    return _bias_lookup(rel_bias, g, j, (j >= 0) & (j <= Q_BLOCK))


def _sample_tables(rel_bias, dec_seq):
    m = np.arange(Q_BLOCK)
    tabc, combo_base = [], []
    for g, (_, dil) in enumerate(DILATED_GROUPS):
        combo_base.append(len(tabc))
        for fl in range((dec_seq - 1) // dil + 1):
            j = Q_BLOCK + fl - m
            col = _bias_lookup(rel_bias, g, j, j <= Q_BLOCK).T
            col = jnp.concatenate([col, jnp.zeros_like(col)], axis=1)
            tabc.append(jnp.broadcast_to(col[:, :, None], (Q_BLOCK, 2 * H_A, LANE)))
    tabn = []
    s = np.arange(dec_seq)[:, None]
    sp = np.arange(dec_seq)[None, :]
    for g, (_, dil) in enumerate(DILATED_GROUPS):
        diff = s - sp
        t = _bias_lookup(rel_bias, g, diff // dil, (diff >= 0) & (diff % dil == 0))
        t = jnp.transpose(t, (1, 2, 0))
        t = jnp.concatenate([t, jnp.zeros_like(t)], axis=2)
        tabn.append(jnp.broadcast_to(t[..., None], (dec_seq, dec_seq, 2 * H_A, LANE)))
    return jnp.stack(tabc), jnp.stack(tabn), tuple(combo_base)


def _attn_prompt_kernel(q_ref, kc_ref, kp_ref, vc_ref, vp_ref, tab_ref, o_ref, lse_ref):
    has_prev = pl.program_id(2) > 0
    for h in range(H_A):
        sl = slice(h * HEAD_DIM, (h + 1) * HEAD_DIM)
        q = q_ref[:, sl].astype(BF16)
        sc = _dot_nt(q, kc_ref[:, sl].astype(BF16)) * ATT_SCALE + tab_ref[h, :, Q_BLOCK:]
        sp = _dot_nt(q, kp_ref[:, sl].astype(BF16)) * ATT_SCALE + tab_ref[h, :, :Q_BLOCK]
        sp = jnp.where(has_prev, sp, NEG)
        m = jnp.maximum(jnp.max(sc, axis=-1, keepdims=True), jnp.max(sp, axis=-1, keepdims=True))
        pc = jnp.exp(sc - m)
        pp = jnp.exp(sp - m)
        l = jnp.sum(pc, axis=-1, keepdims=True) + jnp.sum(pp, axis=-1, keepdims=True)
        inv = 1.0 / l
        o = _dot((pc * inv).astype(BF16), vc_ref[:, sl].astype(BF16)) + _dot((pp * inv).astype(BF16), vp_ref[:, sl].astype(BF16))
        o_ref[:, sl] = o
        lse_ref[:, sl] = jnp.broadcast_to(m + jnp.log(l), (Q_BLOCK, HEAD_DIM))


def _attn_prompt_strided_kernel(*refs, dil, with_prev):
    if with_prev:
        q_ref, kc_ref, kp_ref, vc_ref, vp_ref, tab_ref, o_ref, lse_ref = refs
    else:
        q_ref, kc_ref, vc_ref, tab_ref, o_ref, lse_ref = refs
    has_prev = pl.program_id(1) > 0

    def body(r, carry):
        idx = pl.ds(r, Q_BLOCK, stride=dil)
        q = q_ref[idx, :].astype(BF16)
        sc = _dot_nt(q, kc_ref[idx, :].astype(BF16)) * ATT_SCALE + tab_ref[:, Q_BLOCK:]
        m = jnp.max(sc, axis=-1, keepdims=True)
        if with_prev:
            sp = _dot_nt(q, kp_ref[idx, :].astype(BF16)) * ATT_SCALE + tab_ref[:, :Q_BLOCK]
            sp = jnp.where(has_prev, sp, NEG)
            m = jnp.maximum(m, jnp.max(sp, axis=-1, keepdims=True))
        pc = jnp.exp(sc - m)
        l = jnp.sum(pc, axis=-1, keepdims=True)
        if with_prev:
            pp = jnp.exp(sp - m)
            l = l + jnp.sum(pp, axis=-1, keepdims=True)
        inv = 1.0 / l
        o = _dot((pc * inv).astype(BF16), vc_ref[idx, :].astype(BF16))
        if with_prev:
            o = o + _dot((pp * inv).astype(BF16), vp_ref[idx, :].astype(BF16))
        o_ref[idx, :] = o
        lse_ref[idx, :] = jnp.broadcast_to(m + jnp.log(l), (Q_BLOCK, HEAD_DIM))
        return carry

    lax.fori_loop(0, dil, body, 0, unroll=min(dil, 4))


def _attn_prompt_strided(pa, table, g, batch, seq):
    _, dil = DILATED_GROUPS[g]
    rows = dil * Q_BLOCK
    nblk = seq // rows
    with_prev = nblk > 1
    hcols = A_QKV_WIDTH // HEAD_DIM

    def spec(sec, prev):
        if prev:
            return pl.BlockSpec((rows, HEAD_DIM), lambda b, i, h: (b * nblk + jnp.maximum(i - 1, 0), sec * hcols + g * H_A + h))
        return pl.BlockSpec((rows, HEAD_DIM), lambda b, i, h: (b * nblk + i, sec * hcols + g * H_A + h))

    in_specs = [spec(0, False), spec(1, False)] + ([spec(1, True)] if with_prev else []) + [spec(2, False)] + (
        [spec(2, True)] if with_prev else []) + [pl.BlockSpec((None, Q_BLOCK, 2 * Q_BLOCK), lambda b, i, h: (h, 0, 0))]
    out_spec = pl.BlockSpec((rows, HEAD_DIM), lambda b, i, h: (b * nblk + i, h))
    return pl.pallas_call(
        functools.partial(_attn_prompt_strided_kernel, dil=dil, with_prev=with_prev),
        grid=(batch, nblk, H_A),
        in_specs=in_specs,
        out_specs=[out_spec, out_spec],
        out_shape=[jax.ShapeDtypeStruct((batch * seq, A_WIDTH), F32)] * 2,
        compiler_params=_cparams(("parallel", "arbitrary", "arbitrary")),
        name=f"attn_prompt_g{g}",
    )(*([pa] * (len(in_specs) - 1)), table)


def _attn_prompt_group(pa, table, g, batch, seq):
    _, dil = DILATED_GROUPS[g]
    if dil > 1:
        return _attn_prompt_strided(pa, table, g, batch, seq)
    sub = seq // dil
    nqb = sub // Q_BLOCK
    wblk = PA_W // A_WIDTH
    pv = pa.reshape(batch, sub, dil * PA_W)

    def spec(off, prev):
        if prev:
            return pl.BlockSpec((None, Q_BLOCK, A_WIDTH), lambda b, r, i: (b, jnp.maximum(i - 1, 0), r * wblk + off + g))
        return pl.BlockSpec((None, Q_BLOCK, A_WIDTH), lambda b, r, i: (b, i, r * wblk + off + g))

    out_spec = pl.BlockSpec((None, Q_BLOCK, A_WIDTH), lambda b, r, i: (b, i, r))
    o, lse = pl.pallas_call(
        _attn_prompt_kernel,
        grid=(batch, dil, nqb),
        in_specs=[spec(0, False), spec(N_GROUPS, False), spec(N_GROUPS, True), spec(2 * N_GROUPS, False),
                  spec(2 * N_GROUPS, True), pl.BlockSpec((H_A, Q_BLOCK, 2 * Q_BLOCK), lambda b, r, i: (0, 0, 0))],
        out_specs=[out_spec, out_spec],
        out_shape=[jax.ShapeDtypeStruct((batch, sub, dil * A_WIDTH), F32)] * 2,
        compiler_params=_cparams(("parallel", "parallel", "arbitrary")),
        name=f"attn_prompt_g{g}",
    )(pv, pv, pv, pv, pv, table)
    return o.reshape(batch * seq, A_WIDTH), lse.reshape(batch * seq, A_WIDTH)


def _attn_heads_kernel(*refs, dil, with_prev):
    if with_prev:
        q_ref, kc_ref, kp_ref, vc_ref, vp_ref, tab_ref, o_ref, lse_ref = refs
    else:
        q_ref, kc_ref, vc_ref, tab_ref, o_ref, lse_ref = refs
    has_prev = pl.program_id(1) > 0
    zero = jnp.zeros((Q_BLOCK, HEAD_DIM), BF16)
    tab_c = tab_ref[:, Q_BLOCK:]
    if with_prev:
        tab = jnp.concatenate([jnp.where(has_prev, tab_ref[:, :Q_BLOCK], NEG), tab_c], axis=1)
    else:
        tab = tab_c

    def heads(ref, idx):
        return jnp.concatenate([ref[h, idx, :].astype(BF16) for h in range(H_A)], axis=1)

    def body(r, carry):
        idx = pl.ds(r, Q_BLOCK, stride=dil)
        qs = [q_ref[h, idx, :].astype(BF16) for h in range(H_A)]
        q_stack = jnp.concatenate(
            [jnp.concatenate([qs[h] if c == h else zero for c in range(H_A)], axis=1) for h in range(H_A)], axis=0)
        if with_prev:
            k_all = jnp.concatenate([heads(kp_ref, idx), heads(kc_ref, idx)], axis=0)
            v_all = jnp.concatenate([heads(vp_ref, idx), heads(vc_ref, idx)], axis=0)
        else:
            k_all, v_all = heads(kc_ref, idx), heads(vc_ref, idx)
        s = _dot_nt(q_stack, k_all) * ATT_SCALE + tab
        m = jnp.max(s, axis=-1, keepdims=True)
        p = jnp.exp(s - m)
        l = jnp.sum(p, axis=-1, keepdims=True)
        o = _dot((p * (1.0 / l)).astype(BF16), v_all)
        lse = m + jnp.log(l)
        for h in range(H_A):
            rows = slice(h * Q_BLOCK, (h + 1) * Q_BLOCK)
            o_ref[h, idx, :] = o[rows, h * HEAD_DIM:(h + 1) * HEAD_DIM]
            lse_ref[h, idx, :] = jnp.broadcast_to(lse[rows], (Q_BLOCK, HEAD_DIM))
        return carry

    lax.fori_loop(0, dil, body, 0, unroll=min(dil, 2))


def _attn_heads(pa_hm, table, g, batch, seq):
    _, dil = DILATED_GROUPS[g]
    rows = dil * Q_BLOCK
    nblk = seq // rows
    with_prev = nblk > 1

    def spec(sec, prev):
        if prev:
            return pl.BlockSpec((H_A, rows, HEAD_DIM), lambda b, i: (sec * N_GROUPS + g, b * nblk + jnp.maximum(i - 1, 0), 0))
        return pl.BlockSpec((H_A, rows, HEAD_DIM), lambda b, i: (sec * N_GROUPS + g, b * nblk + i, 0))

    in_specs = [spec(0, False), spec(1, False)] + ([spec(1, True)] if with_prev else []) + [spec(2, False)] + (
        [spec(2, True)] if with_prev else []) + [pl.BlockSpec((H_A * Q_BLOCK, 2 * Q_BLOCK), lambda b, i: (0, 0))]
    out_spec = pl.BlockSpec((H_A, rows, HEAD_DIM), lambda b, i: (0, b * nblk + i, 0))
    return pl.pallas_call(
        functools.partial(_attn_heads_kernel, dil=dil, with_prev=with_prev),
        grid=(batch, nblk),
        in_specs=in_specs,
        out_specs=[out_spec, out_spec],
        out_shape=[jax.ShapeDtypeStruct((H_A, batch * seq, HEAD_DIM), F32)] * 2,
        compiler_params=_cparams(("parallel", "arbitrary")),
        name=f"attn_prompt_g{g}",
    )(*([pa_hm] * (len(in_specs) - 1)), table.reshape(H_A * Q_BLOCK, 2 * Q_BLOCK))


KV_ROWS = 2 * H_A


def _kv_pack_kernel(k_ref, v_ref, o_ref):
    o_ref[...] = jnp.stack([k_ref[h] for h in range(H_A)] + [v_ref[h] for h in range(H_A)], axis=1)


def _kv_pack(pa_hm, g, batch, seq, keep, tm):
    nblk, blk0, per_b = keep // tm, (seq - keep) // tm, seq // tm
    out = pl.pallas_call(
        _kv_pack_kernel,
        grid=(batch, nblk),
        in_specs=[pl.BlockSpec((H_A, tm, HEAD_DIM), lambda b, i: (N_GROUPS + g, b * per_b + blk0 + i, 0)),
                  pl.BlockSpec((H_A, tm, HEAD_DIM), lambda b, i: (2 * N_GROUPS + g, b * per_b + blk0 + i, 0))],
        out_specs=pl.BlockSpec((tm, KV_ROWS, HEAD_DIM), lambda b, i: (b * nblk + i, 0, 0)),
        out_shape=jax.ShapeDtypeStruct((batch * keep, KV_ROWS, HEAD_DIM), F32),
        compiler_params=_cparams(("parallel", "parallel")),
        name=f"kv_pack_g{g}_{keep}",
    )(pa_hm, pa_hm)
    return out.reshape(batch, keep, KV_ROWS, HEAD_DIM)


def _attn_sample_kernel(qkv_ref, n1_ref, n2_ref, n3_ref, c1_ref, c2_ref, c3_ref, tabc_ref, tabn_ref, o_ref, *,
                        dec_seq, combo_base):
    caches = (c1_ref, c2_ref, c3_ref)
    news = (n1_ref, n2_ref, n3_ref)
    zeros = jnp.zeros((H_A, HEAD_DIM), F32)
    for s in range(dec_seq):
        outs, lses = [], []
        for g, (_, dil) in enumerate(DILATED_GROUPS):
            rho, fl = s % dil, s // dil
            qm = jnp.concatenate([qkv_ref[g * H_A + h, s:s + 1, :] for h in range(H_A)] + [zeros], axis=0)
            kc = caches[g][:, rho]
            kn = news[g][...]
            sc = jnp.sum(kc * qm[None], axis=-1, keepdims=True) * ATT_SCALE + tabc_ref[combo_base[g] + fl]
            sn = jnp.sum(kn * qm[None], axis=-1, keepdims=True) * ATT_SCALE + tabn_ref[g, s]
            m = jnp.maximum(jnp.max(sc, axis=0), jnp.max(sn, axis=0))
            pc = jnp.exp(sc - m[None])
            pn = jnp.exp(sn - m[None])
            l = jnp.sum(pc, axis=0) + jnp.sum(pn, axis=0)
            acc = jnp.sum(pltpu.roll(pc, H_A, 1) * kc, axis=0) + jnp.sum(pltpu.roll(pn, H_A, 1) * kn, axis=0)
            outs.append(acc / pltpu.roll(l, H_A, 0))
            lses.append(pltpu.roll(m + jnp.log(l), H_A, 0))
        mm = jnp.maximum(jnp.maximum(lses[0], lses[1]), lses[2])
        ws = [jnp.exp(x - mm) for x in lses]
        o_ref[s] = (ws[0] * outs[0] + ws[1] * outs[1] + ws[2] * outs[2]) / (ws[0] + ws[1] + ws[2])


def _attn_sample(pa_hm, new_rows, caches, tabc, tabn, combo_base, batch, dec_seq):
    views, specs = [], []
    for g, (window, dil) in enumerate(DILATED_GROUPS):
        assert caches[g].shape[1] == window and dec_seq <= Q_BLOCK
        views.append(caches[g].reshape(batch, Q_BLOCK, dil, KV_ROWS, HEAD_DIM))
        used = min(dil, dec_seq)
        specs.append(pl.BlockSpec((None, Q_BLOCK, used, KV_ROWS, HEAD_DIM), lambda b: (b, 0, 0, 0, 0)))
    new_spec = pl.BlockSpec((None, dec_seq, KV_ROWS, HEAD_DIM), lambda b: (b, 0, 0, 0))
    out = pl.pallas_call(
        functools.partial(_attn_sample_kernel, dec_seq=dec_seq, combo_base=combo_base),
        grid=(batch,),
        in_specs=[pl.BlockSpec((N_GROUPS * H_A, dec_seq, HEAD_DIM), lambda b: (0, b, 0))] + [new_spec] * N_GROUPS + specs + [
            pl.BlockSpec(tabc.shape, lambda b: (0, 0, 0, 0)), pl.BlockSpec(tabn.shape, lambda b: (0, 0, 0, 0, 0))],
        out_specs=pl.BlockSpec((None, dec_seq, KV_ROWS, HEAD_DIM), lambda b: (b, 0, 0, 0)),
        out_shape=jax.ShapeDtypeStruct((batch, dec_seq, KV_ROWS, HEAD_DIM), F32),
        compiler_params=_cparams(("parallel",)),
        name="attn_sample",
    )(pa_hm, *new_rows, *views, tabc, tabn)
    return out[:, :, H_A:, :].reshape(batch * dec_seq, A_WIDTH)


def _sc_cache_shift(caches, drop):
    info = plsc.get_sparse_core_info()
    n_workers = info.num_cores * info.num_subcores
    batch = caches[0].shape[0]
    assert batch % n_workers == 0
    mesh = plsc.VectorSubcoreMesh(core_axis_name="c", subcore_axis_name="s")

    row_bytes = KV_ROWS * HEAD_DIM * 4
    chunks = []
    for c in caches:
        keep = c.shape[1] - drop
        ch = max(d for d in range(1, SC_CHUNK_BYTES // row_bytes + 1) if keep % d == 0)
        chunks.append(ch)
    buf_rows = max(chunks)

    @functools.partial(pl.kernel, mesh=mesh, out_type=[jax.ShapeDtypeStruct(c.shape, c.dtype) for c in caches],
                       scratch_types=[pltpu.VMEM((buf_rows, KV_ROWS, HEAD_DIM), caches[0].dtype)])
    def shift(*refs):
        srcs, dsts, buf = refs[:len(caches)], refs[len(caches):2 * len(caches)], refs[-1]
        wid = lax.axis_index("s") * info.num_cores + lax.axis_index("c")
        for j in range(batch // n_workers):
            b = wid * (batch // n_workers) + j
            for src, dst, ch in zip(srcs, dsts, chunks):
                stage = buf.at[pl.ds(0, ch)]

                @pl.loop(0, (src.shape[1] - drop) // ch)
                def _(i):
                    pltpu.sync_copy(src.at[b, pl.ds(drop + i * ch, ch)], stage)
                    pltpu.sync_copy(stage, dst.at[b, pl.ds(i * ch, ch)])

    return shift(*caches)


def _split3(x):
    hi = x.astype(BF16)
    r = x - hi.astype(F32)
    mid = r.astype(BF16)
    lo = (r - mid.astype(F32)).astype(BF16)
    return hi, mid, lo


def _gla_kernel(p_ref, wup_ref, gb_ref, ng_ref, s0_ref, o_ref, st_ref, *, chunk, tb):
    @pl.when(pl.program_id(1) == 0)
    def _():
        st_ref[...] = s0_ref[...]

    sub = min(GLA_SUB, chunk)
    kq = H_B * DK_B

    def rb(x):
        xb = x.astype(BF16)
        return xb if chunk >= 16 else xb.astype(F32)

    row = lax.broadcasted_iota(jnp.int32, (chunk, chunk), 0)
    colm = lax.broadcasted_iota(jnp.int32, (chunk, chunk), 1)
    tri = rb(jnp.where(row >= colm, 1.0, 0.0))
    for c in range(tb // chunk):
        rows = slice(c * chunk, (c + 1) * chunk)
        z = _dot(rb(p_ref[rows, PB_LR:PB_W]), rb(wup_ref[...])) + gb_ref[...]
        la = -(jnp.maximum(-z, 0.0) + jnp.log1p(jnp.exp(-jnp.abs(z)))) * (1.0 / GLA_TAU)
        b = functools.reduce(lambda u, w: u + w, [_dot(tri, rb(t)) for t in _split3(la)])
        blast = b[chunk - 1:chunk, :]
        q = p_ref[rows, 0:kq] * (DK_B ** -0.5)
        k = p_ref[rows, kq:2 * kq]
        qin = rb(q * jnp.exp(b))
        kst = rb(k * jnp.exp(blast - b))
        for h in range(H_B):
            ks = slice(h * DK_B, (h + 1) * DK_B)
            vs = slice(h * DV_B, (h + 1) * DV_B)
            st = st_ref[h]
            vb = rb(p_ref[rows, 2 * kq + h * DV_B:2 * kq + (h + 1) * DV_B])
            o_inter = _dot_nt(qin[:, ks], rb(st))
            parts = []
            for blk in range(chunk // sub):
                r0 = blk * sub
                n = r0 + sub
                ref_b = b[r0 - 1:r0, ks] if blk > 0 else jnp.zeros((1, DK_B), F32)
                qi = rb(q[r0:n, ks] * jnp.exp(b[r0:n, ks] - ref_b))
                ki = rb(k[0:n, ks] * jnp.exp(jnp.minimum(ref_b - b[0:n, ks], GLA_EXP_CLAMP)))
                a = _dot_nt(qi, ki)
                ti = lax.broadcasted_iota(jnp.int32, (sub, n), 0) + r0
                si = lax.broadcasted_iota(jnp.int32, (sub, n), 1)
                a = jnp.where(si <= ti, a, 0.0)
                parts.append(_dot(rb(a), vb[0:n]))
            o = o_inter + (jnp.concatenate(parts, axis=0) if len(parts) > 1 else parts[0])
            st_ref[h] = st * jnp.exp(blast[:, ks]) + _dot_tn(vb, kst[:, ks])
            on = o * lax.rsqrt(jnp.mean(o * o, axis=-1, keepdims=True) + RMS_EPS) * ng_ref[...]
            rg = p_ref[rows, 2 * kq + H_B * DV_B + h * DV_B:2 * kq + H_B * DV_B + (h + 1) * DV_B]
            o_ref[rows, vs] = (on * (rg * jax.nn.sigmoid(rg))).astype(BF16)


def _gla(pb, wup, gb, ng, s0t, batch, seq, chunk, tb):
    p3 = pb.reshape(batch, seq, PB_W)
    o, st = pl.pallas_call(
        functools.partial(_gla_kernel, chunk=chunk, tb=tb),
        grid=(batch, seq // tb),
        in_specs=[pl.BlockSpec((None, tb, PB_W), lambda b, i: (b, i, 0)),
                  pl.BlockSpec(wup.shape, lambda b, i: (0, 0)),
                  pl.BlockSpec(gb.shape, lambda b, i: (0, 0)),
                  pl.BlockSpec(ng.shape, lambda b, i: (0, 0)),
                  pl.BlockSpec((None, H_B, DV_B, DK_B), lambda b, i: (b, 0, 0, 0))],
        out_specs=[pl.BlockSpec((None, tb, H_B * DV_B), lambda b, i: (b, i, 0)),
                   pl.BlockSpec((None, H_B, DV_B, DK_B), lambda b, i: (b, 0, 0, 0))],
        out_shape=[jax.ShapeDtypeStruct((batch, seq, H_B * DV_B), BF16),
                   jax.ShapeDtypeStruct((batch, H_B, DV_B, DK_B), F32)],
        compiler_params=_cparams(("parallel", "arbitrary")),
        name=f"gla_c{chunk}",
    )(p3, wup, gb, ng, s0t)
    return o.reshape(batch * seq, H_B * DV_B), st


def _gla_heads_kernel(p_ref, wup_ref, gb_ref, ng_ref, s0_ref, o_ref, st_ref, *, chunk, tb):
    @pl.when(pl.program_id(1) == 0)
    def _():
        st_ref[...] = s0_ref[...]

    sub = min(GLA_SUB, chunk)
    n_sub = chunk // sub
    kq, vw = H_B * DK_B, H_B * DV_B

    def rb(x):
        xb = x.astype(BF16)
        return xb if chunk >= 16 else xb.astype(F32)

    row = lax.broadcasted_iota(jnp.int32, (chunk, chunk), 0)
    colm = lax.broadcasted_iota(jnp.int32, (chunk, chunk), 1)
    tri = rb(jnp.where(row >= colm, 1.0, 0.0))
    lane_head = lax.broadcasted_iota(jnp.int32, (1, kq), 1) // DK_B
    diag_blocks = (lax.broadcasted_iota(jnp.int32, (vw, kq), 0) // DV_B
                   == lax.broadcasted_iota(jnp.int32, (vw, kq), 1) // DK_B)
    a_cols = LANE if n_sub > 1 else chunk
    t_in = lax.broadcasted_iota(jnp.int32, (H_B * chunk, a_cols), 0) % chunk
    s_in = lax.broadcasted_iota(jnp.int32, (H_B * chunk, a_cols), 1)
    for c in range(tb // chunk):
        rows = slice(c * chunk, (c + 1) * chunk)
        z = _dot(rb(p_ref[rows, PB_LR:PB_W]), rb(wup_ref[...])) + gb_ref[...]
        la = -(jnp.maximum(-z, 0.0) + jnp.log1p(jnp.exp(-jnp.abs(z)))) * (1.0 / GLA_TAU)
        b3 = _dot(tri, rb(jnp.concatenate(_split3(la), axis=1)))
        b = b3[:, :kq] + b3[:, kq:2 * kq] + b3[:, 2 * kq:]
        blast = b[chunk - 1:chunk, :]
        q = p_ref[rows, 0:kq] * (DK_B ** -0.5)
        k = p_ref[rows, kq:2 * kq]
        v = rb(p_ref[rows, 2 * kq:2 * kq + vw])
        st = st_ref[...]
        o_inter = _dot_nt(rb(q * jnp.exp(b)), rb(st))
        refs = [jnp.zeros((1, kq), F32)] + [b[i * sub - 1:i * sub, :] for i in range(1, n_sub)]
        ref_rows = jnp.concatenate([jnp.broadcast_to(r, (sub, kq)) for r in refs], axis=0)
        qs = q * jnp.exp(b - ref_rows)
        q_stack = rb(jnp.concatenate([jnp.where(lane_head == h, qs, 0.0) for h in range(H_B)], axis=0))
        k_parts = []
        for r in refs:
            k_parts.append(k * jnp.exp(jnp.minimum(r - b, GLA_EXP_CLAMP)))
            if a_cols > chunk:
                k_parts.append(jnp.zeros((a_cols - chunk, kq), F32))
        raw = _dot_nt(q_stack, rb(jnp.concatenate(k_parts, axis=0)))
        a = jnp.zeros((H_B * chunk, a_cols), F32)
        for i in range(n_sub):
            a = a + jnp.where(t_in // sub == i, raw[:, i * a_cols:(i + 1) * a_cols], 0.0)
        a = jnp.where(s_in <= t_in, a, 0.0)[:, :chunk]
        o_all = _dot(rb(a), v)
        o = o_inter + jnp.concatenate(
            [o_all[h * chunk:(h + 1) * chunk, h * DV_B:(h + 1) * DV_B] for h in range(H_B)], axis=1)
        st_ref[...] = st * jnp.exp(blast) + jnp.where(diag_blocks, _dot_tn(v, rb(k * jnp.exp(blast - b))), 0.0)
        normed = []
        for h in range(H_B):
            oh = o[:, h * DV_B:(h + 1) * DV_B]
            normed.append(oh * lax.rsqrt(jnp.mean(oh * oh, axis=-1, keepdims=True) + RMS_EPS) * ng_ref[...])
        rg = p_ref[rows, 2 * kq + vw:2 * kq + 2 * vw]
        o_ref[rows, :] = (jnp.concatenate(normed, axis=1) * (rg * jax.nn.sigmoid(rg))).astype(BF16)


def _gla_heads(pb, wup, gb, ng, s0, batch, seq, chunk, tb):
    kq, vw = H_B * DK_B, H_B * DV_B
    eye = jnp.eye(H_B, dtype=F32)
    s0t = jnp.swapaxes(s0, -1, -2)
    s0_bd = (s0t[:, :, :, None, :] * eye[None, :, None, :, None]).reshape(batch, vw, kq)
    p3 = pb.reshape(batch, seq, PB_W)
    o, st = pl.pallas_call(
        functools.partial(_gla_heads_kernel, chunk=chunk, tb=tb),
        grid=(batch, seq // tb),
        in_specs=[pl.BlockSpec((None, tb, PB_W), lambda b, i: (b, i, 0)),
                  pl.BlockSpec(wup.shape, lambda b, i: (0, 0)),
                  pl.BlockSpec(gb.shape, lambda b, i: (0, 0)),
                  pl.BlockSpec(ng.shape, lambda b, i: (0, 0)),
                  pl.BlockSpec((None, vw, kq), lambda b, i: (b, 0, 0))],
        out_specs=[pl.BlockSpec((None, tb, vw), lambda b, i: (b, i, 0)),
                   pl.BlockSpec((None, vw, kq), lambda b, i: (b, 0, 0))],
        out_shape=[jax.ShapeDtypeStruct((batch, seq, vw), BF16), jax.ShapeDtypeStruct((batch, vw, kq), F32)],
        compiler_params=_cparams(("parallel", "arbitrary")),
        name=f"gla_c{chunk}",
    )(p3, wup, gb, ng, s0_bd)
    st5 = st.reshape(batch, H_B, DV_B, H_B, DK_B)
    st_heads = jnp.stack([st5[:, h, :, h, :] for h in range(H_B)], axis=1)
    return o.reshape(batch * seq, vw), jnp.swapaxes(st_heads, -1, -2)


def _layer_norm(u, g, b):
    mu = jnp.mean(u, axis=-1, keepdims=True)
    d = u - mu
    var = jnp.mean(d * d, axis=-1, keepdims=True)
    return d * lax.rsqrt(var + LN_EPS) * g + b


def _merge_kernel(*refs, n_groups):
    x_ref = refs[0]
    oa_refs = refs[1:1 + 2 * n_groups] if n_groups > 1 else refs[1:2]
    rest = refs[1 + (2 * n_groups if n_groups > 1 else 1):]
    (ob_ref, pg_a_ref, pg_b_ref, wpa_ref, wpb_ref, wo_ref, g1_ref, b1_ref, rwh_ref, rwl_ref, rb_ref,
     x1_ref, x1p_ref, ti_ref, gt_ref) = rest
    if n_groups > 1:
        def lanes(r):
            return jnp.concatenate([r[h] for h in range(H_A)], axis=1)

        os_ = [lanes(r) for r in oa_refs[:n_groups]]
        ls = [lanes(r) for r in oa_refs[n_groups:]]
        mm = functools.reduce(jnp.maximum, ls)
        ws = [jnp.exp(x - mm) for x in ls]
        oa = sum(w * o for w, o in zip(ws, os_)) / sum(ws)
    else:
        oa = oa_refs[0][...]
    ya = _dot(oa.astype(BF16), wpa_ref[...])
    yb = _dot(ob_ref[...], wpb_ref[...])
    branch = jax.nn.sigmoid(pg_a_ref[...]) * ya + jax.nn.sigmoid(pg_b_ref[...]) * yb
    y = _dot(branch.astype(BF16), wo_ref[...])
    x1 = _layer_norm(DEEPNORM_ALPHA * x_ref[...] + y, g1_ref[...], b1_ref[...])
    x1_ref[...] = x1
    xh = x1.astype(BF16)
    xhf = xh.astype(F32)
    bits = lax.bitcast_convert_type(xhf, jnp.int32)
    half = D_MODEL // 2
    x1p_ref[...] = lax.shift_right_logical(bits[:, :half], 16) | bits[:, half:]
    xl = (x1 - xhf).astype(BF16)
    logits = _dot(xh, rwh_ref[...]) + _dot(xl, rwh_ref[...]) + _dot(xh, rwl_ref[...]) + rb_ref[...]
    lane = lax.broadcasted_iota(jnp.int32, logits.shape, 1)
    vals = logits
    top_v, top_i = [], []
    for _ in range(TOP_K):
        m = jnp.max(vals, axis=-1, keepdims=True)
        ik = jnp.min(jnp.where(vals == m, lane, LANE), axis=-1, keepdims=True)
        vals = jnp.where(lane == ik, -jnp.inf, vals)
        top_v.append(m)
        top_i.append(ik)
    es = [jnp.exp(v - top_v[0]) for v in top_v]
    tot = functools.reduce(lambda a, b: a + b, es)
    ti_ref[...] = jnp.concatenate(top_i, axis=1)
    gt_ref[...] = jnp.concatenate([e / tot for e in es], axis=1)


def _merge(x, oas, ob, pg, wpa, wpb, wo, g1, b1, rwh, rwl, rbp, tm):
    T = x.shape[0]
    n_groups = len(oas) // 2 if len(oas) > 1 else 1

    def row(w):
        return pl.BlockSpec((tm, w), lambda i: (i, 0))

    def const(a):
        return pl.BlockSpec(a.shape, lambda i: (0,) * a.ndim, pipeline_mode=pl.Buffered(1))

    def oa_spec(a):
        return pl.BlockSpec((H_A, tm, HEAD_DIM), lambda i: (0, i, 0)) if a.ndim == 3 else row(A_WIDTH)

    in_specs = ([row(D_MODEL)] + [oa_spec(a) for a in oas] + [row(H_B * DV_B), row(D_MODEL),
                pl.BlockSpec((tm, D_MODEL), lambda i: (i, 1))] + [const(a) for a in (wpa, wpb, wo, g1, b1, rwh, rwl, rbp)])
    return pl.pallas_call(
        functools.partial(_merge_kernel, n_groups=n_groups),
        grid=(T // tm,),
        in_specs=in_specs,
        out_specs=[row(D_MODEL), row(D_MODEL // 2), row(TOP_K), row(TOP_K)],
        out_shape=[jax.ShapeDtypeStruct((T, D_MODEL), F32), jax.ShapeDtypeStruct((T, D_MODEL // 2), jnp.int32),
                   jax.ShapeDtypeStruct((T, TOP_K), jnp.int32), jax.ShapeDtypeStruct((T, TOP_K), F32)],
        compiler_params=_cparams(("parallel",)),
        name=f"merge_g{n_groups}",
    )(x, *oas, ob, pg, pg, wpa, wpb, wo, g1, b1, rwh, rwl, rbp)


def _sc_gather(table, idx):
    info = plsc.get_sparse_core_info()
    n_workers = info.num_cores * info.num_subcores
    n, width = idx.shape[0], table.shape[1]
    per_worker = n // n_workers
    chunk = SC_CHUNK_BYTES // (width * table.dtype.itemsize)
    assert per_worker * n_workers == n and per_worker % chunk == 0 and chunk % 8 == 0
    mesh = plsc.VectorSubcoreMesh(core_axis_name="c", subcore_axis_name="s")

    @functools.partial(
        pl.kernel, mesh=mesh,
        out_type=jax.ShapeDtypeStruct((n, width), table.dtype),
        scratch_types=[pltpu.VMEM((chunk,), jnp.int32), pltpu.VMEM((chunk, width), table.dtype),
                       pltpu.SemaphoreType.DMA],
    )
    def gather(table_hbm, idx_hbm, out_hbm, idx_v, rows_v, sem):
        wid = lax.axis_index("s") * info.num_cores + lax.axis_index("c")
        base = wid * per_worker

        @pl.loop(0, per_worker // chunk)
        def _(c):
            off = pl.multiple_of(base + c * chunk, chunk)
            pltpu.sync_copy(idx_hbm.at[pl.ds(off, chunk)], idx_v)
            pltpu.async_copy(table_hbm.at[idx_v], rows_v, sem).wait()
            pltpu.sync_copy(rows_v, out_hbm.at[pl.ds(off, chunk)])

    return gather(table, idx)


def _sc_scatter_rows(table, idx, n_out):
    info = plsc.get_sparse_core_info()
    n_workers = info.num_cores * info.num_subcores
    n_idx, n = idx.shape
    width = table.shape[1]
    per_worker = n // n_workers
    max_rows = SC_SCATTER_BYTES // (width * table.dtype.itemsize)
    chunk = max(d for d in range(8, max_rows + 1, 8) if per_worker % d == 0)
    assert per_worker * n_workers == n and table.shape[0] == n
    mesh = plsc.VectorSubcoreMesh(core_axis_name="c", subcore_axis_name="s")

    @functools.partial(
        pl.kernel, mesh=mesh,
        out_type=jax.ShapeDtypeStruct((n_out, width), table.dtype),
        scratch_types=[pltpu.VMEM((chunk,), jnp.int32)] * n_idx + [pltpu.VMEM((chunk, width), table.dtype)],
    )
    def scatter(table_hbm, idx_hbm, out_hbm, *scratch):
        idx_vs, rows_v = scratch[:n_idx], scratch[n_idx]
        wid = lax.axis_index("s") * info.num_cores + lax.axis_index("c")
        base = wid * per_worker

        @pl.loop(0, per_worker // chunk)
        def _(c):
            off = pl.multiple_of(base + c * chunk, 8)
            pltpu.sync_copy(table_hbm.at[pl.ds(off, chunk)], rows_v)
            for k in range(n_idx):
                pltpu.sync_copy(idx_hbm.at[pl.ds(pl.multiple_of(k * n + off, 8), chunk)], idx_vs[k])
            for k in range(n_idx):
                pltpu.sync_copy(rows_v, out_hbm.at[idx_vs[k]])

    return scatter(table, idx.reshape(n_idx * n))


def _expert_kernel(be_ref, rows_ref, nu_ref, xs_ref, wg_ref, wu_ref, bg_ref, bu_ref, wd_ref, bd_ref, o_ref, hid_ref):
    del be_ref, nu_ref
    i = pl.program_id(0)
    p = pl.program_id(1)
    nrows = rows_ref[i]
    n_live = (nrows + (MOE_SB - 1)) // MOE_SB
    half = D_MODEL // 2

    @pl.when(p < MOE_NF)
    def _():
        def body(s, carry):
            r0 = pl.multiple_of(s * MOE_SB, MOE_SB)
            rid = r0 + lax.broadcasted_iota(jnp.int32, (MOE_SB, half), 0)
            packed = jnp.where(rid < nrows, xs_ref[pl.ds(r0, MOE_SB), :], 0)
            lo = lax.bitcast_convert_type(lax.shift_left(packed, 16), F32)
            hi = lax.bitcast_convert_type(packed & jnp.int32(-65536), F32)
            x = jnp.concatenate([lo.astype(BF16), hi.astype(BF16)], axis=1)
            g = jnp.minimum(_dot(x, wg_ref[...].astype(BF16)) + bg_ref[...], SWIGLU_LIMIT)
            u = jnp.clip(_dot(x, wu_ref[...].astype(BF16)) + bu_ref[...], -SWIGLU_LIMIT, SWIGLU_LIMIT)
            hid_ref[p, pl.ds(r0, MOE_SB), :] = ((u + 1.0) * g * jax.nn.sigmoid(SWIGLU_ALPHA * g)).astype(BF16)
            return carry

        lax.fori_loop(0, n_live, body, 0)

    @pl.when(p >= MOE_NF)
    def _():
        def body(s, carry):
            r0 = pl.multiple_of(s * MOE_SB, MOE_SB)
            y = bd_ref[...]
            for f in range(MOE_NF):
                y = y + _dot(hid_ref[f, pl.ds(r0, MOE_SB), :], wd_ref[f * MOE_TF:(f + 1) * MOE_TF, :].astype(BF16))
            o_ref[pl.ds(r0, MOE_SB), :] = y
            return carry

        def zero_body(s, carry):
            o_ref[pl.ds(pl.multiple_of(s * MOE_SB, MOE_SB), MOE_SB), :] = jnp.zeros((MOE_SB, MOE_TN), F32)
            return carry

        lax.fori_loop(0, n_live, body, 0)
        lax.fori_loop(n_live, MOE_BM // MOE_SB, zero_body, 0)


def _experts(xs, block_expert, block_rows, n_used, w_gate_up, b_gate_up, w_down, b_down):
    nf, nn = MOE_NF, D_MODEL // MOE_TN

    def gate_map(col0, lead):
        def index_map(i, p, be, rw, nu):
            ahead = p >= nf + nn - lead
            e = jnp.where(ahead, be[jnp.minimum(i + 1, nu[0] - 1)], be[i])
            return (e, 0, col0 + jnp.where(ahead, 0, jnp.minimum(p, nf - 1)))
        return index_map

    def down_map(i, p, be, rw, nu):
        parked = p < nf
        e = jnp.where(parked, be[jnp.maximum(i - 1, 0)], be[i])
        return (e, 0, jnp.where(parked, jnp.where(i > 0, nn - 1, 0), p - nf))

    grid_spec = pltpu.PrefetchScalarGridSpec(
        num_scalar_prefetch=3,
        grid=(n_used[0], nf + nn),
        in_specs=[
            pl.BlockSpec((MOE_BM, D_MODEL // 2), lambda i, p, be, rw, nu: (i, 0), pipeline_mode=pl.Buffered(1)),
            pl.BlockSpec((None, D_MODEL, MOE_TF), gate_map(0, nn // 2)),
            pl.BlockSpec((None, D_MODEL, MOE_TF), gate_map(nf, nn // 4)),
            pl.BlockSpec((None, 1, MOE_TF), gate_map(0, nn // 2)),
            pl.BlockSpec((None, 1, MOE_TF), gate_map(nf, nn // 4)),
            pl.BlockSpec((None, D_FF, MOE_TN), down_map),
            pl.BlockSpec((None, 1, MOE_TN), down_map),
        ],
        out_specs=pl.BlockSpec((MOE_BM, MOE_TN), lambda i, p, be, rw, nu: (i, jnp.maximum(p - nf, 0))),
        scratch_shapes=[pltpu.VMEM((nf, MOE_BM, MOE_TF), BF16)],
    )
    bgu = b_gate_up.reshape(N_EXPERTS, 1, 2 * D_FF)
    bd = b_down.reshape(N_EXPERTS, 1, D_MODEL)
    return pl.pallas_call(
        _expert_kernel,
        grid_spec=grid_spec,
        out_shape=jax.ShapeDtypeStruct((xs.shape[0], D_MODEL), F32),
        compiler_params=_cparams(("arbitrary", "arbitrary")),
        name="moe_experts",
    )(block_expert, block_rows, n_used, xs, w_gate_up, w_gate_up, bgu, bgu, w_down, bd)


def _route(top_i, n_blocks):
    T = top_i.shape[0]
    expert = top_i.reshape(T * TOP_K)
    onehot = (expert[:, None] == jnp.arange(N_EXPERTS, dtype=jnp.int32)[None, :]).astype(jnp.int32)
    csum = jnp.cumsum(onehot, axis=0)
    counts = csum[-1]
    rank = jnp.take_along_axis(csum, expert[:, None], axis=1)[:, 0] - 1
    bpe = (counts + MOE_BM - 1) // MOE_BM
    bend = jnp.cumsum(bpe)
    bstart = bend - bpe
    dest = (bstart[expert] * MOE_BM + rank).astype(jnp.int32)
    n_used = bend[-1]
    blk = jnp.arange(n_blocks, dtype=jnp.int32)
    be = jnp.minimum(jnp.searchsorted(bend, jnp.minimum(blk, n_used - 1), side="right"), N_EXPERTS - 1).astype(jnp.int32)
    rows = jnp.clip(counts[be] - (blk - bstart[be]) * MOE_BM, 0, MOE_BM)
    rows = jnp.where(blk < n_used, rows, 0).astype(jnp.int32)
    return dest.reshape(T, TOP_K).T, be, rows, n_used.reshape(1).astype(jnp.int32)


def _combine_kernel(x1_ref, ge_ref, gt_ref, g2_ref, b2_ref, o_ref):
    gt = gt_ref[...]
    m = gt[:, 0:1] * ge_ref[0]
    for k in range(1, TOP_K):
        m = m + gt[:, k:k + 1] * ge_ref[k]
    o_ref[...] = _layer_norm(DEEPNORM_ALPHA * x1_ref[...] + m, g2_ref[...], b2_ref[...])


def _combine(x1, ge, gate, g2, b2, row0, tm):
    n_rows = x1.shape[0]
    b0 = row0 // tm
    return pl.pallas_call(
        _combine_kernel,
        grid=(n_rows // tm,),
        in_specs=[pl.BlockSpec((tm, D_MODEL), lambda i: (i, 0)),
                  pl.BlockSpec((TOP_K, tm, D_MODEL), lambda i: (0, b0 + i, 0)),
                  pl.BlockSpec((tm, TOP_K), lambda i: (b0 + i, 0)),
                  pl.BlockSpec(g2.shape, lambda i: (0, 0)), pl.BlockSpec(b2.shape, lambda i: (0, 0))],
        out_specs=pl.BlockSpec((tm, D_MODEL), lambda i: (i, 0)),
        out_shape=jax.ShapeDtypeStruct((n_rows, D_MODEL), F32),
        compiler_params=_cparams(("parallel",)),
        name="moe_combine",
    )(x1, ge, gate, g2, b2)


def _layer(xp, xs, caches, state, w_in, rel_bias, gla_w_up, gla_b, gla_norm_g, w_pa, w_pb, w_o, ln1_g, ln1_b,
           router_w, router_b, w_gate_up, b_gate_up, w_down, b_down, ln2_g, ln2_b):
    batch, seq, _ = xp.shape
    dbatch, dseq, _ = xs.shape
    tp, ts = batch * seq, dbatch * dseq
    xp2, xs2 = xp.reshape(tp, D_MODEL), xs.reshape(ts, D_MODEL)

    o_b0, o_lr, o_g = PA_W, PA_W + PB_LR, PA_W + PB_LR + GLA_RANK
    w_a = w_in[:, :PA_W].astype(BF16)
    w_b = jnp.concatenate([w_in[:, o_b0:o_g], jnp.zeros((D_MODEL, LANE - GLA_RANK), F32)], axis=1).astype(BF16)
    w_g = w_in[:, o_g:].astype(BF16)
    wup = jnp.concatenate([gla_w_up, jnp.zeros((LANE - GLA_RANK, H_B * DK_B), F32)], axis=0).astype(BF16)
    gb = gla_b.reshape(1, H_B * DK_B)
    ng = gla_norm_g.reshape(1, DV_B)
    wpa, wpb, wo = w_pa.astype(BF16), w_pb.astype(BF16), w_o.astype(BF16)
    g1, b1 = ln1_g.reshape(1, D_MODEL), ln1_b.reshape(1, D_MODEL)
    g2, b2 = ln2_g.reshape(1, D_MODEL), ln2_b.reshape(1, D_MODEL)
    rw = jnp.concatenate([router_w, jnp.zeros((D_MODEL, LANE - N_EXPERTS), F32)], axis=1)
    rwh = rw.astype(BF16)
    rwl = (rw - rwh.astype(F32)).astype(BF16)
    rbp = jnp.concatenate([router_b, jnp.full((LANE - N_EXPERTS,), NEG, F32)]).reshape(1, LANE)
    caches8 = [c.reshape(dbatch, c.shape[1], KV_ROWS, HEAD_DIM) for c in caches]

    pa_p = _project(xp2, w_a, PROJ_TM, A_QKV_WIDTH, "proj_a_prompt", head_major=True)
    pb_p = _project(xp2, w_b, PROJ_TM, PB_W, "proj_b_prompt")
    pg_p = _project(xp2, w_g, PROJ_TM, 1024, "proj_g_prompt")
    oas, lses = [], []
    for g in range(N_GROUPS):
        o, lse = _attn_heads(pa_p, _prompt_table(rel_bias, g), g, batch, seq)
        oas.append(o)
        lses.append(lse)
    ob_p, st_p = _gla_heads(pb_p, wup, gb, ng, jnp.zeros((batch, H_B, DK_B, DV_B), F32), batch, seq, GLA_CHUNK, 256)
    x1_p, x1p_p, ti_p, gt_p = _merge(xp2, oas + lses, ob_p, pg_p, wpa, wpb, wo, g1, b1, rwh, rwl, rbp, 256)

    pa_s = _project(xs2, w_a, ts, A_QKV_WIDTH, "proj_a_sample", head_major=True)
    pb_s = _project(xs2, w_b, ts, PB_W, "proj_b_sample")
    pg_s = _project(xs2, w_g, ts, 1024, "proj_g_sample")
    tabc, tabn, combo_base = _sample_tables(rel_bias, dseq)
    new_rows = [_kv_pack(pa_s, g, dbatch, dseq, dseq, dseq) for g in range(N_GROUPS)]
    oa_s = _attn_sample(pa_s, new_rows, caches8, tabc, tabn, combo_base, dbatch, dseq)
    chunk_s = int(np.gcd(dseq, GLA_CHUNK))
    ob_s, st_s = _gla_heads(pb_s, wup, gb, ng, state, dbatch, dseq, chunk_s, dseq)
    x1_s, x1p_s, ti_s, gt_s = _merge(xs2, [oa_s], ob_s, pg_s, wpa, wpb, wo, g1, b1, rwh, rwl, rbp, ts)

    x1p = jnp.concatenate([x1p_p, x1p_s], axis=0)
    top_i = jnp.concatenate([ti_p, ti_s], axis=0)
    gate = jnp.concatenate([gt_p, gt_s], axis=0)
    t_all = tp + ts
    n_blocks = -(-(t_all * TOP_K) // MOE_BM) + N_EXPERTS
    dest, be, rows, n_used = _route(top_i, n_blocks)
    xsorted = _sc_scatter_rows(x1p, dest, n_blocks * MOE_BM)
    eo = _experts(xsorted, be, rows, n_used, w_gate_up, b_gate_up, w_down, b_down)
    ge = _sc_gather(eo, dest.reshape(TOP_K * t_all)).reshape(TOP_K, t_all, D_MODEL)
    y_p = _combine(x1_p, ge, gate, g2, b2, 0, 256)
    y_s = _combine(x1_s, ge, gate, g2, b2, tp, 256)

    shifted = _sc_cache_shift(caches8, dseq)
    bufs_p, bufs_s = [], []
    for g, (window, _) in enumerate(DILATED_GROUPS):
        keep = min(window, seq)
        bufs_p.append(_kv_pack(pa_p, g, batch, seq, keep, Q_BLOCK).reshape(batch, keep, 2, H_A, HEAD_DIM))
        clen = caches[g].shape[1]
        assert clen == window and dseq <= clen
        buf = lax.dynamic_update_slice(shifted[g], new_rows[g], (0, clen - dseq, 0, 0))
        bufs_s.append(buf.reshape(dbatch, clen, 2, H_A, HEAD_DIM))
    return y_p.reshape(batch, seq, D_MODEL), y_s.reshape(dbatch, dseq, D_MODEL), bufs_p, st_p, bufs_s, st_s


def kernel(x_prompt, x_sample, cache_a1_kv, cache_a2_kv, cache_a3_kv, state_b_s, w_in, rel_bias, gla_w_up, gla_b,
           gla_norm_g, w_pa, w_pb, w_o, ln1_g, ln1_b, router_w, router_b, w_gate_up, b_gate_up, w_down, b_down,
           ln2_g, ln2_b):
    assert w_in.shape[0] == DEPTH
    yp, ys, bufs_p, st_p, bufs_s, st_s = _layer(
        x_prompt, x_sample, (cache_a1_kv[0], cache_a2_kv[0], cache_a3_kv[0]), state_b_s[0], w_in[0], rel_bias,
        gla_w_up[0], gla_b[0], gla_norm_g[0], w_pa[0], w_pb[0], w_o[0], ln1_g[0], ln1_b[0], router_w[0], router_b[0],
        w_gate_up[0], b_gate_up[0], w_down[0], b_down[0], ln2_g[0], ln2_b[0])
    return (yp, ys, bufs_p[0][None], bufs_p[1][None], bufs_p[2][None], st_p[None],
            bufs_s[0][None], bufs_s[1][None], bufs_s[2][None], st_s[None].astype(state_b_s.dtype))
```

```python
import functools

import numpy as np
import jax
import jax.numpy as jnp
from jax import lax
from jax.experimental import pallas as pl
from jax.experimental.pallas import tpu as pltpu
from jax.experimental.pallas import tpu_sc as plsc

F32 = jnp.float32
BF16 = jnp.bfloat16

D_MODEL = 2048
HEAD_DIM = 128
DILATED_GROUPS = ((128, 1), (512, 4), (2048, 16))
N_GROUPS = 3
H_A = 4
A_WIDTH = H_A * HEAD_DIM
A_QKV_WIDTH = N_GROUPS * A_WIDTH
Q_BLOCK = 128
N_BUCKETS = 32
REL_MAX_DIST = 2048
H_B = 4
DK_B = 64
DV_B = 128
GLA_RANK = 16
GLA_TAU = 16.0
GLA_CHUNK = 64
GLA_SUB = 16
GLA_EXP_CLAMP = 80.0
N_EXPERTS = 32
TOP_K = 4
D_FF = 2048
SWIGLU_LIMIT = 7.0
SWIGLU_ALPHA = 1.702
LN_EPS = 1e-5
RMS_EPS = 1e-6
DEPTH = 1
DEEPNORM_ALPHA = (2.0 * DEPTH) ** 0.25
ATT_SCALE = HEAD_DIM ** -0.5
NEG = float(np.finfo(np.float32).min)

VMEM_LIMIT = 56 * 1024 * 1024
LANE = 128

PA_W = 3 * A_QKV_WIDTH
PB_LR = H_B * DK_B * 2 + H_B * DV_B * 2
PB_W = PB_LR + LANE
PG_W = 2 * D_MODEL

PROJ_TM = 1024

MOE_BM = 2560
MOE_SB = 320
MOE_TF = 512
MOE_NF = D_FF // MOE_TF
MOE_TN = 256
SC_CHUNK_BYTES = 128 * 1024
SC_SCATTER_BYTES = 160 * 1024


def _cparams(sem):
    return pltpu.CompilerParams(dimension_semantics=sem, vmem_limit_bytes=VMEM_LIMIT)


def _dot(a, b):
    return jnp.dot(a, b, preferred_element_type=F32)


def _dot_nt(a, b):
    return lax.dot_general(a, b, (((1,), (1,)), ((), ())), preferred_element_type=F32)


def _dot_tn(a, b):
    return lax.dot_general(a, b, (((0,), (0,)), ((), ())), preferred_element_type=F32)


def _proj_kernel(x_ref, w_ref, o_ref, xb_ref):
    @pl.when(pl.program_id(1) == 0)
    def _():
        xb_ref[...] = x_ref[...].astype(BF16)

    o_ref[...] = _dot(xb_ref[...], w_ref[...])


def _proj_heads_kernel(x_ref, w_ref, o_ref, xb_ref):
    @pl.when(pl.program_id(1) == 0)
    def _():
        xb_ref[...] = x_ref[...].astype(BF16)

    acc = _dot(xb_ref[...], w_ref[...])
    for c in range(o_ref.shape[0]):
        o_ref[c] = acc[:, c * HEAD_DIM:(c + 1) * HEAD_DIM]


def _project(x, w, tm, tn, name, head_major=False):
    T, D = x.shape
    N = w.shape[1]
    if head_major:
        nh = tn // HEAD_DIM
        out_spec = pl.BlockSpec((nh, tm, HEAD_DIM), lambda i, j: (j, i, 0))
        out_shape = jax.ShapeDtypeStruct((N // HEAD_DIM, T, HEAD_DIM), F32)
    else:
        out_spec = pl.BlockSpec((tm, tn), lambda i, j: (i, j))
        out_shape = jax.ShapeDtypeStruct((T, N), F32)
    return pl.pallas_call(
        _proj_heads_kernel if head_major else _proj_kernel,
        grid=(T // tm, N // tn),
        in_specs=[pl.BlockSpec((tm, D), lambda i, j: (i, 0)), pl.BlockSpec((D, tn), lambda i, j: (0, j))],
        out_specs=out_spec,
        out_shape=out_shape,
        scratch_shapes=[pltpu.VMEM((tm, D), BF16)],
        compiler_params=_cparams(("parallel", "arbitrary")),
        name=name,
    )(x, w)


def _t5_bucket(dist):
    max_exact = N_BUCKETS // 2
    d = np.maximum(dist, 1).astype(np.float32)
    large = max_exact + (np.log(d / max_exact) / np.log(REL_MAX_DIST / max_exact) * (N_BUCKETS - max_exact)).astype(np.int32)
    large = np.minimum(large, N_BUCKETS - 1)
    return np.where(dist < max_exact, dist, large).astype(np.int32)


def _bias_lookup(rel_bias, g, j, valid):
    _, dil = DILATED_GROUPS[g]
    bucket = _t5_bucket(dil * np.clip(j, 0, Q_BLOCK))
    onehot = bucket[None] == np.arange(N_BUCKETS).reshape((N_BUCKETS,) + (1,) * j.ndim)
    rb = rel_bias[:, g * H_A:(g + 1) * H_A].astype(F32).T.reshape((H_A, N_BUCKETS) + (1,) * j.ndim)
    vals = jnp.sum(jnp.where(onehot[None], rb, 0.0), axis=1)
    return jnp.where(valid[None], vals, NEG)


def _prompt_table(rel_bias, g):
    qi = np.arange(Q_BLOCK)[:, None]
    kj = np.arange(2 * Q_BLOCK)[None, :]
    j = Q_BLOCK + qi - kj
    return _bias_lookup(rel_bias, g, j, (j >= 0) & (j <= Q_BLOCK))


def _sample_tables(rel_bias, dec_seq):
    m = np.arange(Q_BLOCK)
    tabc, combo_base = [], []
    for g, (_, dil) in enumerate(DILATED_GROUPS):
        combo_base.append(len(tabc))
        for fl in range((dec_seq - 1) // dil + 1):
            j = Q_BLOCK + fl - m
            col = _bias_lookup(rel_bias, g, j, j <= Q_BLOCK).T
            col = jnp.concatenate([col, jnp.zeros_like(col)], axis=1)
            tabc.append(jnp.broadcast_to(col[:, :, None], (Q_BLOCK, 2 * H_A, LANE)))
    tabn = []
    s = np.arange(dec_seq)[:, None]
    sp = np.arange(dec_seq)[None, :]
    for g, (_, dil) in enumerate(DILATED_GROUPS):
        diff = s - sp
        t = _bias_lookup(rel_bias, g, diff // dil, (diff >= 0) & (diff % dil == 0))
        t = jnp.transpose(t, (1, 2, 0))
        t = jnp.concatenate([t, jnp.zeros_like(t)], axis=2)
        tabn.append(jnp.broadcast_to(t[..., None], (dec_seq, dec_seq, 2 * H_A, LANE)))
    return jnp.stack(tabc), jnp.stack(tabn), tuple(combo_base)


def _attn_prompt_kernel(q_ref, kc_ref, kp_ref, vc_ref, vp_ref, tab_ref, o_ref, lse_ref):
    has_prev = pl.program_id(2) > 0
    for h in range(H_A):
        sl = slice(h * HEAD_DIM, (h + 1) * HEAD_DIM)
        q = q_ref[:, sl].astype(BF16)
        sc = _dot_nt(q, kc_ref[:, sl].astype(BF16)) * ATT_SCALE + tab_ref[h, :, Q_BLOCK:]
        sp = _dot_nt(q, kp_ref[:, sl].astype(BF16)) * ATT_SCALE + tab_ref[h, :, :Q_BLOCK]
        sp = jnp.where(has_prev, sp, NEG)
        m = jnp.maximum(jnp.max(sc, axis=-1, keepdims=True), jnp.max(sp, axis=-1, keepdims=True))
        pc = jnp.exp(sc - m)
        pp = jnp.exp(sp - m)
        l = jnp.sum(pc, axis=-1, keepdims=True) + jnp.sum(pp, axis=-1, keepdims=True)
        inv = 1.0 / l
        o = _dot((pc * inv).astype(BF16), vc_ref[:, sl].astype(BF16)) + _dot((pp * inv).astype(BF16), vp_ref[:, sl].astype(BF16))
        o_ref[:, sl] = o
        lse_ref[:, sl] = jnp.broadcast_to(m + jnp.log(l), (Q_BLOCK, HEAD_DIM))


def _attn_prompt_strided_kernel(*refs, dil, with_prev):
    if with_prev:
        q_ref, kc_ref, kp_ref, vc_ref, vp_ref, tab_ref, o_ref, lse_ref = refs
    else:
        q_ref, kc_ref, vc_ref, tab_ref, o_ref, lse_ref = refs
    has_prev = pl.program_id(1) > 0

    def body(r, carry):
        idx = pl.ds(r, Q_BLOCK, stride=dil)
        q = q_ref[idx, :].astype(BF16)
        sc = _dot_nt(q, kc_ref[idx, :].astype(BF16)) * ATT_SCALE + tab_ref[:, Q_BLOCK:]
        m = jnp.max(sc, axis=-1, keepdims=True)
        if with_prev:
            sp = _dot_nt(q, kp_ref[idx, :].astype(BF16)) * ATT_SCALE + tab_ref[:, :Q_BLOCK]
            sp = jnp.where(has_prev, sp, NEG)
            m = jnp.maximum(m, jnp.max(sp, axis=-1, keepdims=True))
        pc = jnp.exp(sc - m)
        l = jnp.sum(pc, axis=-1, keepdims=True)
        if with_prev:
            pp = jnp.exp(sp - m)
            l = l + jnp.sum(pp, axis=-1, keepdims=True)
        inv = 1.0 / l
        o = _dot((pc * inv).astype(BF16), vc_ref[idx, :].astype(BF16))
        if with_prev:
            o = o + _dot((pp * inv).astype(BF16), vp_ref[idx, :].astype(BF16))
        o_ref[idx, :] = o
        lse_ref[idx, :] = jnp.broadcast_to(m + jnp.log(l), (Q_BLOCK, HEAD_DIM))
        return carry

    lax.fori_loop(0, dil, body, 0, unroll=min(dil, 4))


def _attn_prompt_strided(pa, table, g, batch, seq):
    _, dil = DILATED_GROUPS[g]
    rows = dil * Q_BLOCK
    nblk = seq // rows
    with_prev = nblk > 1
    hcols = A_QKV_WIDTH // HEAD_DIM

    def spec(sec, prev):
        if prev:
            return pl.BlockSpec((rows, HEAD_DIM), lambda b, i, h: (b * nblk + jnp.maximum(i - 1, 0), sec * hcols + g * H_A + h))
        return pl.BlockSpec((rows, HEAD_DIM), lambda b, i, h: (b * nblk + i, sec * hcols + g * H_A + h))

    in_specs = [spec(0, False), spec(1, False)] + ([spec(1, True)] if with_prev else []) + [spec(2, False)] + (
        [spec(2, True)] if with_prev else []) + [pl.BlockSpec((None, Q_BLOCK, 2 * Q_BLOCK), lambda b, i, h: (h, 0, 0))]
    out_spec = pl.BlockSpec((rows, HEAD_DIM), lambda b, i, h: (b * nblk + i, h))
    return pl.pallas_call(
        functools.partial(_attn_prompt_strided_kernel, dil=dil, with_prev=with_prev),
        grid=(batch, nblk, H_A),
        in_specs=in_specs,
        out_specs=[out_spec, out_spec],
        out_shape=[jax.ShapeDtypeStruct((batch * seq, A_WIDTH), F32)] * 2,
        compiler_params=_cparams(("parallel", "arbitrary", "arbitrary")),
        name=f"attn_prompt_g{g}",
    )(*([pa] * (len(in_specs) - 1)), table)


def _attn_prompt_group(pa, table, g, batch, seq):
    _, dil = DILATED_GROUPS[g]
    if dil > 1:
        return _attn_prompt_strided(pa, table, g, batch, seq)
    sub = seq // dil
    nqb = sub // Q_BLOCK
    wblk = PA_W // A_WIDTH
    pv = pa.reshape(batch, sub, dil * PA_W)

    def spec(off, prev):
        if prev:
            return pl.BlockSpec((None, Q_BLOCK, A_WIDTH), lambda b, r, i: (b, jnp.maximum(i - 1, 0), r * wblk + off + g))
        return pl.BlockSpec((None, Q_BLOCK, A_WIDTH), lambda b, r, i: (b, i, r * wblk + off + g))

    out_spec = pl.BlockSpec((None, Q_BLOCK, A_WIDTH), lambda b, r, i: (b, i, r))
    o, lse = pl.pallas_call(
        _attn_prompt_kernel,
        grid=(batch, dil, nqb),
        in_specs=[spec(0, False), spec(N_GROUPS, False), spec(N_GROUPS, True), spec(2 * N_GROUPS, False),
                  spec(2 * N_GROUPS, True), pl.BlockSpec((H_A, Q_BLOCK, 2 * Q_BLOCK), lambda b, r, i: (0, 0, 0))],
        out_specs=[out_spec, out_spec],
        out_shape=[jax.ShapeDtypeStruct((batch, sub, dil * A_WIDTH), F32)] * 2,
        compiler_params=_cparams(("parallel", "parallel", "arbitrary")),
        name=f"attn_prompt_g{g}",
    )(pv, pv, pv, pv, pv, table)
    return o.reshape(batch * seq, A_WIDTH), lse.reshape(batch * seq, A_WIDTH)


def _attn_heads_kernel(*refs, dil, nb, with_prev):
    if with_prev:
        q_ref, kc_ref, kp_ref, vc_ref, vp_ref, tab_ref, o_ref, lse_ref = refs
    else:
        q_ref, kc_ref, vc_ref, tab_ref, o_ref, lse_ref = refs
    has_prev = pl.program_id(1) > 0
    zero = jnp.zeros((Q_BLOCK, HEAD_DIM), BF16)
    tab_c = tab_ref[:, Q_BLOCK:]
    if with_prev:
        tab = jnp.concatenate([jnp.where(has_prev, tab_ref[:, :Q_BLOCK], NEG), tab_c], axis=1)
    else:
        tab = tab_c

    def heads(ref, bi, idx):
        return jnp.concatenate([ref[h, bi, idx, :].astype(BF16) for h in range(H_A)], axis=1)

    def body(u, carry):
        bi, r = u // dil, u % dil
        idx = pl.ds(r, Q_BLOCK, stride=dil)
        qs = [q_ref[h, bi, idx, :].astype(BF16) for h in range(H_A)]
        q_stack = jnp.concatenate(
            [jnp.concatenate([qs[h] if c == h else zero for c in range(H_A)], axis=1) for h in range(H_A)], axis=0)
        if with_prev:
            k_all = jnp.concatenate([heads(kp_ref, bi, idx), heads(kc_ref, bi, idx)], axis=0)
            v_all = jnp.concatenate([heads(vp_ref, bi, idx), heads(vc_ref, bi, idx)], axis=0)
        else:
            k_all, v_all = heads(kc_ref, bi, idx), heads(vc_ref, bi, idx)
        s = _dot_nt(q_stack, k_all) * ATT_SCALE + tab
        m = jnp.max(s, axis=-1, keepdims=True)
        p = jnp.exp(s - m)
        l = jnp.sum(p, axis=-1, keepdims=True)
        o = _dot((p * (1.0 / l)).astype(BF16), v_all)
        lse = m + jnp.log(l)
        for h in range(H_A):
            rows = slice(h * Q_BLOCK, (h + 1) * Q_BLOCK)
            o_ref[h, bi, idx, :] = o[rows, h * HEAD_DIM:(h + 1) * HEAD_DIM]
            lse_ref[h, bi, idx, :] = jnp.broadcast_to(lse[rows], (Q_BLOCK, HEAD_DIM))
        return carry

    lax.fori_loop(0, nb * dil, body, 0, unroll=min(nb * dil, 2))


def _attn_heads(pa_hm, table, g, batch, seq):
    _, dil = DILATED_GROUPS[g]
    rows = dil * Q_BLOCK
    nblk = seq // rows
    with_prev = nblk > 1
    nb = 2 if (dil == 1 and batch % 2 == 0) else 1
    pv = pa_hm.reshape(pa_hm.shape[0], batch, seq, HEAD_DIM)

    def spec(sec, prev):
        if prev:
            return pl.BlockSpec((H_A, nb, rows, HEAD_DIM), lambda b, i: (sec * N_GROUPS + g, b, jnp.maximum(i - 1, 0), 0))
        return pl.BlockSpec((H_A, nb, rows, HEAD_DIM), lambda b, i: (sec * N_GROUPS + g, b, i, 0))

    in_specs = [spec(0, False), spec(1, False)] + ([spec(1, True)] if with_prev else []) + [spec(2, False)] + (
        [spec(2, True)] if with_prev else []) + [pl.BlockSpec((H_A * Q_BLOCK, 2 * Q_BLOCK), lambda b, i: (0, 0))]
    out_spec = pl.BlockSpec((H_A, nb, rows, HEAD_DIM), lambda b, i: (0, b, i, 0))
    o, lse = pl.pallas_call(
        functools.partial(_attn_heads_kernel, dil=dil, nb=nb, with_prev=with_prev),
        grid=(batch // nb, nblk),
        in_specs=in_specs,
        out_specs=[out_spec, out_spec],
        out_shape=[jax.ShapeDtypeStruct((H_A, batch, seq, HEAD_DIM), F32)] * 2,
        compiler_params=_cparams(("parallel", "arbitrary")),
        name=f"attn_prompt_g{g}",
    )(*([pv] * (len(in_specs) - 1)), table.reshape(H_A * Q_BLOCK, 2 * Q_BLOCK))
    return o.reshape(H_A, batch * seq, HEAD_DIM), lse.reshape(H_A, batch * seq, HEAD_DIM)


KV_ROWS = 2 * H_A


def _kv_pack_kernel(k_ref, v_ref, o_ref):
    o_ref[...] = jnp.stack([k_ref[h] for h in range(H_A)] + [v_ref[h] for h in range(H_A)], axis=1)


def _kv_pack(pa_hm, g, batch, seq, keep, tm):
    nblk, blk0, per_b = keep // tm, (seq - keep) // tm, seq // tm
    out = pl.pallas_call(
        _kv_pack_kernel,
        grid=(batch, nblk),
        in_specs=[pl.BlockSpec((H_A, tm, HEAD_DIM), lambda b, i: (N_GROUPS + g, b * per_b + blk0 + i, 0)),
                  pl.BlockSpec((H_A, tm, HEAD_DIM), lambda b, i: (2 * N_GROUPS + g, b * per_b + blk0 + i, 0))],
        out_specs=pl.BlockSpec((tm, KV_ROWS, HEAD_DIM), lambda b, i: (b * nblk + i, 0, 0)),
        out_shape=jax.ShapeDtypeStruct((batch * keep, KV_ROWS, HEAD_DIM), F32),
        compiler_params=_cparams(("parallel", "parallel")),
        name=f"kv_pack_g{g}_{keep}",
    )(pa_hm, pa_hm)
    return out.reshape(batch, keep, KV_ROWS, HEAD_DIM)


def _attn_sample_kernel(qkv_ref, n1_ref, n2_ref, n3_ref, c1_ref, c2_ref, c3_ref, tabc_ref, tabn_ref, o_ref, *,
                        dec_seq, combo_base):
    caches = (c1_ref, c2_ref, c3_ref)
    news = (n1_ref, n2_ref, n3_ref)
    zeros = jnp.zeros((H_A, HEAD_DIM), F32)
    for s in range(dec_seq):
        outs, lses = [], []
        for g, (_, dil) in enumerate(DILATED_GROUPS):
            rho, fl = s % dil, s // dil
            qm = jnp.concatenate([qkv_ref[g * H_A + h, s:s + 1, :] for h in range(H_A)] + [zeros], axis=0)
            kc = caches[g][:, rho]
            kn = news[g][...]
            sc = jnp.sum(kc * qm[None], axis=-1, keepdims=True) * ATT_SCALE + tabc_ref[combo_base[g] + fl]
            sn = jnp.sum(kn * qm[None], axis=-1, keepdims=True) * ATT_SCALE + tabn_ref[g, s]
            m = jnp.maximum(jnp.max(sc, axis=0), jnp.max(sn, axis=0))
            pc = jnp.exp(sc - m[None])
            pn = jnp.exp(sn - m[None])
            l = jnp.sum(pc, axis=0) + jnp.sum(pn, axis=0)
            acc = jnp.sum(pltpu.roll(pc, H_A, 1) * kc, axis=0) + jnp.sum(pltpu.roll(pn, H_A, 1) * kn, axis=0)
            outs.append(acc / pltpu.roll(l, H_A, 0))
            lses.append(pltpu.roll(m + jnp.log(l), H_A, 0))
        mm = jnp.maximum(jnp.maximum(lses[0], lses[1]), lses[2])
        ws = [jnp.exp(x - mm) for x in lses]
        o_ref[s] = (ws[0] * outs[0] + ws[1] * outs[1] + ws[2] * outs[2]) / (ws[0] + ws[1] + ws[2])


def _attn_sample(pa_hm, new_rows, caches, tabc, tabn, combo_base, batch, dec_seq):
    views, specs = [], []
    for g, (window, dil) in enumerate(DILATED_GROUPS):
        assert caches[g].shape[1] == window and dec_seq <= Q_BLOCK
        views.append(caches[g].reshape(batch, Q_BLOCK, dil, KV_ROWS, HEAD_DIM))
        used = min(dil, dec_seq)
        specs.append(pl.BlockSpec((None, Q_BLOCK, used, KV_ROWS, HEAD_DIM), lambda b: (b, 0, 0, 0, 0)))
    new_spec = pl.BlockSpec((None, dec_seq, KV_ROWS, HEAD_DIM), lambda b: (b, 0, 0, 0))
    out = pl.pallas_call(
        functools.partial(_attn_sample_kernel, dec_seq=dec_seq, combo_base=combo_base),
        grid=(batch,),
        in_specs=[pl.BlockSpec((N_GROUPS * H_A, dec_seq, HEAD_DIM), lambda b: (0, b, 0))] + [new_spec] * N_GROUPS + specs + [
            pl.BlockSpec(tabc.shape, lambda b: (0, 0, 0, 0)), pl.BlockSpec(tabn.shape, lambda b: (0, 0, 0, 0, 0))],
        out_specs=pl.BlockSpec((None, dec_seq, KV_ROWS, HEAD_DIM), lambda b: (b, 0, 0, 0)),
        out_shape=jax.ShapeDtypeStruct((batch, dec_seq, KV_ROWS, HEAD_DIM), F32),
        compiler_params=_cparams(("parallel",)),
        name="attn_sample",
    )(pa_hm, *new_rows, *views, tabc, tabn)
    return out[:, :, H_A:, :].reshape(batch * dec_seq, A_WIDTH)


def _sc_cache_shift(caches, drop):
    info = plsc.get_sparse_core_info()
    n_workers = info.num_cores * info.num_subcores
    batch = caches[0].shape[0]
    assert batch % n_workers == 0
    mesh = plsc.VectorSubcoreMesh(core_axis_name="c", subcore_axis_name="s")

    row_bytes = KV_ROWS * HEAD_DIM * 4
    chunks = []
    for c in caches:
        keep = c.shape[1] - drop
        ch = max(d for d in range(1, SC_CHUNK_BYTES // row_bytes + 1) if keep % d == 0)
        chunks.append(ch)
    buf_rows = max(chunks)

    @functools.partial(pl.kernel, mesh=mesh, out_type=[jax.ShapeDtypeStruct(c.shape, c.dtype) for c in caches],
                       scratch_types=[pltpu.VMEM((buf_rows, KV_ROWS, HEAD_DIM), caches[0].dtype)])
    def shift(*refs):
        srcs, dsts, buf = refs[:len(caches)], refs[len(caches):2 * len(caches)], refs[-1]
        wid = lax.axis_index("s") * info.num_cores + lax.axis_index("c")
        for j in range(batch // n_workers):
            b = wid * (batch // n_workers) + j
            for src, dst, ch in zip(srcs, dsts, chunks):
                stage = buf.at[pl.ds(0, ch)]

                @pl.loop(0, (src.shape[1] - drop) // ch)
                def _(i):
                    pltpu.sync_copy(src.at[b, pl.ds(drop + i * ch, ch)], stage)
                    pltpu.sync_copy(stage, dst.at[b, pl.ds(i * ch, ch)])

    return shift(*caches)


def _split3(x):
    hi = x.astype(BF16)
    r = x - hi.astype(F32)
    mid = r.astype(BF16)
    lo = (r - mid.astype(F32)).astype(BF16)
    return hi, mid, lo


def _gla_kernel(p_ref, wup_ref, gb_ref, ng_ref, s0_ref, o_ref, st_ref, *, chunk, tb):
    @pl.when(pl.program_id(1) == 0)
    def _():
        st_ref[...] = s0_ref[...]

    sub = min(GLA_SUB, chunk)
    kq = H_B * DK_B

    def rb(x):
        xb = x.astype(BF16)
        return xb if chunk >= 16 else xb.astype(F32)

    row = lax.broadcasted_iota(jnp.int32, (chunk, chunk), 0)
    colm = lax.broadcasted_iota(jnp.int32, (chunk, chunk), 1)
    tri = rb(jnp.where(row >= colm, 1.0, 0.0))
    for c in range(tb // chunk):
        rows = slice(c * chunk, (c + 1) * chunk)
        z = _dot(rb(p_ref[rows, PB_LR:PB_W]), rb(wup_ref[...])) + gb_ref[...]
        la = -(jnp.maximum(-z, 0.0) + jnp.log1p(jnp.exp(-jnp.abs(z)))) * (1.0 / GLA_TAU)
        b = functools.reduce(lambda u, w: u + w, [_dot(tri, rb(t)) for t in _split3(la)])
        blast = b[chunk - 1:chunk, :]
        q = p_ref[rows, 0:kq] * (DK_B ** -0.5)
        k = p_ref[rows, kq:2 * kq]
        qin = rb(q * jnp.exp(b))
        kst = rb(k * jnp.exp(blast - b))
        for h in range(H_B):
            ks = slice(h * DK_B, (h + 1) * DK_B)
            vs = slice(h * DV_B, (h + 1) * DV_B)
            st = st_ref[h]
            vb = rb(p_ref[rows, 2 * kq + h * DV_B:2 * kq + (h + 1) * DV_B])
            o_inter = _dot_nt(qin[:, ks], rb(st))
            parts = []
            for blk in range(chunk // sub):
                r0 = blk * sub
                n = r0 + sub
                ref_b = b[r0 - 1:r0, ks] if blk > 0 else jnp.zeros((1, DK_B), F32)
                qi = rb(q[r0:n, ks] * jnp.exp(b[r0:n, ks] - ref_b))
                ki = rb(k[0:n, ks] * jnp.exp(jnp.minimum(ref_b - b[0:n, ks], GLA_EXP_CLAMP)))
                a = _dot_nt(qi, ki)
                ti = lax.broadcasted_iota(jnp.int32, (sub, n), 0) + r0
                si = lax.broadcasted_iota(jnp.int32, (sub, n), 1)
                a = jnp.where(si <= ti, a, 0.0)
                parts.append(_dot(rb(a), vb[0:n]))
            o = o_inter + (jnp.concatenate(parts, axis=0) if len(parts) > 1 else parts[0])
            st_ref[h] = st * jnp.exp(blast[:, ks]) + _dot_tn(vb, kst[:, ks])
            on = o * lax.rsqrt(jnp.mean(o * o, axis=-1, keepdims=True) + RMS_EPS) * ng_ref[...]
            rg = p_ref[rows, 2 * kq + H_B * DV_B + h * DV_B:2 * kq + H_B * DV_B + (h + 1) * DV_B]
            o_ref[rows, vs] = (on * (rg * jax.nn.sigmoid(rg))).astype(BF16)


def _gla(pb, wup, gb, ng, s0t, batch, seq, chunk, tb):
    p3 = pb.reshape(batch, seq, PB_W)
    o, st = pl.pallas_call(
        functools.partial(_gla_kernel, chunk=chunk, tb=tb),
        grid=(batch, seq // tb),
        in_specs=[pl.BlockSpec((None, tb, PB_W), lambda b, i: (b, i, 0)),
                  pl.BlockSpec(wup.shape, lambda b, i: (0, 0)),
                  pl.BlockSpec(gb.shape, lambda b, i: (0, 0)),
                  pl.BlockSpec(ng.shape, lambda b, i: (0, 0)),
                  pl.BlockSpec((None, H_B, DV_B, DK_B), lambda b, i: (b, 0, 0, 0))],
        out_specs=[pl.BlockSpec((None, tb, H_B * DV_B), lambda b, i: (b, i, 0)),
                   pl.BlockSpec((None, H_B, DV_B, DK_B), lambda b, i: (b, 0, 0, 0))],
        out_shape=[jax.ShapeDtypeStruct((batch, seq, H_B * DV_B), BF16),
                   jax.ShapeDtypeStruct((batch, H_B, DV_B, DK_B), F32)],
        compiler_params=_cparams(("parallel", "arbitrary")),
        name=f"gla_c{chunk}",
    )(p3, wup, gb, ng, s0t)
    return o.reshape(batch * seq, H_B * DV_B), st


def _gla_heads_kernel(p_ref, wup_ref, gb_ref, ng_ref, s0_ref, o_ref, st_ref, *, chunk, tb):
    @pl.when(pl.program_id(1) == 0)
    def _():
        st_ref[...] = s0_ref[...]

    sub = min(GLA_SUB, chunk)
    n_sub = chunk // sub
    kq, vw = H_B * DK_B, H_B * DV_B

    def rb(x):
        xb = x.astype(BF16)
        return xb if chunk >= 16 else xb.astype(F32)

    row = lax.broadcasted_iota(jnp.int32, (chunk, chunk), 0)
    colm = lax.broadcasted_iota(jnp.int32, (chunk, chunk), 1)
    tri = rb(jnp.where(row >= colm, 1.0, 0.0))
    lane_head = lax.broadcasted_iota(jnp.int32, (1, kq), 1) // DK_B
    diag_blocks = (lax.broadcasted_iota(jnp.int32, (vw, kq), 0) // DV_B
                   == lax.broadcasted_iota(jnp.int32, (vw, kq), 1) // DK_B)
    a_cols = LANE if n_sub > 1 else chunk
    t_in = lax.broadcasted_iota(jnp.int32, (H_B * chunk, a_cols), 0) % chunk
    s_in = lax.broadcasted_iota(jnp.int32, (H_B * chunk, a_cols), 1)
    for c in range(tb // chunk):
        rows = slice(c * chunk, (c + 1) * chunk)
        z = _dot(rb(p_ref[rows, PB_LR:PB_W]), rb(wup_ref[...])) + gb_ref[...]
        la = -(jnp.maximum(-z, 0.0) + jnp.log1p(jnp.exp(-jnp.abs(z)))) * (1.0 / GLA_TAU)
        b3 = _dot(tri, rb(jnp.concatenate(_split3(la), axis=1)))
        b = b3[:, :kq] + b3[:, kq:2 * kq] + b3[:, 2 * kq:]
        blast = b[chunk - 1:chunk, :]
        q = p_ref[rows, 0:kq] * (DK_B ** -0.5)
        k = p_ref[rows, kq:2 * kq]
        v = rb(p_ref[rows, 2 * kq:2 * kq + vw])
        st = st_ref[...]
        o_inter = _dot_nt(rb(q * jnp.exp(b)), rb(st))
        refs = [jnp.zeros((1, kq), F32)] + [b[i * sub - 1:i * sub, :] for i in range(1, n_sub)]
        ref_rows = jnp.concatenate([jnp.broadcast_to(r, (sub, kq)) for r in refs], axis=0)
        qs = q * jnp.exp(b - ref_rows)
        q_stack = rb(jnp.concatenate([jnp.where(lane_head == h, qs, 0.0) for h in range(H_B)], axis=0))
        k_parts = []
        for r in refs:
            k_parts.append(k * jnp.exp(jnp.minimum(r - b, GLA_EXP_CLAMP)))
            if a_cols > chunk:
                k_parts.append(jnp.zeros((a_cols - chunk, kq), F32))
        raw = _dot_nt(q_stack, rb(jnp.concatenate(k_parts, axis=0)))
        a = jnp.zeros((H_B * chunk, a_cols), F32)
        for i in range(n_sub):
            a = a + jnp.where(t_in // sub == i, raw[:, i * a_cols:(i + 1) * a_cols], 0.0)
        a = jnp.where(s_in <= t_in, a, 0.0)[:, :chunk]
        o_all = _dot(rb(a), v)
        o = o_inter + jnp.concatenate(
            [o_all[h * chunk:(h + 1) * chunk, h * DV_B:(h + 1) * DV_B] for h in range(H_B)], axis=1)
        st_ref[...] = st * jnp.exp(blast) + jnp.where(diag_blocks, _dot_tn(v, rb(k * jnp.exp(blast - b))), 0.0)
        normed = []
        for h in range(H_B):
            oh = o[:, h * DV_B:(h + 1) * DV_B]
            normed.append(oh * lax.rsqrt(jnp.mean(oh * oh, axis=-1, keepdims=True) + RMS_EPS) * ng_ref[...])
        rg = p_ref[rows, 2 * kq + vw:2 * kq + 2 * vw]
        o_ref[rows, :] = (jnp.concatenate(normed, axis=1) * (rg * jax.nn.sigmoid(rg))).astype(BF16)


def _gla_heads(pb, wup, gb, ng, s0, batch, seq, chunk, tb):
    kq, vw = H_B * DK_B, H_B * DV_B
    eye = jnp.eye(H_B, dtype=F32)
    s0t = jnp.swapaxes(s0, -1, -2)
    s0_bd = (s0t[:, :, :, None, :] * eye[None, :, None, :, None]).reshape(batch, vw, kq)
    p3 = pb.reshape(batch, seq, PB_W)
    o, st = pl.pallas_call(
        functools.partial(_gla_heads_kernel, chunk=chunk, tb=tb),
        grid=(batch, seq // tb),
        in_specs=[pl.BlockSpec((None, tb, PB_W), lambda b, i: (b, i, 0)),
                  pl.BlockSpec(wup.shape, lambda b, i: (0, 0)),
                  pl.BlockSpec(gb.shape, lambda b, i: (0, 0)),
                  pl.BlockSpec(ng.shape, lambda b, i: (0, 0)),
                  pl.BlockSpec((None, vw, kq), lambda b, i: (b, 0, 0))],
        out_specs=[pl.BlockSpec((None, tb, vw), lambda b, i: (b, i, 0)),
                   pl.BlockSpec((None, vw, kq), lambda b, i: (b, 0, 0))],
        out_shape=[jax.ShapeDtypeStruct((batch, seq, vw), BF16), jax.ShapeDtypeStruct((batch, vw, kq), F32)],
        compiler_params=_cparams(("parallel", "arbitrary")),
        name=f"gla_c{chunk}",
    )(p3, wup, gb, ng, s0_bd)
    st5 = st.reshape(batch, H_B, DV_B, H_B, DK_B)
    st_heads = jnp.stack([st5[:, h, :, h, :] for h in range(H_B)], axis=1)
    return o.reshape(batch * seq, vw), jnp.swapaxes(st_heads, -1, -2)


def _layer_norm(u, g, b):
    mu = jnp.mean(u, axis=-1, keepdims=True)
    d = u - mu
    var = jnp.mean(d * d, axis=-1, keepdims=True)
    return d * lax.rsqrt(var + LN_EPS) * g + b


def _merge_kernel(*refs, n_groups):
    x_ref = refs[0]
    oa_refs = refs[1:1 + 2 * n_groups] if n_groups > 1 else refs[1:2]
    rest = refs[1 + (2 * n_groups if n_groups > 1 else 1):]
    (ob_ref, pg_a_ref, pg_b_ref, wpa_ref, wpb_ref, wo_ref, g1_ref, b1_ref, rwh_ref, rwl_ref, rb_ref, cnt0_ref,
     x1_ref, x1p_ref, ti_ref, gt_ref, rank_ref, cnt_ref) = rest
    if n_groups > 1:
        def lanes(r):
            return jnp.concatenate([r[h] for h in range(H_A)], axis=1)

        os_ = [lanes(r) for r in oa_refs[:n_groups]]
        ls = [lanes(r) for r in oa_refs[n_groups:]]
        mm = functools.reduce(jnp.maximum, ls)
        ws = [jnp.exp(x - mm) for x in ls]
        oa = sum(w * o for w, o in zip(ws, os_)) / sum(ws)
    else:
        oa = oa_refs[0][...]
    ya = _dot(oa.astype(BF16), wpa_ref[...])
    yb = _dot(ob_ref[...], wpb_ref[...])
    branch = jax.nn.sigmoid(pg_a_ref[...]) * ya + jax.nn.sigmoid(pg_b_ref[...]) * yb
    y = _dot(branch.astype(BF16), wo_ref[...])
    x1 = _layer_norm(DEEPNORM_ALPHA * x_ref[...] + y, g1_ref[...], b1_ref[...])
    x1_ref[...] = x1
    xh = x1.astype(BF16)
    xhf = xh.astype(F32)
    bits = lax.bitcast_convert_type(xhf, jnp.int32)
    half = D_MODEL // 2
    x1p_ref[...] = lax.shift_right_logical(bits[:, :half], 16) | bits[:, half:]
    xl = (x1 - xhf).astype(BF16)
    logits = _dot(xh, rwh_ref[...]) + _dot(xl, rwh_ref[...]) + _dot(xh, rwl_ref[...]) + rb_ref[...]
    lane = lax.broadcasted_iota(jnp.int32, logits.shape, 1)
    vals = logits
    top_v, top_i = [], []
    for _ in range(TOP_K):
        m = jnp.max(vals, axis=-1, keepdims=True)
        ik = jnp.min(jnp.where(vals == m, lane, LANE), axis=-1, keepdims=True)
        vals = jnp.where(lane == ik, -jnp.inf, vals)
        top_v.append(m)
        top_i.append(ik)
    es = [jnp.exp(v - top_v[0]) for v in top_v]
    tot = functools.reduce(lambda a, b: a + b, es)
    ti_ref[...] = jnp.concatenate(top_i, axis=1)
    gt_ref[...] = jnp.concatenate([e / tot for e in es], axis=1)
    @pl.when(pl.program_id(0) == 0)
    def _():
        cnt_ref[...] = cnt0_ref[...]

    tm = logits.shape[0]
    picks = [jnp.where(lane == ik, 1.0, 0.0) for ik in top_i]
    onehot = picks[0] + picks[1] + picks[2] + picks[3]
    tri = jnp.where(lax.broadcasted_iota(jnp.int32, (tm, tm), 0) > lax.broadcasted_iota(jnp.int32, (tm, tm), 1), 1.0, 0.0)
    before = _dot(tri.astype(BF16), onehot.astype(BF16)) + cnt_ref[...]
    rank_ref[...] = jnp.concatenate([jnp.sum(pk * before, axis=-1, keepdims=True) for pk in picks], axis=1).astype(jnp.int32)
    cnt_ref[...] = cnt_ref[...] + jnp.sum(onehot, axis=0, keepdims=True)


def _merge(x, oas, ob, pg, wpa, wpb, wo, g1, b1, rwh, rwl, rbp, cnt0, tm):
    T = x.shape[0]
    n_groups = len(oas) // 2 if len(oas) > 1 else 1

    def row(w):
        return pl.BlockSpec((tm, w), lambda i: (i, 0))

    def const(a):
        return pl.BlockSpec(a.shape, lambda i: (0,) * a.ndim, pipeline_mode=pl.Buffered(1))

    def oa_spec(a):
        return pl.BlockSpec((H_A, tm, HEAD_DIM), lambda i: (0, i, 0)) if a.ndim == 3 else row(A_WIDTH)

    in_specs = ([row(D_MODEL)] + [oa_spec(a) for a in oas] + [row(H_B * DV_B), row(D_MODEL),
                pl.BlockSpec((tm, D_MODEL), lambda i: (i, 1))]
                + [const(a) for a in (wpa, wpb, wo, g1, b1, rwh, rwl, rbp, cnt0)])
    return pl.pallas_call(
        functools.partial(_merge_kernel, n_groups=n_groups),
        grid=(T // tm,),
        in_specs=in_specs,
        out_specs=[row(D_MODEL), row(D_MODEL // 2), row(TOP_K), row(TOP_K), row(TOP_K),
                   pl.BlockSpec((1, LANE), lambda i: (0, 0))],
        out_shape=[jax.ShapeDtypeStruct((T, D_MODEL), F32), jax.ShapeDtypeStruct((T, D_MODEL // 2), jnp.int32),
                   jax.ShapeDtypeStruct((T, TOP_K), jnp.int32), jax.ShapeDtypeStruct((T, TOP_K), F32),
                   jax.ShapeDtypeStruct((T, TOP_K), jnp.int32), jax.ShapeDtypeStruct((1, LANE), F32)],
        compiler_params=_cparams(("arbitrary",)),
        name=f"merge_g{n_groups}",
    )(x, *oas, ob, pg, pg, wpa, wpb, wo, g1, b1, rwh, rwl, rbp, cnt0)


def _sc_gather(table, idx):
    info = plsc.get_sparse_core_info()
    n_workers = info.num_cores * info.num_subcores
    n, width = idx.shape[0], table.shape[1]
    per_worker = n // n_workers
    chunk = SC_CHUNK_BYTES // (width * table.dtype.itemsize)
    assert per_worker * n_workers == n and per_worker % chunk == 0 and chunk % 8 == 0
    mesh = plsc.VectorSubcoreMesh(core_axis_name="c", subcore_axis_name="s")

    @functools.partial(
        pl.kernel, mesh=mesh,
        out_type=jax.ShapeDtypeStruct((n, width), table.dtype),
        scratch_types=[pltpu.VMEM((chunk,), jnp.int32), pltpu.VMEM((chunk, width), table.dtype),
                       pltpu.SemaphoreType.DMA],
    )
    def gather(table_hbm, idx_hbm, out_hbm, idx_v, rows_v, sem):
        wid = lax.axis_index("s") * info.num_cores + lax.axis_index("c")
        base = wid * per_worker

        @pl.loop(0, per_worker // chunk)
        def _(c):
            off = pl.multiple_of(base + c * chunk, chunk)
            pltpu.sync_copy(idx_hbm.at[pl.ds(off, chunk)], idx_v)
            pltpu.async_copy(table_hbm.at[idx_v], rows_v, sem).wait()
            pltpu.sync_copy(rows_v, out_hbm.at[pl.ds(off, chunk)])

    return gather(table, idx)


def _sc_scatter_rows(table, idx, n_out):
    info = plsc.get_sparse_core_info()
    n_workers = info.num_cores * info.num_subcores
    n_idx, n = idx.shape
    width = table.shape[1]
    per_worker = n // n_workers
    max_rows = SC_SCATTER_BYTES // (width * table.dtype.itemsize)
    chunk = max(d for d in range(8, max_rows + 1, 8) if per_worker % d == 0)
    assert per_worker * n_workers == n and table.shape[0] == n
    mesh = plsc.VectorSubcoreMesh(core_axis_name="c", subcore_axis_name="s")

    @functools.partial(
        pl.kernel, mesh=mesh,
        out_type=jax.ShapeDtypeStruct((n_out, width), table.dtype),
        scratch_types=[pltpu.VMEM((chunk,), jnp.int32)] * n_idx + [pltpu.VMEM((chunk, width), table.dtype)],
    )
    def scatter(table_hbm, idx_hbm, out_hbm, *scratch):
        idx_vs, rows_v = scratch[:n_idx], scratch[n_idx]
        wid = lax.axis_index("s") * info.num_cores + lax.axis_index("c")
        base = wid * per_worker

        @pl.loop(0, per_worker // chunk)
        def _(c):
            off = pl.multiple_of(base + c * chunk, 8)
            pltpu.sync_copy(table_hbm.at[pl.ds(off, chunk)], rows_v)
            for k in range(n_idx):
                pltpu.sync_copy(idx_hbm.at[pl.ds(pl.multiple_of(k * n + off, 8), chunk)], idx_vs[k])
            for k in range(n_idx):
                pltpu.sync_copy(rows_v, out_hbm.at[idx_vs[k]])

    return scatter(table, idx.reshape(n_idx * n))


def _expert_kernel(be_ref, rows_ref, nu_ref, xs_ref, wg_ref, wu_ref, bg_ref, bu_ref, wd_ref, bd_ref, o_ref, hid_ref):
    del be_ref, nu_ref
    i = pl.program_id(0)
    p = pl.program_id(1)
    nrows = rows_ref[i]
    n_live = (nrows + (MOE_SB - 1)) // MOE_SB
    half = D_MODEL // 2

    @pl.when(p < MOE_NF)
    def _():
        def body(s, carry):
            r0 = pl.multiple_of(s * MOE_SB, MOE_SB)
            rid = r0 + lax.broadcasted_iota(jnp.int32, (MOE_SB, half), 0)
            packed = jnp.where(rid < nrows, xs_ref[pl.ds(r0, MOE_SB), :], 0)
            lo = lax.bitcast_convert_type(lax.shift_left(packed, 16), F32)
            hi = lax.bitcast_convert_type(packed & jnp.int32(-65536), F32)
            x = jnp.concatenate([lo.astype(BF16), hi.astype(BF16)], axis=1)
            g = jnp.minimum(_dot(x, wg_ref[...].astype(BF16)) + bg_ref[...], SWIGLU_LIMIT)
            u = jnp.clip(_dot(x, wu_ref[...].astype(BF16)) + bu_ref[...], -SWIGLU_LIMIT, SWIGLU_LIMIT)
            hid_ref[p, pl.ds(r0, MOE_SB), :] = ((u + 1.0) * g * jax.nn.sigmoid(SWIGLU_ALPHA * g)).astype(BF16)
            return carry

        lax.fori_loop(0, n_live, body, 0)

    @pl.when(p >= MOE_NF)
    def _():
        def body(s, carry):
            r0 = pl.multiple_of(s * MOE_SB, MOE_SB)
            y = bd_ref[...]
            for f in range(MOE_NF):
                y = y + _dot(hid_ref[f, pl.ds(r0, MOE_SB), :], wd_ref[f * MOE_TF:(f + 1) * MOE_TF, :].astype(BF16))
            bits = lax.bitcast_convert_type(y.astype(BF16).astype(F32), jnp.int32)
            o_ref[pl.ds(r0, MOE_SB), :] = lax.shift_right_logical(bits[:, :MOE_TN // 2], 16) | bits[:, MOE_TN // 2:]
            return carry

        def zero_body(s, carry):
            o_ref[pl.ds(pl.multiple_of(s * MOE_SB, MOE_SB), MOE_SB), :] = jnp.zeros((MOE_SB, MOE_TN // 2), jnp.int32)
            return carry

        lax.fori_loop(0, n_live, body, 0)
        lax.fori_loop(n_live, MOE_BM // MOE_SB, zero_body, 0)


def _experts(xs, block_expert, block_rows, n_used, w_gate_up, b_gate_up, w_down, b_down):
    nf, nn = MOE_NF, D_MODEL // MOE_TN

    def gate_map(col0, lead):
        def index_map(i, p, be, rw, nu):
            ahead = p >= nf + nn - lead
            e = jnp.where(ahead, be[jnp.minimum(i + 1, nu[0] - 1)], be[i])
            return (e, 0, col0 + jnp.where(ahead, 0, jnp.minimum(p, nf - 1)))
        return index_map

    def down_map(i, p, be, rw, nu):
        parked = p < nf
        e = jnp.where(parked, be[jnp.maximum(i - 1, 0)], be[i])
        return (e, 0, jnp.where(parked, jnp.where(i > 0, nn - 1, 0), p - nf))

    grid_spec = pltpu.PrefetchScalarGridSpec(
        num_scalar_prefetch=3,
        grid=(n_used[0], nf + nn),
        in_specs=[
            pl.BlockSpec((MOE_BM, D_MODEL // 2), lambda i, p, be, rw, nu: (i, 0), pipeline_mode=pl.Buffered(1)),
            pl.BlockSpec((None, D_MODEL, MOE_TF), gate_map(0, nn // 2)),
            pl.BlockSpec((None, D_MODEL, MOE_TF), gate_map(nf, nn // 4)),
            pl.BlockSpec((None, 1, MOE_TF), gate_map(0, nn // 2)),
            pl.BlockSpec((None, 1, MOE_TF), gate_map(nf, nn // 4)),
            pl.BlockSpec((None, D_FF, MOE_TN), down_map),
            pl.BlockSpec((None, 1, MOE_TN), down_map),
        ],
        out_specs=pl.BlockSpec((MOE_BM, MOE_TN // 2), lambda i, p, be, rw, nu: (i, jnp.maximum(p - nf, 0))),
        scratch_shapes=[pltpu.VMEM((nf, MOE_BM, MOE_TF), BF16)],
    )
    bgu = b_gate_up.reshape(N_EXPERTS, 1, 2 * D_FF)
    bd = b_down.reshape(N_EXPERTS, 1, D_MODEL)
    return pl.pallas_call(
        _expert_kernel,
        grid_spec=grid_spec,
        out_shape=jax.ShapeDtypeStruct((xs.shape[0], D_MODEL // 2), jnp.int32),
        compiler_params=_cparams(("arbitrary", "arbitrary")),
        name="moe_experts",
    )(block_expert, block_rows, n_used, xs, w_gate_up, w_gate_up, bgu, bgu, w_down, bd)


def _route(top_i, rank, counts, n_blocks):
    T = top_i.shape[0]
    bpe = (counts + MOE_BM - 1) // MOE_BM
    bend = jnp.cumsum(bpe)
    bstart = bend - bpe
    experts = jnp.arange(N_EXPERTS, dtype=jnp.int32)
    start_of = jnp.sum(jnp.where(top_i[:, :, None] == experts, bstart * MOE_BM, 0), axis=-1)
    dest = (start_of + rank).astype(jnp.int32)
    n_used = bend[-1]
    blk = jnp.arange(n_blocks, dtype=jnp.int32)
    be = jnp.minimum(jnp.searchsorted(bend, jnp.minimum(blk, n_used - 1), side="right"), N_EXPERTS - 1).astype(jnp.int32)
    rows = jnp.clip(counts[be] - (blk - bstart[be]) * MOE_BM, 0, MOE_BM)
    rows = jnp.where(blk < n_used, rows, 0).astype(jnp.int32)
    return dest.reshape(T, TOP_K).T, be, rows, n_used.reshape(1).astype(jnp.int32)


def _combine_kernel(x1_ref, ge_ref, gt_ref, g2_ref, b2_ref, o_ref):
    hw, nn = MOE_TN // 2, D_MODEL // MOE_TN

    def unpack(w):
        lo = lax.bitcast_convert_type(lax.shift_left(w, 16), F32)
        hi = lax.bitcast_convert_type(w & jnp.int32(-65536), F32)
        return jnp.concatenate([part[:, n * hw:(n + 1) * hw] for n in range(nn) for part in (lo, hi)], axis=1)

    gt = gt_ref[...]
    m = gt[:, 0:1] * unpack(ge_ref[0])
    for k in range(1, TOP_K):
        m = m + gt[:, k:k + 1] * unpack(ge_ref[k])
    o_ref[...] = _layer_norm(DEEPNORM_ALPHA * x1_ref[...] + m, g2_ref[...], b2_ref[...])


def _combine(x1, ge, gate, g2, b2, row0, tm):
    n_rows = x1.shape[0]
    b0 = row0 // tm
    return pl.pallas_call(
        _combine_kernel,
        grid=(n_rows // tm,),
        in_specs=[pl.BlockSpec((tm, D_MODEL), lambda i: (i, 0)),
                  pl.BlockSpec((TOP_K, tm, D_MODEL // 2), lambda i: (0, b0 + i, 0)),
                  pl.BlockSpec((tm, TOP_K), lambda i: (b0 + i, 0)),
                  pl.BlockSpec(g2.shape, lambda i: (0, 0)), pl.BlockSpec(b2.shape, lambda i: (0, 0))],
        out_specs=pl.BlockSpec((tm, D_MODEL), lambda i: (i, 0)),
        out_shape=jax.ShapeDtypeStruct((n_rows, D_MODEL), F32),
        compiler_params=_cparams(("parallel",)),
        name="moe_combine",
    )(x1, ge, gate, g2, b2)


def _layer(xp, xs, caches, state, w_in, rel_bias, gla_w_up, gla_b, gla_norm_g, w_pa, w_pb, w_o, ln1_g, ln1_b,
           router_w, router_b, w_gate_up, b_gate_up, w_down, b_down, ln2_g, ln2_b):
    batch, seq, _ = xp.shape
    dbatch, dseq, _ = xs.shape
    tp, ts = batch * seq, dbatch * dseq
    xp2, xs2 = xp.reshape(tp, D_MODEL), xs.reshape(ts, D_MODEL)

    o_b0, o_lr, o_g = PA_W, PA_W + PB_LR, PA_W + PB_LR + GLA_RANK
    w_a = w_in[:, :PA_W].astype(BF16)
    w_b = jnp.concatenate([w_in[:, o_b0:o_g], jnp.zeros((D_MODEL, LANE - GLA_RANK), F32)], axis=1).astype(BF16)
    w_g = w_in[:, o_g:].astype(BF16)
    wup = jnp.concatenate([gla_w_up, jnp.zeros((LANE - GLA_RANK, H_B * DK_B), F32)], axis=0).astype(BF16)
    gb = gla_b.reshape(1, H_B * DK_B)
    ng = gla_norm_g.reshape(1, DV_B)
    wpa, wpb, wo = w_pa.astype(BF16), w_pb.astype(BF16), w_o.astype(BF16)
    g1, b1 = ln1_g.reshape(1, D_MODEL), ln1_b.reshape(1, D_MODEL)
    g2, b2 = ln2_g.reshape(1, D_MODEL), ln2_b.reshape(1, D_MODEL)
    rw = jnp.concatenate([router_w, jnp.zeros((D_MODEL, LANE - N_EXPERTS), F32)], axis=1)
    rwh = rw.astype(BF16)
    rwl = (rw - rwh.astype(F32)).astype(BF16)
    rbp = jnp.concatenate([router_b, jnp.full((LANE - N_EXPERTS,), NEG, F32)]).reshape(1, LANE)
    caches8 = [c.reshape(dbatch, c.shape[1], KV_ROWS, HEAD_DIM) for c in caches]

    pa_p = _project(xp2, w_a, PROJ_TM, A_QKV_WIDTH, "proj_a_prompt", head_major=True)
    pb_p = _project(xp2, w_b, PROJ_TM, PB_W, "proj_b_prompt")
    pg_p = _project(xp2, w_g, PROJ_TM, 1024, "proj_g_prompt")
    oas, lses = [], []
    for g in range(N_GROUPS):
        o, lse = _attn_heads(pa_p, _prompt_table(rel_bias, g), g, batch, seq)
        oas.append(o)
        lses.append(lse)
    ob_p, st_p = _gla_heads(pb_p, wup, gb, ng, jnp.zeros((batch, H_B, DK_B, DV_B), F32), batch, seq, GLA_CHUNK, 256)
    x1_p, x1p_p, ti_p, gt_p, rk_p, cnt_p = _merge(xp2, oas + lses, ob_p, pg_p, wpa, wpb, wo, g1, b1, rwh, rwl, rbp,
                                                  jnp.zeros((1, LANE), F32), 256)

    pa_s = _project(xs2, w_a, ts, A_QKV_WIDTH, "proj_a_sample", head_major=True)
    pb_s = _project(xs2, w_b, ts, PB_W, "proj_b_sample")
    pg_s = _project(xs2, w_g, ts, 1024, "proj_g_sample")
    tabc, tabn, combo_base = _sample_tables(rel_bias, dseq)
    new_rows = [_kv_pack(pa_s, g, dbatch, dseq, dseq, dseq) for g in range(N_GROUPS)]
    oa_s = _attn_sample(pa_s, new_rows, caches8, tabc, tabn, combo_base, dbatch, dseq)
    chunk_s = int(np.gcd(dseq, GLA_CHUNK))
    ob_s, st_s = _gla_heads(pb_s, wup, gb, ng, state, dbatch, dseq, chunk_s, dseq)
    x1_s, x1p_s, ti_s, gt_s, rk_s, cnt_all = _merge(xs2, [oa_s], ob_s, pg_s, wpa, wpb, wo, g1, b1, rwh, rwl, rbp, cnt_p, ts)

    x1p = jnp.concatenate([x1p_p, x1p_s], axis=0)
    top_i = jnp.concatenate([ti_p, ti_s], axis=0)
    gate = jnp.concatenate([gt_p, gt_s], axis=0)
    t_all = tp + ts
    n_blocks = -(-(t_all * TOP_K) // MOE_BM) + N_EXPERTS
    rank = jnp.concatenate([rk_p, rk_s], axis=0)
    counts = cnt_all[0, :N_EXPERTS].astype(jnp.int32)
    dest, be, rows, n_used = _route(top_i, rank, counts, n_blocks)
    xsorted = _sc_scatter_rows(x1p, dest, n_blocks * MOE_BM)
    eo = _experts(xsorted, be, rows, n_used, w_gate_up, b_gate_up, w_down, b_down)
    ge = _sc_gather(eo, dest.reshape(TOP_K * t_all)).reshape(TOP_K, t_all, D_MODEL // 2)
    y_p = _combine(x1_p, ge, gate, g2, b2, 0, 256)
    y_s = _combine(x1_s, ge, gate, g2, b2, tp, 256)

    shifted = _sc_cache_shift(caches8, dseq)
    bufs_p, bufs_s = [], []
    for g, (window, _) in enumerate(DILATED_GROUPS):
        keep = min(window, seq)
        bufs_p.append(_kv_pack(pa_p, g, batch, seq, keep, Q_BLOCK).reshape(batch, keep, 2, H_A, HEAD_DIM))
        clen = caches[g].shape[1]
        assert clen == window and dseq <= clen
        buf = lax.dynamic_update_slice(shifted[g], new_rows[g], (0, clen - dseq, 0, 0))
        bufs_s.append(buf.reshape(dbatch, clen, 2, H_A, HEAD_DIM))
    return y_p.reshape(batch, seq, D_MODEL), y_s.reshape(dbatch, dseq, D_MODEL), bufs_p, st_p, bufs_s, st_s


def kernel(x_prompt, x_sample, cache_a1_kv, cache_a2_kv, cache_a3_kv, state_b_s, w_in, rel_bias, gla_w_up, gla_b,
           gla_norm_g, w_pa, w_pb, w_o, ln1_g, ln1_b, router_w, router_b, w_gate_up, b_gate_up, w_down, b_down,
           ln2_g, ln2_b):
    assert w_in.shape[0] == DEPTH
    yp, ys, bufs_p, st_p, bufs_s, st_s = _layer(
        x_prompt, x_sample, (cache_a1_kv[0], cache_a2_kv[0], cache_a3_kv[0]), state_b_s[0], w_in[0], rel_bias,
        gla_w_up[0], gla_b[0], gla_norm_g[0], w_pa[0], w_pb[0], w_o[0], ln1_g[0], ln1_b[0], router_w[0], router_b[0],
        w_gate_up[0], b_gate_up[0], w_down[0], b_down[0], ln2_g[0], ln2_b[0])
    return (yp, ys, bufs_p[0][None], bufs_p[1][None], bufs_p[2][None], st_p[None],
            bufs_s[0][None], bufs_s[1][None], bufs_s[2][None], st_s[None].astype(state_b_s.dtype))
```

```python
import functools

import numpy as np
import jax
import jax.numpy as jnp
from jax import lax
from jax.experimental import pallas as pl
from jax.experimental.pallas import tpu as pltpu
from jax.experimental.pallas import tpu_sc as plsc

F32 = jnp.float32
BF16 = jnp.bfloat16

D_MODEL = 2048
HEAD_DIM = 128
DILATED_GROUPS = ((128, 1), (512, 4), (2048, 16))
N_GROUPS = 3
H_A = 4
A_WIDTH = H_A * HEAD_DIM
A_QKV_WIDTH = N_GROUPS * A_WIDTH
Q_BLOCK = 128
N_BUCKETS = 32
REL_MAX_DIST = 2048
H_B = 4
DK_B = 64
DV_B = 128
GLA_RANK = 16
GLA_TAU = 16.0
GLA_CHUNK = 64
GLA_SUB = 16
GLA_EXP_CLAMP = 80.0
N_EXPERTS = 32
TOP_K = 4
D_FF = 2048
SWIGLU_LIMIT = 7.0
SWIGLU_ALPHA = 1.702
LN_EPS = 1e-5
RMS_EPS = 1e-6
DEPTH = 1
DEEPNORM_ALPHA = (2.0 * DEPTH) ** 0.25
ATT_SCALE = HEAD_DIM ** -0.5
NEG = float(np.finfo(np.float32).min)

VMEM_LIMIT = 56 * 1024 * 1024
LANE = 128

PA_W = 3 * A_QKV_WIDTH
PB_LR = H_B * DK_B * 2 + H_B * DV_B * 2
PB_W = PB_LR + LANE
PG_W = 2 * D_MODEL

PROJ_TM = 1024

MOE_BM = 2560
MOE_SB = 320
MOE_TF = 512
MOE_NF = D_FF // MOE_TF
MOE_TN = 256
SC_CHUNK_BYTES = 128 * 1024
SC_SCATTER_BYTES = 160 * 1024


def _cparams(sem):
    return pltpu.CompilerParams(dimension_semantics=sem, vmem_limit_bytes=VMEM_LIMIT)


def _dot(a, b):
    return jnp.dot(a, b, preferred_element_type=F32)


def _dot_nt(a, b):
    return lax.dot_general(a, b, (((1,), (1,)), ((), ())), preferred_element_type=F32)


def _dot_tn(a, b):
    return lax.dot_general(a, b, (((0,), (0,)), ((), ())), preferred_element_type=F32)


def _proj_kernel(x_ref, w_ref, o_ref, xb_ref):
    @pl.when(pl.program_id(1) == 0)
    def _():
        xb_ref[...] = x_ref[...].astype(BF16)

    o_ref[...] = _dot(xb_ref[...], w_ref[...])


def _proj_heads_kernel(x_ref, w_ref, o_ref, xb_ref):
    @pl.when(pl.program_id(1) == 0)
    def _():
        xb_ref[...] = x_ref[...].astype(BF16)

    acc = _dot(xb_ref[...], w_ref[...])
    for c in range(o_ref.shape[0]):
        o_ref[c] = acc[:, c * HEAD_DIM:(c + 1) * HEAD_DIM]


def _project(x, w, tm, tn, name, head_major=False):
    T, D = x.shape
    N = w.shape[1]
    if head_major:
        nh = tn // HEAD_DIM
        out_spec = pl.BlockSpec((nh, tm, HEAD_DIM), lambda i, j: (j, i, 0))
        out_shape = jax.ShapeDtypeStruct((N // HEAD_DIM, T, HEAD_DIM), F32)
    else:
        out_spec = pl.BlockSpec((tm, tn), lambda i, j: (i, j))
        out_shape = jax.ShapeDtypeStruct((T, N), F32)
    return pl.pallas_call(
        _proj_heads_kernel if head_major else _proj_kernel,
        grid=(T // tm, N // tn),
        in_specs=[pl.BlockSpec((tm, D), lambda i, j: (i, 0)), pl.BlockSpec((D, tn), lambda i, j: (0, j))],
        out_specs=out_spec,
        out_shape=out_shape,
        scratch_shapes=[pltpu.VMEM((tm, D), BF16)],
        compiler_params=_cparams(("parallel", "arbitrary")),
        name=name,
    )(x, w)


def _t5_bucket(dist):
    max_exact = N_BUCKETS // 2
    d = np.maximum(dist, 1).astype(np.float32)
    large = max_exact + (np.log(d / max_exact) / np.log(REL_MAX_DIST / max_exact) * (N_BUCKETS - max_exact)).astype(np.int32)
    large = np.minimum(large, N_BUCKETS - 1)
    return np.where(dist < max_exact, dist, large).astype(np.int32)


def _bias_lookup(rel_bias, g, j, valid):
    _, dil = DILATED_GROUPS[g]
    bucket = _t5_bucket(dil * np.clip(j, 0, Q_BLOCK))
    onehot = bucket[None] == np.arange(N_BUCKETS).reshape((N_BUCKETS,) + (1,) * j.ndim)
    rb = rel_bias[:, g * H_A:(g + 1) * H_A].astype(F32).T.reshape((H_A, N_BUCKETS) + (1,) * j.ndim)
    vals = jnp.sum(jnp.where(onehot[None], rb, 0.0), axis=1)
    return jnp.where(valid[None], vals, NEG)


def _prompt_table(rel_bias, g):
    qi = np.arange(Q_BLOCK)[:, None]
    kj = np.arange(2 * Q_BLOCK)[None, :]
    j = Q_BLOCK + qi - kj
    return _bias_lookup(rel_bias, g, j, (j >= 0) & (j <= Q_BLOCK))


def _sample_tables(rel_bias, dec_seq):
    m = np.arange(Q_BLOCK)
    tabc, combo_base = [], []
    for g, (_, dil) in enumerate(DILATED_GROUPS):
        combo_base.append(len(tabc))
        for fl in range((dec_seq - 1) // dil + 1):
            j = Q_BLOCK + fl - m
            col = _bias_lookup(rel_bias, g, j, j <= Q_BLOCK).T
            col = jnp.concatenate([col, jnp.zeros_like(col)], axis=1)
            tabc.append(jnp.broadcast_to(col[:, :, None], (Q_BLOCK, 2 * H_A, LANE)))
    tabn = []
    s = np.arange(dec_seq)[:, None]
    sp = np.arange(dec_seq)[None, :]
    for g, (_, dil) in enumerate(DILATED_GROUPS):
        diff = s - sp
        t = _bias_lookup(rel_bias, g, diff // dil, (diff >= 0) & (diff % dil == 0))
        t = jnp.transpose(t, (1, 2, 0))
        t = jnp.concatenate([t, jnp.zeros_like(t)], axis=2)
        tabn.append(jnp.broadcast_to(t[..., None], (dec_seq, dec_seq, 2 * H_A, LANE)))
    return jnp.stack(tabc), jnp.stack(tabn), tuple(combo_base)


def _attn_prompt_kernel(q_ref, kc_ref, kp_ref, vc_ref, vp_ref, tab_ref, o_ref, lse_ref):
    has_prev = pl.program_id(2) > 0
    for h in range(H_A):
        sl = slice(h * HEAD_DIM, (h + 1) * HEAD_DIM)
        q = q_ref[:, sl].astype(BF16)
        sc = _dot_nt(q, kc_ref[:, sl].astype(BF16)) * ATT_SCALE + tab_ref[h, :, Q_BLOCK:]
        sp = _dot_nt(q, kp_ref[:, sl].astype(BF16)) * ATT_SCALE + tab_ref[h, :, :Q_BLOCK]
        sp = jnp.where(has_prev, sp, NEG)
        m = jnp.maximum(jnp.max(sc, axis=-1, keepdims=True), jnp.max(sp, axis=-1, keepdims=True))
        pc = jnp.exp(sc - m)
        pp = jnp.exp(sp - m)
        l = jnp.sum(pc, axis=-1, keepdims=True) + jnp.sum(pp, axis=-1, keepdims=True)
        inv = 1.0 / l
        o = _dot((pc * inv).astype(BF16), vc_ref[:, sl].astype(BF16)) + _dot((pp * inv).astype(BF16), vp_ref[:, sl].astype(BF16))
        o_ref[:, sl] = o
        lse_ref[:, sl] = jnp.broadcast_to(m + jnp.log(l), (Q_BLOCK, HEAD_DIM))


def _attn_prompt_strided_kernel(*refs, dil, with_prev):
    if with_prev:
        q_ref, kc_ref, kp_ref, vc_ref, vp_ref, tab_ref, o_ref, lse_ref = refs
    else:
        q_ref, kc_ref, vc_ref, tab_ref, o_ref, lse_ref = refs
    has_prev = pl.program_id(1) > 0

    def body(r, carry):
        idx = pl.ds(r, Q_BLOCK, stride=dil)
        q = q_ref[idx, :].astype(BF16)
        sc = _dot_nt(q, kc_ref[idx, :].astype(BF16)) * ATT_SCALE + tab_ref[:, Q_BLOCK:]
        m = jnp.max(sc, axis=-1, keepdims=True)
        if with_prev:
            sp = _dot_nt(q, kp_ref[idx, :].astype(BF16)) * ATT_SCALE + tab_ref[:, :Q_BLOCK]
            sp = jnp.where(has_prev, sp, NEG)
            m = jnp.maximum(m, jnp.max(sp, axis=-1, keepdims=True))
        pc = jnp.exp(sc - m)
        l = jnp.sum(pc, axis=-1, keepdims=True)
        if with_prev:
            pp = jnp.exp(sp - m)
            l = l + jnp.sum(pp, axis=-1, keepdims=True)
        inv = 1.0 / l
        o = _dot((pc * inv).astype(BF16), vc_ref[idx, :].astype(BF16))
        if with_prev:
            o = o + _dot((pp * inv).astype(BF16), vp_ref[idx, :].astype(BF16))
        o_ref[idx, :] = o
        lse_ref[idx, :] = jnp.broadcast_to(m + jnp.log(l), (Q_BLOCK, HEAD_DIM))
        return carry

    lax.fori_loop(0, dil, body, 0, unroll=min(dil, 4))


def _attn_prompt_strided(pa, table, g, batch, seq):
    _, dil = DILATED_GROUPS[g]
    rows = dil * Q_BLOCK
    nblk = seq // rows
    with_prev = nblk > 1
    hcols = A_QKV_WIDTH // HEAD_DIM

    def spec(sec, prev):
        if prev:
            return pl.BlockSpec((rows, HEAD_DIM), lambda b, i, h: (b * nblk + jnp.maximum(i - 1, 0), sec * hcols + g * H_A + h))
        return pl.BlockSpec((rows, HEAD_DIM), lambda b, i, h: (b * nblk + i, sec * hcols + g * H_A + h))

    in_specs = [spec(0, False), spec(1, False)] + ([spec(1, True)] if with_prev else []) + [spec(2, False)] + (
        [spec(2, True)] if with_prev else []) + [pl.BlockSpec((None, Q_BLOCK, 2 * Q_BLOCK), lambda b, i, h: (h, 0, 0))]
    out_spec = pl.BlockSpec((rows, HEAD_DIM), lambda b, i, h: (b * nblk + i, h))
    return pl.pallas_call(
        functools.partial(_attn_prompt_strided_kernel, dil=dil, with_prev=with_prev),
        grid=(batch, nblk, H_A),
        in_specs=in_specs,
        out_specs=[out_spec, out_spec],
        out_shape=[jax.ShapeDtypeStruct((batch * seq, A_WIDTH), F32)] * 2,
        compiler_params=_cparams(("parallel", "arbitrary", "arbitrary")),
        name=f"attn_prompt_g{g}",
    )(*([pa] * (len(in_specs) - 1)), table)


def _attn_prompt_group(pa, table, g, batch, seq):
    _, dil = DILATED_GROUPS[g]
    if dil > 1:
        return _attn_prompt_strided(pa, table, g, batch, seq)
    sub = seq // dil
    nqb = sub // Q_BLOCK
    wblk = PA_W // A_WIDTH
    pv = pa.reshape(batch, sub, dil * PA_W)

    def spec(off, prev):
        if prev:
            return pl.BlockSpec((None, Q_BLOCK, A_WIDTH), lambda b, r, i: (b, jnp.maximum(i - 1, 0), r * wblk + off + g))
        return pl.BlockSpec((None, Q_BLOCK, A_WIDTH), lambda b, r, i: (b, i, r * wblk + off + g))

    out_spec = pl.BlockSpec((None, Q_BLOCK, A_WIDTH), lambda b, r, i: (b, i, r))
    o, lse = pl.pallas_call(
        _attn_prompt_kernel,
        grid=(batch, dil, nqb),
        in_specs=[spec(0, False), spec(N_GROUPS, False), spec(N_GROUPS, True), spec(2 * N_GROUPS, False),
                  spec(2 * N_GROUPS, True), pl.BlockSpec((H_A, Q_BLOCK, 2 * Q_BLOCK), lambda b, r, i: (0, 0, 0))],
        out_specs=[out_spec, out_spec],
        out_shape=[jax.ShapeDtypeStruct((batch, sub, dil * A_WIDTH), F32)] * 2,
        compiler_params=_cparams(("parallel", "parallel", "arbitrary")),
        name=f"attn_prompt_g{g}",
    )(pv, pv, pv, pv, pv, table)
    return o.reshape(batch * seq, A_WIDTH), lse.reshape(batch * seq, A_WIDTH)


def _attn_heads_kernel(*refs, dil, nb, with_prev):
    if with_prev:
        q_ref, kc_ref, kp_ref, vc_ref, vp_ref, tab_ref, o_ref, lse_ref = refs
    else:
        q_ref, kc_ref, vc_ref, tab_ref, o_ref, lse_ref = refs
    has_prev = pl.program_id(1) > 0
    zero = jnp.zeros((Q_BLOCK, HEAD_DIM), BF16)
    tab_c = tab_ref[:, Q_BLOCK:]
    if with_prev:
        tab = jnp.concatenate([jnp.where(has_prev, tab_ref[:, :Q_BLOCK], NEG), tab_c], axis=1)
    else:
        tab = tab_c

    def heads(ref, bi, idx):
        return jnp.concatenate([ref[h, bi, idx, :].astype(BF16) for h in range(H_A)], axis=1)

    def body(u, carry):
        bi, r = u // dil, u % dil
        idx = pl.ds(r, Q_BLOCK, stride=dil)
        qs = [q_ref[h, bi, idx, :].astype(BF16) for h in range(H_A)]
        q_stack = jnp.concatenate(
            [jnp.concatenate([qs[h] if c == h else zero for c in range(H_A)], axis=1) for h in range(H_A)], axis=0)
        if with_prev:
            k_all = jnp.concatenate([heads(kp_ref, bi, idx), heads(kc_ref, bi, idx)], axis=0)
            v_all = jnp.concatenate([heads(vp_ref, bi, idx), heads(vc_ref, bi, idx)], axis=0)
        else:
            k_all, v_all = heads(kc_ref, bi, idx), heads(vc_ref, bi, idx)
        s = _dot_nt(q_stack, k_all) * ATT_SCALE + tab
        m = jnp.max(s, axis=-1, keepdims=True)
        p = jnp.exp(s - m)
        l = jnp.sum(p, axis=-1, keepdims=True)
        o = _dot((p * (1.0 / l)).astype(BF16), v_all)
        lse = m + jnp.log(l)
        for h in range(H_A):
            rows = slice(h * Q_BLOCK, (h + 1) * Q_BLOCK)
            o_ref[h, bi, idx, :] = o[rows, h * HEAD_DIM:(h + 1) * HEAD_DIM]
            lse_ref[h, bi, idx, :] = jnp.broadcast_to(lse[rows], (Q_BLOCK, HEAD_DIM))
        return carry

    lax.fori_loop(0, nb * dil, body, 0, unroll=min(nb * dil, 2))


def _attn_heads(pa_hm, table, g, batch, seq):
    _, dil = DILATED_GROUPS[g]
    rows = dil * Q_BLOCK
    nblk = seq // rows
    with_prev = nblk > 1
    nb = 2 if (dil == 1 and batch % 2 == 0) else 1
    pv = pa_hm.reshape(pa_hm.shape[0], batch, seq, HEAD_DIM)

    def spec(sec, prev):
        if prev:
            return pl.BlockSpec((H_A, nb, rows, HEAD_DIM), lambda b, i: (sec * N_GROUPS + g, b, jnp.maximum(i - 1, 0), 0))
        return pl.BlockSpec((H_A, nb, rows, HEAD_DIM), lambda b, i: (sec * N_GROUPS + g, b, i, 0))

    in_specs = [spec(0, False), spec(1, False)] + ([spec(1, True)] if with_prev else []) + [spec(2, False)] + (
        [spec(2, True)] if with_prev else []) + [pl.BlockSpec((H_A * Q_BLOCK, 2 * Q_BLOCK), lambda b, i: (0, 0))]
    out_spec = pl.BlockSpec((H_A, nb, rows, HEAD_DIM), lambda b, i: (0, b, i, 0))
    o, lse = pl.pallas_call(
        functools.partial(_attn_heads_kernel, dil=dil, nb=nb, with_prev=with_prev),
        grid=(batch // nb, nblk),
        in_specs=in_specs,
        out_specs=[out_spec, out_spec],
        out_shape=[jax.ShapeDtypeStruct((H_A, batch, seq, HEAD_DIM), F32)] * 2,
        compiler_params=_cparams(("parallel", "arbitrary")),
        name=f"attn_prompt_g{g}",
    )(*([pv] * (len(in_specs) - 1)), table.reshape(H_A * Q_BLOCK, 2 * Q_BLOCK))
    return o.reshape(H_A, batch * seq, HEAD_DIM), lse.reshape(H_A, batch * seq, HEAD_DIM)


KV_ROWS = 2 * H_A


def _kv_pack_kernel(k_ref, v_ref, o_ref):
    o_ref[...] = jnp.stack([k_ref[h] for h in range(H_A)] + [v_ref[h] for h in range(H_A)], axis=1)


def _kv_pack(pa_hm, g, batch, seq, keep, tm):
    nblk, blk0, per_b = keep // tm, (seq - keep) // tm, seq // tm
    out = pl.pallas_call(
        _kv_pack_kernel,
        grid=(batch, nblk),
        in_specs=[pl.BlockSpec((H_A, tm, HEAD_DIM), lambda b, i: (N_GROUPS + g, b * per_b + blk0 + i, 0)),
                  pl.BlockSpec((H_A, tm, HEAD_DIM), lambda b, i: (2 * N_GROUPS + g, b * per_b + blk0 + i, 0))],
        out_specs=pl.BlockSpec((tm, KV_ROWS, HEAD_DIM), lambda b, i: (b * nblk + i, 0, 0)),
        out_shape=jax.ShapeDtypeStruct((batch * keep, KV_ROWS, HEAD_DIM), F32),
        compiler_params=_cparams(("parallel", "parallel")),
        name=f"kv_pack_g{g}_{keep}",
    )(pa_hm, pa_hm)
    return out.reshape(batch, keep, KV_ROWS, HEAD_DIM)


def _attn_sample_kernel(qkv_ref, n1_ref, n2_ref, n3_ref, c1_ref, c2_ref, c3_ref, tabc_ref, tabn_ref, o_ref, *,
                        dec_seq, combo_base):
    caches = (c1_ref, c2_ref, c3_ref)
    news = (n1_ref, n2_ref, n3_ref)
    zeros = jnp.zeros((H_A, HEAD_DIM), F32)
    for s in range(dec_seq):
        outs, lses = [], []
        for g, (_, dil) in enumerate(DILATED_GROUPS):
            rho, fl = s % dil, s // dil
            qm = jnp.concatenate([qkv_ref[g * H_A + h, s:s + 1, :] for h in range(H_A)] + [zeros], axis=0)
            kc = caches[g][:, rho]
            kn = news[g][...]
            sc = jnp.sum(kc * qm[None], axis=-1, keepdims=True) * ATT_SCALE + tabc_ref[combo_base[g] + fl]
            sn = jnp.sum(kn * qm[None], axis=-1, keepdims=True) * ATT_SCALE + tabn_ref[g, s]
            m = jnp.maximum(jnp.max(sc, axis=0), jnp.max(sn, axis=0))
            pc = jnp.exp(sc - m[None])
            pn = jnp.exp(sn - m[None])
            l = jnp.sum(pc, axis=0) + jnp.sum(pn, axis=0)
            acc = jnp.sum(pltpu.roll(pc, H_A, 1) * kc, axis=0) + jnp.sum(pltpu.roll(pn, H_A, 1) * kn, axis=0)
            outs.append(acc / pltpu.roll(l, H_A, 0))
            lses.append(pltpu.roll(m + jnp.log(l), H_A, 0))
        mm = jnp.maximum(jnp.maximum(lses[0], lses[1]), lses[2])
        ws = [jnp.exp(x - mm) for x in lses]
        o_ref[s] = (ws[0] * outs[0] + ws[1] * outs[1] + ws[2] * outs[2]) / (ws[0] + ws[1] + ws[2])


def _attn_sample(pa_hm, new_rows, caches, tabc, tabn, combo_base, batch, dec_seq):
    views, specs = [], []
    for g, (window, dil) in enumerate(DILATED_GROUPS):
        assert caches[g].shape[1] == window and dec_seq <= Q_BLOCK
        views.append(caches[g].reshape(batch, Q_BLOCK, dil, KV_ROWS, HEAD_DIM))
        used = min(dil, dec_seq)
        specs.append(pl.BlockSpec((None, Q_BLOCK, used, KV_ROWS, HEAD_DIM), lambda b: (b, 0, 0, 0, 0)))
    new_spec = pl.BlockSpec((None, dec_seq, KV_ROWS, HEAD_DIM), lambda b: (b, 0, 0, 0))
    out = pl.pallas_call(
        functools.partial(_attn_sample_kernel, dec_seq=dec_seq, combo_base=combo_base),
        grid=(batch,),
        in_specs=[pl.BlockSpec((N_GROUPS * H_A, dec_seq, HEAD_DIM), lambda b: (0, b, 0))] + [new_spec] * N_GROUPS + specs + [
            pl.BlockSpec(tabc.shape, lambda b: (0, 0, 0, 0)), pl.BlockSpec(tabn.shape, lambda b: (0, 0, 0, 0, 0))],
        out_specs=pl.BlockSpec((None, dec_seq, KV_ROWS, HEAD_DIM), lambda b: (b, 0, 0, 0)),
        out_shape=jax.ShapeDtypeStruct((batch, dec_seq, KV_ROWS, HEAD_DIM), F32),
        compiler_params=_cparams(("parallel",)),
        name="attn_sample",
    )(pa_hm, *new_rows, *views, tabc, tabn)
    return out[:, :, H_A:, :].reshape(batch * dec_seq, A_WIDTH)


def _sc_cache_shift(caches, drop):
    info = plsc.get_sparse_core_info()
    n_workers = info.num_cores * info.num_subcores
    batch = caches[0].shape[0]
    assert batch % n_workers == 0
    mesh = plsc.VectorSubcoreMesh(core_axis_name="c", subcore_axis_name="s")

    row_bytes = KV_ROWS * HEAD_DIM * 4
    chunks = []
    for c in caches:
        keep = c.shape[1] - drop
        ch = max(d for d in range(1, SC_CHUNK_BYTES // row_bytes + 1) if keep % d == 0)
        chunks.append(ch)
    buf_rows = max(chunks)

    @functools.partial(pl.kernel, mesh=mesh, out_type=[jax.ShapeDtypeStruct(c.shape, c.dtype) for c in caches],
                       scratch_types=[pltpu.VMEM((buf_rows, KV_ROWS, HEAD_DIM), caches[0].dtype)])
    def shift(*refs):
        srcs, dsts, buf = refs[:len(caches)], refs[len(caches):2 * len(caches)], refs[-1]
        wid = lax.axis_index("s") * info.num_cores + lax.axis_index("c")
        for j in range(batch // n_workers):
            b = wid * (batch // n_workers) + j
            for src, dst, ch in zip(srcs, dsts, chunks):
                stage = buf.at[pl.ds(0, ch)]

                @pl.loop(0, (src.shape[1] - drop) // ch)
                def _(i):
                    pltpu.sync_copy(src.at[b, pl.ds(drop + i * ch, ch)], stage)
                    pltpu.sync_copy(stage, dst.at[b, pl.ds(i * ch, ch)])

    return shift(*caches)


def _split3(x):
    hi = x.astype(BF16)
    r = x - hi.astype(F32)
    mid = r.astype(BF16)
    lo = (r - mid.astype(F32)).astype(BF16)
    return hi, mid, lo


def _gla_kernel(p_ref, wup_ref, gb_ref, ng_ref, s0_ref, o_ref, st_ref, *, chunk, tb):
    @pl.when(pl.program_id(1) == 0)
    def _():
        st_ref[...] = s0_ref[...]

    sub = min(GLA_SUB, chunk)
    kq = H_B * DK_B

    def rb(x):
        xb = x.astype(BF16)
        return xb if chunk >= 16 else xb.astype(F32)

    row = lax.broadcasted_iota(jnp.int32, (chunk, chunk), 0)
    colm = lax.broadcasted_iota(jnp.int32, (chunk, chunk), 1)
    tri = rb(jnp.where(row >= colm, 1.0, 0.0))
    for c in range(tb // chunk):
        rows = slice(c * chunk, (c + 1) * chunk)
        z = _dot(rb(p_ref[rows, PB_LR:PB_W]), rb(wup_ref[...])) + gb_ref[...]
        la = -(jnp.maximum(-z, 0.0) + jnp.log1p(jnp.exp(-jnp.abs(z)))) * (1.0 / GLA_TAU)
        b = functools.reduce(lambda u, w: u + w, [_dot(tri, rb(t)) for t in _split3(la)])
        blast = b[chunk - 1:chunk, :]
        q = p_ref[rows, 0:kq] * (DK_B ** -0.5)
        k = p_ref[rows, kq:2 * kq]
        qin = rb(q * jnp.exp(b))
        kst = rb(k * jnp.exp(blast - b))
        for h in range(H_B):
            ks = slice(h * DK_B, (h + 1) * DK_B)
            vs = slice(h * DV_B, (h + 1) * DV_B)
            st = st_ref[h]
            vb = rb(p_ref[rows, 2 * kq + h * DV_B:2 * kq + (h + 1) * DV_B])
            o_inter = _dot_nt(qin[:, ks], rb(st))
            parts = []
            for blk in range(chunk // sub):
                r0 = blk * sub
                n = r0 + sub
                ref_b = b[r0 - 1:r0, ks] if blk > 0 else jnp.zeros((1, DK_B), F32)
                qi = rb(q[r0:n, ks] * jnp.exp(b[r0:n, ks] - ref_b))
                ki = rb(k[0:n, ks] * jnp.exp(jnp.minimum(ref_b - b[0:n, ks], GLA_EXP_CLAMP)))
                a = _dot_nt(qi, ki)
                ti = lax.broadcasted_iota(jnp.int32, (sub, n), 0) + r0
                si = lax.broadcasted_iota(jnp.int32, (sub, n), 1)
                a = jnp.where(si <= ti, a, 0.0)
                parts.append(_dot(rb(a), vb[0:n]))
            o = o_inter + (jnp.concatenate(parts, axis=0) if len(parts) > 1 else parts[0])
            st_ref[h] = st * jnp.exp(blast[:, ks]) + _dot_tn(vb, kst[:, ks])
            on = o * lax.rsqrt(jnp.mean(o * o, axis=-1, keepdims=True) + RMS_EPS) * ng_ref[...]
            rg = p_ref[rows, 2 * kq + H_B * DV_B + h * DV_B:2 * kq + H_B * DV_B + (h + 1) * DV_B]
            o_ref[rows, vs] = (on * (rg * jax.nn.sigmoid(rg))).astype(BF16)


def _gla(pb, wup, gb, ng, s0t, batch, seq, chunk, tb):
    p3 = pb.reshape(batch, seq, PB_W)
    o, st = pl.pallas_call(
        functools.partial(_gla_kernel, chunk=chunk, tb=tb),
        grid=(batch, seq // tb),
        in_specs=[pl.BlockSpec((None, tb, PB_W), lambda b, i: (b, i, 0)),
                  pl.BlockSpec(wup.shape, lambda b, i: (0, 0)),
                  pl.BlockSpec(gb.shape, lambda b, i: (0, 0)),
                  pl.BlockSpec(ng.shape, lambda b, i: (0, 0)),
                  pl.BlockSpec((None, H_B, DV_B, DK_B), lambda b, i: (b, 0, 0, 0))],
        out_specs=[pl.BlockSpec((None, tb, H_B * DV_B), lambda b, i: (b, i, 0)),
                   pl.BlockSpec((None, H_B, DV_B, DK_B), lambda b, i: (b, 0, 0, 0))],
        out_shape=[jax.ShapeDtypeStruct((batch, seq, H_B * DV_B), BF16),
                   jax.ShapeDtypeStruct((batch, H_B, DV_B, DK_B), F32)],
        compiler_params=_cparams(("parallel", "arbitrary")),
        name=f"gla_c{chunk}",
    )(p3, wup, gb, ng, s0t)
    return o.reshape(batch * seq, H_B * DV_B), st


def _gla_heads_kernel(p_ref, wup_ref, gb_ref, ng_ref, s0_ref, o_ref, st_ref, *, chunk, tb):
    @pl.when(pl.program_id(1) == 0)
    def _():
        st_ref[...] = s0_ref[...]

    sub = min(GLA_SUB, chunk)
    n_sub = chunk // sub
    kq, vw = H_B * DK_B, H_B * DV_B

    def rb(x):
        xb = x.astype(BF16)
        return xb if chunk >= 16 else xb.astype(F32)

    row = lax.broadcasted_iota(jnp.int32, (chunk, chunk), 0)
    colm = lax.broadcasted_iota(jnp.int32, (chunk, chunk), 1)
    tri = rb(jnp.where(row >= colm, 1.0, 0.0))
    lane_head = lax.broadcasted_iota(jnp.int32, (1, kq), 1) // DK_B
    diag_blocks = (lax.broadcasted_iota(jnp.int32, (vw, kq), 0) // DV_B
                   == lax.broadcasted_iota(jnp.int32, (vw, kq), 1) // DK_B)
    a_cols = LANE if n_sub > 1 else chunk
    t_in = lax.broadcasted_iota(jnp.int32, (H_B * chunk, a_cols), 0) % chunk
    s_in = lax.broadcasted_iota(jnp.int32, (H_B * chunk, a_cols), 1)
    for c in range(tb // chunk):
        rows = slice(c * chunk, (c + 1) * chunk)
        z = _dot(rb(p_ref[rows, PB_LR:PB_W]), rb(wup_ref[...])) + gb_ref[...]
        la = -(jnp.maximum(-z, 0.0) + jnp.log1p(jnp.exp(-jnp.abs(z)))) * (1.0 / GLA_TAU)
        b3 = _dot(tri, rb(jnp.concatenate(_split3(la), axis=1)))
        b = b3[:, :kq] + b3[:, kq:2 * kq] + b3[:, 2 * kq:]
        blast = b[chunk - 1:chunk, :]
        q = p_ref[rows, 0:kq] * (DK_B ** -0.5)
        k = p_ref[rows, kq:2 * kq]
        v = rb(p_ref[rows, 2 * kq:2 * kq + vw])
        st = st_ref[...]
        o_inter = _dot_nt(rb(q * jnp.exp(b)), rb(st))
        refs = [jnp.zeros((1, kq), F32)] + [b[i * sub - 1:i * sub, :] for i in range(1, n_sub)]
        ref_rows = jnp.concatenate([jnp.broadcast_to(r, (sub, kq)) for r in refs], axis=0)
        qs = q * jnp.exp(b - ref_rows)
        q_stack = rb(jnp.concatenate([jnp.where(lane_head == h, qs, 0.0) for h in range(H_B)], axis=0))
        k_parts = []
        for r in refs:
            k_parts.append(k * jnp.exp(jnp.minimum(r - b, GLA_EXP_CLAMP)))
            if a_cols > chunk:
                k_parts.append(jnp.zeros((a_cols - chunk, kq), F32))
        raw = _dot_nt(q_stack, rb(jnp.concatenate(k_parts, axis=0)))
        a = jnp.zeros((H_B * chunk, a_cols), F32)
        for i in range(n_sub):
            a = a + jnp.where(t_in // sub == i, raw[:, i * a_cols:(i + 1) * a_cols], 0.0)
        a = jnp.where(s_in <= t_in, a, 0.0)[:, :chunk]
        o_all = _dot(rb(a), v)
        o = o_inter + jnp.concatenate(
            [o_all[h * chunk:(h + 1) * chunk, h * DV_B:(h + 1) * DV_B] for h in range(H_B)], axis=1)
        st_ref[...] = st * jnp.exp(blast) + jnp.where(diag_blocks, _dot_tn(v, rb(k * jnp.exp(blast - b))), 0.0)
        normed = []
        for h in range(H_B):
            oh = o[:, h * DV_B:(h + 1) * DV_B]
            normed.append(oh * lax.rsqrt(jnp.mean(oh * oh, axis=-1, keepdims=True) + RMS_EPS) * ng_ref[...])
        rg = p_ref[rows, 2 * kq + vw:2 * kq + 2 * vw]
        o_ref[rows, :] = (jnp.concatenate(normed, axis=1) * (rg * jax.nn.sigmoid(rg))).astype(BF16)


def _gla_heads(pb, wup, gb, ng, s0, batch, seq, chunk, tb):
    kq, vw = H_B * DK_B, H_B * DV_B
    eye = jnp.eye(H_B, dtype=F32)
    s0t = jnp.swapaxes(s0, -1, -2)
    s0_bd = (s0t[:, :, :, None, :] * eye[None, :, None, :, None]).reshape(batch, vw, kq)
    p3 = pb.reshape(batch, seq, PB_W)
    o, st = pl.pallas_call(
        functools.partial(_gla_heads_kernel, chunk=chunk, tb=tb),
        grid=(batch, seq // tb),
        in_specs=[pl.BlockSpec((None, tb, PB_W), lambda b, i: (b, i, 0)),
                  pl.BlockSpec(wup.shape, lambda b, i: (0, 0)),
                  pl.BlockSpec(gb.shape, lambda b, i: (0, 0)),
                  pl.BlockSpec(ng.shape, lambda b, i: (0, 0)),
                  pl.BlockSpec((None, vw, kq), lambda b, i: (b, 0, 0))],
        out_specs=[pl.BlockSpec((None, tb, vw), lambda b, i: (b, i, 0)),
                   pl.BlockSpec((None, vw, kq), lambda b, i: (b, 0, 0))],
        out_shape=[jax.ShapeDtypeStruct((batch, seq, vw), BF16), jax.ShapeDtypeStruct((batch, vw, kq), F32)],
        compiler_params=_cparams(("parallel", "arbitrary")),
        name=f"gla_c{chunk}",
    )(p3, wup, gb, ng, s0_bd)
    st5 = st.reshape(batch, H_B, DV_B, H_B, DK_B)
    st_heads = jnp.stack([st5[:, h, :, h, :] for h in range(H_B)], axis=1)
    return o.reshape(batch * seq, vw), jnp.swapaxes(st_heads, -1, -2)


def _layer_norm(u, g, b):
    mu = jnp.mean(u, axis=-1, keepdims=True)
    d = u - mu
    var = jnp.mean(d * d, axis=-1, keepdims=True)
    return d * lax.rsqrt(var + LN_EPS) * g + b


def _merge_kernel(*refs, n_groups):
    x_ref = refs[0]
    oa_refs = refs[1:1 + 2 * n_groups] if n_groups > 1 else refs[1:2]
    rest = refs[1 + (2 * n_groups if n_groups > 1 else 1):]
    (ob_ref, pg_a_ref, pg_b_ref, wpa_ref, wpb_ref, wo_ref, g1_ref, b1_ref, rwh_ref, rwl_ref, rb_ref, cnt0_ref,
     x1_ref, x1p_ref, ti_ref, gt_ref, rank_ref, cnt_ref) = rest
    tm = x_ref.shape[0]
    half = D_MODEL // 2

    def rows_part(rs):
        if n_groups > 1:
            def lanes(r):
                return jnp.concatenate([r[h, rs, :] for h in range(H_A)], axis=1)

            os_ = [lanes(r) for r in oa_refs[:n_groups]]
            ls = [lanes(r) for r in oa_refs[n_groups:]]
            mm = functools.reduce(jnp.maximum, ls)
            ws = [jnp.exp(x - mm) for x in ls]
            oa = sum(w * o for w, o in zip(ws, os_)) / sum(ws)
        else:
            oa = oa_refs[0][rs, :]
        ya = _dot(oa.astype(BF16), wpa_ref[...])
        yb = _dot(ob_ref[rs, :], wpb_ref[...])
        branch = jax.nn.sigmoid(pg_a_ref[rs, :]) * ya + jax.nn.sigmoid(pg_b_ref[rs, :]) * yb
        y = _dot(branch.astype(BF16), wo_ref[...])
        x1 = _layer_norm(DEEPNORM_ALPHA * x_ref[rs, :] + y, g1_ref[...], b1_ref[...])
        x1_ref[rs, :] = x1
        xh = x1.astype(BF16)
        xhf = xh.astype(F32)
        bits = lax.bitcast_convert_type(xhf, jnp.int32)
        x1p_ref[rs, :] = lax.shift_right_logical(bits[:, :half], 16) | bits[:, half:]
        xl = (x1 - xhf).astype(BF16)
        logits = _dot(xh, rwh_ref[...]) + _dot(xl, rwh_ref[...]) + _dot(xh, rwl_ref[...]) + rb_ref[...]
        lane = lax.broadcasted_iota(jnp.int32, logits.shape, 1)
        vals = logits
        top_v, top_i = [], []
        for _ in range(TOP_K):
            m = jnp.max(vals, axis=-1, keepdims=True)
            ik = jnp.min(jnp.where(vals == m, lane, LANE), axis=-1, keepdims=True)
            vals = jnp.where(lane == ik, -jnp.inf, vals)
            top_v.append(m)
            top_i.append(ik)
        es = [jnp.exp(v - top_v[0]) for v in top_v]
        tot = functools.reduce(lambda a, b: a + b, es)
        ti_ref[rs, :] = jnp.concatenate(top_i, axis=1)
        gt_ref[rs, :] = jnp.concatenate([e / tot for e in es], axis=1)
        return [jnp.where(lane == ik, 1.0, 0.0) for ik in top_i]

    picks = rows_part(slice(0, tm))
    @pl.when(pl.program_id(0) == 0)
    def _():
        cnt_ref[...] = cnt0_ref[...]

    onehot = picks[0] + picks[1] + picks[2] + picks[3]
    tri = jnp.where(lax.broadcasted_iota(jnp.int32, (tm, tm), 0) > lax.broadcasted_iota(jnp.int32, (tm, tm), 1), 1.0, 0.0)
    before = _dot(tri.astype(BF16), onehot.astype(BF16)) + cnt_ref[...]
    rank_ref[...] = jnp.concatenate([jnp.sum(pk * before, axis=-1, keepdims=True) for pk in picks], axis=1).astype(jnp.int32)
    cnt_ref[...] = cnt_ref[...] + jnp.sum(onehot, axis=0, keepdims=True)


def _merge(x, oas, ob, pg, wpa, wpb, wo, g1, b1, rwh, rwl, rbp, cnt0, tm):
    T = x.shape[0]
    n_groups = len(oas) // 2 if len(oas) > 1 else 1

    def row(w):
        return pl.BlockSpec((tm, w), lambda i: (i, 0))

    def const(a):
        return pl.BlockSpec(a.shape, lambda i: (0,) * a.ndim, pipeline_mode=pl.Buffered(1))

    def oa_spec(a):
        return pl.BlockSpec((H_A, tm, HEAD_DIM), lambda i: (0, i, 0)) if a.ndim == 3 else row(A_WIDTH)

    in_specs = ([row(D_MODEL)] + [oa_spec(a) for a in oas] + [row(H_B * DV_B), row(D_MODEL),
                pl.BlockSpec((tm, D_MODEL), lambda i: (i, 1))]
                + [const(a) for a in (wpa, wpb, wo, g1, b1, rwh, rwl, rbp, cnt0)])
    return pl.pallas_call(
        functools.partial(_merge_kernel, n_groups=n_groups),
        grid=(T // tm,),
        in_specs=in_specs,
        out_specs=[row(D_MODEL), row(D_MODEL // 2), row(TOP_K), row(TOP_K), row(TOP_K),
                   pl.BlockSpec((1, LANE), lambda i: (0, 0))],
        out_shape=[jax.ShapeDtypeStruct((T, D_MODEL), F32), jax.ShapeDtypeStruct((T, D_MODEL // 2), jnp.int32),
                   jax.ShapeDtypeStruct((T, TOP_K), jnp.int32), jax.ShapeDtypeStruct((T, TOP_K), F32),
                   jax.ShapeDtypeStruct((T, TOP_K), jnp.int32), jax.ShapeDtypeStruct((1, LANE), F32)],
        compiler_params=_cparams(("arbitrary",)),
        name=f"merge_g{n_groups}",
    )(x, *oas, ob, pg, pg, wpa, wpb, wo, g1, b1, rwh, rwl, rbp, cnt0)


def _sc_gather(table, idx):
    info = plsc.get_sparse_core_info()
    n_workers = info.num_cores * info.num_subcores
    n, width = idx.shape[0], table.shape[1]
    per_worker = n // n_workers
    chunk = SC_CHUNK_BYTES // (width * table.dtype.itemsize)
    assert per_worker * n_workers == n and per_worker % chunk == 0 and chunk % 8 == 0
    mesh = plsc.VectorSubcoreMesh(core_axis_name="c", subcore_axis_name="s")

    @functools.partial(
        pl.kernel, mesh=mesh,
        out_type=jax.ShapeDtypeStruct((n, width), table.dtype),
        scratch_types=[pltpu.VMEM((chunk,), jnp.int32), pltpu.VMEM((chunk, width), table.dtype),
                       pltpu.SemaphoreType.DMA],
    )
    def gather(table_hbm, idx_hbm, out_hbm, idx_v, rows_v, sem):
        wid = lax.axis_index("s") * info.num_cores + lax.axis_index("c")
        base = wid * per_worker

        @pl.loop(0, per_worker // chunk)
        def _(c):
            off = pl.multiple_of(base + c * chunk, chunk)
            pltpu.sync_copy(idx_hbm.at[pl.ds(off, chunk)], idx_v)
            pltpu.async_copy(table_hbm.at[idx_v], rows_v, sem).wait()
            pltpu.sync_copy(rows_v, out_hbm.at[pl.ds(off, chunk)])

    return gather(table, idx)


def _sc_scatter_rows(table, idx, n_out):
    info = plsc.get_sparse_core_info()
    n_workers = info.num_cores * info.num_subcores
    n_idx, n = idx.shape
    width = table.shape[1]
    per_worker = n // n_workers
    max_rows = SC_SCATTER_BYTES // (width * table.dtype.itemsize)
    chunk = max(d for d in range(8, max_rows + 1, 8) if per_worker % d == 0)
    assert per_worker * n_workers == n and table.shape[0] == n
    mesh = plsc.VectorSubcoreMesh(core_axis_name="c", subcore_axis_name="s")

    @functools.partial(
        pl.kernel, mesh=mesh,
        out_type=jax.ShapeDtypeStruct((n_out, width), table.dtype),
        scratch_types=[pltpu.VMEM((chunk,), jnp.int32)] * n_idx + [pltpu.VMEM((chunk, width), table.dtype)],
    )
    def scatter(table_hbm, idx_hbm, out_hbm, *scratch):
        idx_vs, rows_v = scratch[:n_idx], scratch[n_idx]
        wid = lax.axis_index("s") * info.num_cores + lax.axis_index("c")
        base = wid * per_worker

        @pl.loop(0, per_worker // chunk)
        def _(c):
            off = pl.multiple_of(base + c * chunk, 8)
            pltpu.sync_copy(table_hbm.at[pl.ds(off, chunk)], rows_v)
            for k in range(n_idx):
                pltpu.sync_copy(idx_hbm.at[pl.ds(pl.multiple_of(k * n + off, 8), chunk)], idx_vs[k])
            for k in range(n_idx):
                pltpu.sync_copy(rows_v, out_hbm.at[idx_vs[k]])

    return scatter(table, idx.reshape(n_idx * n))


def _expert_kernel(be_ref, rows_ref, nu_ref, xs_ref, wg_ref, wu_ref, bg_ref, bu_ref, wd_ref, bd_ref, o_ref, hid_ref):
    del be_ref, nu_ref
    i = pl.program_id(0)
    p = pl.program_id(1)
    nrows = rows_ref[i]
    n_live = (nrows + (MOE_SB - 1)) // MOE_SB
    half = D_MODEL // 2

    def paired(one):
        def pair(j, carry):
            one(2 * j)
            one(2 * j + 1)
            return carry

        lax.fori_loop(0, n_live // 2, pair, 0)

        @pl.when(n_live % 2 == 1)
        def _():
            one(n_live - 1)

    @pl.when(p < MOE_NF)
    def _():
        def gate_up(s):
            r0 = pl.multiple_of(s * MOE_SB, MOE_SB)
            rid = r0 + lax.broadcasted_iota(jnp.int32, (MOE_SB, half), 0)
            packed = jnp.where(rid < nrows, xs_ref[pl.ds(r0, MOE_SB), :], 0)
            lo = lax.bitcast_convert_type(lax.shift_left(packed, 16), F32)
            hi = lax.bitcast_convert_type(packed & jnp.int32(-65536), F32)
            x = jnp.concatenate([lo.astype(BF16), hi.astype(BF16)], axis=1)
            g = jnp.minimum(_dot(x, wg_ref[...].astype(BF16)) + bg_ref[...], SWIGLU_LIMIT)
            u = jnp.clip(_dot(x, wu_ref[...].astype(BF16)) + bu_ref[...], -SWIGLU_LIMIT, SWIGLU_LIMIT)
            hid_ref[p, pl.ds(r0, MOE_SB), :] = ((u + 1.0) * g * jax.nn.sigmoid(SWIGLU_ALPHA * g)).astype(BF16)

        paired(gate_up)

    @pl.when(p >= MOE_NF)
    def _():
        def down(s):
            r0 = pl.multiple_of(s * MOE_SB, MOE_SB)
            y = bd_ref[...]
            for f in range(MOE_NF):
                y = y + _dot(hid_ref[f, pl.ds(r0, MOE_SB), :], wd_ref[f * MOE_TF:(f + 1) * MOE_TF, :].astype(BF16))
            bits = lax.bitcast_convert_type(y.astype(BF16).astype(F32), jnp.int32)
            o_ref[pl.ds(r0, MOE_SB), :] = lax.shift_right_logical(bits[:, :MOE_TN // 2], 16) | bits[:, MOE_TN // 2:]

        def zero_body(s, carry):
            o_ref[pl.ds(pl.multiple_of(s * MOE_SB, MOE_SB), MOE_SB), :] = jnp.zeros((MOE_SB, MOE_TN // 2), jnp.int32)
            return carry

        paired(down)
        lax.fori_loop(n_live, MOE_BM // MOE_SB, zero_body, 0)


def _experts(xs, block_expert, block_rows, n_used, w_gate_up, b_gate_up, w_down, b_down):
    nf, nn = MOE_NF, D_MODEL // MOE_TN

    def gate_map(col0, lead):
        def index_map(i, p, be, rw, nu):
            ahead = p >= nf + nn - lead
            e = jnp.where(ahead, be[jnp.minimum(i + 1, nu[0] - 1)], be[i])
            return (e, 0, col0 + jnp.where(ahead, 0, jnp.minimum(p, nf - 1)))
        return index_map

    def down_map(i, p, be, rw, nu):
        parked = p < nf
        e = jnp.where(parked, be[jnp.maximum(i - 1, 0)], be[i])
        return (e, 0, jnp.where(parked, jnp.where(i > 0, nn - 1, 0), p - nf))

    grid_spec = pltpu.PrefetchScalarGridSpec(
        num_scalar_prefetch=3,
        grid=(n_used[0], nf + nn),
        in_specs=[
            pl.BlockSpec((MOE_BM, D_MODEL // 2), lambda i, p, be, rw, nu: (i, 0), pipeline_mode=pl.Buffered(1)),
            pl.BlockSpec((None, D_MODEL, MOE_TF), gate_map(0, nn // 2)),
            pl.BlockSpec((None, D_MODEL, MOE_TF), gate_map(nf, nn // 4)),
            pl.BlockSpec((None, 1, MOE_TF), gate_map(0, nn // 2)),
            pl.BlockSpec((None, 1, MOE_TF), gate_map(nf, nn // 4)),
            pl.BlockSpec((None, D_FF, MOE_TN), down_map),
            pl.BlockSpec((None, 1, MOE_TN), down_map),
        ],
        out_specs=pl.BlockSpec((MOE_BM, MOE_TN // 2), lambda i, p, be, rw, nu: (i, jnp.maximum(p - nf, 0))),
        scratch_shapes=[pltpu.VMEM((nf, MOE_BM, MOE_TF), BF16)],
    )
    bgu = b_gate_up.reshape(N_EXPERTS, 1, 2 * D_FF)
    bd = b_down.reshape(N_EXPERTS, 1, D_MODEL)
    return pl.pallas_call(
        _expert_kernel,
        grid_spec=grid_spec,
        out_shape=jax.ShapeDtypeStruct((xs.shape[0], D_MODEL // 2), jnp.int32),
        compiler_params=_cparams(("arbitrary", "arbitrary")),
        name="moe_experts",
    )(block_expert, block_rows, n_used, xs, w_gate_up, w_gate_up, bgu, bgu, w_down, bd)


def _route(top_i, rank, counts, n_blocks):
    T = top_i.shape[0]
    bpe = (counts + MOE_BM - 1) // MOE_BM
    bend = jnp.cumsum(bpe)
    bstart = bend - bpe
    experts = jnp.arange(N_EXPERTS, dtype=jnp.int32)
    start_of = jnp.sum(jnp.where(top_i[:, :, None] == experts, bstart * MOE_BM, 0), axis=-1)
    dest = (start_of + rank).astype(jnp.int32)
    n_used = bend[-1]
    blk = jnp.arange(n_blocks, dtype=jnp.int32)
    be = jnp.minimum(jnp.searchsorted(bend, jnp.minimum(blk, n_used - 1), side="right"), N_EXPERTS - 1).astype(jnp.int32)
    rows = jnp.clip(counts[be] - (blk - bstart[be]) * MOE_BM, 0, MOE_BM)
    rows = jnp.where(blk < n_used, rows, 0).astype(jnp.int32)
    return dest.reshape(T, TOP_K).T, be, rows, n_used.reshape(1).astype(jnp.int32)


def _combine_kernel(x1_ref, ge_ref, gt_ref, g2_ref, b2_ref, o_ref):
    hw, nn = MOE_TN // 2, D_MODEL // MOE_TN

    def unpack(w):
        lo = lax.bitcast_convert_type(lax.shift_left(w, 16), F32)
        hi = lax.bitcast_convert_type(w & jnp.int32(-65536), F32)
        return jnp.concatenate([part[:, n * hw:(n + 1) * hw] for n in range(nn) for part in (lo, hi)], axis=1)

    gt = gt_ref[...]
    m = gt[:, 0:1] * unpack(ge_ref[0])
    for k in range(1, TOP_K):
        m = m + gt[:, k:k + 1] * unpack(ge_ref[k])
    o_ref[...] = _layer_norm(DEEPNORM_ALPHA * x1_ref[...] + m, g2_ref[...], b2_ref[...])


def _combine_into_kernel(x1_ref, ge_ref, gt_ref, g2_ref, b2_ref, prev_ref, o_ref):
    del prev_ref
    _combine_kernel(x1_ref, ge_ref, gt_ref, g2_ref, b2_ref, o_ref)


def _combine(x1, ge, gate, g2, b2, n_rows, tm, x_row0=0, ge_row0=0, gate_row0=0, out_rows=None, into=None):
    xb, eb, gb = x_row0 // tm, ge_row0 // tm, gate_row0 // tm
    in_specs = [pl.BlockSpec((tm, D_MODEL), lambda i: (xb + i, 0)),
                pl.BlockSpec((TOP_K, tm, D_MODEL // 2), lambda i: (0, eb + i, 0)),
                pl.BlockSpec((tm, TOP_K), lambda i: (gb + i, 0)),
                pl.BlockSpec(g2.shape, lambda i: (0, 0)), pl.BlockSpec(b2.shape, lambda i: (0, 0))]
    args = [x1, ge, gate, g2, b2]
    aliases = {}
    if into is not None:
        in_specs.append(pl.BlockSpec(memory_space=pl.ANY))
        args.append(into)
        aliases = {len(args) - 1: 0}
        out_rows = into.shape[0]
    return pl.pallas_call(
        _combine_kernel if into is None else _combine_into_kernel,
        grid=(n_rows // tm,),
        in_specs=in_specs,
        out_specs=pl.BlockSpec((tm, D_MODEL), lambda i: (xb + i, 0)),
        out_shape=jax.ShapeDtypeStruct((out_rows if out_rows is not None else n_rows, D_MODEL), F32),
        input_output_aliases=aliases,
        compiler_params=_cparams(("parallel",)),
        name="moe_combine",
    )(*args)


def _layer(xp, xs, caches, state, w_in, rel_bias, gla_w_up, gla_b, gla_norm_g, w_pa, w_pb, w_o, ln1_g, ln1_b,
           router_w, router_b, w_gate_up, b_gate_up, w_down, b_down, ln2_g, ln2_b):
    batch, seq, _ = xp.shape
    dbatch, dseq, _ = xs.shape
    tp, ts = batch * seq, dbatch * dseq
    xp2, xs2 = xp.reshape(tp, D_MODEL), xs.reshape(ts, D_MODEL)

    o_b0, o_lr, o_g = PA_W, PA_W + PB_LR, PA_W + PB_LR + GLA_RANK
    w_a = w_in[:, :PA_W].astype(BF16)
    w_b = jnp.concatenate([w_in[:, o_b0:o_g], jnp.zeros((D_MODEL, LANE - GLA_RANK), F32)], axis=1).astype(BF16)
    w_g = w_in[:, o_g:].astype(BF16)
    wup = jnp.concatenate([gla_w_up, jnp.zeros((LANE - GLA_RANK, H_B * DK_B), F32)], axis=0).astype(BF16)
    gb = gla_b.reshape(1, H_B * DK_B)
    ng = gla_norm_g.reshape(1, DV_B)
    wpa, wpb, wo = w_pa.astype(BF16), w_pb.astype(BF16), w_o.astype(BF16)
    g1, b1 = ln1_g.reshape(1, D_MODEL), ln1_b.reshape(1, D_MODEL)
    g2, b2 = ln2_g.reshape(1, D_MODEL), ln2_b.reshape(1, D_MODEL)
    rw = jnp.concatenate([router_w, jnp.zeros((D_MODEL, LANE - N_EXPERTS), F32)], axis=1)
    rwh = rw.astype(BF16)
    rwl = (rw - rwh.astype(F32)).astype(BF16)
    rbp = jnp.concatenate([router_b, jnp.full((LANE - N_EXPERTS,), NEG, F32)]).reshape(1, LANE)
    caches8 = [c.reshape(dbatch, c.shape[1], KV_ROWS, HEAD_DIM) for c in caches]

    pa_p = _project(xp2, w_a, PROJ_TM, A_QKV_WIDTH, "proj_a_prompt", head_major=True)
    pb_p = _project(xp2, w_b, PROJ_TM, PB_W, "proj_b_prompt")
    pg_p = _project(xp2, w_g, PROJ_TM, 1024, "proj_g_prompt")
    oas, lses = [], []
    for g in range(N_GROUPS):
        o, lse = _attn_heads(pa_p, _prompt_table(rel_bias, g), g, batch, seq)
        oas.append(o)
        lses.append(lse)
    ob_p, st_p = _gla_heads(pb_p, wup, gb, ng, jnp.zeros((batch, H_B, DK_B, DV_B), F32), batch, seq, GLA_CHUNK, 256)
    x1_p, x1p_p, ti_p, gt_p, rk_p, cnt_p = _merge(xp2, oas + lses, ob_p, pg_p, wpa, wpb, wo, g1, b1, rwh, rwl, rbp,
                                                  jnp.zeros((1, LANE), F32), 256)

    pa_s = _project(xs2, w_a, ts, A_QKV_WIDTH, "proj_a_sample", head_major=True)
    pb_s = _project(xs2, w_b, ts, PB_W, "proj_b_sample")
    pg_s = _project(xs2, w_g, ts, 1024, "proj_g_sample")
    tabc, tabn, combo_base = _sample_tables(rel_bias, dseq)
    new_rows = [_kv_pack(pa_s, g, dbatch, dseq, dseq, dseq) for g in range(N_GROUPS)]
    oa_s = _attn_sample(pa_s, new_rows, caches8, tabc, tabn, combo_base, dbatch, dseq)
    chunk_s = int(np.gcd(dseq, GLA_CHUNK))
    ob_s, st_s = _gla_heads(pb_s, wup, gb, ng, state, dbatch, dseq, chunk_s, dseq)
    x1_s, x1p_s, ti_s, gt_s, rk_s, cnt_all = _merge(xs2, [oa_s], ob_s, pg_s, wpa, wpb, wo, g1, b1, rwh, rwl, rbp, cnt_p, ts)

    x1p = jnp.concatenate([x1p_p, x1p_s], axis=0)
    top_i = jnp.concatenate([ti_p, ti_s], axis=0)
    gate = jnp.concatenate([gt_p, gt_s], axis=0)
    t_all = tp + ts
    n_blocks = -(-(t_all * TOP_K) // MOE_BM) + N_EXPERTS
    rank = jnp.concatenate([rk_p, rk_s], axis=0)
    counts = cnt_all[0, :N_EXPERTS].astype(jnp.int32)
    dest, be, rows, n_used = _route(top_i, rank, counts, n_blocks)
    xsorted = _sc_scatter_rows(x1p, dest, n_blocks * MOE_BM)
    keeps = [min(window, seq) for window, _ in DILATED_GROUPS]
    packed_p = {g: _kv_pack(pa_p, g, batch, seq, keeps[g], Q_BLOCK) for g in range(N_GROUPS - 1)}
    eo = _experts(xsorted, be, rows, n_used, w_gate_up, b_gate_up, w_down, b_down)
    h1 = (tp // 2 // 256) * 256
    ge1 = _sc_gather(eo, dest[:, :h1].reshape(TOP_K * h1)).reshape(TOP_K, h1, D_MODEL // 2)
    ge2 = _sc_gather(eo, dest[:, h1:].reshape(TOP_K * (t_all - h1))).reshape(TOP_K, t_all - h1, D_MODEL // 2)
    packed_p[N_GROUPS - 1] = _kv_pack(pa_p, N_GROUPS - 1, batch, seq, keeps[N_GROUPS - 1], Q_BLOCK)
    y_p = _combine(x1_p, ge1, gate, g2, b2, h1, 256, out_rows=tp)
    y_p = _combine(x1_p, ge2, gate, g2, b2, tp - h1, 256, x_row0=h1, gate_row0=h1, into=y_p)
    y_s = _combine(x1_s, ge2, gate, g2, b2, ts, 256, ge_row0=tp - h1, gate_row0=tp)

    shifted = _sc_cache_shift(caches8, dseq)
    bufs_p, bufs_s = [], []
    for g, (window, _) in enumerate(DILATED_GROUPS):
        keep = keeps[g]
        bufs_p.append(packed_p[g].reshape(batch, keep, 2, H_A, HEAD_DIM))
        clen = caches[g].shape[1]
        assert clen == window and dseq <= clen
        buf = lax.dynamic_update_slice(shifted[g], new_rows[g], (0, clen - dseq, 0, 0))
        bufs_s.append(buf.reshape(dbatch, clen, 2, H_A, HEAD_DIM))
    return y_p.reshape(batch, seq, D_MODEL), y_s.reshape(dbatch, dseq, D_MODEL), bufs_p, st_p, bufs_s, st_s


def kernel(x_prompt, x_sample, cache_a1_kv, cache_a2_kv, cache_a3_kv, state_b_s, w_in, rel_bias, gla_w_up, gla_b,
           gla_norm_g, w_pa, w_pb, w_o, ln1_g, ln1_b, router_w, router_b, w_gate_up, b_gate_up, w_down, b_down,
           ln2_g, ln2_b):
    assert w_in.shape[0] == DEPTH
    yp, ys, bufs_p, st_p, bufs_s, st_s = _layer(
        x_prompt, x_sample, (cache_a1_kv[0], cache_a2_kv[0], cache_a3_kv[0]), state_b_s[0], w_in[0], rel_bias,
        gla_w_up[0], gla_b[0], gla_norm_g[0], w_pa[0], w_pb[0], w_o[0], ln1_g[0], ln1_b[0], router_w[0], router_b[0],
        w_gate_up[0], b_gate_up[0], w_down[0], b_down[0], ln2_g[0], ln2_b[0])
    return (yp, ys, bufs_p[0][None], bufs_p[1][None], bufs_p[2][None], st_p[None],
            bufs_s[0][None], bufs_s[1][None], bufs_s[2][None], st_s[None].astype(state_b_s.dtype))
```

```python
import functools

import numpy as np
import jax
import jax.numpy as jnp
from jax import lax
from jax.experimental import pallas as pl
from jax.experimental.pallas import tpu as pltpu
from jax.experimental.pallas import tpu_sc as plsc

F32 = jnp.float32
BF16 = jnp.bfloat16

D_MODEL = 2048
HEAD_DIM = 128
DILATED_GROUPS = ((128, 1), (512, 4), (2048, 16))
N_GROUPS = 3
H_A = 4
A_WIDTH = H_A * HEAD_DIM
A_QKV_WIDTH = N_GROUPS * A_WIDTH
Q_BLOCK = 128
N_BUCKETS = 32
REL_MAX_DIST = 2048
H_B = 4
DK_B = 64
DV_B = 128
GLA_RANK = 16
GLA_TAU = 16.0
GLA_CHUNK = 64
GLA_SUB = 16
GLA_EXP_CLAMP = 80.0
N_EXPERTS = 32
TOP_K = 4
D_FF = 2048
SWIGLU_LIMIT = 7.0
SWIGLU_ALPHA = 1.702
LN_EPS = 1e-5
RMS_EPS = 1e-6
DEPTH = 1
DEEPNORM_ALPHA = (2.0 * DEPTH) ** 0.25
ATT_SCALE = HEAD_DIM ** -0.5
NEG = float(np.finfo(np.float32).min)

VMEM_LIMIT = 56 * 1024 * 1024
LANE = 128

PA_W = 3 * A_QKV_WIDTH
PB_LR = H_B * DK_B * 2 + H_B * DV_B * 2
PB_W = PB_LR + LANE
PG_W = 2 * D_MODEL

PROJ_TM = 1024

MOE_BM = 2560
MOE_SB = 320
MOE_TF = 512
MOE_NF = D_FF // MOE_TF
MOE_TN = 512
MERGE_TC = 512
SC_CHUNK_BYTES = 128 * 1024
SC_SCATTER_BYTES = 160 * 1024


def _cparams(sem):
    return pltpu.CompilerParams(dimension_semantics=sem, vmem_limit_bytes=VMEM_LIMIT)


def _dot(a, b):
    return jnp.dot(a, b, preferred_element_type=F32)


def _dot_nt(a, b):
    return lax.dot_general(a, b, (((1,), (1,)), ((), ())), preferred_element_type=F32)


def _dot_tn(a, b):
    return lax.dot_general(a, b, (((0,), (0,)), ((), ())), preferred_element_type=F32)


def _proj_kernel(x_ref, w_ref, o_ref, xb_ref):
    @pl.when(pl.program_id(1) == 0)
    def _():
        xb_ref[...] = x_ref[...].astype(BF16)

    o_ref[...] = _dot(xb_ref[...], w_ref[...])


def _proj_heads_kernel(x_ref, w_ref, o_ref, xb_ref):
    @pl.when(pl.program_id(1) == 0)
    def _():
        xb_ref[...] = x_ref[...].astype(BF16)

    acc = _dot(xb_ref[...], w_ref[...])
    for c in range(o_ref.shape[0]):
        o_ref[c] = acc[:, c * HEAD_DIM:(c + 1) * HEAD_DIM]


def _project(x, w, tm, tn, name, head_major=False):
    T, D = x.shape
    N = w.shape[1]
    if head_major:
        nh = tn // HEAD_DIM
        out_spec = pl.BlockSpec((nh, tm, HEAD_DIM), lambda i, j: (j, i, 0))
        out_shape = jax.ShapeDtypeStruct((N // HEAD_DIM, T, HEAD_DIM), F32)
    else:
        out_spec = pl.BlockSpec((tm, tn), lambda i, j: (i, j))
        out_shape = jax.ShapeDtypeStruct((T, N), F32)
    return pl.pallas_call(
        _proj_heads_kernel if head_major else _proj_kernel,
        grid=(T // tm, N // tn),
        in_specs=[pl.BlockSpec((tm, D), lambda i, j: (i, 0)), pl.BlockSpec((D, tn), lambda i, j: (0, j))],
        out_specs=out_spec,
        out_shape=out_shape,
        scratch_shapes=[pltpu.VMEM((tm, D), BF16)],
        compiler_params=_cparams(("parallel", "arbitrary")),
        name=name,
    )(x, w)


def _t5_bucket(dist):
    max_exact = N_BUCKETS // 2
    d = np.maximum(dist, 1).astype(np.float32)
    large = max_exact + (np.log(d / max_exact) / np.log(REL_MAX_DIST / max_exact) * (N_BUCKETS - max_exact)).astype(np.int32)
    large = np.minimum(large, N_BUCKETS - 1)
    return np.where(dist < max_exact, dist, large).astype(np.int32)


def _bias_lookup(rel_bias, g, j, valid):
    _, dil = DILATED_GROUPS[g]
    bucket = _t5_bucket(dil * np.clip(j, 0, Q_BLOCK))
    onehot = bucket[None] == np.arange(N_BUCKETS).reshape((N_BUCKETS,) + (1,) * j.ndim)
    rb = rel_bias[:, g * H_A:(g + 1) * H_A].astype(F32).T.reshape((H_A, N_BUCKETS) + (1,) * j.ndim)
    vals = jnp.sum(jnp.where(onehot[None], rb, 0.0), axis=1)
    return jnp.where(valid[None], vals, NEG)


def _prompt_table(rel_bias, g):
    qi = np.arange(Q_BLOCK)[:, None]
    kj = np.arange(2 * Q_BLOCK)[None, :]
    j = Q_BLOCK + qi - kj
    return _bias_lookup(rel_bias, g, j, (j >= 0) & (j <= Q_BLOCK))


def _sample_tables(rel_bias, dec_seq):
    m = np.arange(Q_BLOCK)
    tabc, combo_base = [], []
    for g, (_, dil) in enumerate(DILATED_GROUPS):
        combo_base.append(len(tabc))
        for fl in range((dec_seq - 1) // dil + 1):
            j = Q_BLOCK + fl - m
            col = _bias_lookup(rel_bias, g, j, j <= Q_BLOCK).T
            col = jnp.concatenate([col, jnp.zeros_like(col)], axis=1)
            tabc.append(jnp.broadcast_to(col[:, :, None], (Q_BLOCK, 2 * H_A, LANE)))
    tabn = []
    s = np.arange(dec_seq)[:, None]
    sp = np.arange(dec_seq)[None, :]
    for g, (_, dil) in enumerate(DILATED_GROUPS):
        diff = s - sp
        t = _bias_lookup(rel_bias, g, diff // dil, (diff >= 0) & (diff % dil == 0))
        t = jnp.transpose(t, (1, 2, 0))
        t = jnp.concatenate([t, jnp.zeros_like(t)], axis=2)
        tabn.append(jnp.broadcast_to(t[..., None], (dec_seq, dec_seq, 2 * H_A, LANE)))
    return jnp.stack(tabc), jnp.stack(tabn), tuple(combo_base)


def _attn_prompt_kernel(q_ref, kc_ref, kp_ref, vc_ref, vp_ref, tab_ref, o_ref, lse_ref):
    has_prev = pl.program_id(2) > 0
    for h in range(H_A):
        sl = slice(h * HEAD_DIM, (h + 1) * HEAD_DIM)
        q = q_ref[:, sl].astype(BF16)
        sc = _dot_nt(q, kc_ref[:, sl].astype(BF16)) * ATT_SCALE + tab_ref[h, :, Q_BLOCK:]
        sp = _dot_nt(q, kp_ref[:, sl].astype(BF16)) * ATT_SCALE + tab_ref[h, :, :Q_BLOCK]
        sp = jnp.where(has_prev, sp, NEG)
        m = jnp.maximum(jnp.max(sc, axis=-1, keepdims=True), jnp.max(sp, axis=-1, keepdims=True))
        pc = jnp.exp(sc - m)
        pp = jnp.exp(sp - m)
        l = jnp.sum(pc, axis=-1, keepdims=True) + jnp.sum(pp, axis=-1, keepdims=True)
        inv = 1.0 / l
        o = _dot((pc * inv).astype(BF16), vc_ref[:, sl].astype(BF16)) + _dot((pp * inv).astype(BF16), vp_ref[:, sl].astype(BF16))
        o_ref[:, sl] = o
        lse_ref[:, sl] = jnp.broadcast_to(m + jnp.log(l), (Q_BLOCK, HEAD_DIM))


def _attn_prompt_strided_kernel(*refs, dil, with_prev):
    if with_prev:
        q_ref, kc_ref, kp_ref, vc_ref, vp_ref, tab_ref, o_ref, lse_ref = refs
    else:
        q_ref, kc_ref, vc_ref, tab_ref, o_ref, lse_ref = refs
    has_prev = pl.program_id(1) > 0

    def body(r, carry):
        idx = pl.ds(r, Q_BLOCK, stride=dil)
        q = q_ref[idx, :].astype(BF16)
        sc = _dot_nt(q, kc_ref[idx, :].astype(BF16)) * ATT_SCALE + tab_ref[:, Q_BLOCK:]
        m = jnp.max(sc, axis=-1, keepdims=True)
        if with_prev:
            sp = _dot_nt(q, kp_ref[idx, :].astype(BF16)) * ATT_SCALE + tab_ref[:, :Q_BLOCK]
            sp = jnp.where(has_prev, sp, NEG)
            m = jnp.maximum(m, jnp.max(sp, axis=-1, keepdims=True))
        pc = jnp.exp(sc - m)
        l = jnp.sum(pc, axis=-1, keepdims=True)
        if with_prev:
            pp = jnp.exp(sp - m)
            l = l + jnp.sum(pp, axis=-1, keepdims=True)
        inv = 1.0 / l
        o = _dot((pc * inv).astype(BF16), vc_ref[idx, :].astype(BF16))
        if with_prev:
            o = o + _dot((pp * inv).astype(BF16), vp_ref[idx, :].astype(BF16))
        o_ref[idx, :] = o
        lse_ref[idx, :] = jnp.broadcast_to(m + jnp.log(l), (Q_BLOCK, HEAD_DIM))
        return carry

    lax.fori_loop(0, dil, body, 0, unroll=min(dil, 4))


def _attn_prompt_strided(pa, table, g, batch, seq):
    _, dil = DILATED_GROUPS[g]
    rows = dil * Q_BLOCK
    nblk = seq // rows
    with_prev = nblk > 1
    hcols = A_QKV_WIDTH // HEAD_DIM

    def spec(sec, prev):
        if prev:
            return pl.BlockSpec((rows, HEAD_DIM), lambda b, i, h: (b * nblk + jnp.maximum(i - 1, 0), sec * hcols + g * H_A + h))
        return pl.BlockSpec((rows, HEAD_DIM), lambda b, i, h: (b * nblk + i, sec * hcols + g * H_A + h))

    in_specs = [spec(0, False), spec(1, False)] + ([spec(1, True)] if with_prev else []) + [spec(2, False)] + (
        [spec(2, True)] if with_prev else []) + [pl.BlockSpec((None, Q_BLOCK, 2 * Q_BLOCK), lambda b, i, h: (h, 0, 0))]
    out_spec = pl.BlockSpec((rows, HEAD_DIM), lambda b, i, h: (b * nblk + i, h))
    return pl.pallas_call(
        functools.partial(_attn_prompt_strided_kernel, dil=dil, with_prev=with_prev),
        grid=(batch, nblk, H_A),
        in_specs=in_specs,
        out_specs=[out_spec, out_spec],
        out_shape=[jax.ShapeDtypeStruct((batch * seq, A_WIDTH), F32)] * 2,
        compiler_params=_cparams(("parallel", "arbitrary", "arbitrary")),
        name=f"attn_prompt_g{g}",
    )(*([pa] * (len(in_specs) - 1)), table)


def _attn_prompt_group(pa, table, g, batch, seq):
    _, dil = DILATED_GROUPS[g]
    if dil > 1:
        return _attn_prompt_strided(pa, table, g, batch, seq)
    sub = seq // dil
    nqb = sub // Q_BLOCK
    wblk = PA_W // A_WIDTH
    pv = pa.reshape(batch, sub, dil * PA_W)

    def spec(off, prev):
        if prev:
            return pl.BlockSpec((None, Q_BLOCK, A_WIDTH), lambda b, r, i: (b, jnp.maximum(i - 1, 0), r * wblk + off + g))
        return pl.BlockSpec((None, Q_BLOCK, A_WIDTH), lambda b, r, i: (b, i, r * wblk + off + g))

    out_spec = pl.BlockSpec((None, Q_BLOCK, A_WIDTH), lambda b, r, i: (b, i, r))
    o, lse = pl.pallas_call(
        _attn_prompt_kernel,
        grid=(batch, dil, nqb),
        in_specs=[spec(0, False), spec(N_GROUPS, False), spec(N_GROUPS, True), spec(2 * N_GROUPS, False),
                  spec(2 * N_GROUPS, True), pl.BlockSpec((H_A, Q_BLOCK, 2 * Q_BLOCK), lambda b, r, i: (0, 0, 0))],
        out_specs=[out_spec, out_spec],
        out_shape=[jax.ShapeDtypeStruct((batch, sub, dil * A_WIDTH), F32)] * 2,
        compiler_params=_cparams(("parallel", "parallel", "arbitrary")),
        name=f"attn_prompt_g{g}",
    )(pv, pv, pv, pv, pv, table)
    return o.reshape(batch * seq, A_WIDTH), lse.reshape(batch * seq, A_WIDTH)


def _attn_heads_kernel(*refs, dil, nb, with_prev):
    if with_prev:
        q_ref, kc_ref, kp_ref, vc_ref, vp_ref, tab_ref, o_ref, lse_ref = refs
    else:
        q_ref, kc_ref, vc_ref, tab_ref, o_ref, lse_ref = refs
    has_prev = pl.program_id(1) > 0
    zero = jnp.zeros((Q_BLOCK, HEAD_DIM), BF16)
    tab_c = tab_ref[:, Q_BLOCK:]
    if with_prev:
        tab = jnp.concatenate([jnp.where(has_prev, tab_ref[:, :Q_BLOCK], NEG), tab_c], axis=1)
    else:
        tab = tab_c

    def heads(ref, bi, idx):
        return jnp.concatenate([ref[h, bi, idx, :].astype(BF16) for h in range(H_A)], axis=1)

    def body(u, carry):
        bi, r = u // dil, u % dil
        idx = pl.ds(r, Q_BLOCK, stride=dil)
        qs = [q_ref[h, bi, idx, :].astype(BF16) for h in range(H_A)]
        q_stack = jnp.concatenate(
            [jnp.concatenate([qs[h] if c == h else zero for c in range(H_A)], axis=1) for h in range(H_A)], axis=0)
        if with_prev:
            k_all = jnp.concatenate([heads(kp_ref, bi, idx), heads(kc_ref, bi, idx)], axis=0)
            v_all = jnp.concatenate([heads(vp_ref, bi, idx), heads(vc_ref, bi, idx)], axis=0)
        else:
            k_all, v_all = heads(kc_ref, bi, idx), heads(vc_ref, bi, idx)
        s = _dot_nt(q_stack, k_all) * ATT_SCALE + tab
        m = jnp.max(s, axis=-1, keepdims=True)
        p = jnp.exp(s - m)
        l = jnp.sum(p, axis=-1, keepdims=True)
        o = _dot((p * (1.0 / l)).astype(BF16), v_all)
        lse = m + jnp.log(l)
        for h in range(H_A):
            rows = slice(h * Q_BLOCK, (h + 1) * Q_BLOCK)
            o_ref[h, bi, idx, :] = o[rows, h * HEAD_DIM:(h + 1) * HEAD_DIM]
            lse_ref[h, bi, idx, :] = jnp.broadcast_to(lse[rows], (Q_BLOCK, HEAD_DIM))
        return carry

    lax.fori_loop(0, nb * dil, body, 0, unroll=min(nb * dil, 2))


def _attn_heads(pa_hm, table, g, batch, seq):
    _, dil = DILATED_GROUPS[g]
    rows = dil * Q_BLOCK
    nblk = seq // rows
    with_prev = nblk > 1
    nb = 2 if (dil == 1 and batch % 2 == 0) else 1
    pv = pa_hm.reshape(pa_hm.shape[0], batch, seq, HEAD_DIM)

    def spec(sec, prev):
        if prev:
            return pl.BlockSpec((H_A, nb, rows, HEAD_DIM), lambda b, i: (sec * N_GROUPS + g, b, jnp.maximum(i - 1, 0), 0))
        return pl.BlockSpec((H_A, nb, rows, HEAD_DIM), lambda b, i: (sec * N_GROUPS + g, b, i, 0))

    in_specs = [spec(0, False), spec(1, False)] + ([spec(1, True)] if with_prev else []) + [spec(2, False)] + (
        [spec(2, True)] if with_prev else []) + [pl.BlockSpec((H_A * Q_BLOCK, 2 * Q_BLOCK), lambda b, i: (0, 0))]
    out_spec = pl.BlockSpec((H_A, nb, rows, HEAD_DIM), lambda b, i: (0, b, i, 0))
    o, lse = pl.pallas_call(
        functools.partial(_attn_heads_kernel, dil=dil, nb=nb, with_prev=with_prev),
        grid=(batch // nb, nblk),
        in_specs=in_specs,
        out_specs=[out_spec, out_spec],
        out_shape=[jax.ShapeDtypeStruct((H_A, batch, seq, HEAD_DIM), F32)] * 2,
        compiler_params=_cparams(("parallel", "arbitrary")),
        name=f"attn_prompt_g{g}",
    )(*([pv] * (len(in_specs) - 1)), table.reshape(H_A * Q_BLOCK, 2 * Q_BLOCK))
    return o.reshape(H_A, batch * seq, HEAD_DIM), lse.reshape(H_A, batch * seq, HEAD_DIM)


KV_ROWS = 2 * H_A


def _kv_pack_kernel(k_ref, v_ref, o_ref):
    o_ref[...] = jnp.stack([k_ref[h] for h in range(H_A)] + [v_ref[h] for h in range(H_A)], axis=1)


def _kv_pack(pa_hm, g, batch, seq, keep, tm):
    nblk, blk0, per_b = keep // tm, (seq - keep) // tm, seq // tm
    out = pl.pallas_call(
        _kv_pack_kernel,
        grid=(batch, nblk),
        in_specs=[pl.BlockSpec((H_A, tm, HEAD_DIM), lambda b, i: (N_GROUPS + g, b * per_b + blk0 + i, 0)),
                  pl.BlockSpec((H_A, tm, HEAD_DIM), lambda b, i: (2 * N_GROUPS + g, b * per_b + blk0 + i, 0))],
        out_specs=pl.BlockSpec((tm, KV_ROWS, HEAD_DIM), lambda b, i: (b * nblk + i, 0, 0)),
        out_shape=jax.ShapeDtypeStruct((batch * keep, KV_ROWS, HEAD_DIM), F32),
        compiler_params=_cparams(("parallel", "parallel")),
        name=f"kv_pack_g{g}_{keep}",
    )(pa_hm, pa_hm)
    return out.reshape(batch, keep, KV_ROWS, HEAD_DIM)


def _attn_sample_kernel(qkv_ref, n1_ref, n2_ref, n3_ref, c1_ref, c2_ref, c3_ref, tabc_ref, tabn_ref, o_ref, *,
                        dec_seq, combo_base):
    caches = (c1_ref, c2_ref, c3_ref)
    news = (n1_ref, n2_ref, n3_ref)
    zeros = jnp.zeros((H_A, HEAD_DIM), F32)
    for s in range(dec_seq):
        outs, lses = [], []
        for g, (_, dil) in enumerate(DILATED_GROUPS):
            rho, fl = s % dil, s // dil
            qm = jnp.concatenate([qkv_ref[g * H_A + h, s:s + 1, :] for h in range(H_A)] + [zeros], axis=0)
            kc = caches[g][:, rho]
            kn = news[g][...]
            sc = jnp.sum(kc * qm[None], axis=-1, keepdims=True) * ATT_SCALE + tabc_ref[combo_base[g] + fl]
            sn = jnp.sum(kn * qm[None], axis=-1, keepdims=True) * ATT_SCALE + tabn_ref[g, s]
            m = jnp.maximum(jnp.max(sc, axis=0), jnp.max(sn, axis=0))
            pc = jnp.exp(sc - m[None])
            pn = jnp.exp(sn - m[None])
            l = jnp.sum(pc, axis=0) + jnp.sum(pn, axis=0)
            acc = jnp.sum(pltpu.roll(pc, H_A, 1) * kc, axis=0) + jnp.sum(pltpu.roll(pn, H_A, 1) * kn, axis=0)
            outs.append(acc / pltpu.roll(l, H_A, 0))
            lses.append(pltpu.roll(m + jnp.log(l), H_A, 0))
        mm = jnp.maximum(jnp.maximum(lses[0], lses[1]), lses[2])
        ws = [jnp.exp(x - mm) for x in lses]
        o_ref[s] = (ws[0] * outs[0] + ws[1] * outs[1] + ws[2] * outs[2]) / (ws[0] + ws[1] + ws[2])


def _attn_sample(pa_hm, new_rows, caches, tabc, tabn, combo_base, batch, dec_seq):
    views, specs = [], []
    for g, (window, dil) in enumerate(DILATED_GROUPS):
        assert caches[g].shape[1] == window and dec_seq <= Q_BLOCK
        views.append(caches[g].reshape(batch, Q_BLOCK, dil, KV_ROWS, HEAD_DIM))
        used = min(dil, dec_seq)
        specs.append(pl.BlockSpec((None, Q_BLOCK, used, KV_ROWS, HEAD_DIM), lambda b: (b, 0, 0, 0, 0)))
    new_spec = pl.BlockSpec((None, dec_seq, KV_ROWS, HEAD_DIM), lambda b: (b, 0, 0, 0))
    out = pl.pallas_call(
        functools.partial(_attn_sample_kernel, dec_seq=dec_seq, combo_base=combo_base),
        grid=(batch,),
        in_specs=[pl.BlockSpec((N_GROUPS * H_A, dec_seq, HEAD_DIM), lambda b: (0, b, 0))] + [new_spec] * N_GROUPS + specs + [
            pl.BlockSpec(tabc.shape, lambda b: (0, 0, 0, 0)), pl.BlockSpec(tabn.shape, lambda b: (0, 0, 0, 0, 0))],
        out_specs=pl.BlockSpec((None, dec_seq, KV_ROWS, HEAD_DIM), lambda b: (b, 0, 0, 0)),
        out_shape=jax.ShapeDtypeStruct((batch, dec_seq, KV_ROWS, HEAD_DIM), F32),
        compiler_params=_cparams(("parallel",)),
        name="attn_sample",
    )(pa_hm, *new_rows, *views, tabc, tabn)
    return out[:, :, H_A:, :].reshape(batch * dec_seq, A_WIDTH)


def _sc_cache_shift(caches, drop):
    info = plsc.get_sparse_core_info()
    n_workers = info.num_cores * info.num_subcores
    batch = caches[0].shape[0]
    assert batch % n_workers == 0
    mesh = plsc.VectorSubcoreMesh(core_axis_name="c", subcore_axis_name="s")

    row_bytes = KV_ROWS * HEAD_DIM * 4
    chunks = []
    for c in caches:
        keep = c.shape[1] - drop
        ch = max(d for d in range(1, SC_CHUNK_BYTES // row_bytes + 1) if keep % d == 0)
        chunks.append(ch)
    buf_rows = max(chunks)

    @functools.partial(pl.kernel, mesh=mesh, out_type=[jax.ShapeDtypeStruct(c.shape, c.dtype) for c in caches],
                       scratch_types=[pltpu.VMEM((buf_rows, KV_ROWS, HEAD_DIM), caches[0].dtype)])
    def shift(*refs):
        srcs, dsts, buf = refs[:len(caches)], refs[len(caches):2 * len(caches)], refs[-1]
        wid = lax.axis_index("s") * info.num_cores + lax.axis_index("c")
        for j in range(batch // n_workers):
            b = wid * (batch // n_workers) + j
            for src, dst, ch in zip(srcs, dsts, chunks):
                stage = buf.at[pl.ds(0, ch)]

                @pl.loop(0, (src.shape[1] - drop) // ch)
                def _(i):
                    pltpu.sync_copy(src.at[b, pl.ds(drop + i * ch, ch)], stage)
                    pltpu.sync_copy(stage, dst.at[b, pl.ds(i * ch, ch)])

    return shift(*caches)


def _split3(x):
    hi = x.astype(BF16)
    r = x - hi.astype(F32)
    mid = r.astype(BF16)
    lo = (r - mid.astype(F32)).astype(BF16)
    return hi, mid, lo


def _gla_kernel(p_ref, wup_ref, gb_ref, ng_ref, s0_ref, o_ref, st_ref, *, chunk, tb):
    @pl.when(pl.program_id(1) == 0)
    def _():
        st_ref[...] = s0_ref[...]

    sub = min(GLA_SUB, chunk)
    kq = H_B * DK_B

    def rb(x):
        xb = x.astype(BF16)
        return xb if chunk >= 16 else xb.astype(F32)

    row = lax.broadcasted_iota(jnp.int32, (chunk, chunk), 0)
    colm = lax.broadcasted_iota(jnp.int32, (chunk, chunk), 1)
    tri = rb(jnp.where(row >= colm, 1.0, 0.0))
    for c in range(tb // chunk):
        rows = slice(c * chunk, (c + 1) * chunk)
        z = _dot(rb(p_ref[rows, PB_LR:PB_W]), rb(wup_ref[...])) + gb_ref[...]
        la = -(jnp.maximum(-z, 0.0) + jnp.log1p(jnp.exp(-jnp.abs(z)))) * (1.0 / GLA_TAU)
        b = functools.reduce(lambda u, w: u + w, [_dot(tri, rb(t)) for t in _split3(la)])
        blast = b[chunk - 1:chunk, :]
        q = p_ref[rows, 0:kq] * (DK_B ** -0.5)
        k = p_ref[rows, kq:2 * kq]
        qin = rb(q * jnp.exp(b))
        kst = rb(k * jnp.exp(blast - b))
        for h in range(H_B):
            ks = slice(h * DK_B, (h + 1) * DK_B)
            vs = slice(h * DV_B, (h + 1) * DV_B)
            st = st_ref[h]
            vb = rb(p_ref[rows, 2 * kq + h * DV_B:2 * kq + (h + 1) * DV_B])
            o_inter = _dot_nt(qin[:, ks], rb(st))
            parts = []
            for blk in range(chunk // sub):
                r0 = blk * sub
                n = r0 + sub
                ref_b = b[r0 - 1:r0, ks] if blk > 0 else jnp.zeros((1, DK_B), F32)
                qi = rb(q[r0:n, ks] * jnp.exp(b[r0:n, ks] - ref_b))
                ki = rb(k[0:n, ks] * jnp.exp(jnp.minimum(ref_b - b[0:n, ks], GLA_EXP_CLAMP)))
                a = _dot_nt(qi, ki)
                ti = lax.broadcasted_iota(jnp.int32, (sub, n), 0) + r0
                si = lax.broadcasted_iota(jnp.int32, (sub, n), 1)
                a = jnp.where(si <= ti, a, 0.0)
                parts.append(_dot(rb(a), vb[0:n]))
            o = o_inter + (jnp.concatenate(parts, axis=0) if len(parts) > 1 else parts[0])
            st_ref[h] = st * jnp.exp(blast[:, ks]) + _dot_tn(vb, kst[:, ks])
            on = o * lax.rsqrt(jnp.mean(o * o, axis=-1, keepdims=True) + RMS_EPS) * ng_ref[...]
            rg = p_ref[rows, 2 * kq + H_B * DV_B + h * DV_B:2 * kq + H_B * DV_B + (h + 1) * DV_B]
            o_ref[rows, vs] = (on * (rg * jax.nn.sigmoid(rg))).astype(BF16)


def _gla(pb, wup, gb, ng, s0t, batch, seq, chunk, tb):
    p3 = pb.reshape(batch, seq, PB_W)
    o, st = pl.pallas_call(
        functools.partial(_gla_kernel, chunk=chunk, tb=tb),
        grid=(batch, seq // tb),
        in_specs=[pl.BlockSpec((None, tb, PB_W), lambda b, i: (b, i, 0)),
                  pl.BlockSpec(wup.shape, lambda b, i: (0, 0)),
                  pl.BlockSpec(gb.shape, lambda b, i: (0, 0)),
                  pl.BlockSpec(ng.shape, lambda b, i: (0, 0)),
                  pl.BlockSpec((None, H_B, DV_B, DK_B), lambda b, i: (b, 0, 0, 0))],
        out_specs=[pl.BlockSpec((None, tb, H_B * DV_B), lambda b, i: (b, i, 0)),
                   pl.BlockSpec((None, H_B, DV_B, DK_B), lambda b, i: (b, 0, 0, 0))],
        out_shape=[jax.ShapeDtypeStruct((batch, seq, H_B * DV_B), BF16),
                   jax.ShapeDtypeStruct((batch, H_B, DV_B, DK_B), F32)],
        compiler_params=_cparams(("parallel", "arbitrary")),
        name=f"gla_c{chunk}",
    )(p3, wup, gb, ng, s0t)
    return o.reshape(batch * seq, H_B * DV_B), st


def _gla_heads_kernel(p_ref, wup_ref, gb_ref, ng_ref, s0_ref, o_ref, st_ref, *, chunk, tb):
    @pl.when(pl.program_id(1) == 0)
    def _():
        st_ref[...] = s0_ref[...]

    sub = min(GLA_SUB, chunk)
    n_sub = chunk // sub
    kq, vw = H_B * DK_B, H_B * DV_B

    def rb(x):
        xb = x.astype(BF16)
        return xb if chunk >= 16 else xb.astype(F32)

    row = lax.broadcasted_iota(jnp.int32, (chunk, chunk), 0)
    colm = lax.broadcasted_iota(jnp.int32, (chunk, chunk), 1)
    tri = rb(jnp.where(row >= colm, 1.0, 0.0))
    lane_head = lax.broadcasted_iota(jnp.int32, (1, kq), 1) // DK_B
    diag_blocks = (lax.broadcasted_iota(jnp.int32, (vw, kq), 0) // DV_B
                   == lax.broadcasted_iota(jnp.int32, (vw, kq), 1) // DK_B)
    a_cols = LANE if n_sub > 1 else chunk
    t_in = lax.broadcasted_iota(jnp.int32, (H_B * chunk, a_cols), 0) % chunk
    s_in = lax.broadcasted_iota(jnp.int32, (H_B * chunk, a_cols), 1)
    for c in range(tb // chunk):
        rows = slice(c * chunk, (c + 1) * chunk)
        z = _dot(rb(p_ref[rows, PB_LR:PB_W]), rb(wup_ref[...])) + gb_ref[...]
        la = -(jnp.maximum(-z, 0.0) + jnp.log1p(jnp.exp(-jnp.abs(z)))) * (1.0 / GLA_TAU)
        b3 = _dot(tri, rb(jnp.concatenate(_split3(la), axis=1)))
        b = b3[:, :kq] + b3[:, kq:2 * kq] + b3[:, 2 * kq:]
        blast = b[chunk - 1:chunk, :]
        q = p_ref[rows, 0:kq] * (DK_B ** -0.5)
        k = p_ref[rows, kq:2 * kq]
        v = rb(p_ref[rows, 2 * kq:2 * kq + vw])
        st = st_ref[...]
        o_inter = _dot_nt(rb(q * jnp.exp(b)), rb(st))
        refs = [jnp.zeros((1, kq), F32)] + [b[i * sub - 1:i * sub, :] for i in range(1, n_sub)]
        ref_rows = jnp.concatenate([jnp.broadcast_to(r, (sub, kq)) for r in refs], axis=0)
        qs = q * jnp.exp(b - ref_rows)
        q_stack = rb(jnp.concatenate([jnp.where(lane_head == h, qs, 0.0) for h in range(H_B)], axis=0))
        k_parts = []
        for r in refs:
            k_parts.append(k * jnp.exp(jnp.minimum(r - b, GLA_EXP_CLAMP)))
            if a_cols > chunk:
                k_parts.append(jnp.zeros((a_cols - chunk, kq), F32))
        raw = _dot_nt(q_stack, rb(jnp.concatenate(k_parts, axis=0)))
        a = jnp.zeros((H_B * chunk, a_cols), F32)
        for i in range(n_sub):
            a = a + jnp.where(t_in // sub == i, raw[:, i * a_cols:(i + 1) * a_cols], 0.0)
        a = jnp.where(s_in <= t_in, a, 0.0)[:, :chunk]
        o_all = _dot(rb(a), v)
        o = o_inter + jnp.concatenate(
            [o_all[h * chunk:(h + 1) * chunk, h * DV_B:(h + 1) * DV_B] for h in range(H_B)], axis=1)
        st_ref[...] = st * jnp.exp(blast) + jnp.where(diag_blocks, _dot_tn(v, rb(k * jnp.exp(blast - b))), 0.0)
        normed = []
        for h in range(H_B):
            oh = o[:, h * DV_B:(h + 1) * DV_B]
            normed.append(oh * lax.rsqrt(jnp.mean(oh * oh, axis=-1, keepdims=True) + RMS_EPS) * ng_ref[...])
        rg = p_ref[rows, 2 * kq + vw:2 * kq + 2 * vw]
        o_ref[rows, :] = (jnp.concatenate(normed, axis=1) * (rg * jax.nn.sigmoid(rg))).astype(BF16)


def _gla_heads(pb, wup, gb, ng, s0, batch, seq, chunk, tb):
    kq, vw = H_B * DK_B, H_B * DV_B
    eye = jnp.eye(H_B, dtype=F32)
    s0t = jnp.swapaxes(s0, -1, -2)
    s0_bd = (s0t[:, :, :, None, :] * eye[None, :, None, :, None]).reshape(batch, vw, kq)
    p3 = pb.reshape(batch, seq, PB_W)
    o, st = pl.pallas_call(
        functools.partial(_gla_heads_kernel, chunk=chunk, tb=tb),
        grid=(batch, seq // tb),
        in_specs=[pl.BlockSpec((None, tb, PB_W), lambda b, i: (b, i, 0)),
                  pl.BlockSpec(wup.shape, lambda b, i: (0, 0)),
                  pl.BlockSpec(gb.shape, lambda b, i: (0, 0)),
                  pl.BlockSpec(ng.shape, lambda b, i: (0, 0)),
                  pl.BlockSpec((None, vw, kq), lambda b, i: (b, 0, 0))],
        out_specs=[pl.BlockSpec((None, tb, vw), lambda b, i: (b, i, 0)),
                   pl.BlockSpec((None, vw, kq), lambda b, i: (b, 0, 0))],
        out_shape=[jax.ShapeDtypeStruct((batch, seq, vw), BF16), jax.ShapeDtypeStruct((batch, vw, kq), F32)],
        compiler_params=_cparams(("parallel", "arbitrary")),
        name=f"gla_c{chunk}",
    )(p3, wup, gb, ng, s0_bd)
    st5 = st.reshape(batch, H_B, DV_B, H_B, DK_B)
    st_heads = jnp.stack([st5[:, h, :, h, :] for h in range(H_B)], axis=1)
    return o.reshape(batch * seq, vw), jnp.swapaxes(st_heads, -1, -2)


def _layer_norm(u, g, b):
    mu = jnp.mean(u, axis=-1, keepdims=True)
    d = u - mu
    var = jnp.mean(d * d, axis=-1, keepdims=True)
    return d * lax.rsqrt(var + LN_EPS) * g + b


def _merge_kernel(*refs, n_groups):
    x_ref = refs[0]
    oa_refs = refs[1:1 + 2 * n_groups] if n_groups > 1 else refs[1:2]
    rest = refs[1 + (2 * n_groups if n_groups > 1 else 1):]
    (ob_ref, pg_a_ref, pg_b_ref, wpa_ref, wpb_ref, wo_ref, g1_ref, b1_ref, rwh_ref, rwl_ref, rb_ref, cnt0_ref) = rest[:12]
    x1_ref, x1p_ref, ti_ref, gt_ref, rank_ref, cnt_ref, br_ref, u_ref = rest[-8:]
    tm = x_ref.shape[0]
    half = D_MODEL // 2

    def rows_part(rs):
        if n_groups > 1:
            def lanes(r):
                return jnp.concatenate([r[h, rs, :] for h in range(H_A)], axis=1)

            os_ = [lanes(r) for r in oa_refs[:n_groups]]
            ls = [lanes(r) for r in oa_refs[n_groups:]]
            mm = functools.reduce(jnp.maximum, ls)
            ws = [jnp.exp(x - mm) for x in ls]
            oa = sum(w * o for w, o in zip(ws, os_)) / sum(ws)
        else:
            oa = oa_refs[0][rs, :]
        oab, obb = oa.astype(BF16), ob_ref[rs, :]
        for c in range(D_MODEL // MERGE_TC):
            cs = slice(c * MERGE_TC, (c + 1) * MERGE_TC)
            br_ref[rs, cs] = (jax.nn.sigmoid(pg_a_ref[rs, cs]) * _dot(oab, wpa_ref[:, cs])
                              + jax.nn.sigmoid(pg_b_ref[rs, cs]) * _dot(obb, wpb_ref[:, cs])).astype(BF16)
        for c in range(D_MODEL // MERGE_TC):
            cs = slice(c * MERGE_TC, (c + 1) * MERGE_TC)
            u_ref[rs, cs] = DEEPNORM_ALPHA * x_ref[rs, cs] + _dot(br_ref[rs, :], wo_ref[:, cs])
        x1 = _layer_norm(u_ref[rs, :], g1_ref[...], b1_ref[...])
        x1_ref[rs, :] = x1
        xh = x1.astype(BF16)
        xhf = xh.astype(F32)
        bits = lax.bitcast_convert_type(xhf, jnp.int32)
        x1p_ref[rs, :] = lax.shift_right_logical(bits[:, :half], 16) | bits[:, half:]
        xl = (x1 - xhf).astype(BF16)
        logits = _dot(xh, rwh_ref[...]) + _dot(xl, rwh_ref[...]) + _dot(xh, rwl_ref[...]) + rb_ref[...]
        lane = lax.broadcasted_iota(jnp.int32, logits.shape, 1)
        vals = logits
        top_v, top_i = [], []
        for _ in range(TOP_K):
            m = jnp.max(vals, axis=-1, keepdims=True)
            ik = jnp.min(jnp.where(vals == m, lane, LANE), axis=-1, keepdims=True)
            vals = jnp.where(lane == ik, -jnp.inf, vals)
            top_v.append(m)
            top_i.append(ik)
        es = [jnp.exp(v - top_v[0]) for v in top_v]
        tot = functools.reduce(lambda a, b: a + b, es)
        ti_ref[rs, :] = jnp.concatenate(top_i, axis=1)
        gt_ref[rs, :] = jnp.concatenate([e / tot for e in es], axis=1)
        return [jnp.where(lane == ik, 1.0, 0.0) for ik in top_i]

    picks = rows_part(slice(0, tm))
    @pl.when(pl.program_id(0) == 0)
    def _():
        cnt_ref[...] = cnt0_ref[...]

    onehot = picks[0] + picks[1] + picks[2] + picks[3]
    tri = jnp.where(lax.broadcasted_iota(jnp.int32, (tm, tm), 0) > lax.broadcasted_iota(jnp.int32, (tm, tm), 1), 1.0, 0.0)
    before = _dot(tri.astype(BF16), onehot.astype(BF16)) + cnt_ref[...]
    rank_ref[...] = jnp.concatenate([jnp.sum(pk * before, axis=-1, keepdims=True) for pk in picks], axis=1).astype(jnp.int32)
    cnt_ref[...] = cnt_ref[...] + jnp.sum(onehot, axis=0, keepdims=True)


def _merge(x, oas, ob, pg, wpa, wpb, wo, g1, b1, rwh, rwl, rbp, cnt0, tm, packed_rows=None, packed_into=None):
    T = x.shape[0]
    n_groups = len(oas) // 2 if len(oas) > 1 else 1
    packed_rows = T if packed_rows is None else packed_rows
    packed_blk0 = 0
    extra_specs, extra_args, aliases = [], [], {}
    if packed_into is not None:
        packed_rows = packed_into.shape[0]
        packed_blk0 = (packed_rows - T) // tm
        extra_specs, extra_args = [pl.BlockSpec(memory_space=pl.ANY)], [packed_into]

    def row(w):
        return pl.BlockSpec((tm, w), lambda i: (i, 0))

    def const(a):
        return pl.BlockSpec(a.shape, lambda i: (0,) * a.ndim, pipeline_mode=pl.Buffered(1))

    def oa_spec(a):
        return pl.BlockSpec((H_A, tm, HEAD_DIM), lambda i: (0, i, 0)) if a.ndim == 3 else row(A_WIDTH)

    in_specs = ([row(D_MODEL)] + [oa_spec(a) for a in oas] + [row(H_B * DV_B), row(D_MODEL),
                pl.BlockSpec((tm, D_MODEL), lambda i: (i, 1))]
                + [const(a) for a in (wpa, wpb, wo, g1, b1, rwh, rwl, rbp, cnt0)] + extra_specs)
    if packed_into is not None:
        aliases = {len(in_specs) - 1: 1}
    return pl.pallas_call(
        functools.partial(_merge_kernel, n_groups=n_groups),
        grid=(T // tm,),
        in_specs=in_specs,
        out_specs=[row(D_MODEL), pl.BlockSpec((tm, D_MODEL // 2), lambda i: (packed_blk0 + i, 0)), row(TOP_K), row(TOP_K),
                   row(TOP_K), pl.BlockSpec((1, LANE), lambda i: (0, 0))],
        out_shape=[jax.ShapeDtypeStruct((T, D_MODEL), F32), jax.ShapeDtypeStruct((packed_rows, D_MODEL // 2), jnp.int32),
                   jax.ShapeDtypeStruct((T, TOP_K), jnp.int32), jax.ShapeDtypeStruct((T, TOP_K), F32),
                   jax.ShapeDtypeStruct((T, TOP_K), jnp.int32), jax.ShapeDtypeStruct((1, LANE), F32)],
        scratch_shapes=[pltpu.VMEM((tm, D_MODEL), BF16), pltpu.VMEM((tm, D_MODEL), F32)],
        input_output_aliases=aliases,
        compiler_params=_cparams(("arbitrary",)),
        name=f"merge_g{n_groups}",
    )(x, *oas, ob, pg, pg, wpa, wpb, wo, g1, b1, rwh, rwl, rbp, cnt0, *extra_args)


def _sc_gather(table, idx):
    info = plsc.get_sparse_core_info()
    n_workers = info.num_cores * info.num_subcores
    n, width = idx.shape[0], table.shape[1]
    per_worker = n // n_workers
    chunk = SC_CHUNK_BYTES // (width * table.dtype.itemsize)
    assert per_worker * n_workers == n and per_worker % chunk == 0 and chunk % 8 == 0
    mesh = plsc.VectorSubcoreMesh(core_axis_name="c", subcore_axis_name="s")

    @functools.partial(
        pl.kernel, mesh=mesh,
        out_type=jax.ShapeDtypeStruct((n, width), table.dtype),
        scratch_types=[pltpu.VMEM((chunk,), jnp.int32), pltpu.VMEM((chunk, width), table.dtype),
                       pltpu.SemaphoreType.DMA],
    )
    def gather(table_hbm, idx_hbm, out_hbm, idx_v, rows_v, sem):
        wid = lax.axis_index("s") * info.num_cores + lax.axis_index("c")
        base = wid * per_worker

        @pl.loop(0, per_worker // chunk)
        def _(c):
            off = pl.multiple_of(base + c * chunk, chunk)
            pltpu.sync_copy(idx_hbm.at[pl.ds(off, chunk)], idx_v)
            pltpu.async_copy(table_hbm.at[idx_v], rows_v, sem).wait()
            pltpu.sync_copy(rows_v, out_hbm.at[pl.ds(off, chunk)])

    return gather(table, idx)


def _sc_scatter_rows(tables, idx, n_out):
    info = plsc.get_sparse_core_info()
    n_workers = info.num_cores * info.num_subcores
    n_idx, n = idx.shape
    width, dtype = tables[0].shape[1], tables[0].dtype
    max_rows = SC_SCATTER_BYTES // (width * dtype.itemsize)
    plan, tok0 = [], 0
    for t in tables:
        per_worker = t.shape[0] // n_workers
        assert per_worker * n_workers == t.shape[0] and per_worker % 8 == 0 and tok0 % 8 == 0
        plan.append((per_worker, max(d for d in range(8, max_rows + 1, 8) if per_worker % d == 0), tok0))
        tok0 += t.shape[0]
    assert tok0 == n
    mesh = plsc.VectorSubcoreMesh(core_axis_name="c", subcore_axis_name="s")
    scratch_types = []
    for _, chunk, _ in plan:
        scratch_types += [pltpu.VMEM((chunk,), jnp.int32)] * n_idx + [pltpu.VMEM((chunk, width), dtype)]

    @functools.partial(pl.kernel, mesh=mesh, out_type=jax.ShapeDtypeStruct((n_out, width), dtype),
                       scratch_types=scratch_types)
    def scatter(*refs):
        table_refs, idx_hbm, out_hbm = refs[:len(tables)], refs[len(tables)], refs[len(tables) + 1]
        scratch = refs[len(tables) + 2:]
        wid = lax.axis_index("s") * info.num_cores + lax.axis_index("c")
        for j, (per_worker, chunk, first) in enumerate(plan):
            idx_vs = scratch[j * (n_idx + 1):j * (n_idx + 1) + n_idx]
            rows_v = scratch[j * (n_idx + 1) + n_idx]
            table_hbm = table_refs[j]

            @pl.loop(0, per_worker // chunk)
            def _(c):
                off = pl.multiple_of(wid * per_worker + c * chunk, 8)
                pltpu.sync_copy(table_hbm.at[pl.ds(off, chunk)], rows_v)
                for k in range(n_idx):
                    pltpu.sync_copy(idx_hbm.at[pl.ds(pl.multiple_of(k * n + first + off, 8), chunk)], idx_vs[k])
                for k in range(n_idx):
                    pltpu.sync_copy(rows_v, out_hbm.at[idx_vs[k]])

    return scatter(*tables, idx.reshape(n_idx * n))


def _expert_kernel(be_ref, rows_ref, nu_ref, xs_ref, wg_ref, wu_ref, bg_ref, bu_ref, wd_ref, bd_ref, o_ref, hid_ref):
    del be_ref, nu_ref
    i = pl.program_id(0)
    p = pl.program_id(1)
    nrows = rows_ref[i]
    n_live = (nrows + (MOE_SB - 1)) // MOE_SB
    half = D_MODEL // 2

    def paired(one):
        def pair(j, carry):
            one(2 * j)
            one(2 * j + 1)
            return carry

        lax.fori_loop(0, n_live // 2, pair, 0)

        @pl.when(n_live % 2 == 1)
        def _():
            one(n_live - 1)

    @pl.when(p < MOE_NF)
    def _():
        def gate_up(s):
            r0 = pl.multiple_of(s * MOE_SB, MOE_SB)
            rid = r0 + lax.broadcasted_iota(jnp.int32, (MOE_SB, half), 0)
            packed = jnp.where(rid < nrows, xs_ref[pl.ds(r0, MOE_SB), :], 0)
            lo = lax.bitcast_convert_type(lax.shift_left(packed, 16), F32)
            hi = lax.bitcast_convert_type(packed & jnp.int32(-65536), F32)
            x = jnp.concatenate([lo.astype(BF16), hi.astype(BF16)], axis=1)
            g = jnp.minimum(_dot(x, wg_ref[...].astype(BF16)) + bg_ref[...], SWIGLU_LIMIT)
            u = jnp.clip(_dot(x, wu_ref[...].astype(BF16)) + bu_ref[...], -SWIGLU_LIMIT, SWIGLU_LIMIT)
            hid_ref[p, pl.ds(r0, MOE_SB), :] = ((u + 1.0) * g * jax.nn.sigmoid(SWIGLU_ALPHA * g)).astype(BF16)

        paired(gate_up)

    @pl.when(p >= MOE_NF)
    def _():
        def down(s):
            r0 = pl.multiple_of(s * MOE_SB, MOE_SB)
            y = bd_ref[...]
            for f in range(MOE_NF):
                y = y + _dot(hid_ref[f, pl.ds(r0, MOE_SB), :], wd_ref[f * MOE_TF:(f + 1) * MOE_TF, :].astype(BF16))
            bits = lax.bitcast_convert_type(y.astype(BF16).astype(F32), jnp.int32)
            o_ref[pl.ds(r0, MOE_SB), :] = lax.shift_right_logical(bits[:, :MOE_TN // 2], 16) | bits[:, MOE_TN // 2:]

        def zero_body(s, carry):
            o_ref[pl.ds(pl.multiple_of(s * MOE_SB, MOE_SB), MOE_SB), :] = jnp.zeros((MOE_SB, MOE_TN // 2), jnp.int32)
            return carry

        paired(down)
        lax.fori_loop(n_live, MOE_BM // MOE_SB, zero_body, 0)


def _experts(xs, block_expert, block_rows, n_used, w_gate_up, b_gate_up, w_down, b_down):
    nf, nn = MOE_NF, D_MODEL // MOE_TN

    def gate_map(col0, lead):
        def index_map(i, p, be, rw, nu):
            ahead = p >= nf + nn - lead
            e = jnp.where(ahead, be[jnp.minimum(i + 1, nu[0] - 1)], be[i])
            return (e, 0, col0 + jnp.where(ahead, 0, jnp.minimum(p, nf - 1)))
        return index_map

    def down_map(i, p, be, rw, nu):
        parked = p < nf
        e = jnp.where(parked, be[jnp.maximum(i - 1, 0)], be[i])
        return (e, 0, jnp.where(parked, jnp.where(i > 0, nn - 1, 0), p - nf))

    grid_spec = pltpu.PrefetchScalarGridSpec(
        num_scalar_prefetch=3,
        grid=(n_used[0], nf + nn),
        in_specs=[
            pl.BlockSpec((MOE_BM, D_MODEL // 2), lambda i, p, be, rw, nu: (i, 0), pipeline_mode=pl.Buffered(1)),
            pl.BlockSpec((None, D_MODEL, MOE_TF), gate_map(0, nn // 2)),
            pl.BlockSpec((None, D_MODEL, MOE_TF), gate_map(nf, nn // 4)),
            pl.BlockSpec((None, 1, MOE_TF), gate_map(0, nn // 2)),
            pl.BlockSpec((None, 1, MOE_TF), gate_map(nf, nn // 4)),
            pl.BlockSpec((None, D_FF, MOE_TN), down_map),
            pl.BlockSpec((None, 1, MOE_TN), down_map),
        ],
        out_specs=pl.BlockSpec((MOE_BM, MOE_TN // 2), lambda i, p, be, rw, nu: (i, jnp.maximum(p - nf, 0))),
        scratch_shapes=[pltpu.VMEM((nf, MOE_BM, MOE_TF), BF16)],
    )
    bgu = b_gate_up.reshape(N_EXPERTS, 1, 2 * D_FF)
    bd = b_down.reshape(N_EXPERTS, 1, D_MODEL)
    return pl.pallas_call(
        _expert_kernel,
        grid_spec=grid_spec,
        out_shape=jax.ShapeDtypeStruct((xs.shape[0], D_MODEL // 2), jnp.int32),
        compiler_params=_cparams(("arbitrary", "arbitrary")),
        name="moe_experts",
    )(block_expert, block_rows, n_used, xs, w_gate_up, w_gate_up, bgu, bgu, w_down, bd)


def _route(top_i, rank, counts, n_blocks):
    T = top_i.shape[0]
    bpe = (counts + MOE_BM - 1) // MOE_BM
    bend = jnp.cumsum(bpe)
    bstart = bend - bpe
    experts = jnp.arange(N_EXPERTS, dtype=jnp.int32)
    start_of = jnp.sum(jnp.where(top_i[:, :, None] == experts, bstart * MOE_BM, 0), axis=-1)
    dest = (start_of + rank).astype(jnp.int32)
    n_used = bend[-1]
    blk = jnp.arange(n_blocks, dtype=jnp.int32)
    be = jnp.minimum(jnp.searchsorted(bend, jnp.minimum(blk, n_used - 1), side="right"), N_EXPERTS - 1).astype(jnp.int32)
    rows = jnp.clip(counts[be] - (blk - bstart[be]) * MOE_BM, 0, MOE_BM)
    rows = jnp.where(blk < n_used, rows, 0).astype(jnp.int32)
    return dest.reshape(T, TOP_K).T, be, rows, n_used.reshape(1).astype(jnp.int32)


def _combine_kernel(x1_ref, ge_ref, gt_ref, g2_ref, b2_ref, o_ref):
    hw, nn = MOE_TN // 2, D_MODEL // MOE_TN

    def unpack(w):
        lo = lax.bitcast_convert_type(lax.shift_left(w, 16), F32)
        hi = lax.bitcast_convert_type(w & jnp.int32(-65536), F32)
        return jnp.concatenate([part[:, n * hw:(n + 1) * hw] for n in range(nn) for part in (lo, hi)], axis=1)

    gt = gt_ref[...]
    m = gt[:, 0:1] * unpack(ge_ref[0])
    for k in range(1, TOP_K):
        m = m + gt[:, k:k + 1] * unpack(ge_ref[k])
    o_ref[...] = _layer_norm(DEEPNORM_ALPHA * x1_ref[...] + m, g2_ref[...], b2_ref[...])


def _combine_into_kernel(x1_ref, ge_ref, gt_ref, g2_ref, b2_ref, prev_ref, o_ref):
    del prev_ref
    _combine_kernel(x1_ref, ge_ref, gt_ref, g2_ref, b2_ref, o_ref)


def _combine(x1, ge, gate, g2, b2, n_rows, tm, x_row0=0, ge_row0=0, gate_row0=0, out_rows=None, into=None):
    xb, eb, gb = x_row0 // tm, ge_row0 // tm, gate_row0 // tm
    in_specs = [pl.BlockSpec((tm, D_MODEL), lambda i: (xb + i, 0)),
                pl.BlockSpec((TOP_K, tm, D_MODEL // 2), lambda i: (0, eb + i, 0)),
                pl.BlockSpec((tm, TOP_K), lambda i: (gb + i, 0)),
                pl.BlockSpec(g2.shape, lambda i: (0, 0)), pl.BlockSpec(b2.shape, lambda i: (0, 0))]
    args = [x1, ge, gate, g2, b2]
    aliases = {}
    if into is not None:
        in_specs.append(pl.BlockSpec(memory_space=pl.ANY))
        args.append(into)
        aliases = {len(args) - 1: 0}
        out_rows = into.shape[0]
    return pl.pallas_call(
        _combine_kernel if into is None else _combine_into_kernel,
        grid=(n_rows // tm,),
        in_specs=in_specs,
        out_specs=pl.BlockSpec((tm, D_MODEL), lambda i: (xb + i, 0)),
        out_shape=jax.ShapeDtypeStruct((out_rows if out_rows is not None else n_rows, D_MODEL), F32),
        input_output_aliases=aliases,
        compiler_params=_cparams(("parallel",)),
        name="moe_combine",
    )(*args)


def _layer(xp, xs, caches, state, w_in, rel_bias, gla_w_up, gla_b, gla_norm_g, w_pa, w_pb, w_o, ln1_g, ln1_b,
           router_w, router_b, w_gate_up, b_gate_up, w_down, b_down, ln2_g, ln2_b):
    batch, seq, _ = xp.shape
    dbatch, dseq, _ = xs.shape
    tp, ts = batch * seq, dbatch * dseq
    xp2, xs2 = xp.reshape(tp, D_MODEL), xs.reshape(ts, D_MODEL)

    o_b0, o_lr, o_g = PA_W, PA_W + PB_LR, PA_W + PB_LR + GLA_RANK
    w_a = w_in[:, :PA_W].astype(BF16)
    w_b = jnp.concatenate([w_in[:, o_b0:o_g], jnp.zeros((D_MODEL, LANE - GLA_RANK), F32)], axis=1).astype(BF16)
    w_g = w_in[:, o_g:].astype(BF16)
    wup = jnp.concatenate([gla_w_up, jnp.zeros((LANE - GLA_RANK, H_B * DK_B), F32)], axis=0).astype(BF16)
    gb = gla_b.reshape(1, H_B * DK_B)
    ng = gla_norm_g.reshape(1, DV_B)
    wpa, wpb, wo = w_pa.astype(BF16), w_pb.astype(BF16), w_o.astype(BF16)
    g1, b1 = ln1_g.reshape(1, D_MODEL), ln1_b.reshape(1, D_MODEL)
    g2, b2 = ln2_g.reshape(1, D_MODEL), ln2_b.reshape(1, D_MODEL)
    rw = jnp.concatenate([router_w, jnp.zeros((D_MODEL, LANE - N_EXPERTS), F32)], axis=1)
    rwh = rw.astype(BF16)
    rwl = (rw - rwh.astype(F32)).astype(BF16)
    rbp = jnp.concatenate([router_b, jnp.full((LANE - N_EXPERTS,), NEG, F32)]).reshape(1, LANE)
    caches8 = [c.reshape(dbatch, c.shape[1], KV_ROWS, HEAD_DIM) for c in caches]

    pa_p = _project(xp2, w_a, PROJ_TM, A_QKV_WIDTH, "proj_a_prompt", head_major=True)
    pb_p = _project(xp2, w_b, PROJ_TM, PB_W, "proj_b_prompt")
    pg_p = _project(xp2, w_g, PROJ_TM, 1024, "proj_g_prompt")
    oas, lses = [], []
    for g in range(N_GROUPS):
        o, lse = _attn_heads(pa_p, _prompt_table(rel_bias, g), g, batch, seq)
        oas.append(o)
        lses.append(lse)
    ob_p, st_p = _gla_heads(pb_p, wup, gb, ng, jnp.zeros((batch, H_B, DK_B, DV_B), F32), batch, seq, GLA_CHUNK, 256)
    x1_p, x1p_p, ti_p, gt_p, rk_p, cnt_p = _merge(xp2, oas + lses, ob_p, pg_p, wpa, wpb, wo, g1, b1, rwh, rwl, rbp,
                                                  jnp.zeros((1, LANE), F32), 256)

    pa_s = _project(xs2, w_a, ts, A_QKV_WIDTH, "proj_a_sample", head_major=True)
    pb_s = _project(xs2, w_b, ts, PB_W, "proj_b_sample")
    pg_s = _project(xs2, w_g, ts, 1024, "proj_g_sample")
    tabc, tabn, combo_base = _sample_tables(rel_bias, dseq)
    new_rows = [_kv_pack(pa_s, g, dbatch, dseq, dseq, dseq) for g in range(N_GROUPS)]
    oa_s = _attn_sample(pa_s, new_rows, caches8, tabc, tabn, combo_base, dbatch, dseq)
    chunk_s = int(np.gcd(dseq, GLA_CHUNK))
    ob_s, st_s = _gla_heads(pb_s, wup, gb, ng, state, dbatch, dseq, chunk_s, dseq)
    x1_s, x1p_s, ti_s, gt_s, rk_s, cnt_all = _merge(xs2, [oa_s], ob_s, pg_s, wpa, wpb, wo, g1, b1, rwh, rwl, rbp, cnt_p, ts)

    top_i = jnp.concatenate([ti_p, ti_s], axis=0)
    gate = jnp.concatenate([gt_p, gt_s], axis=0)
    t_all = tp + ts
    n_blocks = -(-(t_all * TOP_K) // MOE_BM) + N_EXPERTS
    rank = jnp.concatenate([rk_p, rk_s], axis=0)
    counts = cnt_all[0, :N_EXPERTS].astype(jnp.int32)
    dest, be, rows, n_used = _route(top_i, rank, counts, n_blocks)
    xsorted = _sc_scatter_rows([x1p_p, x1p_s], dest, n_blocks * MOE_BM)
    keeps = [min(window, seq) for window, _ in DILATED_GROUPS]
    packed_p = {g: _kv_pack(pa_p, g, batch, seq, keeps[g], Q_BLOCK) for g in range(N_GROUPS - 1)}
    eo = _experts(xsorted, be, rows, n_used, w_gate_up, b_gate_up, w_down, b_down)
    ge = _sc_gather(eo, dest.reshape(TOP_K * t_all)).reshape(TOP_K, t_all, D_MODEL // 2)
    packed_p[N_GROUPS - 1] = _kv_pack(pa_p, N_GROUPS - 1, batch, seq, keeps[N_GROUPS - 1], Q_BLOCK)
    y_p = _combine(x1_p, ge, gate, g2, b2, tp, 256)
    y_s = _combine(x1_s, ge, gate, g2, b2, ts, 256, ge_row0=tp, gate_row0=tp)

    shifted = _sc_cache_shift(caches8, dseq)
    bufs_p, bufs_s = [], []
    for g, (window, _) in enumerate(DILATED_GROUPS):
        keep = keeps[g]
        bufs_p.append(packed_p[g].reshape(batch, keep, 2, H_A, HEAD_DIM))
        clen = caches[g].shape[1]
        assert clen == window and dseq <= clen
        buf = lax.dynamic_update_slice(shifted[g], new_rows[g], (0, clen - dseq, 0, 0))
        bufs_s.append(buf.reshape(dbatch, clen, 2, H_A, HEAD_DIM))
    return y_p.reshape(batch, seq, D_MODEL), y_s.reshape(dbatch, dseq, D_MODEL), bufs_p, st_p, bufs_s, st_s


def kernel(x_prompt, x_sample, cache_a1_kv, cache_a2_kv, cache_a3_kv, state_b_s, w_in, rel_bias, gla_w_up, gla_b,
           gla_norm_g, w_pa, w_pb, w_o, ln1_g, ln1_b, router_w, router_b, w_gate_up, b_gate_up, w_down, b_down,
           ln2_g, ln2_b):
    assert w_in.shape[0] == DEPTH
    yp, ys, bufs_p, st_p, bufs_s, st_s = _layer(
        x_prompt, x_sample, (cache_a1_kv[0], cache_a2_kv[0], cache_a3_kv[0]), state_b_s[0], w_in[0], rel_bias,
        gla_w_up[0], gla_b[0], gla_norm_g[0], w_pa[0], w_pb[0], w_o[0], ln1_g[0], ln1_b[0], router_w[0], router_b[0],
        w_gate_up[0], b_gate_up[0], w_down[0], b_down[0], ln2_g[0], ln2_b[0])
    return (yp, ys, bufs_p[0][None], bufs_p[1][None], bufs_p[2][None], st_p[None],
            bufs_s[0][None], bufs_s[1][None], bufs_s[2][None], st_s[None].astype(state_b_s.dtype))
```

```python
import functools

import numpy as np
import jax
import jax.numpy as jnp
from jax import lax
from jax.experimental import pallas as pl
from jax.experimental.pallas import tpu as pltpu
from jax.experimental.pallas import tpu_sc as plsc

F32 = jnp.float32
BF16 = jnp.bfloat16

D_MODEL = 2048
HEAD_DIM = 128
DILATED_GROUPS = ((128, 1), (512, 4), (2048, 16))
N_GROUPS = 3
H_A = 4
A_WIDTH = H_A * HEAD_DIM
A_QKV_WIDTH = N_GROUPS * A_WIDTH
Q_BLOCK = 128
N_BUCKETS = 32
REL_MAX_DIST = 2048
H_B = 4
DK_B = 64
DV_B = 128
GLA_RANK = 16
GLA_TAU = 16.0
GLA_CHUNK = 64
GLA_SUB = 16
GLA_EXP_CLAMP = 80.0
N_EXPERTS = 32
TOP_K = 4
D_FF = 2048
SWIGLU_LIMIT = 7.0
SWIGLU_ALPHA = 1.702
LN_EPS = 1e-5
RMS_EPS = 1e-6
DEPTH = 1
DEEPNORM_ALPHA = (2.0 * DEPTH) ** 0.25
ATT_SCALE = HEAD_DIM ** -0.5
NEG = float(np.finfo(np.float32).min)

VMEM_LIMIT = 56 * 1024 * 1024
LANE = 128

PA_W = 3 * A_QKV_WIDTH
PB_LR = H_B * DK_B * 2 + H_B * DV_B * 2
PB_W = PB_LR + LANE
PG_W = 2 * D_MODEL

PROJ_TM = 1024

MOE_BM = 2560
MOE_SB = 320
MOE_TF = 512
MOE_NF = D_FF // MOE_TF
MOE_TN = 512
MERGE_TC = 512
SC_CHUNK_BYTES = 128 * 1024
SC_SCATTER_BYTES = 160 * 1024


def _cparams(sem):
    return pltpu.CompilerParams(dimension_semantics=sem, vmem_limit_bytes=VMEM_LIMIT)


def _dot(a, b):
    return jnp.dot(a, b, preferred_element_type=F32)


def _dot_nt(a, b):
    return lax.dot_general(a, b, (((1,), (1,)), ((), ())), preferred_element_type=F32)


def _dot_tn(a, b):
    return lax.dot_general(a, b, (((0,), (0,)), ((), ())), preferred_element_type=F32)


def _proj_kernel(x_ref, w_ref, o_ref, xb_ref):
    @pl.when(pl.program_id(1) == 0)
    def _():
        xb_ref[...] = x_ref[...].astype(BF16)

    o_ref[...] = _dot(xb_ref[...], w_ref[...])


def _proj_heads_kernel(x_ref, w_ref, o_ref, xb_ref):
    @pl.when(pl.program_id(1) == 0)
    def _():
        xb_ref[...] = x_ref[...].astype(BF16)

    acc = _dot(xb_ref[...], w_ref[...])
    for c in range(o_ref.shape[0]):
        o_ref[c] = acc[:, c * HEAD_DIM:(c + 1) * HEAD_DIM]


def _project(x, w, tm, tn, name, head_major=False):
    T, D = x.shape
    N = w.shape[1]
    if head_major:
        nh = tn // HEAD_DIM
        out_spec = pl.BlockSpec((nh, tm, HEAD_DIM), lambda i, j: (j, i, 0))
        out_shape = jax.ShapeDtypeStruct((N // HEAD_DIM, T, HEAD_DIM), F32)
    else:
        out_spec = pl.BlockSpec((tm, tn), lambda i, j: (i, j))
        out_shape = jax.ShapeDtypeStruct((T, N), F32)
    return pl.pallas_call(
        _proj_heads_kernel if head_major else _proj_kernel,
        grid=(T // tm, N // tn),
        in_specs=[pl.BlockSpec((tm, D), lambda i, j: (i, 0)), pl.BlockSpec((D, tn), lambda i, j: (0, j))],
        out_specs=out_spec,
        out_shape=out_shape,
        scratch_shapes=[pltpu.VMEM((tm, D), BF16)],
        compiler_params=_cparams(("parallel", "arbitrary")),
        name=name,
    )(x, w)


def _t5_bucket(dist):
    max_exact = N_BUCKETS // 2
    d = np.maximum(dist, 1).astype(np.float32)
    large = max_exact + (np.log(d / max_exact) / np.log(REL_MAX_DIST / max_exact) * (N_BUCKETS - max_exact)).astype(np.int32)
    large = np.minimum(large, N_BUCKETS - 1)
    return np.where(dist < max_exact, dist, large).astype(np.int32)


def _bias_lookup(rel_bias, g, j, valid):
    _, dil = DILATED_GROUPS[g]
    bucket = _t5_bucket(dil * np.clip(j, 0, Q_BLOCK))
    onehot = bucket[None] == np.arange(N_BUCKETS).reshape((N_BUCKETS,) + (1,) * j.ndim)
    rb = rel_bias[:, g * H_A:(g + 1) * H_A].astype(F32).T.reshape((H_A, N_BUCKETS) + (1,) * j.ndim)
    vals = jnp.sum(jnp.where(onehot[None], rb, 0.0), axis=1)
    return jnp.where(valid[None], vals, NEG)


def _prompt_table(rel_bias, g):
    qi = np.arange(Q_BLOCK)[:, None]
    kj = np.arange(2 * Q_BLOCK)[None, :]
    j = Q_BLOCK + qi - kj
    return _bias_lookup(rel_bias, g, j, (j >= 0) & (j <= Q_BLOCK))


def _sample_tables(rel_bias, dec_seq):
    m = np.arange(Q_BLOCK)
    tabc, combo_base = [], []
    for g, (_, dil) in enumerate(DILATED_GROUPS):
        combo_base.append(len(tabc))
        for fl in range((dec_seq - 1) // dil + 1):
            j = Q_BLOCK + fl - m
            col = _bias_lookup(rel_bias, g, j, j <= Q_BLOCK).T
            col = jnp.concatenate([col, jnp.zeros_like(col)], axis=1)
            tabc.append(jnp.broadcast_to(col[:, :, None], (Q_BLOCK, 2 * H_A, LANE)))
    tabn = []
    s = np.arange(dec_seq)[:, None]
    sp = np.arange(dec_seq)[None, :]
    for g, (_, dil) in enumerate(DILATED_GROUPS):
        diff = s - sp
        t = _bias_lookup(rel_bias, g, diff // dil, (diff >= 0) & (diff % dil == 0))
        t = jnp.transpose(t, (1, 2, 0))
        t = jnp.concatenate([t, jnp.zeros_like(t)], axis=2)
        tabn.append(jnp.broadcast_to(t[..., None], (dec_seq, dec_seq, 2 * H_A, LANE)))
    return jnp.stack(tabc), jnp.stack(tabn), tuple(combo_base)


def _attn_prompt_kernel(q_ref, kc_ref, kp_ref, vc_ref, vp_ref, tab_ref, o_ref, lse_ref):
    has_prev = pl.program_id(2) > 0
    for h in range(H_A):
        sl = slice(h * HEAD_DIM, (h + 1) * HEAD_DIM)
        q = q_ref[:, sl].astype(BF16)
        sc = _dot_nt(q, kc_ref[:, sl].astype(BF16)) * ATT_SCALE + tab_ref[h, :, Q_BLOCK:]
        sp = _dot_nt(q, kp_ref[:, sl].astype(BF16)) * ATT_SCALE + tab_ref[h, :, :Q_BLOCK]
        sp = jnp.where(has_prev, sp, NEG)
        m = jnp.maximum(jnp.max(sc, axis=-1, keepdims=True), jnp.max(sp, axis=-1, keepdims=True))
        pc = jnp.exp(sc - m)
        pp = jnp.exp(sp - m)
        l = jnp.sum(pc, axis=-1, keepdims=True) + jnp.sum(pp, axis=-1, keepdims=True)
        inv = 1.0 / l
        o = _dot((pc * inv).astype(BF16), vc_ref[:, sl].astype(BF16)) + _dot((pp * inv).astype(BF16), vp_ref[:, sl].astype(BF16))
        o_ref[:, sl] = o
        lse_ref[:, sl] = jnp.broadcast_to(m + jnp.log(l), (Q_BLOCK, HEAD_DIM))


def _attn_prompt_strided_kernel(*refs, dil, with_prev):
    if with_prev:
        q_ref, kc_ref, kp_ref, vc_ref, vp_ref, tab_ref, o_ref, lse_ref = refs
    else:
        q_ref, kc_ref, vc_ref, tab_ref, o_ref, lse_ref = refs
    has_prev = pl.program_id(1) > 0

    def body(r, carry):
        idx = pl.ds(r, Q_BLOCK, stride=dil)
        q = q_ref[idx, :].astype(BF16)
        sc = _dot_nt(q, kc_ref[idx, :].astype(BF16)) * ATT_SCALE + tab_ref[:, Q_BLOCK:]
        m = jnp.max(sc, axis=-1, keepdims=True)
        if with_prev:
            sp = _dot_nt(q, kp_ref[idx, :].astype(BF16)) * ATT_SCALE + tab_ref[:, :Q_BLOCK]
            sp = jnp.where(has_prev, sp, NEG)
            m = jnp.maximum(m, jnp.max(sp, axis=-1, keepdims=True))
        pc = jnp.exp(sc - m)
        l = jnp.sum(pc, axis=-1, keepdims=True)
        if with_prev:
            pp = jnp.exp(sp - m)
            l = l + jnp.sum(pp, axis=-1, keepdims=True)
        inv = 1.0 / l
        o = _dot((pc * inv).astype(BF16), vc_ref[idx, :].astype(BF16))
        if with_prev:
            o = o + _dot((pp * inv).astype(BF16), vp_ref[idx, :].astype(BF16))
        o_ref[idx, :] = o
        lse_ref[idx, :] = jnp.broadcast_to(m + jnp.log(l), (Q_BLOCK, HEAD_DIM))
        return carry

    lax.fori_loop(0, dil, body, 0, unroll=min(dil, 4))


def _attn_prompt_strided(pa, table, g, batch, seq):
    _, dil = DILATED_GROUPS[g]
    rows = dil * Q_BLOCK
    nblk = seq // rows
    with_prev = nblk > 1
    hcols = A_QKV_WIDTH // HEAD_DIM

    def spec(sec, prev):
        if prev:
            return pl.BlockSpec((rows, HEAD_DIM), lambda b, i, h: (b * nblk + jnp.maximum(i - 1, 0), sec * hcols + g * H_A + h))
        return pl.BlockSpec((rows, HEAD_DIM), lambda b, i, h: (b * nblk + i, sec * hcols + g * H_A + h))

    in_specs = [spec(0, False), spec(1, False)] + ([spec(1, True)] if with_prev else []) + [spec(2, False)] + (
        [spec(2, True)] if with_prev else []) + [pl.BlockSpec((None, Q_BLOCK, 2 * Q_BLOCK), lambda b, i, h: (h, 0, 0))]
    out_spec = pl.BlockSpec((rows, HEAD_DIM), lambda b, i, h: (b * nblk + i, h))
    return pl.pallas_call(
        functools.partial(_attn_prompt_strided_kernel, dil=dil, with_prev=with_prev),
        grid=(batch, nblk, H_A),
        in_specs=in_specs,
        out_specs=[out_spec, out_spec],
        out_shape=[jax.ShapeDtypeStruct((batch * seq, A_WIDTH), F32)] * 2,
        compiler_params=_cparams(("parallel", "arbitrary", "arbitrary")),
        name=f"attn_prompt_g{g}",
    )(*([pa] * (len(in_specs) - 1)), table)


def _attn_prompt_group(pa, table, g, batch, seq):
    _, dil = DILATED_GROUPS[g]
    if dil > 1:
        return _attn_prompt_strided(pa, table, g, batch, seq)
    sub = seq // dil
    nqb = sub // Q_BLOCK
    wblk = PA_W // A_WIDTH
    pv = pa.reshape(batch, sub, dil * PA_W)

    def spec(off, prev):
        if prev:
            return pl.BlockSpec((None, Q_BLOCK, A_WIDTH), lambda b, r, i: (b, jnp.maximum(i - 1, 0), r * wblk + off + g))
        return pl.BlockSpec((None, Q_BLOCK, A_WIDTH), lambda b, r, i: (b, i, r * wblk + off + g))

    out_spec = pl.BlockSpec((None, Q_BLOCK, A_WIDTH), lambda b, r, i: (b, i, r))
    o, lse = pl.pallas_call(
        _attn_prompt_kernel,
        grid=(batch, dil, nqb),
        in_specs=[spec(0, False), spec(N_GROUPS, False), spec(N_GROUPS, True), spec(2 * N_GROUPS, False),
                  spec(2 * N_GROUPS, True), pl.BlockSpec((H_A, Q_BLOCK, 2 * Q_BLOCK), lambda b, r, i: (0, 0, 0))],
        out_specs=[out_spec, out_spec],
        out_shape=[jax.ShapeDtypeStruct((batch, sub, dil * A_WIDTH), F32)] * 2,
        compiler_params=_cparams(("parallel", "parallel", "arbitrary")),
        name=f"attn_prompt_g{g}",
    )(pv, pv, pv, pv, pv, table)
    return o.reshape(batch * seq, A_WIDTH), lse.reshape(batch * seq, A_WIDTH)


def _attn_heads_kernel(*refs, dil, nb, with_prev):
    if with_prev:
        q_ref, kc_ref, kp_ref, vc_ref, vp_ref, tab_ref, o_ref, lse_ref = refs
    else:
        q_ref, kc_ref, vc_ref, tab_ref, o_ref, lse_ref = refs
    has_prev = pl.program_id(1) > 0
    zero = jnp.zeros((Q_BLOCK, HEAD_DIM), BF16)
    tab_c = tab_ref[:, Q_BLOCK:]
    if with_prev:
        tab = jnp.concatenate([jnp.where(has_prev, tab_ref[:, :Q_BLOCK], NEG), tab_c], axis=1)
    else:
        tab = tab_c

    def heads(ref, bi, idx):
        return jnp.concatenate([ref[h, bi, idx, :].astype(BF16) for h in range(H_A)], axis=1)

    def body(u, carry):
        bi, r = u // dil, u % dil
        idx = pl.ds(r, Q_BLOCK, stride=dil)
        qs = [q_ref[h, bi, idx, :].astype(BF16) for h in range(H_A)]
        q_stack = jnp.concatenate(
            [jnp.concatenate([qs[h] if c == h else zero for c in range(H_A)], axis=1) for h in range(H_A)], axis=0)
        if with_prev:
            k_all = jnp.concatenate([heads(kp_ref, bi, idx), heads(kc_ref, bi, idx)], axis=0)
            v_all = jnp.concatenate([heads(vp_ref, bi, idx), heads(vc_ref, bi, idx)], axis=0)
        else:
            k_all, v_all = heads(kc_ref, bi, idx), heads(vc_ref, bi, idx)
        s = _dot_nt(q_stack, k_all) * ATT_SCALE + tab
        m = jnp.max(s, axis=-1, keepdims=True)
        p = jnp.exp(s - m)
        l = jnp.sum(p, axis=-1, keepdims=True)
        o = _dot((p * (1.0 / l)).astype(BF16), v_all)
        lse = m + jnp.log(l)
        for h in range(H_A):
            rows = slice(h * Q_BLOCK, (h + 1) * Q_BLOCK)
            o_ref[h, bi, idx, :] = o[rows, h * HEAD_DIM:(h + 1) * HEAD_DIM]
            lse_ref[h, bi, idx, :] = jnp.broadcast_to(lse[rows], (Q_BLOCK, HEAD_DIM))
        return carry

    lax.fori_loop(0, nb * dil, body, 0, unroll=min(nb * dil, 2))


def _attn_heads(pa_hm, table, g, batch, seq):
    _, dil = DILATED_GROUPS[g]
    rows = dil * Q_BLOCK
    nblk = seq // rows
    with_prev = nblk > 1
    nb = 2 if (dil == 1 and batch % 2 == 0) else 1
    pv = pa_hm.reshape(pa_hm.shape[0], batch, seq, HEAD_DIM)

    def spec(sec, prev):
        if prev:
            return pl.BlockSpec((H_A, nb, rows, HEAD_DIM), lambda b, i: (sec * N_GROUPS + g, b, jnp.maximum(i - 1, 0), 0))
        return pl.BlockSpec((H_A, nb, rows, HEAD_DIM), lambda b, i: (sec * N_GROUPS + g, b, i, 0))

    in_specs = [spec(0, False), spec(1, False)] + ([spec(1, True)] if with_prev else []) + [spec(2, False)] + (
        [spec(2, True)] if with_prev else []) + [pl.BlockSpec((H_A * Q_BLOCK, 2 * Q_BLOCK), lambda b, i: (0, 0))]
    out_spec = pl.BlockSpec((H_A, nb, rows, HEAD_DIM), lambda b, i: (0, b, i, 0))
    o, lse = pl.pallas_call(
        functools.partial(_attn_heads_kernel, dil=dil, nb=nb, with_prev=with_prev),
        grid=(batch // nb, nblk),
        in_specs=in_specs,
        out_specs=[out_spec, out_spec],
        out_shape=[jax.ShapeDtypeStruct((H_A, batch, seq, HEAD_DIM), F32)] * 2,
        compiler_params=_cparams(("parallel", "arbitrary")),
        name=f"attn_prompt_g{g}",
    )(*([pv] * (len(in_specs) - 1)), table.reshape(H_A * Q_BLOCK, 2 * Q_BLOCK))
    return o.reshape(H_A, batch * seq, HEAD_DIM), lse.reshape(H_A, batch * seq, HEAD_DIM)


KV_ROWS = 2 * H_A


def _kv_pack_kernel(k_ref, v_ref, o_ref):
    o_ref[...] = jnp.stack([k_ref[h] for h in range(H_A)] + [v_ref[h] for h in range(H_A)], axis=1)


def _kv_pack(pa_hm, g, batch, seq, keep, tm):
    nblk, blk0, per_b = keep // tm, (seq - keep) // tm, seq // tm
    out = pl.pallas_call(
        _kv_pack_kernel,
        grid=(batch, nblk),
        in_specs=[pl.BlockSpec((H_A, tm, HEAD_DIM), lambda b, i: (N_GROUPS + g, b * per_b + blk0 + i, 0)),
                  pl.BlockSpec((H_A, tm, HEAD_DIM), lambda b, i: (2 * N_GROUPS + g, b * per_b + blk0 + i, 0))],
        out_specs=pl.BlockSpec((tm, KV_ROWS, HEAD_DIM), lambda b, i: (b * nblk + i, 0, 0)),
        out_shape=jax.ShapeDtypeStruct((batch * keep, KV_ROWS, HEAD_DIM), F32),
        compiler_params=_cparams(("parallel", "parallel")),
        name=f"kv_pack_g{g}_{keep}",
    )(pa_hm, pa_hm)
    return out.reshape(batch, keep, KV_ROWS, HEAD_DIM)


def _attn_sample_kernel(qkv_ref, n1_ref, n2_ref, n3_ref, c1_ref, c2_ref, c3_ref, tabc_ref, tabn_ref, o_ref, *,
                        dec_seq, combo_base):
    caches = (c1_ref, c2_ref, c3_ref)
    news = (n1_ref, n2_ref, n3_ref)
    zeros = jnp.zeros((H_A, HEAD_DIM), F32)
    for s in range(dec_seq):
        outs, lses = [], []
        for g, (_, dil) in enumerate(DILATED_GROUPS):
            rho, fl = s % dil, s // dil
            qm = jnp.concatenate([qkv_ref[g * H_A + h, s:s + 1, :] for h in range(H_A)] + [zeros], axis=0)
            kc = caches[g][:, rho]
            kn = news[g][...]
            sc = jnp.sum(kc * qm[None], axis=-1, keepdims=True) * ATT_SCALE + tabc_ref[combo_base[g] + fl]
            sn = jnp.sum(kn * qm[None], axis=-1, keepdims=True) * ATT_SCALE + tabn_ref[g, s]
            m = jnp.maximum(jnp.max(sc, axis=0), jnp.max(sn, axis=0))
            pc = jnp.exp(sc - m[None])
            pn = jnp.exp(sn - m[None])
            l = jnp.sum(pc, axis=0) + jnp.sum(pn, axis=0)
            acc = jnp.sum(pltpu.roll(pc, H_A, 1) * kc, axis=0) + jnp.sum(pltpu.roll(pn, H_A, 1) * kn, axis=0)
            outs.append(acc / pltpu.roll(l, H_A, 0))
            lses.append(pltpu.roll(m + jnp.log(l), H_A, 0))
        mm = jnp.maximum(jnp.maximum(lses[0], lses[1]), lses[2])
        ws = [jnp.exp(x - mm) for x in lses]
        o_ref[s] = (ws[0] * outs[0] + ws[1] * outs[1] + ws[2] * outs[2]) / (ws[0] + ws[1] + ws[2])


def _attn_sample(pa_hm, new_rows, caches, tabc, tabn, combo_base, batch, dec_seq):
    views, specs = [], []
    for g, (window, dil) in enumerate(DILATED_GROUPS):
        assert caches[g].shape[1] == window and dec_seq <= Q_BLOCK
        views.append(caches[g].reshape(batch, Q_BLOCK, dil, KV_ROWS, HEAD_DIM))
        used = min(dil, dec_seq)
        specs.append(pl.BlockSpec((None, Q_BLOCK, used, KV_ROWS, HEAD_DIM), lambda b: (b, 0, 0, 0, 0)))
    new_spec = pl.BlockSpec((None, dec_seq, KV_ROWS, HEAD_DIM), lambda b: (b, 0, 0, 0))
    out = pl.pallas_call(
        functools.partial(_attn_sample_kernel, dec_seq=dec_seq, combo_base=combo_base),
        grid=(batch,),
        in_specs=[pl.BlockSpec((N_GROUPS * H_A, dec_seq, HEAD_DIM), lambda b: (0, b, 0))] + [new_spec] * N_GROUPS + specs + [
            pl.BlockSpec(tabc.shape, lambda b: (0, 0, 0, 0)), pl.BlockSpec(tabn.shape, lambda b: (0, 0, 0, 0, 0))],
        out_specs=pl.BlockSpec((None, dec_seq, KV_ROWS, HEAD_DIM), lambda b: (b, 0, 0, 0)),
        out_shape=jax.ShapeDtypeStruct((batch, dec_seq, KV_ROWS, HEAD_DIM), F32),
        compiler_params=_cparams(("parallel",)),
        name="attn_sample",
    )(pa_hm, *new_rows, *views, tabc, tabn)
    return out[:, :, H_A:, :].reshape(batch * dec_seq, A_WIDTH)


def _sc_cache_shift(caches, drop):
    info = plsc.get_sparse_core_info()
    n_workers = info.num_cores * info.num_subcores
    batch = caches[0].shape[0]
    assert batch % n_workers == 0
    mesh = plsc.VectorSubcoreMesh(core_axis_name="c", subcore_axis_name="s")

    row_bytes = KV_ROWS * HEAD_DIM * 4
    chunks = []
    for c in caches:
        keep = c.shape[1] - drop
        ch = max(d for d in range(1, SC_CHUNK_BYTES // row_bytes + 1) if keep % d == 0)
        chunks.append(ch)
    buf_rows = max(chunks)

    @functools.partial(pl.kernel, mesh=mesh, out_type=[jax.ShapeDtypeStruct(c.shape, c.dtype) for c in caches],
                       scratch_types=[pltpu.VMEM((buf_rows, KV_ROWS, HEAD_DIM), caches[0].dtype)])
    def shift(*refs):
        srcs, dsts, buf = refs[:len(caches)], refs[len(caches):2 * len(caches)], refs[-1]
        wid = lax.axis_index("s") * info.num_cores + lax.axis_index("c")
        for j in range(batch // n_workers):
            b = wid * (batch // n_workers) + j
            for src, dst, ch in zip(srcs, dsts, chunks):
                stage = buf.at[pl.ds(0, ch)]

                @pl.loop(0, (src.shape[1] - drop) // ch)
                def _(i):
                    pltpu.sync_copy(src.at[b, pl.ds(drop + i * ch, ch)], stage)
                    pltpu.sync_copy(stage, dst.at[b, pl.ds(i * ch, ch)])

    return shift(*caches)


def _split3(x):
    hi = x.astype(BF16)
    r = x - hi.astype(F32)
    mid = r.astype(BF16)
    lo = (r - mid.astype(F32)).astype(BF16)
    return hi, mid, lo


def _gla_kernel(p_ref, wup_ref, gb_ref, ng_ref, s0_ref, o_ref, st_ref, *, chunk, tb):
    @pl.when(pl.program_id(1) == 0)
    def _():
        st_ref[...] = s0_ref[...]

    sub = min(GLA_SUB, chunk)
    kq = H_B * DK_B

    def rb(x):
        xb = x.astype(BF16)
        return xb if chunk >= 16 else xb.astype(F32)

    row = lax.broadcasted_iota(jnp.int32, (chunk, chunk), 0)
    colm = lax.broadcasted_iota(jnp.int32, (chunk, chunk), 1)
    tri = rb(jnp.where(row >= colm, 1.0, 0.0))
    for c in range(tb // chunk):
        rows = slice(c * chunk, (c + 1) * chunk)
        z = _dot(rb(p_ref[rows, PB_LR:PB_W]), rb(wup_ref[...])) + gb_ref[...]
        la = -(jnp.maximum(-z, 0.0) + jnp.log1p(jnp.exp(-jnp.abs(z)))) * (1.0 / GLA_TAU)
        b = functools.reduce(lambda u, w: u + w, [_dot(tri, rb(t)) for t in _split3(la)])
        blast = b[chunk - 1:chunk, :]
        q = p_ref[rows, 0:kq] * (DK_B ** -0.5)
        k = p_ref[rows, kq:2 * kq]
        qin = rb(q * jnp.exp(b))
        kst = rb(k * jnp.exp(blast - b))
        for h in range(H_B):
            ks = slice(h * DK_B, (h + 1) * DK_B)
            vs = slice(h * DV_B, (h + 1) * DV_B)
            st = st_ref[h]
            vb = rb(p_ref[rows, 2 * kq + h * DV_B:2 * kq + (h + 1) * DV_B])
            o_inter = _dot_nt(qin[:, ks], rb(st))
            parts = []
            for blk in range(chunk // sub):
                r0 = blk * sub
                n = r0 + sub
                ref_b = b[r0 - 1:r0, ks] if blk > 0 else jnp.zeros((1, DK_B), F32)
                qi = rb(q[r0:n, ks] * jnp.exp(b[r0:n, ks] - ref_b))
                ki = rb(k[0:n, ks] * jnp.exp(jnp.minimum(ref_b - b[0:n, ks], GLA_EXP_CLAMP)))
                a = _dot_nt(qi, ki)
                ti = lax.broadcasted_iota(jnp.int32, (sub, n), 0) + r0
                si = lax.broadcasted_iota(jnp.int32, (sub, n), 1)
                a = jnp.where(si <= ti, a, 0.0)
                parts.append(_dot(rb(a), vb[0:n]))
            o = o_inter + (jnp.concatenate(parts, axis=0) if len(parts) > 1 else parts[0])
            st_ref[h] = st * jnp.exp(blast[:, ks]) + _dot_tn(vb, kst[:, ks])
            on = o * lax.rsqrt(jnp.mean(o * o, axis=-1, keepdims=True) + RMS_EPS) * ng_ref[...]
            rg = p_ref[rows, 2 * kq + H_B * DV_B + h * DV_B:2 * kq + H_B * DV_B + (h + 1) * DV_B]
            o_ref[rows, vs] = (on * (rg * jax.nn.sigmoid(rg))).astype(BF16)


def _gla(pb, wup, gb, ng, s0t, batch, seq, chunk, tb):
    p3 = pb.reshape(batch, seq, PB_W)
    o, st = pl.pallas_call(
        functools.partial(_gla_kernel, chunk=chunk, tb=tb),
        grid=(batch, seq // tb),
        in_specs=[pl.BlockSpec((None, tb, PB_W), lambda b, i: (b, i, 0)),
                  pl.BlockSpec(wup.shape, lambda b, i: (0, 0)),
                  pl.BlockSpec(gb.shape, lambda b, i: (0, 0)),
                  pl.BlockSpec(ng.shape, lambda b, i: (0, 0)),
                  pl.BlockSpec((None, H_B, DV_B, DK_B), lambda b, i: (b, 0, 0, 0))],
        out_specs=[pl.BlockSpec((None, tb, H_B * DV_B), lambda b, i: (b, i, 0)),
                   pl.BlockSpec((None, H_B, DV_B, DK_B), lambda b, i: (b, 0, 0, 0))],
        out_shape=[jax.ShapeDtypeStruct((batch, seq, H_B * DV_B), BF16),
                   jax.ShapeDtypeStruct((batch, H_B, DV_B, DK_B), F32)],
        compiler_params=_cparams(("parallel", "arbitrary")),
        name=f"gla_c{chunk}",
    )(p3, wup, gb, ng, s0t)
    return o.reshape(batch * seq, H_B * DV_B), st


def _gla_heads_kernel(p_ref, wup_ref, gb_ref, ng_ref, s0_ref, o_ref, st_ref, *, chunk, tb):
    @pl.when(pl.program_id(1) == 0)
    def _():
        st_ref[...] = s0_ref[...]

    sub = min(GLA_SUB, chunk)
    n_sub = chunk // sub
    kq, vw = H_B * DK_B, H_B * DV_B

    def rb(x):
        xb = x.astype(BF16)
        return xb if chunk >= 16 else xb.astype(F32)

    row = lax.broadcasted_iota(jnp.int32, (chunk, chunk), 0)
    colm = lax.broadcasted_iota(jnp.int32, (chunk, chunk), 1)
    tri = rb(jnp.where(row >= colm, 1.0, 0.0))
    lane_head = lax.broadcasted_iota(jnp.int32, (1, kq), 1) // DK_B
    diag_blocks = (lax.broadcasted_iota(jnp.int32, (vw, kq), 0) // DV_B
                   == lax.broadcasted_iota(jnp.int32, (vw, kq), 1) // DK_B)
    a_cols = LANE if n_sub > 1 else chunk
    t_in = lax.broadcasted_iota(jnp.int32, (H_B * chunk, a_cols), 0) % chunk
    s_in = lax.broadcasted_iota(jnp.int32, (H_B * chunk, a_cols), 1)
    for c in range(tb // chunk):
        rows = slice(c * chunk, (c + 1) * chunk)
        z = _dot(rb(p_ref[rows, PB_LR:PB_W]), rb(wup_ref[...])) + gb_ref[...]
        la = -(jnp.maximum(-z, 0.0) + jnp.log1p(jnp.exp(-jnp.abs(z)))) * (1.0 / GLA_TAU)
        b3 = _dot(tri, rb(jnp.concatenate(_split3(la), axis=1)))
        b = b3[:, :kq] + b3[:, kq:2 * kq] + b3[:, 2 * kq:]
        blast = b[chunk - 1:chunk, :]
        q = p_ref[rows, 0:kq] * (DK_B ** -0.5)
        k = p_ref[rows, kq:2 * kq]
        v = rb(p_ref[rows, 2 * kq:2 * kq + vw])
        st = st_ref[...]
        o_inter = _dot_nt(rb(q * jnp.exp(b)), rb(st))
        refs = [jnp.zeros((1, kq), F32)] + [b[i * sub - 1:i * sub, :] for i in range(1, n_sub)]
        ref_rows = jnp.concatenate([jnp.broadcast_to(r, (sub, kq)) for r in refs], axis=0)
        qs = q * jnp.exp(b - ref_rows)
        q_stack = rb(jnp.concatenate([jnp.where(lane_head == h, qs, 0.0) for h in range(H_B)], axis=0))
        k_parts = []
        for r in refs:
            k_parts.append(k * jnp.exp(jnp.minimum(r - b, GLA_EXP_CLAMP)))
            if a_cols > chunk:
                k_parts.append(jnp.zeros((a_cols - chunk, kq), F32))
        raw = _dot_nt(q_stack, rb(jnp.concatenate(k_parts, axis=0)))
        a = jnp.zeros((H_B * chunk, a_cols), F32)
        for i in range(n_sub):
            a = a + jnp.where(t_in // sub == i, raw[:, i * a_cols:(i + 1) * a_cols], 0.0)
        a = jnp.where(s_in <= t_in, a, 0.0)[:, :chunk]
        o_all = _dot(rb(a), v)
        o = o_inter + jnp.concatenate(
            [o_all[h * chunk:(h + 1) * chunk, h * DV_B:(h + 1) * DV_B] for h in range(H_B)], axis=1)
        st_ref[...] = st * jnp.exp(blast) + jnp.where(diag_blocks, _dot_tn(v, rb(k * jnp.exp(blast - b))), 0.0)
        normed = []
        for h in range(H_B):
            oh = o[:, h * DV_B:(h + 1) * DV_B]
            normed.append(oh * lax.rsqrt(jnp.mean(oh * oh, axis=-1, keepdims=True) + RMS_EPS) * ng_ref[...])
        rg = p_ref[rows, 2 * kq + vw:2 * kq + 2 * vw]
        o_ref[rows, :] = (jnp.concatenate(normed, axis=1) * (rg * jax.nn.sigmoid(rg))).astype(BF16)


def _gla_heads(pb, wup, gb, ng, s0, batch, seq, chunk, tb):
    kq, vw = H_B * DK_B, H_B * DV_B
    eye = jnp.eye(H_B, dtype=F32)
    s0t = jnp.swapaxes(s0, -1, -2)
    s0_bd = (s0t[:, :, :, None, :] * eye[None, :, None, :, None]).reshape(batch, vw, kq)
    p3 = pb.reshape(batch, seq, PB_W)
    o, st = pl.pallas_call(
        functools.partial(_gla_heads_kernel, chunk=chunk, tb=tb),
        grid=(batch, seq // tb),
        in_specs=[pl.BlockSpec((None, tb, PB_W), lambda b, i: (b, i, 0)),
                  pl.BlockSpec(wup.shape, lambda b, i: (0, 0)),
                  pl.BlockSpec(gb.shape, lambda b, i: (0, 0)),
                  pl.BlockSpec(ng.shape, lambda b, i: (0, 0)),
                  pl.BlockSpec((None, vw, kq), lambda b, i: (b, 0, 0))],
        out_specs=[pl.BlockSpec((None, tb, vw), lambda b, i: (b, i, 0)),
                   pl.BlockSpec((None, vw, kq), lambda b, i: (b, 0, 0))],
        out_shape=[jax.ShapeDtypeStruct((batch, seq, vw), BF16), jax.ShapeDtypeStruct((batch, vw, kq), F32)],
        compiler_params=_cparams(("parallel", "arbitrary")),
        name=f"gla_c{chunk}",
    )(p3, wup, gb, ng, s0_bd)
    st5 = st.reshape(batch, H_B, DV_B, H_B, DK_B)
    st_heads = jnp.stack([st5[:, h, :, h, :] for h in range(H_B)], axis=1)
    return o.reshape(batch * seq, vw), jnp.swapaxes(st_heads, -1, -2)


def _layer_norm(u, g, b):
    mu = jnp.mean(u, axis=-1, keepdims=True)
    d = u - mu
    var = jnp.mean(d * d, axis=-1, keepdims=True)
    return d * lax.rsqrt(var + LN_EPS) * g + b


def _merge_kernel(*refs, n_groups):
    x_ref = refs[0]
    oa_refs = refs[1:1 + 2 * n_groups] if n_groups > 1 else refs[1:2]
    rest = refs[1 + (2 * n_groups if n_groups > 1 else 1):]
    (ob_ref, pg_a_ref, pg_b_ref, wpa_ref, wpb_ref, wo_ref, g1_ref, b1_ref, rwh_ref, rwl_ref, rb_ref, cnt0_ref) = rest[:12]
    x1_ref, x1p_ref, ti_ref, gt_ref, rank_ref, cnt_ref, br_ref, u_ref = rest[-8:]
    tm = x_ref.shape[0]
    half = D_MODEL // 2

    def rows_part(rs):
        if n_groups > 1:
            def lanes(r):
                return jnp.concatenate([r[h, rs, :] for h in range(H_A)], axis=1)

            os_ = [lanes(r) for r in oa_refs[:n_groups]]
            ls = [lanes(r) for r in oa_refs[n_groups:]]
            mm = functools.reduce(jnp.maximum, ls)
            ws = [jnp.exp(x - mm) for x in ls]
            oa = sum(w * o for w, o in zip(ws, os_)) / sum(ws)
        else:
            oa = oa_refs[0][rs, :]
        oab, obb = oa.astype(BF16), ob_ref[rs, :]
        for c in range(D_MODEL // MERGE_TC):
            cs = slice(c * MERGE_TC, (c + 1) * MERGE_TC)
            br_ref[rs, cs] = (jax.nn.sigmoid(pg_a_ref[rs, cs]) * _dot(oab, wpa_ref[:, cs])
                              + jax.nn.sigmoid(pg_b_ref[rs, cs]) * _dot(obb, wpb_ref[:, cs])).astype(BF16)
        for c in range(D_MODEL // MERGE_TC):
            cs = slice(c * MERGE_TC, (c + 1) * MERGE_TC)
            u_ref[rs, cs] = DEEPNORM_ALPHA * x_ref[rs, cs] + _dot(br_ref[rs, :], wo_ref[:, cs])
        x1 = _layer_norm(u_ref[rs, :], g1_ref[...], b1_ref[...])
        x1_ref[rs, :] = x1
        xh = x1.astype(BF16)
        xhf = xh.astype(F32)
        bits = lax.bitcast_convert_type(xhf, jnp.int32)
        x1p_ref[rs, :] = lax.shift_right_logical(bits[:, :half], 16) | bits[:, half:]
        xl = (x1 - xhf).astype(BF16)
        logits = _dot(xh, rwh_ref[...]) + _dot(xl, rwh_ref[...]) + _dot(xh, rwl_ref[...]) + rb_ref[...]
        lane = lax.broadcasted_iota(jnp.int32, logits.shape, 1)
        vals = logits
        top_v, top_i = [], []
        for _ in range(TOP_K):
            m = jnp.max(vals, axis=-1, keepdims=True)
            ik = jnp.min(jnp.where(vals == m, lane, LANE), axis=-1, keepdims=True)
            vals = jnp.where(lane == ik, -jnp.inf, vals)
            top_v.append(m)
            top_i.append(ik)
        es = [jnp.exp(v - top_v[0]) for v in top_v]
        tot = functools.reduce(lambda a, b: a + b, es)
        ti_ref[rs, :] = jnp.concatenate(top_i, axis=1)
        gt_ref[rs, :] = jnp.concatenate([e / tot for e in es], axis=1)
        return [jnp.where(lane == ik, 1.0, 0.0) for ik in top_i]

    picks = rows_part(slice(0, tm))
    @pl.when(pl.program_id(0) == 0)
    def _():
        cnt_ref[...] = cnt0_ref[...]

    onehot = picks[0] + picks[1] + picks[2] + picks[3]
    tri = jnp.where(lax.broadcasted_iota(jnp.int32, (tm, tm), 0) > lax.broadcasted_iota(jnp.int32, (tm, tm), 1), 1.0, 0.0)
    before = _dot(tri.astype(BF16), onehot.astype(BF16)) + cnt_ref[...]
    rank_ref[...] = jnp.concatenate([jnp.sum(pk * before, axis=-1, keepdims=True) for pk in picks], axis=1).astype(jnp.int32)
    cnt_ref[...] = cnt_ref[...] + jnp.sum(onehot, axis=0, keepdims=True)


def _merge(x, oas, ob, pg, wpa, wpb, wo, g1, b1, rwh, rwl, rbp, cnt0, tm, packed_rows=None, packed_into=None):
    T = x.shape[0]
    n_groups = len(oas) // 2 if len(oas) > 1 else 1
    packed_rows = T if packed_rows is None else packed_rows
    packed_blk0 = 0
    extra_specs, extra_args, aliases = [], [], {}
    if packed_into is not None:
        packed_rows = packed_into.shape[0]
        packed_blk0 = (packed_rows - T) // tm
        extra_specs, extra_args = [pl.BlockSpec(memory_space=pl.ANY)], [packed_into]

    def row(w):
        return pl.BlockSpec((tm, w), lambda i: (i, 0))

    def const(a):
        return pl.BlockSpec(a.shape, lambda i: (0,) * a.ndim, pipeline_mode=pl.Buffered(1))

    def oa_spec(a):
        return pl.BlockSpec((H_A, tm, HEAD_DIM), lambda i: (0, i, 0)) if a.ndim == 3 else row(A_WIDTH)

    in_specs = ([row(D_MODEL)] + [oa_spec(a) for a in oas] + [row(H_B * DV_B), row(D_MODEL),
                pl.BlockSpec((tm, D_MODEL), lambda i: (i, 1))]
                + [const(a) for a in (wpa, wpb, wo, g1, b1, rwh, rwl, rbp, cnt0)] + extra_specs)
    if packed_into is not None:
        aliases = {len(in_specs) - 1: 1}
    return pl.pallas_call(
        functools.partial(_merge_kernel, n_groups=n_groups),
        grid=(T // tm,),
        in_specs=in_specs,
        out_specs=[row(D_MODEL), pl.BlockSpec((tm, D_MODEL // 2), lambda i: (packed_blk0 + i, 0)), row(TOP_K), row(TOP_K),
                   row(TOP_K), pl.BlockSpec((1, LANE), lambda i: (0, 0))],
        out_shape=[jax.ShapeDtypeStruct((T, D_MODEL), F32), jax.ShapeDtypeStruct((packed_rows, D_MODEL // 2), jnp.int32),
                   jax.ShapeDtypeStruct((T, TOP_K), jnp.int32), jax.ShapeDtypeStruct((T, TOP_K), F32),
                   jax.ShapeDtypeStruct((T, TOP_K), jnp.int32), jax.ShapeDtypeStruct((1, LANE), F32)],
        scratch_shapes=[pltpu.VMEM((tm, D_MODEL), BF16), pltpu.VMEM((tm, D_MODEL), F32)],
        input_output_aliases=aliases,
        compiler_params=_cparams(("arbitrary",)),
        name=f"merge_g{n_groups}",
    )(x, *oas, ob, pg, pg, wpa, wpb, wo, g1, b1, rwh, rwl, rbp, cnt0, *extra_args)


def _sc_gather(table, idx):
    info = plsc.get_sparse_core_info()
    n_workers = info.num_cores * info.num_subcores
    n, width = idx.shape[0], table.shape[1]
    per_worker = n // n_workers
    chunk = SC_CHUNK_BYTES // (width * table.dtype.itemsize)
    assert per_worker * n_workers == n and per_worker % chunk == 0 and chunk % 8 == 0
    mesh = plsc.VectorSubcoreMesh(core_axis_name="c", subcore_axis_name="s")

    @functools.partial(
        pl.kernel, mesh=mesh,
        out_type=jax.ShapeDtypeStruct((n, width), table.dtype),
        scratch_types=[pltpu.VMEM((chunk,), jnp.int32), pltpu.VMEM((chunk, width), table.dtype),
                       pltpu.SemaphoreType.DMA],
    )
    def gather(table_hbm, idx_hbm, out_hbm, idx_v, rows_v, sem):
        wid = lax.axis_index("s") * info.num_cores + lax.axis_index("c")
        base = wid * per_worker

        @pl.loop(0, per_worker // chunk)
        def _(c):
            off = pl.multiple_of(base + c * chunk, chunk)
            pltpu.sync_copy(idx_hbm.at[pl.ds(off, chunk)], idx_v)
            pltpu.async_copy(table_hbm.at[idx_v], rows_v, sem).wait()
            pltpu.sync_copy(rows_v, out_hbm.at[pl.ds(off, chunk)])

    return gather(table, idx)


def _sc_scatter_rows(tables, idx, n_out):
    info = plsc.get_sparse_core_info()
    n_workers = info.num_cores * info.num_subcores
    n_idx, n = idx.shape
    width, dtype = tables[0].shape[1], tables[0].dtype
    max_rows = SC_SCATTER_BYTES // (width * dtype.itemsize)
    plan, tok0 = [], 0
    for t in tables:
        per_worker = t.shape[0] // n_workers
        assert per_worker * n_workers == t.shape[0] and per_worker % 8 == 0 and tok0 % 8 == 0
        plan.append((per_worker, max(d for d in range(8, max_rows + 1, 8) if per_worker % d == 0), tok0))
        tok0 += t.shape[0]
    assert tok0 == n
    mesh = plsc.VectorSubcoreMesh(core_axis_name="c", subcore_axis_name="s")
    scratch_types = []
    for _, chunk, _ in plan:
        scratch_types += [pltpu.VMEM((chunk,), jnp.int32)] * n_idx + [pltpu.VMEM((chunk, width), dtype)]

    @functools.partial(pl.kernel, mesh=mesh, out_type=jax.ShapeDtypeStruct((n_out, width), dtype),
                       scratch_types=scratch_types)
    def scatter(*refs):
        table_refs, idx_hbm, out_hbm = refs[:len(tables)], refs[len(tables)], refs[len(tables) + 1]
        scratch = refs[len(tables) + 2:]
        wid = lax.axis_index("s") * info.num_cores + lax.axis_index("c")
        for j, (per_worker, chunk, first) in enumerate(plan):
            idx_vs = scratch[j * (n_idx + 1):j * (n_idx + 1) + n_idx]
            rows_v = scratch[j * (n_idx + 1) + n_idx]
            table_hbm = table_refs[j]

            @pl.loop(0, per_worker // chunk)
            def _(c):
                off = pl.multiple_of(wid * per_worker + c * chunk, 8)
                pltpu.sync_copy(table_hbm.at[pl.ds(off, chunk)], rows_v)
                for k in range(n_idx):
                    pltpu.sync_copy(idx_hbm.at[pl.ds(pl.multiple_of(k * n + first + off, 8), chunk)], idx_vs[k])
                for k in range(n_idx):
                    pltpu.sync_copy(rows_v, out_hbm.at[idx_vs[k]])

    return scatter(*tables, idx.reshape(n_idx * n))


def _expert_kernel(be_ref, rows_ref, nu_ref, xs_hbm, wg_ref, wu_ref, bg_ref, bu_ref, wd_ref, bd_ref, o_ref,
                   hid_ref, xs_ref, xs_sem):
    del be_ref
    i = pl.program_id(0)
    p = pl.program_id(1)
    nrows = rows_ref[i]
    n_live = (nrows + (MOE_SB - 1)) // MOE_SB
    half = D_MODEL // 2

    def xs_copy(blk):
        return pltpu.make_async_copy(xs_hbm.at[pl.ds(pl.multiple_of(blk * MOE_BM, MOE_BM), MOE_BM), :], xs_ref, xs_sem)

    @pl.when(jnp.logical_and(i == 0, p == 0))
    def _():
        xs_copy(0).start()

    @pl.when(p == 0)
    def _():
        xs_copy(i).wait()

    @pl.when(jnp.logical_and(p == MOE_NF, i + 1 < nu_ref[0]))
    def _():
        xs_copy(i + 1).start()

    def paired(one):
        def pair(j, carry):
            one(2 * j)
            one(2 * j + 1)
            return carry

        lax.fori_loop(0, n_live // 2, pair, 0)

        @pl.when(n_live % 2 == 1)
        def _():
            one(n_live - 1)

    @pl.when(p < MOE_NF)
    def _():
        def gate_up(s):
            r0 = pl.multiple_of(s * MOE_SB, MOE_SB)
            rid = r0 + lax.broadcasted_iota(jnp.int32, (MOE_SB, half), 0)
            packed = jnp.where(rid < nrows, xs_ref[pl.ds(r0, MOE_SB), :], 0)
            lo = lax.bitcast_convert_type(lax.shift_left(packed, 16), F32)
            hi = lax.bitcast_convert_type(packed & jnp.int32(-65536), F32)
            x = jnp.concatenate([lo.astype(BF16), hi.astype(BF16)], axis=1)
            g = jnp.minimum(_dot(x, wg_ref[...].astype(BF16)) + bg_ref[...], SWIGLU_LIMIT)
            u = jnp.clip(_dot(x, wu_ref[...].astype(BF16)) + bu_ref[...], -SWIGLU_LIMIT, SWIGLU_LIMIT)
            hid_ref[p, pl.ds(r0, MOE_SB), :] = ((u + 1.0) * g * jax.nn.sigmoid(SWIGLU_ALPHA * g)).astype(BF16)

        paired(gate_up)

    @pl.when(p >= MOE_NF)
    def _():
        def down(s):
            r0 = pl.multiple_of(s * MOE_SB, MOE_SB)
            y = bd_ref[...]
            for f in range(MOE_NF):
                y = y + _dot(hid_ref[f, pl.ds(r0, MOE_SB), :], wd_ref[f * MOE_TF:(f + 1) * MOE_TF, :].astype(BF16))
            bits = lax.bitcast_convert_type(y.astype(BF16).astype(F32), jnp.int32)
            o_ref[pl.ds(r0, MOE_SB), :] = lax.shift_right_logical(bits[:, :MOE_TN // 2], 16) | bits[:, MOE_TN // 2:]

        def zero_body(s, carry):
            o_ref[pl.ds(pl.multiple_of(s * MOE_SB, MOE_SB), MOE_SB), :] = jnp.zeros((MOE_SB, MOE_TN // 2), jnp.int32)
            return carry

        paired(down)
        lax.fori_loop(n_live, MOE_BM // MOE_SB, zero_body, 0)


def _experts(xs, block_expert, block_rows, n_used, w_gate_up, b_gate_up, w_down, b_down):
    nf, nn = MOE_NF, D_MODEL // MOE_TN

    def gate_map(col0, lead):
        def index_map(i, p, be, rw, nu):
            ahead = p >= nf + nn - lead
            e = jnp.where(ahead, be[jnp.minimum(i + 1, nu[0] - 1)], be[i])
            return (e, 0, col0 + jnp.where(ahead, 0, jnp.minimum(p, nf - 1)))
        return index_map

    def down_map(i, p, be, rw, nu):
        parked = p < nf
        e = jnp.where(parked, be[jnp.maximum(i - 1, 0)], be[i])
        return (e, 0, jnp.where(parked, jnp.where(i > 0, nn - 1, 0), p - nf))

    grid_spec = pltpu.PrefetchScalarGridSpec(
        num_scalar_prefetch=3,
        grid=(n_used[0], nf + nn),
        in_specs=[
            pl.BlockSpec(memory_space=pl.ANY),
            pl.BlockSpec((None, D_MODEL, MOE_TF), gate_map(0, nn // 2)),
            pl.BlockSpec((None, D_MODEL, MOE_TF), gate_map(nf, nn // 4)),
            pl.BlockSpec((None, 1, MOE_TF), gate_map(0, nn // 2)),
            pl.BlockSpec((None, 1, MOE_TF), gate_map(nf, nn // 4)),
            pl.BlockSpec((None, D_FF, MOE_TN), down_map),
            pl.BlockSpec((None, 1, MOE_TN), down_map),
        ],
        out_specs=pl.BlockSpec((MOE_BM, MOE_TN // 2), lambda i, p, be, rw, nu: (i, jnp.maximum(p - nf, 0))),
        scratch_shapes=[pltpu.VMEM((nf, MOE_BM, MOE_TF), BF16), pltpu.VMEM((MOE_BM, D_MODEL // 2), jnp.int32),
                        pltpu.SemaphoreType.DMA],
    )
    bgu = b_gate_up.reshape(N_EXPERTS, 1, 2 * D_FF)
    bd = b_down.reshape(N_EXPERTS, 1, D_MODEL)
    return pl.pallas_call(
        _expert_kernel,
        grid_spec=grid_spec,
        out_shape=jax.ShapeDtypeStruct((xs.shape[0], D_MODEL // 2), jnp.int32),
        compiler_params=_cparams(("arbitrary", "arbitrary")),
        name="moe_experts",
    )(block_expert, block_rows, n_used, xs, w_gate_up, w_gate_up, bgu, bgu, w_down, bd)


def _route(top_i, rank, counts, n_blocks):
    T = top_i.shape[0]
    bpe = (counts + MOE_BM - 1) // MOE_BM
    bend = jnp.cumsum(bpe)
    bstart = bend - bpe
    experts = jnp.arange(N_EXPERTS, dtype=jnp.int32)
    start_of = jnp.sum(jnp.where(top_i[:, :, None] == experts, bstart * MOE_BM, 0), axis=-1)
    dest = (start_of + rank).astype(jnp.int32)
    n_used = bend[-1]
    blk = jnp.arange(n_blocks, dtype=jnp.int32)
    be = jnp.minimum(jnp.searchsorted(bend, jnp.minimum(blk, n_used - 1), side="right"), N_EXPERTS - 1).astype(jnp.int32)
    rows = jnp.clip(counts[be] - (blk - bstart[be]) * MOE_BM, 0, MOE_BM)
    rows = jnp.where(blk < n_used, rows, 0).astype(jnp.int32)
    return dest.reshape(T, TOP_K).T, be, rows, n_used.reshape(1).astype(jnp.int32)


def _combine_kernel(x1_ref, ge_ref, gt_ref, g2_ref, b2_ref, o_ref):
    hw, nn = MOE_TN // 2, D_MODEL // MOE_TN

    def unpack(w):
        lo = lax.bitcast_convert_type(lax.shift_left(w, 16), F32)
        hi = lax.bitcast_convert_type(w & jnp.int32(-65536), F32)
        return jnp.concatenate([part[:, n * hw:(n + 1) * hw] for n in range(nn) for part in (lo, hi)], axis=1)

    gt = gt_ref[...]
    m = gt[:, 0:1] * unpack(ge_ref[0])
    for k in range(1, TOP_K):
        m = m + gt[:, k:k + 1] * unpack(ge_ref[k])
    o_ref[...] = _layer_norm(DEEPNORM_ALPHA * x1_ref[...] + m, g2_ref[...], b2_ref[...])


def _combine_into_kernel(x1_ref, ge_ref, gt_ref, g2_ref, b2_ref, prev_ref, o_ref):
    del prev_ref
    _combine_kernel(x1_ref, ge_ref, gt_ref, g2_ref, b2_ref, o_ref)


def _combine(x1, ge, gate, g2, b2, n_rows, tm, x_row0=0, ge_row0=0, gate_row0=0, out_rows=None, into=None):
    xb, eb, gb = x_row0 // tm, ge_row0 // tm, gate_row0 // tm
    in_specs = [pl.BlockSpec((tm, D_MODEL), lambda i: (xb + i, 0)),
                pl.BlockSpec((TOP_K, tm, D_MODEL // 2), lambda i: (0, eb + i, 0)),
                pl.BlockSpec((tm, TOP_K), lambda i: (gb + i, 0)),
                pl.BlockSpec(g2.shape, lambda i: (0, 0)), pl.BlockSpec(b2.shape, lambda i: (0, 0))]
    args = [x1, ge, gate, g2, b2]
    aliases = {}
    if into is not None:
        in_specs.append(pl.BlockSpec(memory_space=pl.ANY))
        args.append(into)
        aliases = {len(args) - 1: 0}
        out_rows = into.shape[0]
    return pl.pallas_call(
        _combine_kernel if into is None else _combine_into_kernel,
        grid=(n_rows // tm,),
        in_specs=in_specs,
        out_specs=pl.BlockSpec((tm, D_MODEL), lambda i: (xb + i, 0)),
        out_shape=jax.ShapeDtypeStruct((out_rows if out_rows is not None else n_rows, D_MODEL), F32),
        input_output_aliases=aliases,
        compiler_params=_cparams(("parallel",)),
        name="moe_combine",
    )(*args)


def _layer(xp, xs, caches, state, w_in, rel_bias, gla_w_up, gla_b, gla_norm_g, w_pa, w_pb, w_o, ln1_g, ln1_b,
           router_w, router_b, w_gate_up, b_gate_up, w_down, b_down, ln2_g, ln2_b):
    batch, seq, _ = xp.shape
    dbatch, dseq, _ = xs.shape
    tp, ts = batch * seq, dbatch * dseq
    xp2, xs2 = xp.reshape(tp, D_MODEL), xs.reshape(ts, D_MODEL)

    o_b0, o_lr, o_g = PA_W, PA_W + PB_LR, PA_W + PB_LR + GLA_RANK
    w_a = w_in[:, :PA_W].astype(BF16)
    w_b = jnp.concatenate([w_in[:, o_b0:o_g], jnp.zeros((D_MODEL, LANE - GLA_RANK), F32)], axis=1).astype(BF16)
    w_g = w_in[:, o_g:].astype(BF16)
    wup = jnp.concatenate([gla_w_up, jnp.zeros((LANE - GLA_RANK, H_B * DK_B), F32)], axis=0).astype(BF16)
    gb = gla_b.reshape(1, H_B * DK_B)
    ng = gla_norm_g.reshape(1, DV_B)
    wpa, wpb, wo = w_pa.astype(BF16), w_pb.astype(BF16), w_o.astype(BF16)
    g1, b1 = ln1_g.reshape(1, D_MODEL), ln1_b.reshape(1, D_MODEL)
    g2, b2 = ln2_g.reshape(1, D_MODEL), ln2_b.reshape(1, D_MODEL)
    rw = jnp.concatenate([router_w, jnp.zeros((D_MODEL, LANE - N_EXPERTS), F32)], axis=1)
    rwh = rw.astype(BF16)
    rwl = (rw - rwh.astype(F32)).astype(BF16)
    rbp = jnp.concatenate([router_b, jnp.full((LANE - N_EXPERTS,), NEG, F32)]).reshape(1, LANE)
    caches8 = [c.reshape(dbatch, c.shape[1], KV_ROWS, HEAD_DIM) for c in caches]

    pa_p = _project(xp2, w_a, PROJ_TM, A_QKV_WIDTH, "proj_a_prompt", head_major=True)
    pb_p = _project(xp2, w_b, PROJ_TM, PB_W, "proj_b_prompt")
    pg_p = _project(xp2, w_g, PROJ_TM, 1024, "proj_g_prompt")
    oas, lses = [], []
    for g in range(N_GROUPS):
        o, lse = _attn_heads(pa_p, _prompt_table(rel_bias, g), g, batch, seq)
        oas.append(o)
        lses.append(lse)
    ob_p, st_p = _gla_heads(pb_p, wup, gb, ng, jnp.zeros((batch, H_B, DK_B, DV_B), F32), batch, seq, GLA_CHUNK, 256)
    x1_p, x1p_p, ti_p, gt_p, rk_p, cnt_p = _merge(xp2, oas + lses, ob_p, pg_p, wpa, wpb, wo, g1, b1, rwh, rwl, rbp,
                                                  jnp.zeros((1, LANE), F32), 256)

    pa_s = _project(xs2, w_a, ts, A_QKV_WIDTH, "proj_a_sample", head_major=True)
    pb_s = _project(xs2, w_b, ts, PB_W, "proj_b_sample")
    pg_s = _project(xs2, w_g, ts, 1024, "proj_g_sample")
    tabc, tabn, combo_base = _sample_tables(rel_bias, dseq)
    new_rows = [_kv_pack(pa_s, g, dbatch, dseq, dseq, dseq) for g in range(N_GROUPS)]
    oa_s = _attn_sample(pa_s, new_rows, caches8, tabc, tabn, combo_base, dbatch, dseq)
    chunk_s = int(np.gcd(dseq, GLA_CHUNK))
    ob_s, st_s = _gla_heads(pb_s, wup, gb, ng, state, dbatch, dseq, chunk_s, dseq)
    x1_s, x1p_s, ti_s, gt_s, rk_s, cnt_all = _merge(xs2, [oa_s], ob_s, pg_s, wpa, wpb, wo, g1, b1, rwh, rwl, rbp, cnt_p, ts)

    top_i = jnp.concatenate([ti_p, ti_s], axis=0)
    gate = jnp.concatenate([gt_p, gt_s], axis=0)
    t_all = tp + ts
    n_blocks = -(-(t_all * TOP_K) // MOE_BM) + N_EXPERTS
    rank = jnp.concatenate([rk_p, rk_s], axis=0)
    counts = cnt_all[0, :N_EXPERTS].astype(jnp.int32)
    dest, be, rows, n_used = _route(top_i, rank, counts, n_blocks)
    xsorted = _sc_scatter_rows([x1p_p, x1p_s], dest, n_blocks * MOE_BM)
    keeps = [min(window, seq) for window, _ in DILATED_GROUPS]
    packed_p = {g: _kv_pack(pa_p, g, batch, seq, keeps[g], Q_BLOCK) for g in range(N_GROUPS - 1)}
    eo = _experts(xsorted, be, rows, n_used, w_gate_up, b_gate_up, w_down, b_down)
    ge = _sc_gather(eo, dest.reshape(TOP_K * t_all)).reshape(TOP_K, t_all, D_MODEL // 2)
    packed_p[N_GROUPS - 1] = _kv_pack(pa_p, N_GROUPS - 1, batch, seq, keeps[N_GROUPS - 1], Q_BLOCK)
    y_p = _combine(x1_p, ge, gate, g2, b2, tp, 256)
    y_s = _combine(x1_s, ge, gate, g2, b2, ts, 256, ge_row0=tp, gate_row0=tp)

    shifted = _sc_cache_shift(caches8, dseq)
    bufs_p, bufs_s = [], []
    for g, (window, _) in enumerate(DILATED_GROUPS):
        keep = keeps[g]
        bufs_p.append(packed_p[g].reshape(batch, keep, 2, H_A, HEAD_DIM))
        clen = caches[g].shape[1]
        assert clen == window and dseq <= clen
        buf = lax.dynamic_update_slice(shifted[g], new_rows[g], (0, clen - dseq, 0, 0))
        bufs_s.append(buf.reshape(dbatch, clen, 2, H_A, HEAD_DIM))
    return y_p.reshape(batch, seq, D_MODEL), y_s.reshape(dbatch, dseq, D_MODEL), bufs_p, st_p, bufs_s, st_s


def kernel(x_prompt, x_sample, cache_a1_kv, cache_a2_kv, cache_a3_kv, state_b_s, w_in, rel_bias, gla_w_up, gla_b,
           gla_norm_g, w_pa, w_pb, w_o, ln1_g, ln1_b, router_w, router_b, w_gate_up, b_gate_up, w_down, b_down,
           ln2_g, ln2_b):
    assert w_in.shape[0] == DEPTH
    yp, ys, bufs_p, st_p, bufs_s, st_s = _layer(
        x_prompt, x_sample, (cache_a1_kv[0], cache_a2_kv[0], cache_a3_kv[0]), state_b_s[0], w_in[0], rel_bias,
        gla_w_up[0], gla_b[0], gla_norm_g[0], w_pa[0], w_pb[0], w_o[0], ln1_g[0], ln1_b[0], router_w[0], router_b[0],
        w_gate_up[0], b_gate_up[0], w_down[0], b_down[0], ln2_g[0], ln2_b[0])
    return (yp, ys, bufs_p[0][None], bufs_p[1][None], bufs_p[2][None], st_p[None],
            bufs_s[0][None], bufs_s[1][None], bufs_s[2][None], st_s[None].astype(state_b_s.dtype))
```

```python
import functools

import numpy as np
import jax
import jax.numpy as jnp
from jax import lax
from jax.experimental import pallas as pl
from jax.experimental.pallas import tpu as pltpu
from jax.experimental.pallas import tpu_sc as plsc

F32 = jnp.float32
BF16 = jnp.bfloat16

D_MODEL = 2048
HEAD_DIM = 128
DILATED_GROUPS = ((128, 1), (512, 4), (2048, 16))
N_GROUPS = 3
H_A = 4
A_WIDTH = H_A * HEAD_DIM
A_QKV_WIDTH = N_GROUPS * A_WIDTH
Q_BLOCK = 128
N_BUCKETS = 32
REL_MAX_DIST = 2048
H_B = 4
DK_B = 64
DV_B = 128
GLA_RANK = 16
GLA_TAU = 16.0
GLA_CHUNK = 64
GLA_SUB = 16
GLA_EXP_CLAMP = 80.0
N_EXPERTS = 32
TOP_K = 4
D_FF = 2048
SWIGLU_LIMIT = 7.0
SWIGLU_ALPHA = 1.702
LN_EPS = 1e-5
RMS_EPS = 1e-6
DEPTH = 1
DEEPNORM_ALPHA = (2.0 * DEPTH) ** 0.25
ATT_SCALE = HEAD_DIM ** -0.5
NEG = float(np.finfo(np.float32).min)

VMEM_LIMIT = 56 * 1024 * 1024
LANE = 128

PA_W = 3 * A_QKV_WIDTH
PB_LR = H_B * DK_B * 2 + H_B * DV_B * 2
PB_W = PB_LR + LANE
PG_W = 2 * D_MODEL

PROJ_TM = 1024

MOE_BM = 2560
MOE_SB = 320
MOE_TF = 512
MOE_NF = D_FF // MOE_TF
MOE_TN = 512
MERGE_TC = 512
SC_CHUNK_BYTES = 128 * 1024
SC_SCATTER_BYTES = 160 * 1024


def _cparams(sem):
    return pltpu.CompilerParams(dimension_semantics=sem, vmem_limit_bytes=VMEM_LIMIT)


def _dot(a, b):
    return jnp.dot(a, b, preferred_element_type=F32)


def _dot_nt(a, b):
    return lax.dot_general(a, b, (((1,), (1,)), ((), ())), preferred_element_type=F32)


def _dot_tn(a, b):
    return lax.dot_general(a, b, (((0,), (0,)), ((), ())), preferred_element_type=F32)


def _proj_kernel(x_ref, w_ref, o_ref, xb_ref):
    @pl.when(pl.program_id(1) == 0)
    def _():
        xb_ref[...] = x_ref[...].astype(BF16)

    o_ref[...] = _dot(xb_ref[...], w_ref[...])


def _proj_heads_kernel(x_ref, w_ref, o_ref, xb_ref):
    @pl.when(pl.program_id(1) == 0)
    def _():
        xb_ref[...] = x_ref[...].astype(BF16)

    acc = _dot(xb_ref[...], w_ref[...])
    for c in range(o_ref.shape[0]):
        o_ref[c] = acc[:, c * HEAD_DIM:(c + 1) * HEAD_DIM]


def _project(x, w, tm, tn, name, head_major=False):
    T, D = x.shape
    N = w.shape[1]
    if head_major:
        nh = tn // HEAD_DIM
        out_spec = pl.BlockSpec((nh, tm, HEAD_DIM), lambda i, j: (j, i, 0))
        out_shape = jax.ShapeDtypeStruct((N // HEAD_DIM, T, HEAD_DIM), F32)
    else:
        out_spec = pl.BlockSpec((tm, tn), lambda i, j: (i, j))
        out_shape = jax.ShapeDtypeStruct((T, N), F32)
    return pl.pallas_call(
        _proj_heads_kernel if head_major else _proj_kernel,
        grid=(T // tm, N // tn),
        in_specs=[pl.BlockSpec((tm, D), lambda i, j: (i, 0)), pl.BlockSpec((D, tn), lambda i, j: (0, j))],
        out_specs=out_spec,
        out_shape=out_shape,
        scratch_shapes=[pltpu.VMEM((tm, D), BF16)],
        compiler_params=_cparams(("parallel", "arbitrary")),
        name=name,
    )(x, w)


def _t5_bucket(dist):
    max_exact = N_BUCKETS // 2
    d = np.maximum(dist, 1).astype(np.float32)
    large = max_exact + (np.log(d / max_exact) / np.log(REL_MAX_DIST / max_exact) * (N_BUCKETS - max_exact)).astype(np.int32)
    large = np.minimum(large, N_BUCKETS - 1)
    return np.where(dist < max_exact, dist, large).astype(np.int32)


def _bias_lookup(rel_bias, g, j, valid):
    _, dil = DILATED_GROUPS[g]
    bucket = _t5_bucket(dil * np.clip(j, 0, Q_BLOCK))
    onehot = bucket[None] == np.arange(N_BUCKETS).reshape((N_BUCKETS,) + (1,) * j.ndim)
    rb = rel_bias[:, g * H_A:(g + 1) * H_A].astype(F32).T.reshape((H_A, N_BUCKETS) + (1,) * j.ndim)
    vals = jnp.sum(jnp.where(onehot[None], rb, 0.0), axis=1)
    return jnp.where(valid[None], vals, NEG)


def _prompt_table(rel_bias, g):
    qi = np.arange(Q_BLOCK)[:, None]
    kj = np.arange(2 * Q_BLOCK)[None, :]
    j = Q_BLOCK + qi - kj
    return _bias_lookup(rel_bias, g, j, (j >= 0) & (j <= Q_BLOCK))


def _sample_tables(rel_bias, dec_seq):
    m = np.arange(Q_BLOCK)
    tabc, combo_base = [], []
    for g, (_, dil) in enumerate(DILATED_GROUPS):
        combo_base.append(len(tabc))
        for fl in range((dec_seq - 1) // dil + 1):
            j = Q_BLOCK + fl - m
            col = _bias_lookup(rel_bias, g, j, j <= Q_BLOCK).T
            col = jnp.concatenate([col, jnp.zeros_like(col)], axis=1)
            tabc.append(jnp.broadcast_to(col[:, :, None], (Q_BLOCK, 2 * H_A, LANE)))
    tabn = []
    s = np.arange(dec_seq)[:, None]
    sp = np.arange(dec_seq)[None, :]
    for g, (_, dil) in enumerate(DILATED_GROUPS):
        diff = s - sp
        t = _bias_lookup(rel_bias, g, diff // dil, (diff >= 0) & (diff % dil == 0))
        t = jnp.transpose(t, (1, 2, 0))
        t = jnp.concatenate([t, jnp.zeros_like(t)], axis=2)
        tabn.append(jnp.broadcast_to(t[..., None], (dec_seq, dec_seq, 2 * H_A, LANE)))
    return jnp.stack(tabc), jnp.stack(tabn), tuple(combo_base)


def _attn_heads_kernel(*refs, dil, nb, with_prev):
    if with_prev:
        q_ref, kc_ref, kp_ref, vc_ref, vp_ref, tab_ref, o_ref, lse_ref = refs
    else:
        q_ref, kc_ref, vc_ref, tab_ref, o_ref, lse_ref = refs
    has_prev = pl.program_id(1) > 0
    zero = jnp.zeros((Q_BLOCK, HEAD_DIM), BF16)
    tab_c = tab_ref[:, Q_BLOCK:]
    if with_prev:
        tab = jnp.concatenate([jnp.where(has_prev, tab_ref[:, :Q_BLOCK], NEG), tab_c], axis=1)
    else:
        tab = tab_c

    def heads(ref, bi, idx):
        return jnp.concatenate([ref[h, bi, idx, :].astype(BF16) for h in range(H_A)], axis=1)

    def body(u, carry):
        bi, r = u // dil, u % dil
        idx = pl.ds(r, Q_BLOCK, stride=dil)
        qs = [q_ref[h, bi, idx, :].astype(BF16) for h in range(H_A)]
        q_stack = jnp.concatenate(
            [jnp.concatenate([qs[h] if c == h else zero for c in range(H_A)], axis=1) for h in range(H_A)], axis=0)
        if with_prev:
            k_all = jnp.concatenate([heads(kp_ref, bi, idx), heads(kc_ref, bi, idx)], axis=0)
            v_all = jnp.concatenate([heads(vp_ref, bi, idx), heads(vc_ref, bi, idx)], axis=0)
        else:
            k_all, v_all = heads(kc_ref, bi, idx), heads(vc_ref, bi, idx)
        s = _dot_nt(q_stack, k_all) * ATT_SCALE + tab
        m = jnp.max(s, axis=-1, keepdims=True)
        p = jnp.exp(s - m)
        l = jnp.sum(p, axis=-1, keepdims=True)
        o = _dot((p * (1.0 / l)).astype(BF16), v_all)
        lse = m + jnp.log(l)
        for h in range(H_A):
            rows = slice(h * Q_BLOCK, (h + 1) * Q_BLOCK)
            o_ref[h, bi, idx, :] = o[rows, h * HEAD_DIM:(h + 1) * HEAD_DIM]
            lse_ref[h, bi, idx, :] = jnp.broadcast_to(lse[rows], (Q_BLOCK, HEAD_DIM))
        return carry

    lax.fori_loop(0, nb * dil, body, 0, unroll=min(nb * dil, 2))


def _attn_heads(pa_hm, table, g, batch, seq):
    _, dil = DILATED_GROUPS[g]
    rows = dil * Q_BLOCK
    nblk = seq // rows
    with_prev = nblk > 1
    nb = 2 if (dil == 1 and batch % 2 == 0) else 1
    pv = pa_hm.reshape(pa_hm.shape[0], batch, seq, HEAD_DIM)

    def spec(sec, prev):
        if prev:
            return pl.BlockSpec((H_A, nb, rows, HEAD_DIM), lambda b, i: (sec * N_GROUPS + g, b, jnp.maximum(i - 1, 0), 0))
        return pl.BlockSpec((H_A, nb, rows, HEAD_DIM), lambda b, i: (sec * N_GROUPS + g, b, i, 0))

    in_specs = [spec(0, False), spec(1, False)] + ([spec(1, True)] if with_prev else []) + [spec(2, False)] + (
        [spec(2, True)] if with_prev else []) + [pl.BlockSpec((H_A * Q_BLOCK, 2 * Q_BLOCK), lambda b, i: (0, 0))]
    out_spec = pl.BlockSpec((H_A, nb, rows, HEAD_DIM), lambda b, i: (0, b, i, 0))
    o, lse = pl.pallas_call(
        functools.partial(_attn_heads_kernel, dil=dil, nb=nb, with_prev=with_prev),
        grid=(batch // nb, nblk),
        in_specs=in_specs,
        out_specs=[out_spec, out_spec],
        out_shape=[jax.ShapeDtypeStruct((H_A, batch, seq, HEAD_DIM), F32)] * 2,
        compiler_params=_cparams(("parallel", "arbitrary")),
        name=f"attn_prompt_g{g}",
    )(*([pv] * (len(in_specs) - 1)), table.reshape(H_A * Q_BLOCK, 2 * Q_BLOCK))
    return o.reshape(H_A, batch * seq, HEAD_DIM), lse.reshape(H_A, batch * seq, HEAD_DIM)


KV_ROWS = 2 * H_A


def _kv_pack_kernel(k_ref, v_ref, o_ref):
    o_ref[...] = jnp.stack([k_ref[h] for h in range(H_A)] + [v_ref[h] for h in range(H_A)], axis=1)


def _kv_pack(pa_hm, g, batch, seq, keep, tm):
    nblk, blk0, per_b = keep // tm, (seq - keep) // tm, seq // tm
    out = pl.pallas_call(
        _kv_pack_kernel,
        grid=(batch, nblk),
        in_specs=[pl.BlockSpec((H_A, tm, HEAD_DIM), lambda b, i: (N_GROUPS + g, b * per_b + blk0 + i, 0)),
                  pl.BlockSpec((H_A, tm, HEAD_DIM), lambda b, i: (2 * N_GROUPS + g, b * per_b + blk0 + i, 0))],
        out_specs=pl.BlockSpec((tm, KV_ROWS, HEAD_DIM), lambda b, i: (b * nblk + i, 0, 0)),
        out_shape=jax.ShapeDtypeStruct((batch * keep, KV_ROWS, HEAD_DIM), F32),
        compiler_params=_cparams(("parallel", "parallel")),
        name=f"kv_pack_g{g}_{keep}",
    )(pa_hm, pa_hm)
    return out.reshape(batch, keep, KV_ROWS, HEAD_DIM)


def _attn_sample_kernel(qkv_ref, n1_ref, n2_ref, n3_ref, c1_ref, c2_ref, c3_ref, tabc_ref, tabn_ref, o_ref, *,
                        dec_seq, combo_base):
    caches = (c1_ref, c2_ref, c3_ref)
    news = (n1_ref, n2_ref, n3_ref)
    zeros = jnp.zeros((H_A, HEAD_DIM), F32)
    for s in range(dec_seq):
        outs, lses = [], []
        for g, (_, dil) in enumerate(DILATED_GROUPS):
            rho, fl = s % dil, s // dil
            qm = jnp.concatenate([qkv_ref[g * H_A + h, s:s + 1, :] for h in range(H_A)] + [zeros], axis=0)
            kc = caches[g][:, rho]
            kn = news[g][...]
            sc = jnp.sum(kc * qm[None], axis=-1, keepdims=True) * ATT_SCALE + tabc_ref[combo_base[g] + fl]
            sn = jnp.sum(kn * qm[None], axis=-1, keepdims=True) * ATT_SCALE + tabn_ref[g, s]
            m = jnp.maximum(jnp.max(sc, axis=0), jnp.max(sn, axis=0))
            pc = jnp.exp(sc - m[None])
            pn = jnp.exp(sn - m[None])
            l = jnp.sum(pc, axis=0) + jnp.sum(pn, axis=0)
            acc = jnp.sum(pltpu.roll(pc, H_A, 1) * kc, axis=0) + jnp.sum(pltpu.roll(pn, H_A, 1) * kn, axis=0)
            outs.append(acc / pltpu.roll(l, H_A, 0))
            lses.append(pltpu.roll(m + jnp.log(l), H_A, 0))
        mm = jnp.maximum(jnp.maximum(lses[0], lses[1]), lses[2])
        ws = [jnp.exp(x - mm) for x in lses]
        o_ref[s] = (ws[0] * outs[0] + ws[1] * outs[1] + ws[2] * outs[2]) / (ws[0] + ws[1] + ws[2])


def _attn_sample(pa_hm, new_rows, caches, tabc, tabn, combo_base, batch, dec_seq):
    views, specs = [], []
    for g, (window, dil) in enumerate(DILATED_GROUPS):
        assert caches[g].shape[1] == window and dec_seq <= Q_BLOCK
        views.append(caches[g].reshape(batch, Q_BLOCK, dil, KV_ROWS, HEAD_DIM))
        used = min(dil, dec_seq)
        specs.append(pl.BlockSpec((None, Q_BLOCK, used, KV_ROWS, HEAD_DIM), lambda b: (b, 0, 0, 0, 0)))
    new_spec = pl.BlockSpec((None, dec_seq, KV_ROWS, HEAD_DIM), lambda b: (b, 0, 0, 0))
    out = pl.pallas_call(
        functools.partial(_attn_sample_kernel, dec_seq=dec_seq, combo_base=combo_base),
        grid=(batch,),
        in_specs=[pl.BlockSpec((N_GROUPS * H_A, dec_seq, HEAD_DIM), lambda b: (0, b, 0))] + [new_spec] * N_GROUPS + specs + [
            pl.BlockSpec(tabc.shape, lambda b: (0, 0, 0, 0)), pl.BlockSpec(tabn.shape, lambda b: (0, 0, 0, 0, 0))],
        out_specs=pl.BlockSpec((None, dec_seq, KV_ROWS, HEAD_DIM), lambda b: (b, 0, 0, 0)),
        out_shape=jax.ShapeDtypeStruct((batch, dec_seq, KV_ROWS, HEAD_DIM), F32),
        compiler_params=_cparams(("parallel",)),
        name="attn_sample",
    )(pa_hm, *new_rows, *views, tabc, tabn)
    return out[:, :, H_A:, :].reshape(batch * dec_seq, A_WIDTH)


def _sc_cache_shift(caches, drop):
    info = plsc.get_sparse_core_info()
    n_workers = info.num_cores * info.num_subcores
    batch = caches[0].shape[0]
    assert batch % n_workers == 0
    mesh = plsc.VectorSubcoreMesh(core_axis_name="c", subcore_axis_name="s")

    row_bytes = KV_ROWS * HEAD_DIM * 4
    chunks = []
    for c in caches:
        keep = c.shape[1] - drop
        ch = max(d for d in range(1, SC_CHUNK_BYTES // row_bytes + 1) if keep % d == 0)
        chunks.append(ch)
    buf_rows = max(chunks)

    @functools.partial(pl.kernel, mesh=mesh, out_type=[jax.ShapeDtypeStruct(c.shape, c.dtype) for c in caches],
                       scratch_types=[pltpu.VMEM((buf_rows, KV_ROWS, HEAD_DIM), caches[0].dtype)])
    def shift(*refs):
        srcs, dsts, buf = refs[:len(caches)], refs[len(caches):2 * len(caches)], refs[-1]
        wid = lax.axis_index("s") * info.num_cores + lax.axis_index("c")
        for j in range(batch // n_workers):
            b = wid * (batch // n_workers) + j
            for src, dst, ch in zip(srcs, dsts, chunks):
                stage = buf.at[pl.ds(0, ch)]

                @pl.loop(0, (src.shape[1] - drop) // ch)
                def _(i):
                    pltpu.sync_copy(src.at[b, pl.ds(drop + i * ch, ch)], stage)
                    pltpu.sync_copy(stage, dst.at[b, pl.ds(i * ch, ch)])

    return shift(*caches)


def _split3(x):
    hi = x.astype(BF16)
    r = x - hi.astype(F32)
    mid = r.astype(BF16)
    lo = (r - mid.astype(F32)).astype(BF16)
    return hi, mid, lo


def _gla_heads_kernel(p_ref, wup_ref, gb_ref, ng_ref, s0_ref, o_ref, st_ref, *, chunk, tb):
    @pl.when(pl.program_id(1) == 0)
    def _():
        st_ref[...] = s0_ref[...]

    sub = min(GLA_SUB, chunk)
    n_sub = chunk // sub
    kq, vw = H_B * DK_B, H_B * DV_B

    def rb(x):
        xb = x.astype(BF16)
        return xb if chunk >= 16 else xb.astype(F32)

    row = lax.broadcasted_iota(jnp.int32, (chunk, chunk), 0)
    colm = lax.broadcasted_iota(jnp.int32, (chunk, chunk), 1)
    tri = rb(jnp.where(row >= colm, 1.0, 0.0))
    lane_head = lax.broadcasted_iota(jnp.int32, (1, kq), 1) // DK_B
    diag_blocks = (lax.broadcasted_iota(jnp.int32, (vw, kq), 0) // DV_B
                   == lax.broadcasted_iota(jnp.int32, (vw, kq), 1) // DK_B)
    a_cols = LANE if n_sub > 1 else chunk
    t_in = lax.broadcasted_iota(jnp.int32, (H_B * chunk, a_cols), 0) % chunk
    s_in = lax.broadcasted_iota(jnp.int32, (H_B * chunk, a_cols), 1)
    for c in range(tb // chunk):
        rows = slice(c * chunk, (c + 1) * chunk)
        z = _dot(rb(p_ref[rows, PB_LR:PB_W]), rb(wup_ref[...])) + gb_ref[...]
        la = -(jnp.maximum(-z, 0.0) + jnp.log1p(jnp.exp(-jnp.abs(z)))) * (1.0 / GLA_TAU)
        b3 = _dot(tri, rb(jnp.concatenate(_split3(la), axis=1)))
        b = b3[:, :kq] + b3[:, kq:2 * kq] + b3[:, 2 * kq:]
        blast = b[chunk - 1:chunk, :]
        q = p_ref[rows, 0:kq] * (DK_B ** -0.5)
        k = p_ref[rows, kq:2 * kq]
        v = rb(p_ref[rows, 2 * kq:2 * kq + vw])
        st = st_ref[...]
        o_inter = _dot_nt(rb(q * jnp.exp(b)), rb(st))
        refs = [jnp.zeros((1, kq), F32)] + [b[i * sub - 1:i * sub, :] for i in range(1, n_sub)]
        ref_rows = jnp.concatenate([jnp.broadcast_to(r, (sub, kq)) for r in refs], axis=0)
        qs = q * jnp.exp(b - ref_rows)
        q_stack = rb(jnp.concatenate([jnp.where(lane_head == h, qs, 0.0) for h in range(H_B)], axis=0))
        k_parts = []
        for r in refs:
            k_parts.append(k * jnp.exp(jnp.minimum(r - b, GLA_EXP_CLAMP)))
            if a_cols > chunk:
                k_parts.append(jnp.zeros((a_cols - chunk, kq), F32))
        raw = _dot_nt(q_stack, rb(jnp.concatenate(k_parts, axis=0)))
        a = jnp.zeros((H_B * chunk, a_cols), F32)
        for i in range(n_sub):
            a = a + jnp.where(t_in // sub == i, raw[:, i * a_cols:(i + 1) * a_cols], 0.0)
        a = jnp.where(s_in <= t_in, a, 0.0)[:, :chunk]
        o_all = _dot(rb(a), v)
        o = o_inter + jnp.concatenate(
            [o_all[h * chunk:(h + 1) * chunk, h * DV_B:(h + 1) * DV_B] for h in range(H_B)], axis=1)
        st_ref[...] = st * jnp.exp(blast) + jnp.where(diag_blocks, _dot_tn(v, rb(k * jnp.exp(blast - b))), 0.0)
        normed = []
        for h in range(H_B):
            oh = o[:, h * DV_B:(h + 1) * DV_B]
            normed.append(oh * lax.rsqrt(jnp.mean(oh * oh, axis=-1, keepdims=True) + RMS_EPS) * ng_ref[...])
        rg = p_ref[rows, 2 * kq + vw:2 * kq + 2 * vw]
        o_ref[rows, :] = (jnp.concatenate(normed, axis=1) * (rg * jax.nn.sigmoid(rg))).astype(BF16)


def _gla_heads(pb, wup, gb, ng, s0, batch, seq, chunk, tb):
    kq, vw = H_B * DK_B, H_B * DV_B
    eye = jnp.eye(H_B, dtype=F32)
    s0t = jnp.swapaxes(s0, -1, -2)
    s0_bd = (s0t[:, :, :, None, :] * eye[None, :, None, :, None]).reshape(batch, vw, kq)
    p3 = pb.reshape(batch, seq, PB_W)
    o, st = pl.pallas_call(
        functools.partial(_gla_heads_kernel, chunk=chunk, tb=tb),
        grid=(batch, seq // tb),
        in_specs=[pl.BlockSpec((None, tb, PB_W), lambda b, i: (b, i, 0)),
                  pl.BlockSpec(wup.shape, lambda b, i: (0, 0)),
                  pl.BlockSpec(gb.shape, lambda b, i: (0, 0)),
                  pl.BlockSpec(ng.shape, lambda b, i: (0, 0)),
                  pl.BlockSpec((None, vw, kq), lambda b, i: (b, 0, 0))],
        out_specs=[pl.BlockSpec((None, tb, vw), lambda b, i: (b, i, 0)),
                   pl.BlockSpec((None, vw, kq), lambda b, i: (b, 0, 0))],
        out_shape=[jax.ShapeDtypeStruct((batch, seq, vw), BF16), jax.ShapeDtypeStruct((batch, vw, kq), F32)],
        compiler_params=_cparams(("parallel", "arbitrary")),
        name=f"gla_c{chunk}",
    )(p3, wup, gb, ng, s0_bd)
    st5 = st.reshape(batch, H_B, DV_B, H_B, DK_B)
    st_heads = jnp.stack([st5[:, h, :, h, :] for h in range(H_B)], axis=1)
    return o.reshape(batch * seq, vw), jnp.swapaxes(st_heads, -1, -2)


def _layer_norm(u, g, b):
    mu = jnp.mean(u, axis=-1, keepdims=True)
    d = u - mu
    var = jnp.mean(d * d, axis=-1, keepdims=True)
    return d * lax.rsqrt(var + LN_EPS) * g + b


def _merge_kernel(*refs, n_groups):
    x_ref = refs[0]
    oa_refs = refs[1:1 + 2 * n_groups] if n_groups > 1 else refs[1:2]
    rest = refs[1 + (2 * n_groups if n_groups > 1 else 1):]
    (ob_ref, pg_a_ref, pg_b_ref, wpa_ref, wpb_ref, wo_ref, g1_ref, b1_ref, rwh_ref, rwl_ref, rb_ref, cnt0_ref,
     x1_ref, x1p_ref, ti_ref, gt_ref, rank_ref, cnt_ref, br_ref, u_ref) = rest
    tm = x_ref.shape[0]
    half = D_MODEL // 2

    def rows_part(rs):
        if n_groups > 1:
            def lanes(r):
                return jnp.concatenate([r[h, rs, :] for h in range(H_A)], axis=1)

            os_ = [lanes(r) for r in oa_refs[:n_groups]]
            ls = [lanes(r) for r in oa_refs[n_groups:]]
            mm = functools.reduce(jnp.maximum, ls)
            ws = [jnp.exp(x - mm) for x in ls]
            oa = sum(w * o for w, o in zip(ws, os_)) / sum(ws)
        else:
            oa = oa_refs[0][rs, :]
        oab, obb = oa.astype(BF16), ob_ref[rs, :]
        for c in range(D_MODEL // MERGE_TC):
            cs = slice(c * MERGE_TC, (c + 1) * MERGE_TC)
            br_ref[rs, cs] = (jax.nn.sigmoid(pg_a_ref[rs, cs]) * _dot(oab, wpa_ref[:, cs])
                              + jax.nn.sigmoid(pg_b_ref[rs, cs]) * _dot(obb, wpb_ref[:, cs])).astype(BF16)
        for c in range(D_MODEL // MERGE_TC):
            cs = slice(c * MERGE_TC, (c + 1) * MERGE_TC)
            u_ref[rs, cs] = DEEPNORM_ALPHA * x_ref[rs, cs] + _dot(br_ref[rs, :], wo_ref[:, cs])
        x1 = _layer_norm(u_ref[rs, :], g1_ref[...], b1_ref[...])
        x1_ref[rs, :] = x1
        xh = x1.astype(BF16)
        xhf = xh.astype(F32)
        bits = lax.bitcast_convert_type(xhf, jnp.int32)
        x1p_ref[rs, :] = lax.shift_right_logical(bits[:, :half], 16) | bits[:, half:]
        xl = (x1 - xhf).astype(BF16)
        logits = _dot(xh, rwh_ref[...]) + _dot(xl, rwh_ref[...]) + _dot(xh, rwl_ref[...]) + rb_ref[...]
        lane = lax.broadcasted_iota(jnp.int32, logits.shape, 1)
        vals = logits
        top_v, top_i = [], []
        for _ in range(TOP_K):
            m = jnp.max(vals, axis=-1, keepdims=True)
            ik = jnp.min(jnp.where(vals == m, lane, LANE), axis=-1, keepdims=True)
            vals = jnp.where(lane == ik, -jnp.inf, vals)
            top_v.append(m)
            top_i.append(ik)
        es = [jnp.exp(v - top_v[0]) for v in top_v]
        tot = functools.reduce(lambda a, b: a + b, es)
        ti_ref[rs, :] = jnp.concatenate(top_i, axis=1)
        gt_ref[rs, :] = jnp.concatenate([e / tot for e in es], axis=1)
        return [jnp.where(lane == ik, 1.0, 0.0) for ik in top_i]

    picks = rows_part(slice(0, tm))
    @pl.when(pl.program_id(0) == 0)
    def _():
        cnt_ref[...] = cnt0_ref[...]

    onehot = picks[0] + picks[1] + picks[2] + picks[3]
    tri = jnp.where(lax.broadcasted_iota(jnp.int32, (tm, tm), 0) > lax.broadcasted_iota(jnp.int32, (tm, tm), 1), 1.0, 0.0)
    before = _dot(tri.astype(BF16), onehot.astype(BF16)) + cnt_ref[...]
    rank_ref[...] = jnp.concatenate([jnp.sum(pk * before, axis=-1, keepdims=True) for pk in picks], axis=1).astype(jnp.int32)
    cnt_ref[...] = cnt_ref[...] + jnp.sum(onehot, axis=0, keepdims=True)


def _merge(x, oas, ob, pg, wpa, wpb, wo, g1, b1, rwh, rwl, rbp, cnt0, tm):
    T = x.shape[0]
    n_groups = len(oas) // 2 if len(oas) > 1 else 1

    def row(w):
        return pl.BlockSpec((tm, w), lambda i: (i, 0))

    def const(a):
        return pl.BlockSpec(a.shape, lambda i: (0,) * a.ndim, pipeline_mode=pl.Buffered(1))

    def oa_spec(a):
        return pl.BlockSpec((H_A, tm, HEAD_DIM), lambda i: (0, i, 0)) if a.ndim == 3 else row(A_WIDTH)

    in_specs = ([row(D_MODEL)] + [oa_spec(a) for a in oas] + [row(H_B * DV_B), row(D_MODEL),
                pl.BlockSpec((tm, D_MODEL), lambda i: (i, 1))]
                + [const(a) for a in (wpa, wpb, wo, g1, b1, rwh, rwl, rbp, cnt0)])
    return pl.pallas_call(
        functools.partial(_merge_kernel, n_groups=n_groups),
        grid=(T // tm,),
        in_specs=in_specs,
        out_specs=[row(D_MODEL), row(D_MODEL // 2), row(TOP_K), row(TOP_K), row(TOP_K),
                   pl.BlockSpec((1, LANE), lambda i: (0, 0))],
        out_shape=[jax.ShapeDtypeStruct((T, D_MODEL), F32), jax.ShapeDtypeStruct((T, D_MODEL // 2), jnp.int32),
                   jax.ShapeDtypeStruct((T, TOP_K), jnp.int32), jax.ShapeDtypeStruct((T, TOP_K), F32),
                   jax.ShapeDtypeStruct((T, TOP_K), jnp.int32), jax.ShapeDtypeStruct((1, LANE), F32)],
        scratch_shapes=[pltpu.VMEM((tm, D_MODEL), BF16), pltpu.VMEM((tm, D_MODEL), F32)],
        compiler_params=_cparams(("arbitrary",)),
        name=f"merge_g{n_groups}",
    )(x, *oas, ob, pg, pg, wpa, wpb, wo, g1, b1, rwh, rwl, rbp, cnt0)


def _sc_gather(table, idx):
    info = plsc.get_sparse_core_info()
    n_workers = info.num_cores * info.num_subcores
    n, width = idx.shape[0], table.shape[1]
    per_worker = n // n_workers
    chunk = SC_CHUNK_BYTES // (width * table.dtype.itemsize)
    assert per_worker * n_workers == n and per_worker % chunk == 0 and chunk % 8 == 0
    mesh = plsc.VectorSubcoreMesh(core_axis_name="c", subcore_axis_name="s")

    @functools.partial(
        pl.kernel, mesh=mesh,
        out_type=jax.ShapeDtypeStruct((n, width), table.dtype),
        scratch_types=[pltpu.VMEM((chunk,), jnp.int32), pltpu.VMEM((chunk, width), table.dtype),
                       pltpu.SemaphoreType.DMA],
    )
    def gather(table_hbm, idx_hbm, out_hbm, idx_v, rows_v, sem):
        wid = lax.axis_index("s") * info.num_cores + lax.axis_index("c")
        base = wid * per_worker

        @pl.loop(0, per_worker // chunk)
        def _(c):
            off = pl.multiple_of(base + c * chunk, chunk)
            pltpu.sync_copy(idx_hbm.at[pl.ds(off, chunk)], idx_v)
            pltpu.async_copy(table_hbm.at[idx_v], rows_v, sem).wait()
            pltpu.sync_copy(rows_v, out_hbm.at[pl.ds(off, chunk)])

    return gather(table, idx)


def _sc_scatter_rows(tables, idx, n_out):
    info = plsc.get_sparse_core_info()
    n_workers = info.num_cores * info.num_subcores
    n_idx, n = idx.shape
    width, dtype = tables[0].shape[1], tables[0].dtype
    max_rows = SC_SCATTER_BYTES // (width * dtype.itemsize)
    plan, tok0 = [], 0
    for t in tables:
        per_worker = t.shape[0] // n_workers
        assert per_worker * n_workers == t.shape[0] and per_worker % 8 == 0 and tok0 % 8 == 0
        plan.append((per_worker, max(d for d in range(8, max_rows + 1, 8) if per_worker % d == 0), tok0))
        tok0 += t.shape[0]
    assert tok0 == n
    mesh = plsc.VectorSubcoreMesh(core_axis_name="c", subcore_axis_name="s")
    scratch_types = []
    for _, chunk, _ in plan:
        scratch_types += [pltpu.VMEM((chunk,), jnp.int32)] * n_idx + [pltpu.VMEM((chunk, width), dtype)]

    @functools.partial(pl.kernel, mesh=mesh, out_type=jax.ShapeDtypeStruct((n_out, width), dtype),
                       scratch_types=scratch_types)
    def scatter(*refs):
        table_refs, idx_hbm, out_hbm = refs[:len(tables)], refs[len(tables)], refs[len(tables) + 1]
        scratch = refs[len(tables) + 2:]
        wid = lax.axis_index("s") * info.num_cores + lax.axis_index("c")
        for j, (per_worker, chunk, first) in enumerate(plan):
            idx_vs = scratch[j * (n_idx + 1):j * (n_idx + 1) + n_idx]
            rows_v = scratch[j * (n_idx + 1) + n_idx]
            table_hbm = table_refs[j]

            @pl.loop(0, per_worker // chunk)
            def _(c):
                off = pl.multiple_of(wid * per_worker + c * chunk, 8)
                pltpu.sync_copy(table_hbm.at[pl.ds(off, chunk)], rows_v)
                for k in range(n_idx):
                    pltpu.sync_copy(idx_hbm.at[pl.ds(pl.multiple_of(k * n + first + off, 8), chunk)], idx_vs[k])
                for k in range(n_idx):
                    pltpu.sync_copy(rows_v, out_hbm.at[idx_vs[k]])

    return scatter(*tables, idx.reshape(n_idx * n))


def _expert_kernel(be_ref, rows_ref, nu_ref, xs_hbm, wg_ref, wu_ref, bg_ref, bu_ref, wd_ref, bd_ref, o_ref,
                   hid_ref, xs_ref, xs_sem):
    del be_ref
    i = pl.program_id(0)
    p = pl.program_id(1)
    nrows = rows_ref[i]
    n_live = (nrows + (MOE_SB - 1)) // MOE_SB
    half = D_MODEL // 2

    def xs_copy(blk):
        return pltpu.make_async_copy(xs_hbm.at[pl.ds(pl.multiple_of(blk * MOE_BM, MOE_BM), MOE_BM), :], xs_ref, xs_sem)

    @pl.when(jnp.logical_and(i == 0, p == 0))
    def _():
        xs_copy(0).start()

    @pl.when(p == 0)
    def _():
        xs_copy(i).wait()

    @pl.when(jnp.logical_and(p == MOE_NF, i + 1 < nu_ref[0]))
    def _():
        xs_copy(i + 1).start()

    def paired(one):
        def pair(j, carry):
            one(2 * j)
            one(2 * j + 1)
            return carry

        lax.fori_loop(0, n_live // 2, pair, 0)

        @pl.when(n_live % 2 == 1)
        def _():
            one(n_live - 1)

    @pl.when(p < MOE_NF)
    def _():
        def gate_up(s):
            r0 = pl.multiple_of(s * MOE_SB, MOE_SB)
            rid = r0 + lax.broadcasted_iota(jnp.int32, (MOE_SB, half), 0)
            packed = jnp.where(rid < nrows, xs_ref[pl.ds(r0, MOE_SB), :], 0)
            lo = lax.bitcast_convert_type(lax.shift_left(packed, 16), F32)
            hi = lax.bitcast_convert_type(packed & jnp.int32(-65536), F32)
            x = jnp.concatenate([lo.astype(BF16), hi.astype(BF16)], axis=1)
            g = jnp.minimum(_dot(x, wg_ref[...].astype(BF16)) + bg_ref[...], SWIGLU_LIMIT)
            u = jnp.clip(_dot(x, wu_ref[...].astype(BF16)) + bu_ref[...], -SWIGLU_LIMIT, SWIGLU_LIMIT)
            hid_ref[p, pl.ds(r0, MOE_SB), :] = ((u + 1.0) * g * jax.nn.sigmoid(SWIGLU_ALPHA * g)).astype(BF16)

        paired(gate_up)

    @pl.when(p >= MOE_NF)
    def _():
        def down(s):
            r0 = pl.multiple_of(s * MOE_SB, MOE_SB)
            y = bd_ref[...]
            for f in range(MOE_NF):
                y = y + _dot(hid_ref[f, pl.ds(r0, MOE_SB), :], wd_ref[f * MOE_TF:(f + 1) * MOE_TF, :].astype(BF16))
            bits = lax.bitcast_convert_type(y.astype(BF16).astype(F32), jnp.int32)
            o_ref[pl.ds(r0, MOE_SB), :] = lax.shift_right_logical(bits[:, :MOE_TN // 2], 16) | bits[:, MOE_TN // 2:]

        def zero_body(s, carry):
            o_ref[pl.ds(pl.multiple_of(s * MOE_SB, MOE_SB), MOE_SB), :] = jnp.zeros((MOE_SB, MOE_TN // 2), jnp.int32)
            return carry

        paired(down)
        lax.fori_loop(n_live, MOE_BM // MOE_SB, zero_body, 0)


def _experts(xs, block_expert, block_rows, n_used, w_gate_up, b_gate_up, w_down, b_down):
    nf, nn = MOE_NF, D_MODEL // MOE_TN

    def gate_map(col0, lead):
        def index_map(i, p, be, rw, nu):
            ahead = p >= nf + nn - lead
            e = jnp.where(ahead, be[jnp.minimum(i + 1, nu[0] - 1)], be[i])
            return (e, 0, col0 + jnp.where(ahead, 0, jnp.minimum(p, nf - 1)))
        return index_map

    def down_map(i, p, be, rw, nu):
        parked = p < nf
        e = jnp.where(parked, be[jnp.maximum(i - 1, 0)], be[i])
        return (e, 0, jnp.where(parked, jnp.where(i > 0, nn - 1, 0), p - nf))

    grid_spec = pltpu.PrefetchScalarGridSpec(
        num_scalar_prefetch=3,
        grid=(n_used[0], nf + nn),
        in_specs=[
            pl.BlockSpec(memory_space=pl.ANY),
            pl.BlockSpec((None, D_MODEL, MOE_TF), gate_map(0, nn // 2)),
            pl.BlockSpec((None, D_MODEL, MOE_TF), gate_map(nf, nn // 4)),
            pl.BlockSpec((None, 1, MOE_TF), gate_map(0, nn // 2)),
            pl.BlockSpec((None, 1, MOE_TF), gate_map(nf, nn // 4)),
            pl.BlockSpec((None, D_FF, MOE_TN), down_map),
            pl.BlockSpec((None, 1, MOE_TN), down_map),
        ],
        out_specs=pl.BlockSpec((MOE_BM, MOE_TN // 2), lambda i, p, be, rw, nu: (i, jnp.maximum(p - nf, 0))),
        scratch_shapes=[pltpu.VMEM((nf, MOE_BM, MOE_TF), BF16), pltpu.VMEM((MOE_BM, D_MODEL // 2), jnp.int32),
                        pltpu.SemaphoreType.DMA],
    )
    bgu = b_gate_up.reshape(N_EXPERTS, 1, 2 * D_FF)
    bd = b_down.reshape(N_EXPERTS, 1, D_MODEL)
    return pl.pallas_call(
        _expert_kernel,
        grid_spec=grid_spec,
        out_shape=jax.ShapeDtypeStruct((xs.shape[0], D_MODEL // 2), jnp.int32),
        compiler_params=_cparams(("arbitrary", "arbitrary")),
        name="moe_experts",
    )(block_expert, block_rows, n_used, xs, w_gate_up, w_gate_up, bgu, bgu, w_down, bd)


def _route(top_i, rank, counts, n_blocks):
    T = top_i.shape[0]
    bpe = (counts + MOE_BM - 1) // MOE_BM
    bend = jnp.cumsum(bpe)
    bstart = bend - bpe
    experts = jnp.arange(N_EXPERTS, dtype=jnp.int32)
    start_of = jnp.sum(jnp.where(top_i[:, :, None] == experts, bstart * MOE_BM, 0), axis=-1)
    dest = (start_of + rank).astype(jnp.int32)
    n_used = bend[-1]
    blk = jnp.arange(n_blocks, dtype=jnp.int32)
    be = jnp.minimum(jnp.searchsorted(bend, jnp.minimum(blk, n_used - 1), side="right"), N_EXPERTS - 1).astype(jnp.int32)
    rows = jnp.clip(counts[be] - (blk - bstart[be]) * MOE_BM, 0, MOE_BM)
    rows = jnp.where(blk < n_used, rows, 0).astype(jnp.int32)
    return dest.reshape(T, TOP_K).T, be, rows, n_used.reshape(1).astype(jnp.int32)


def _combine_kernel(x1_ref, ge_ref, gt_ref, g2_ref, b2_ref, o_ref):
    hw, nn = MOE_TN // 2, D_MODEL // MOE_TN

    def unpack(w):
        lo = lax.bitcast_convert_type(lax.shift_left(w, 16), F32)
        hi = lax.bitcast_convert_type(w & jnp.int32(-65536), F32)
        return jnp.concatenate([part[:, n * hw:(n + 1) * hw] for n in range(nn) for part in (lo, hi)], axis=1)

    gt = gt_ref[...]
    m = gt[:, 0:1] * unpack(ge_ref[0])
    for k in range(1, TOP_K):
        m = m + gt[:, k:k + 1] * unpack(ge_ref[k])
    o_ref[...] = _layer_norm(DEEPNORM_ALPHA * x1_ref[...] + m, g2_ref[...], b2_ref[...])


def _combine(x1, ge, gate, g2, b2, row0, tm):
    n_rows = x1.shape[0]
    b0 = row0 // tm
    return pl.pallas_call(
        _combine_kernel,
        grid=(n_rows // tm,),
        in_specs=[pl.BlockSpec((tm, D_MODEL), lambda i: (i, 0)),
                  pl.BlockSpec((TOP_K, tm, D_MODEL // 2), lambda i: (0, b0 + i, 0)),
                  pl.BlockSpec((tm, TOP_K), lambda i: (b0 + i, 0)),
                  pl.BlockSpec(g2.shape, lambda i: (0, 0)), pl.BlockSpec(b2.shape, lambda i: (0, 0))],
        out_specs=pl.BlockSpec((tm, D_MODEL), lambda i: (i, 0)),
        out_shape=jax.ShapeDtypeStruct((n_rows, D_MODEL), F32),
        compiler_params=_cparams(("parallel",)),
        name="moe_combine",
    )(x1, ge, gate, g2, b2)


def _layer(xp, xs, caches, state, w_in, rel_bias, gla_w_up, gla_b, gla_norm_g, w_pa, w_pb, w_o, ln1_g, ln1_b,
           router_w, router_b, w_gate_up, b_gate_up, w_down, b_down, ln2_g, ln2_b):
    batch, seq, _ = xp.shape
    dbatch, dseq, _ = xs.shape
    tp, ts = batch * seq, dbatch * dseq
    xp2, xs2 = xp.reshape(tp, D_MODEL), xs.reshape(ts, D_MODEL)

    o_b0, o_lr, o_g = PA_W, PA_W + PB_LR, PA_W + PB_LR + GLA_RANK
    w_a = w_in[:, :PA_W].astype(BF16)
    w_b = jnp.concatenate([w_in[:, o_b0:o_g], jnp.zeros((D_MODEL, LANE - GLA_RANK), F32)], axis=1).astype(BF16)
    w_g = w_in[:, o_g:].astype(BF16)
    wup = jnp.concatenate([gla_w_up, jnp.zeros((LANE - GLA_RANK, H_B * DK_B), F32)], axis=0).astype(BF16)
    gb = gla_b.reshape(1, H_B * DK_B)
    ng = gla_norm_g.reshape(1, DV_B)
    wpa, wpb, wo = w_pa.astype(BF16), w_pb.astype(BF16), w_o.astype(BF16)
    g1, b1 = ln1_g.reshape(1, D_MODEL), ln1_b.reshape(1, D_MODEL)
    g2, b2 = ln2_g.reshape(1, D_MODEL), ln2_b.reshape(1, D_MODEL)
    rw = jnp.concatenate([router_w, jnp.zeros((D_MODEL, LANE - N_EXPERTS), F32)], axis=1)
    rwh = rw.astype(BF16)
    rwl = (rw - rwh.astype(F32)).astype(BF16)
    rbp = jnp.concatenate([router_b, jnp.full((LANE - N_EXPERTS,), NEG, F32)]).reshape(1, LANE)
    caches8 = [c.reshape(dbatch, c.shape[1], KV_ROWS, HEAD_DIM) for c in caches]

    pa_p = _project(xp2, w_a, PROJ_TM, A_QKV_WIDTH, "proj_a_prompt", head_major=True)
    pb_p = _project(xp2, w_b, PROJ_TM, PB_W, "proj_b_prompt")
    pg_p = _project(xp2, w_g, PROJ_TM, 1024, "proj_g_prompt")
    oas, lses = [], []
    for g in range(N_GROUPS):
        o, lse = _attn_heads(pa_p, _prompt_table(rel_bias, g), g, batch, seq)
        oas.append(o)
        lses.append(lse)
    ob_p, st_p = _gla_heads(pb_p, wup, gb, ng, jnp.zeros((batch, H_B, DK_B, DV_B), F32), batch, seq, GLA_CHUNK, 256)
    x1_p, x1p_p, ti_p, gt_p, rk_p, cnt_p = _merge(xp2, oas + lses, ob_p, pg_p, wpa, wpb, wo, g1, b1, rwh, rwl, rbp,
                                                  jnp.zeros((1, LANE), F32), 256)

    pa_s = _project(xs2, w_a, ts, A_QKV_WIDTH, "proj_a_sample", head_major=True)
    pb_s = _project(xs2, w_b, ts, PB_W, "proj_b_sample")
    pg_s = _project(xs2, w_g, ts, 1024, "proj_g_sample")
    tabc, tabn, combo_base = _sample_tables(rel_bias, dseq)
    new_rows = [_kv_pack(pa_s, g, dbatch, dseq, dseq, dseq) for g in range(N_GROUPS)]
    oa_s = _attn_sample(pa_s, new_rows, caches8, tabc, tabn, combo_base, dbatch, dseq)
    chunk_s = int(np.gcd(dseq, GLA_CHUNK))
    ob_s, st_s = _gla_heads(pb_s, wup, gb, ng, state, dbatch, dseq, chunk_s, dseq)
    x1_s, x1p_s, ti_s, gt_s, rk_s, cnt_all = _merge(xs2, [oa_s], ob_s, pg_s, wpa, wpb, wo, g1, b1, rwh, rwl, rbp, cnt_p, ts)

    top_i = jnp.concatenate([ti_p, ti_s], axis=0)
    gate = jnp.concatenate([gt_p, gt_s], axis=0)
    t_all = tp + ts
    n_blocks = -(-(t_all * TOP_K) // MOE_BM) + N_EXPERTS
    rank = jnp.concatenate([rk_p, rk_s], axis=0)
    counts = cnt_all[0, :N_EXPERTS].astype(jnp.int32)
    dest, be, rows, n_used = _route(top_i, rank, counts, n_blocks)
    xsorted = _sc_scatter_rows([x1p_p, x1p_s], dest, n_blocks * MOE_BM)
    keeps = [min(window, seq) for window, _ in DILATED_GROUPS]
    packed_p = {g: _kv_pack(pa_p, g, batch, seq, keeps[g], Q_BLOCK) for g in range(N_GROUPS - 1)}
    eo = _experts(xsorted, be, rows, n_used, w_gate_up, b_gate_up, w_down, b_down)
    ge = _sc_gather(eo, dest.reshape(TOP_K * t_all)).reshape(TOP_K, t_all, D_MODEL // 2)
    packed_p[N_GROUPS - 1] = _kv_pack(pa_p, N_GROUPS - 1, batch, seq, keeps[N_GROUPS - 1], Q_BLOCK)
    y_p = _combine(x1_p, ge, gate, g2, b2, 0, 256)
    y_s = _combine(x1_s, ge, gate, g2, b2, tp, 256)

    shifted = _sc_cache_shift(caches8, dseq)
    bufs_p, bufs_s = [], []
    for g, (window, _) in enumerate(DILATED_GROUPS):
        keep = keeps[g]
        bufs_p.append(packed_p[g].reshape(batch, keep, 2, H_A, HEAD_DIM))
        clen = caches[g].shape[1]
        assert clen == window and dseq <= clen
        buf = lax.dynamic_update_slice(shifted[g], new_rows[g], (0, clen - dseq, 0, 0))
        bufs_s.append(buf.reshape(dbatch, clen, 2, H_A, HEAD_DIM))
    return y_p.reshape(batch, seq, D_MODEL), y_s.reshape(dbatch, dseq, D_MODEL), bufs_p, st_p, bufs_s, st_s


def kernel(x_prompt, x_sample, cache_a1_kv, cache_a2_kv, cache_a3_kv, state_b_s, w_in, rel_bias, gla_w_up, gla_b,
           gla_norm_g, w_pa, w_pb, w_o, ln1_g, ln1_b, router_w, router_b, w_gate_up, b_gate_up, w_down, b_down,
           ln2_g, ln2_b):
    assert w_in.shape[0] == DEPTH
    yp, ys, bufs_p, st_p, bufs_s, st_s = _layer(
        x_prompt, x_sample, (cache_a1_kv[0], cache_a2_kv[0], cache_a3_kv[0]), state_b_s[0], w_in[0], rel_bias,
        gla_w_up[0], gla_b[0], gla_norm_g[0], w_pa[0], w_pb[0], w_o[0], ln1_g[0], ln1_b[0], router_w[0], router_b[0],
        w_gate_up[0], b_gate_up[0], w_down[0], b_down[0], ln2_g[0], ln2_b[0])
    return (yp, ys, bufs_p[0][None], bufs_p[1][None], bufs_p[2][None], st_p[None],
            bufs_s[0][None], bufs_s[1][None], bufs_s[2][None], st_s[None].astype(state_b_s.dtype))
```

```python
import functools

import numpy as np
import jax
import jax.numpy as jnp
from jax import lax
from jax.experimental import pallas as pl
from jax.experimental.pallas import tpu as pltpu
from jax.experimental.pallas import tpu_sc as plsc

F32 = jnp.float32
BF16 = jnp.bfloat16

D_MODEL = 2048
HEAD_DIM = 128
DILATED_GROUPS = ((128, 1), (512, 4), (2048, 16))
N_GROUPS = 3
H_A = 4
A_WIDTH = H_A * HEAD_DIM
A_QKV_WIDTH = N_GROUPS * A_WIDTH
Q_BLOCK = 128
N_BUCKETS = 32
REL_MAX_DIST = 2048
H_B = 4
DK_B = 64
DV_B = 128
GLA_RANK = 16
GLA_TAU = 16.0
GLA_CHUNK = 64
GLA_SUB = 16
GLA_EXP_CLAMP = 80.0
N_EXPERTS = 32
TOP_K = 4
D_FF = 2048
SWIGLU_LIMIT = 7.0
SWIGLU_ALPHA = 1.702
LN_EPS = 1e-5
RMS_EPS = 1e-6
DEPTH = 1
DEEPNORM_ALPHA = (2.0 * DEPTH) ** 0.25
ATT_SCALE = HEAD_DIM ** -0.5
NEG = float(np.finfo(np.float32).min)

VMEM_LIMIT = 56 * 1024 * 1024
LANE = 128

PA_W = 3 * A_QKV_WIDTH
PB_LR = H_B * DK_B * 2 + H_B * DV_B * 2
PB_W = PB_LR + LANE
PG_W = 2 * D_MODEL

PROJ_TM = 1024

MOE_BM = 2560
MOE_SB = 320
MOE_TF = 512
MOE_NF = D_FF // MOE_TF
MOE_TN = 512
MERGE_TC = 512
SC_CHUNK_BYTES = 128 * 1024
SC_SCATTER_BYTES = 160 * 1024


def _cparams(sem):
    return pltpu.CompilerParams(dimension_semantics=sem, vmem_limit_bytes=VMEM_LIMIT)


def _dot(a, b):
    return jnp.dot(a, b, preferred_element_type=F32)


def _dot_nt(a, b):
    return lax.dot_general(a, b, (((1,), (1,)), ((), ())), preferred_element_type=F32)


def _dot_tn(a, b):
    return lax.dot_general(a, b, (((0,), (0,)), ((), ())), preferred_element_type=F32)


def _proj_kernel(x_ref, w_ref, o_ref, xb_ref):
    @pl.when(pl.program_id(1) == 0)
    def _():
        xb_ref[...] = x_ref[...].astype(BF16)

    o_ref[...] = _dot(xb_ref[...], w_ref[...])


def _proj_heads_kernel(x_ref, w_ref, o_ref, xb_ref):
    @pl.when(pl.program_id(1) == 0)
    def _():
        xb_ref[...] = x_ref[...].astype(BF16)

    acc = _dot(xb_ref[...], w_ref[...])
    for c in range(o_ref.shape[0]):
        o_ref[c] = acc[:, c * HEAD_DIM:(c + 1) * HEAD_DIM]


def _project(x, w, tm, tn, name, head_major=False):
    T, D = x.shape
    N = w.shape[1]
    if head_major:
        nh = tn // HEAD_DIM
        out_spec = pl.BlockSpec((nh, tm, HEAD_DIM), lambda i, j: (j, i, 0))
        out_shape = jax.ShapeDtypeStruct((N // HEAD_DIM, T, HEAD_DIM), F32)
    else:
        out_spec = pl.BlockSpec((tm, tn), lambda i, j: (i, j))
        out_shape = jax.ShapeDtypeStruct((T, N), F32)
    return pl.pallas_call(
        _proj_heads_kernel if head_major else _proj_kernel,
        grid=(T // tm, N // tn),
        in_specs=[pl.BlockSpec((tm, D), lambda i, j: (i, 0)), pl.BlockSpec((D, tn), lambda i, j: (0, j))],
        out_specs=out_spec,
        out_shape=out_shape,
        scratch_shapes=[pltpu.VMEM((tm, D), BF16)],
        compiler_params=_cparams(("parallel", "arbitrary")),
        name=name,
    )(x, w)


def _t5_bucket(dist):
    max_exact = N_BUCKETS // 2
    d = np.maximum(dist, 1).astype(np.float32)
    large = max_exact + (np.log(d / max_exact) / np.log(REL_MAX_DIST / max_exact) * (N_BUCKETS - max_exact)).astype(np.int32)
    large = np.minimum(large, N_BUCKETS - 1)
    return np.where(dist < max_exact, dist, large).astype(np.int32)


def _bias_lookup(rel_bias, g, j, valid):
    _, dil = DILATED_GROUPS[g]
    bucket = _t5_bucket(dil * np.clip(j, 0, Q_BLOCK))
    onehot = bucket[None] == np.arange(N_BUCKETS).reshape((N_BUCKETS,) + (1,) * j.ndim)
    rb = rel_bias[:, g * H_A:(g + 1) * H_A].astype(F32).T.reshape((H_A, N_BUCKETS) + (1,) * j.ndim)
    vals = jnp.sum(jnp.where(onehot[None], rb, 0.0), axis=1)
    return jnp.where(valid[None], vals, NEG)


def _prompt_table(rel_bias, g):
    qi = np.arange(Q_BLOCK)[:, None]
    kj = np.arange(2 * Q_BLOCK)[None, :]
    j = Q_BLOCK + qi - kj
    return _bias_lookup(rel_bias, g, j, (j >= 0) & (j <= Q_BLOCK))


def _sample_tables(rel_bias, dec_seq):
    m = np.arange(Q_BLOCK)
    tabc, combo_base = [], []
    for g, (_, dil) in enumerate(DILATED_GROUPS):
        combo_base.append(len(tabc))
        for fl in range((dec_seq - 1) // dil + 1):
            j = Q_BLOCK + fl - m
            col = _bias_lookup(rel_bias, g, j, j <= Q_BLOCK).T
            col = jnp.concatenate([col, jnp.zeros_like(col)], axis=1)
            tabc.append(jnp.broadcast_to(col[:, :, None], (Q_BLOCK, 2 * H_A, LANE)))
    tabn = []
    s = np.arange(dec_seq)[:, None]
    sp = np.arange(dec_seq)[None, :]
    for g, (_, dil) in enumerate(DILATED_GROUPS):
        diff = s - sp
        t = _bias_lookup(rel_bias, g, diff // dil, (diff >= 0) & (diff % dil == 0))
        t = jnp.transpose(t, (1, 2, 0))
        t = jnp.concatenate([t, jnp.zeros_like(t)], axis=2)
        tabn.append(jnp.broadcast_to(t[..., None], (dec_seq, dec_seq, 2 * H_A, LANE)))
    return jnp.stack(tabc), jnp.stack(tabn), tuple(combo_base)


def _attn_heads_kernel(*refs, dil, nb, with_prev):
    if with_prev:
        q_ref, kc_ref, kp_ref, vc_ref, vp_ref, tab_ref, o_ref, lse_ref = refs
    else:
        q_ref, kc_ref, vc_ref, tab_ref, o_ref, lse_ref = refs
    has_prev = pl.program_id(1) > 0
    zero = jnp.zeros((Q_BLOCK, HEAD_DIM), BF16)
    tab_c = tab_ref[:, Q_BLOCK:]
    if with_prev:
        tab = jnp.concatenate([jnp.where(has_prev, tab_ref[:, :Q_BLOCK], NEG), tab_c], axis=1)
    else:
        tab = tab_c

    def heads(ref, bi, idx):
        return jnp.concatenate([ref[h, bi, idx, :].astype(BF16) for h in range(H_A)], axis=1)

    def body(u, carry):
        bi, r = u // dil, u % dil
        idx = pl.ds(r, Q_BLOCK, stride=dil)
        qs = [q_ref[h, bi, idx, :].astype(BF16) for h in range(H_A)]
        q_stack = jnp.concatenate(
            [jnp.concatenate([qs[h] if c == h else zero for c in range(H_A)], axis=1) for h in range(H_A)], axis=0)
        if with_prev:
            k_all = jnp.concatenate([heads(kp_ref, bi, idx), heads(kc_ref, bi, idx)], axis=0)
            v_all = jnp.concatenate([heads(vp_ref, bi, idx), heads(vc_ref, bi, idx)], axis=0)
        else:
            k_all, v_all = heads(kc_ref, bi, idx), heads(vc_ref, bi, idx)
        s = _dot_nt(q_stack, k_all) * ATT_SCALE + tab
        m = jnp.max(s, axis=-1, keepdims=True)
        p = jnp.exp(s - m)
        l = jnp.sum(p, axis=-1, keepdims=True)
        o = _dot((p * (1.0 / l)).astype(BF16), v_all)
        lse = m + jnp.log(l)
        for h in range(H_A):
            rows = slice(h * Q_BLOCK, (h + 1) * Q_BLOCK)
            o_ref[h, bi, idx, :] = o[rows, h * HEAD_DIM:(h + 1) * HEAD_DIM]
            lse_ref[h, bi, idx, :] = jnp.broadcast_to(lse[rows], (Q_BLOCK, HEAD_DIM))
        return carry

    lax.fori_loop(0, nb * dil, body, 0, unroll=min(nb * dil, 4))


def _attn_heads(pa_hm, table, g, batch, seq):
    _, dil = DILATED_GROUPS[g]
    rows = dil * Q_BLOCK
    nblk = seq // rows
    with_prev = nblk > 1
    nb = 4 if (dil == 1 and batch % 4 == 0) else 1
    pv = pa_hm.reshape(pa_hm.shape[0], batch, seq, HEAD_DIM)

    def spec(sec, prev):
        if prev:
            return pl.BlockSpec((H_A, nb, rows, HEAD_DIM), lambda b, i: (sec * N_GROUPS + g, b, jnp.maximum(i - 1, 0), 0))
        return pl.BlockSpec((H_A, nb, rows, HEAD_DIM), lambda b, i: (sec * N_GROUPS + g, b, i, 0))

    in_specs = [spec(0, False), spec(1, False)] + ([spec(1, True)] if with_prev else []) + [spec(2, False)] + (
        [spec(2, True)] if with_prev else []) + [pl.BlockSpec((H_A * Q_BLOCK, 2 * Q_BLOCK), lambda b, i: (0, 0))]
    out_spec = pl.BlockSpec((H_A, nb, rows, HEAD_DIM), lambda b, i: (0, b, i, 0))
    o, lse = pl.pallas_call(
        functools.partial(_attn_heads_kernel, dil=dil, nb=nb, with_prev=with_prev),
        grid=(batch // nb, nblk),
        in_specs=in_specs,
        out_specs=[out_spec, out_spec],
        out_shape=[jax.ShapeDtypeStruct((H_A, batch, seq, HEAD_DIM), F32)] * 2,
        compiler_params=_cparams(("parallel", "arbitrary")),
        name=f"attn_prompt_g{g}",
    )(*([pv] * (len(in_specs) - 1)), table.reshape(H_A * Q_BLOCK, 2 * Q_BLOCK))
    return o.reshape(H_A, batch * seq, HEAD_DIM), lse.reshape(H_A, batch * seq, HEAD_DIM)


KV_ROWS = 2 * H_A


def _kv_pack_kernel(k_ref, v_ref, o_ref):
    o_ref[...] = jnp.stack([k_ref[h] for h in range(H_A)] + [v_ref[h] for h in range(H_A)], axis=1)


def _kv_pack(pa_hm, g, batch, seq, keep, tm):
    nblk, blk0, per_b = keep // tm, (seq - keep) // tm, seq // tm
    out = pl.pallas_call(
        _kv_pack_kernel,
        grid=(batch, nblk),
        in_specs=[pl.BlockSpec((H_A, tm, HEAD_DIM), lambda b, i: (N_GROUPS + g, b * per_b + blk0 + i, 0)),
                  pl.BlockSpec((H_A, tm, HEAD_DIM), lambda b, i: (2 * N_GROUPS + g, b * per_b + blk0 + i, 0))],
        out_specs=pl.BlockSpec((tm, KV_ROWS, HEAD_DIM), lambda b, i: (b * nblk + i, 0, 0)),
        out_shape=jax.ShapeDtypeStruct((batch * keep, KV_ROWS, HEAD_DIM), F32),
        compiler_params=_cparams(("parallel", "parallel")),
        name=f"kv_pack_g{g}_{keep}",
    )(pa_hm, pa_hm)
    return out.reshape(batch, keep, KV_ROWS, HEAD_DIM)


def _attn_sample_kernel(qkv_ref, n1_ref, n2_ref, n3_ref, c1_ref, c2_ref, c3_ref, tabc_ref, tabn_ref, o_ref, *,
                        dec_seq, combo_base):
    caches = (c1_ref, c2_ref, c3_ref)
    news = (n1_ref, n2_ref, n3_ref)
    zeros = jnp.zeros((H_A, HEAD_DIM), F32)
    for s in range(dec_seq):
        outs, lses = [], []
        for g, (_, dil) in enumerate(DILATED_GROUPS):
            rho, fl = s % dil, s // dil
            qm = jnp.concatenate([qkv_ref[g * H_A + h, s:s + 1, :] for h in range(H_A)] + [zeros], axis=0)
            qm = qm * ATT_SCALE
            kc = caches[g][:, rho]
            kn = news[g][...]
            sc = jnp.sum(kc * qm[None], axis=-1, keepdims=True) + tabc_ref[combo_base[g] + fl]
            sn = jnp.sum(kn * qm[None], axis=-1, keepdims=True) + tabn_ref[g, s]
            m = jnp.maximum(jnp.max(sc, axis=0), jnp.max(sn, axis=0))
            pc = jnp.exp(sc - m[None])
            pn = jnp.exp(sn - m[None])
            l = jnp.sum(pc, axis=0) + jnp.sum(pn, axis=0)
            acc = jnp.sum(pltpu.roll(pc, H_A, 1) * kc, axis=0) + jnp.sum(pltpu.roll(pn, H_A, 1) * kn, axis=0)
            outs.append(acc / pltpu.roll(l, H_A, 0))
            lses.append(pltpu.roll(m + jnp.log(l), H_A, 0))
        mm = jnp.maximum(jnp.maximum(lses[0], lses[1]), lses[2])
        ws = [jnp.exp(x - mm) for x in lses]
        o_ref[s] = (ws[0] * outs[0] + ws[1] * outs[1] + ws[2] * outs[2]) / (ws[0] + ws[1] + ws[2])


def _attn_sample(pa_hm, new_rows, caches, tabc, tabn, combo_base, batch, dec_seq):
    views, specs = [], []
    for g, (window, dil) in enumerate(DILATED_GROUPS):
        assert caches[g].shape[1] == window and dec_seq <= Q_BLOCK
        views.append(caches[g].reshape(batch, Q_BLOCK, dil, KV_ROWS, HEAD_DIM))
        used = min(dil, dec_seq)
        specs.append(pl.BlockSpec((None, Q_BLOCK, used, KV_ROWS, HEAD_DIM), lambda b: (b, 0, 0, 0, 0)))
    new_spec = pl.BlockSpec((None, dec_seq, KV_ROWS, HEAD_DIM), lambda b: (b, 0, 0, 0))
    out = pl.pallas_call(
        functools.partial(_attn_sample_kernel, dec_seq=dec_seq, combo_base=combo_base),
        grid=(batch,),
        in_specs=[pl.BlockSpec((N_GROUPS * H_A, dec_seq, HEAD_DIM), lambda b: (0, b, 0))] + [new_spec] * N_GROUPS + specs + [
            pl.BlockSpec(tabc.shape, lambda b: (0, 0, 0, 0)), pl.BlockSpec(tabn.shape, lambda b: (0, 0, 0, 0, 0))],
        out_specs=pl.BlockSpec((None, dec_seq, KV_ROWS, HEAD_DIM), lambda b: (b, 0, 0, 0)),
        out_shape=jax.ShapeDtypeStruct((batch, dec_seq, KV_ROWS, HEAD_DIM), F32),
        compiler_params=_cparams(("parallel",)),
        name="attn_sample",
    )(pa_hm, *new_rows, *views, tabc, tabn)
    return out[:, :, H_A:, :].reshape(batch * dec_seq, A_WIDTH)


def _sc_cache_shift(caches, drop):
    info = plsc.get_sparse_core_info()
    n_workers = info.num_cores * info.num_subcores
    batch = caches[0].shape[0]
    assert batch % n_workers == 0
    mesh = plsc.VectorSubcoreMesh(core_axis_name="c", subcore_axis_name="s")

    row_bytes = KV_ROWS * HEAD_DIM * 4
    chunks = []
    for c in caches:
        keep = c.shape[1] - drop
        ch = max(d for d in range(1, SC_CHUNK_BYTES // row_bytes + 1) if keep % d == 0)
        chunks.append(ch)
    buf_rows = max(chunks)

    @functools.partial(pl.kernel, mesh=mesh, out_type=[jax.ShapeDtypeStruct(c.shape, c.dtype) for c in caches],
                       scratch_types=[pltpu.VMEM((buf_rows, KV_ROWS, HEAD_DIM), caches[0].dtype)])
    def shift(*refs):
        srcs, dsts, buf = refs[:len(caches)], refs[len(caches):2 * len(caches)], refs[-1]
        wid = lax.axis_index("s") * info.num_cores + lax.axis_index("c")
        for j in range(batch // n_workers):
            b = wid * (batch // n_workers) + j
            for src, dst, ch in zip(srcs, dsts, chunks):
                stage = buf.at[pl.ds(0, ch)]

                @pl.loop(0, (src.shape[1] - drop) // ch)
                def _(i):
                    pltpu.sync_copy(src.at[b, pl.ds(drop + i * ch, ch)], stage)
                    pltpu.sync_copy(stage, dst.at[b, pl.ds(i * ch, ch)])

    return shift(*caches)


def _split3(x):
    hi = x.astype(BF16)
    r = x - hi.astype(F32)
    mid = r.astype(BF16)
    lo = (r - mid.astype(F32)).astype(BF16)
    return hi, mid, lo


def _gla_heads_kernel(p_ref, wup_ref, gb_ref, ng_ref, s0_ref, o_ref, st_ref, *, chunk, tb):
    @pl.when(pl.program_id(1) == 0)
    def _():
        st_ref[...] = s0_ref[...]

    sub = min(GLA_SUB, chunk)
    n_sub = chunk // sub
    kq, vw = H_B * DK_B, H_B * DV_B

    def rb(x):
        xb = x.astype(BF16)
        return xb if chunk >= 16 else xb.astype(F32)

    row = lax.broadcasted_iota(jnp.int32, (chunk, chunk), 0)
    colm = lax.broadcasted_iota(jnp.int32, (chunk, chunk), 1)
    tri = rb(jnp.where(row >= colm, 1.0, 0.0))
    lane_head = lax.broadcasted_iota(jnp.int32, (1, kq), 1) // DK_B
    diag_blocks = (lax.broadcasted_iota(jnp.int32, (vw, kq), 0) // DV_B
                   == lax.broadcasted_iota(jnp.int32, (vw, kq), 1) // DK_B)
    a_cols = LANE if n_sub > 1 else chunk
    t_in = lax.broadcasted_iota(jnp.int32, (H_B * chunk, a_cols), 0) % chunk
    s_in = lax.broadcasted_iota(jnp.int32, (H_B * chunk, a_cols), 1)
    for c in range(tb // chunk):
        rows = slice(c * chunk, (c + 1) * chunk)
        z = _dot(rb(p_ref[rows, PB_LR:PB_W]), rb(wup_ref[...])) + gb_ref[...]
        la = -(jnp.maximum(-z, 0.0) + jnp.log1p(jnp.exp(-jnp.abs(z)))) * (1.0 / GLA_TAU)
        b3 = _dot(tri, rb(jnp.concatenate(_split3(la), axis=1)))
        b = b3[:, :kq] + b3[:, kq:2 * kq] + b3[:, 2 * kq:]
        blast = b[chunk - 1:chunk, :]
        q = p_ref[rows, 0:kq] * (DK_B ** -0.5)
        k = p_ref[rows, kq:2 * kq]
        v = rb(p_ref[rows, 2 * kq:2 * kq + vw])
        st = st_ref[...]
        o_inter = _dot_nt(rb(q * jnp.exp(b)), rb(st))
        refs = [jnp.zeros((1, kq), F32)] + [b[i * sub - 1:i * sub, :] for i in range(1, n_sub)]
        ref_rows = jnp.concatenate([jnp.broadcast_to(r, (sub, kq)) for r in refs], axis=0)
        qs = q * jnp.exp(b - ref_rows)
        q_stack = rb(jnp.concatenate([jnp.where(lane_head == h, qs, 0.0) for h in range(H_B)], axis=0))
        k_parts = []
        for r in refs:
            k_parts.append(k * jnp.exp(jnp.minimum(r - b, GLA_EXP_CLAMP)))
            if a_cols > chunk:
                k_parts.append(jnp.zeros((a_cols - chunk, kq), F32))
        raw = _dot_nt(q_stack, rb(jnp.concatenate(k_parts, axis=0)))
        a = jnp.zeros((H_B * chunk, a_cols), F32)
        for i in range(n_sub):
            a = a + jnp.where(t_in // sub == i, raw[:, i * a_cols:(i + 1) * a_cols], 0.0)
        a = jnp.where(s_in <= t_in, a, 0.0)[:, :chunk]
        o_all = _dot(rb(a), v)
        o = o_inter + jnp.concatenate(
            [o_all[h * chunk:(h + 1) * chunk, h * DV_B:(h + 1) * DV_B] for h in range(H_B)], axis=1)
        st_ref[...] = st * jnp.exp(blast) + jnp.where(diag_blocks, _dot_tn(v, rb(k * jnp.exp(blast - b))), 0.0)
        normed = []
        for h in range(H_B):
            oh = o[:, h * DV_B:(h + 1) * DV_B]
            normed.append(oh * lax.rsqrt(jnp.mean(oh * oh, axis=-1, keepdims=True) + RMS_EPS) * ng_ref[...])
        rg = p_ref[rows, 2 * kq + vw:2 * kq + 2 * vw]
        o_ref[rows, :] = (jnp.concatenate(normed, axis=1) * (rg * jax.nn.sigmoid(rg))).astype(BF16)


def _gla_heads(pb, wup, gb, ng, s0, batch, seq, chunk, tb):
    kq, vw = H_B * DK_B, H_B * DV_B
    eye = jnp.eye(H_B, dtype=F32)
    s0t = jnp.swapaxes(s0, -1, -2)
    s0_bd = (s0t[:, :, :, None, :] * eye[None, :, None, :, None]).reshape(batch, vw, kq)
    p3 = pb.reshape(batch, seq, PB_W)
    o, st = pl.pallas_call(
        functools.partial(_gla_heads_kernel, chunk=chunk, tb=tb),
        grid=(batch, seq // tb),
        in_specs=[pl.BlockSpec((None, tb, PB_W), lambda b, i: (b, i, 0)),
                  pl.BlockSpec(wup.shape, lambda b, i: (0, 0)),
                  pl.BlockSpec(gb.shape, lambda b, i: (0, 0)),
                  pl.BlockSpec(ng.shape, lambda b, i: (0, 0)),
                  pl.BlockSpec((None, vw, kq), lambda b, i: (b, 0, 0))],
        out_specs=[pl.BlockSpec((None, tb, vw), lambda b, i: (b, i, 0)),
                   pl.BlockSpec((None, vw, kq), lambda b, i: (b, 0, 0))],
        out_shape=[jax.ShapeDtypeStruct((batch, seq, vw), BF16), jax.ShapeDtypeStruct((batch, vw, kq), F32)],
        compiler_params=_cparams(("parallel", "arbitrary")),
        name=f"gla_c{chunk}",
    )(p3, wup, gb, ng, s0_bd)
    st5 = st.reshape(batch, H_B, DV_B, H_B, DK_B)
    st_heads = jnp.stack([st5[:, h, :, h, :] for h in range(H_B)], axis=1)
    return o.reshape(batch * seq, vw), jnp.swapaxes(st_heads, -1, -2)


def _layer_norm(u, g, b):
    mu = jnp.mean(u, axis=-1, keepdims=True)
    d = u - mu
    var = jnp.mean(d * d, axis=-1, keepdims=True)
    return d * lax.rsqrt(var + LN_EPS) * g + b


def _merge_kernel(*refs, n_groups):
    x_ref = refs[0]
    oa_refs = refs[1:1 + 2 * n_groups] if n_groups > 1 else refs[1:2]
    rest = refs[1 + (2 * n_groups if n_groups > 1 else 1):]
    (ob_ref, pg_a_ref, pg_b_ref, wpa_ref, wpb_ref, wo_ref, g1_ref, b1_ref, rwh_ref, rwl_ref, rb_ref, cnt0_ref,
     x1_ref, x1p_ref, ti_ref, gt_ref, rank_ref, cnt_ref, br_ref, u_ref) = rest
    tm = x_ref.shape[0]
    half = D_MODEL // 2

    def rows_part(rs):
        if n_groups > 1:
            def lanes(r):
                return jnp.concatenate([r[h, rs, :] for h in range(H_A)], axis=1)

            os_ = [lanes(r) for r in oa_refs[:n_groups]]
            ls = [lanes(r) for r in oa_refs[n_groups:]]
            mm = functools.reduce(jnp.maximum, ls)
            ws = [jnp.exp(x - mm) for x in ls]
            oa = sum(w * o for w, o in zip(ws, os_)) / sum(ws)
        else:
            oa = oa_refs[0][rs, :]
        oab, obb = oa.astype(BF16), ob_ref[rs, :]
        for c in range(D_MODEL // MERGE_TC):
            cs = slice(c * MERGE_TC, (c + 1) * MERGE_TC)
            br_ref[rs, cs] = (jax.nn.sigmoid(pg_a_ref[rs, cs]) * _dot(oab, wpa_ref[:, cs])
                              + jax.nn.sigmoid(pg_b_ref[rs, cs]) * _dot(obb, wpb_ref[:, cs])).astype(BF16)
        for c in range(D_MODEL // MERGE_TC):
            cs = slice(c * MERGE_TC, (c + 1) * MERGE_TC)
            u_ref[rs, cs] = DEEPNORM_ALPHA * x_ref[rs, cs] + _dot(br_ref[rs, :], wo_ref[:, cs])
        x1 = _layer_norm(u_ref[rs, :], g1_ref[...], b1_ref[...])
        x1_ref[rs, :] = x1
        xh = x1.astype(BF16)
        xhf = xh.astype(F32)
        bits = lax.bitcast_convert_type(xhf, jnp.int32)
        x1p_ref[rs, :] = lax.shift_right_logical(bits[:, :half], 16) | bits[:, half:]
        xl = (x1 - xhf).astype(BF16)
        logits = _dot(xh, rwh_ref[...]) + _dot(xl, rwh_ref[...]) + _dot(xh, rwl_ref[...]) + rb_ref[...]
        lane = lax.broadcasted_iota(jnp.int32, logits.shape, 1)
        vals = logits
        top_v, top_i = [], []
        for _ in range(TOP_K):
            m = jnp.max(vals, axis=-1, keepdims=True)
            ik = jnp.min(jnp.where(vals == m, lane, LANE), axis=-1, keepdims=True)
            vals = jnp.where(lane == ik, -jnp.inf, vals)
            top_v.append(m)
            top_i.append(ik)
        es = [jnp.exp(v - top_v[0]) for v in top_v]
        tot = functools.reduce(lambda a, b: a + b, es)
        ti_ref[rs, :] = jnp.concatenate(top_i, axis=1)
        gt_ref[rs, :] = jnp.concatenate([e / tot for e in es], axis=1)
        return [jnp.where(lane == ik, 1.0, 0.0) for ik in top_i]

    picks = rows_part(slice(0, tm))
    @pl.when(pl.program_id(0) == 0)
    def _():
        cnt_ref[...] = cnt0_ref[...]

    onehot = picks[0] + picks[1] + picks[2] + picks[3]
    tri = jnp.where(lax.broadcasted_iota(jnp.int32, (tm, tm), 0) > lax.broadcasted_iota(jnp.int32, (tm, tm), 1), 1.0, 0.0)
    before = _dot(tri.astype(BF16), onehot.astype(BF16)) + cnt_ref[...]
    rank_ref[...] = jnp.concatenate([jnp.sum(pk * before, axis=-1, keepdims=True) for pk in picks], axis=1).astype(jnp.int32)
    cnt_ref[...] = cnt_ref[...] + jnp.sum(onehot, axis=0, keepdims=True)


def _merge(x, oas, ob, pg, wpa, wpb, wo, g1, b1, rwh, rwl, rbp, cnt0, tm):
    T = x.shape[0]
    n_groups = len(oas) // 2 if len(oas) > 1 else 1

    def row(w):
        return pl.BlockSpec((tm, w), lambda i: (i, 0))

    def const(a):
        return pl.BlockSpec(a.shape, lambda i: (0,) * a.ndim, pipeline_mode=pl.Buffered(1))

    def oa_spec(a):
        return pl.BlockSpec((H_A, tm, HEAD_DIM), lambda i: (0, i, 0)) if a.ndim == 3 else row(A_WIDTH)

    in_specs = ([row(D_MODEL)] + [oa_spec(a) for a in oas] + [row(H_B * DV_B), row(D_MODEL),
                pl.BlockSpec((tm, D_MODEL), lambda i: (i, 1))]
                + [const(a) for a in (wpa, wpb, wo, g1, b1, rwh, rwl, rbp, cnt0)])
    return pl.pallas_call(
        functools.partial(_merge_kernel, n_groups=n_groups),
        grid=(T // tm,),
        in_specs=in_specs,
        out_specs=[row(D_MODEL), row(D_MODEL // 2), row(TOP_K), row(TOP_K), row(TOP_K),
                   pl.BlockSpec((1, LANE), lambda i: (0, 0))],
        out_shape=[jax.ShapeDtypeStruct((T, D_MODEL), F32), jax.ShapeDtypeStruct((T, D_MODEL // 2), jnp.int32),
                   jax.ShapeDtypeStruct((T, TOP_K), jnp.int32), jax.ShapeDtypeStruct((T, TOP_K), F32),
                   jax.ShapeDtypeStruct((T, TOP_K), jnp.int32), jax.ShapeDtypeStruct((1, LANE), F32)],
        scratch_shapes=[pltpu.VMEM((tm, D_MODEL), BF16), pltpu.VMEM((tm, D_MODEL), F32)],
        compiler_params=_cparams(("arbitrary",)),
        name=f"merge_g{n_groups}",
    )(x, *oas, ob, pg, pg, wpa, wpb, wo, g1, b1, rwh, rwl, rbp, cnt0)


def _sc_gather(table, idx):
    info = plsc.get_sparse_core_info()
    n_workers = info.num_cores * info.num_subcores
    n, width = idx.shape[0], table.shape[1]
    per_worker = n // n_workers
    chunk = SC_CHUNK_BYTES // (width * table.dtype.itemsize)
    assert per_worker * n_workers == n and per_worker % chunk == 0 and chunk % 8 == 0
    mesh = plsc.VectorSubcoreMesh(core_axis_name="c", subcore_axis_name="s")

    @functools.partial(
        pl.kernel, mesh=mesh,
        out_type=jax.ShapeDtypeStruct((n, width), table.dtype),
        scratch_types=[pltpu.VMEM((chunk,), jnp.int32), pltpu.VMEM((chunk, width), table.dtype),
                       pltpu.SemaphoreType.DMA],
    )
    def gather(table_hbm, idx_hbm, out_hbm, idx_v, rows_v, sem):
        wid = lax.axis_index("s") * info.num_cores + lax.axis_index("c")
        base = wid * per_worker

        @pl.loop(0, per_worker // chunk)
        def _(c):
            off = pl.multiple_of(base + c * chunk, chunk)
            pltpu.sync_copy(idx_hbm.at[pl.ds(off, chunk)], idx_v)
            pltpu.async_copy(table_hbm.at[idx_v], rows_v, sem).wait()
            pltpu.sync_copy(rows_v, out_hbm.at[pl.ds(off, chunk)])

    return gather(table, idx)


def _sc_scatter_rows(tables, idx, n_out):
    info = plsc.get_sparse_core_info()
    n_workers = info.num_cores * info.num_subcores
    n_idx, n = idx.shape
    width, dtype = tables[0].shape[1], tables[0].dtype
    max_rows = SC_SCATTER_BYTES // (width * dtype.itemsize)
    plan, tok0 = [], 0
    for t in tables:
        per_worker = t.shape[0] // n_workers
        assert per_worker * n_workers == t.shape[0] and per_worker % 8 == 0 and tok0 % 8 == 0
        plan.append((per_worker, max(d for d in range(8, max_rows + 1, 8) if per_worker % d == 0), tok0))
        tok0 += t.shape[0]
    assert tok0 == n
    mesh = plsc.VectorSubcoreMesh(core_axis_name="c", subcore_axis_name="s")
    scratch_types = []
    for _, chunk, _ in plan:
        scratch_types += [pltpu.VMEM((chunk,), jnp.int32)] * n_idx + [pltpu.VMEM((chunk, width), dtype)]

    @functools.partial(pl.kernel, mesh=mesh, out_type=jax.ShapeDtypeStruct((n_out, width), dtype),
                       scratch_types=scratch_types)
    def scatter(*refs):
        table_refs, idx_hbm, out_hbm = refs[:len(tables)], refs[len(tables)], refs[len(tables) + 1]
        scratch = refs[len(tables) + 2:]
        wid = lax.axis_index("s") * info.num_cores + lax.axis_index("c")
        for j, (per_worker, chunk, first) in enumerate(plan):
            idx_vs = scratch[j * (n_idx + 1):j * (n_idx + 1) + n_idx]
            rows_v = scratch[j * (n_idx + 1) + n_idx]
            table_hbm = table_refs[j]

            @pl.loop(0, per_worker // chunk)
            def _(c):
                off = pl.multiple_of(wid * per_worker + c * chunk, 8)
                pltpu.sync_copy(table_hbm.at[pl.ds(off, chunk)], rows_v)
                for k in range(n_idx):
                    pltpu.sync_copy(idx_hbm.at[pl.ds(pl.multiple_of(k * n + first + off, 8), chunk)], idx_vs[k])
                for k in range(n_idx):
                    pltpu.sync_copy(rows_v, out_hbm.at[idx_vs[k]])

    return scatter(*tables, idx.reshape(n_idx * n))


def _expert_kernel(be_ref, rows_ref, nu_ref, xs_hbm, wg_ref, wu_ref, bg_ref, bu_ref, wd_ref, bd_ref, o_ref,
                   hid_ref, xs_ref, xs_sem):
    del be_ref
    i = pl.program_id(0)
    p = pl.program_id(1)
    nrows = rows_ref[i]
    n_live = (nrows + (MOE_SB - 1)) // MOE_SB
    half = D_MODEL // 2

    def xs_copy(blk):
        return pltpu.make_async_copy(xs_hbm.at[pl.ds(pl.multiple_of(blk * MOE_BM, MOE_BM), MOE_BM), :], xs_ref, xs_sem)

    @pl.when(jnp.logical_and(i == 0, p == 0))
    def _():
        xs_copy(0).start()

    @pl.when(p == 0)
    def _():
        xs_copy(i).wait()

    @pl.when(jnp.logical_and(p == MOE_NF, i + 1 < nu_ref[0]))
    def _():
        xs_copy(i + 1).start()

    def paired(one):
        def pair(j, carry):
            one(2 * j)
            one(2 * j + 1)
            return carry

        lax.fori_loop(0, n_live // 2, pair, 0)

        @pl.when(n_live % 2 == 1)
        def _():
            one(n_live - 1)

    @pl.when(p < MOE_NF)
    def _():
        def gate_up(s):
            r0 = pl.multiple_of(s * MOE_SB, MOE_SB)
            rid = r0 + lax.broadcasted_iota(jnp.int32, (MOE_SB, half), 0)
            packed = jnp.where(rid < nrows, xs_ref[pl.ds(r0, MOE_SB), :], 0)
            lo = lax.bitcast_convert_type(lax.shift_left(packed, 16), F32)
            hi = lax.bitcast_convert_type(packed & jnp.int32(-65536), F32)
            x = jnp.concatenate([lo.astype(BF16), hi.astype(BF16)], axis=1)
            g = jnp.minimum(_dot(x, wg_ref[...].astype(BF16)) + bg_ref[...], SWIGLU_LIMIT)
            u = jnp.clip(_dot(x, wu_ref[...].astype(BF16)) + bu_ref[...], -SWIGLU_LIMIT, SWIGLU_LIMIT)
            hid_ref[p, pl.ds(r0, MOE_SB), :] = ((u + 1.0) * g * jax.nn.sigmoid(SWIGLU_ALPHA * g)).astype(BF16)

        paired(gate_up)

    @pl.when(p >= MOE_NF)
    def _():
        def down(s):
            r0 = pl.multiple_of(s * MOE_SB, MOE_SB)
            y = bd_ref[...]
            for f in range(MOE_NF):
                y = y + _dot(hid_ref[f, pl.ds(r0, MOE_SB), :], wd_ref[f * MOE_TF:(f + 1) * MOE_TF, :].astype(BF16))
            bits = lax.bitcast_convert_type(y.astype(BF16).astype(F32), jnp.int32)
            o_ref[pl.ds(r0, MOE_SB), :] = lax.shift_right_logical(bits[:, :MOE_TN // 2], 16) | bits[:, MOE_TN // 2:]

        def zero_body(s, carry):
            o_ref[pl.ds(pl.multiple_of(s * MOE_SB, MOE_SB), MOE_SB), :] = jnp.zeros((MOE_SB, MOE_TN // 2), jnp.int32)
            return carry

        paired(down)
        lax.fori_loop(n_live, MOE_BM // MOE_SB, zero_body, 0)


def _experts(xs, block_expert, block_rows, n_used, w_gate_up, b_gate_up, w_down, b_down):
    nf, nn = MOE_NF, D_MODEL // MOE_TN

    def gate_map(col0, lead):
        def index_map(i, p, be, rw, nu):
            ahead = p >= nf + nn - lead
            e = jnp.where(ahead, be[jnp.minimum(i + 1, nu[0] - 1)], be[i])
            return (e, 0, col0 + jnp.where(ahead, 0, jnp.minimum(p, nf - 1)))
        return index_map

    def down_map(i, p, be, rw, nu):
        parked = p < nf
        e = jnp.where(parked, be[jnp.maximum(i - 1, 0)], be[i])
        return (e, 0, jnp.where(parked, jnp.where(i > 0, nn - 1, 0), p - nf))

    grid_spec = pltpu.PrefetchScalarGridSpec(
        num_scalar_prefetch=3,
        grid=(n_used[0], nf + nn),
        in_specs=[
            pl.BlockSpec(memory_space=pl.ANY),
            pl.BlockSpec((None, D_MODEL, MOE_TF), gate_map(0, nn // 2)),
            pl.BlockSpec((None, D_MODEL, MOE_TF), gate_map(nf, nn // 4)),
            pl.BlockSpec((None, 1, MOE_TF), gate_map(0, nn // 2)),
            pl.BlockSpec((None, 1, MOE_TF), gate_map(nf, nn // 4)),
            pl.BlockSpec((None, D_FF, MOE_TN), down_map),
            pl.BlockSpec((None, 1, MOE_TN), down_map),
        ],
        out_specs=pl.BlockSpec((MOE_BM, MOE_TN // 2), lambda i, p, be, rw, nu: (i, jnp.maximum(p - nf, 0))),
        scratch_shapes=[pltpu.VMEM((nf, MOE_BM, MOE_TF), BF16), pltpu.VMEM((MOE_BM, D_MODEL // 2), jnp.int32),
                        pltpu.SemaphoreType.DMA],
    )
    bgu = b_gate_up.reshape(N_EXPERTS, 1, 2 * D_FF)
    bd = b_down.reshape(N_EXPERTS, 1, D_MODEL)
    return pl.pallas_call(
        _expert_kernel,
        grid_spec=grid_spec,
        out_shape=jax.ShapeDtypeStruct((xs.shape[0], D_MODEL // 2), jnp.int32),
        compiler_params=_cparams(("arbitrary", "arbitrary")),
        name="moe_experts",
    )(block_expert, block_rows, n_used, xs, w_gate_up, w_gate_up, bgu, bgu, w_down, bd)


def _route(top_i, rank, counts, n_blocks):
    T = top_i.shape[0]
    bpe = (counts + MOE_BM - 1) // MOE_BM
    bend = jnp.cumsum(bpe)
    bstart = bend - bpe
    experts = jnp.arange(N_EXPERTS, dtype=jnp.int32)
    start_of = jnp.sum(jnp.where(top_i[:, :, None] == experts, bstart * MOE_BM, 0), axis=-1)
    dest = (start_of + rank).astype(jnp.int32)
    n_used = bend[-1]
    blk = jnp.arange(n_blocks, dtype=jnp.int32)
    be = jnp.minimum(jnp.searchsorted(bend, jnp.minimum(blk, n_used - 1), side="right"), N_EXPERTS - 1).astype(jnp.int32)
    rows = jnp.clip(counts[be] - (blk - bstart[be]) * MOE_BM, 0, MOE_BM)
    rows = jnp.where(blk < n_used, rows, 0).astype(jnp.int32)
    return dest.reshape(T, TOP_K).T, be, rows, n_used.reshape(1).astype(jnp.int32)


def _combine_kernel(x1_ref, ge_ref, gt_ref, g2_ref, b2_ref, o_ref):
    hw, nn = MOE_TN // 2, D_MODEL // MOE_TN

    def unpack(w):
        lo = lax.bitcast_convert_type(lax.shift_left(w, 16), F32)
        hi = lax.bitcast_convert_type(w & jnp.int32(-65536), F32)
        return jnp.concatenate([part[:, n * hw:(n + 1) * hw] for n in range(nn) for part in (lo, hi)], axis=1)

    gt = gt_ref[...]
    m = gt[:, 0:1] * unpack(ge_ref[0])
    for k in range(1, TOP_K):
        m = m + gt[:, k:k + 1] * unpack(ge_ref[k])
    o_ref[...] = _layer_norm(DEEPNORM_ALPHA * x1_ref[...] + m, g2_ref[...], b2_ref[...])


def _combine(x1, ge, gate, g2, b2, row0, tm):
    n_rows = x1.shape[0]
    b0 = row0 // tm
    return pl.pallas_call(
        _combine_kernel,
        grid=(n_rows // tm,),
        in_specs=[pl.BlockSpec((tm, D_MODEL), lambda i: (i, 0)),
                  pl.BlockSpec((TOP_K, tm, D_MODEL // 2), lambda i: (0, b0 + i, 0)),
                  pl.BlockSpec((tm, TOP_K), lambda i: (b0 + i, 0)),
                  pl.BlockSpec(g2.shape, lambda i: (0, 0)), pl.BlockSpec(b2.shape, lambda i: (0, 0))],
        out_specs=pl.BlockSpec((tm, D_MODEL), lambda i: (i, 0)),
        out_shape=jax.ShapeDtypeStruct((n_rows, D_MODEL), F32),
        compiler_params=_cparams(("parallel",)),
        name="moe_combine",
    )(x1, ge, gate, g2, b2)


def _layer(xp, xs, caches, state, w_in, rel_bias, gla_w_up, gla_b, gla_norm_g, w_pa, w_pb, w_o, ln1_g, ln1_b,
           router_w, router_b, w_gate_up, b_gate_up, w_down, b_down, ln2_g, ln2_b):
    batch, seq, _ = xp.shape
    dbatch, dseq, _ = xs.shape
    tp, ts = batch * seq, dbatch * dseq
    xp2, xs2 = xp.reshape(tp, D_MODEL), xs.reshape(ts, D_MODEL)

    o_b0, o_lr, o_g = PA_W, PA_W + PB_LR, PA_W + PB_LR + GLA_RANK
    w_a = w_in[:, :PA_W].astype(BF16)
    w_b = jnp.concatenate([w_in[:, o_b0:o_g], jnp.zeros((D_MODEL, LANE - GLA_RANK), F32)], axis=1).astype(BF16)
    w_g = w_in[:, o_g:].astype(BF16)
    wup = jnp.concatenate([gla_w_up, jnp.zeros((LANE - GLA_RANK, H_B * DK_B), F32)], axis=0).astype(BF16)
    gb = gla_b.reshape(1, H_B * DK_B)
    ng = gla_norm_g.reshape(1, DV_B)
    wpa, wpb, wo = w_pa.astype(BF16), w_pb.astype(BF16), w_o.astype(BF16)
    g1, b1 = ln1_g.reshape(1, D_MODEL), ln1_b.reshape(1, D_MODEL)
    g2, b2 = ln2_g.reshape(1, D_MODEL), ln2_b.reshape(1, D_MODEL)
    rw = jnp.concatenate([router_w, jnp.zeros((D_MODEL, LANE - N_EXPERTS), F32)], axis=1)
    rwh = rw.astype(BF16)
    rwl = (rw - rwh.astype(F32)).astype(BF16)
    rbp = jnp.concatenate([router_b, jnp.full((LANE - N_EXPERTS,), NEG, F32)]).reshape(1, LANE)
    caches8 = [c.reshape(dbatch, c.shape[1], KV_ROWS, HEAD_DIM) for c in caches]

    pa_p = _project(xp2, w_a, PROJ_TM, A_QKV_WIDTH, "proj_a_prompt", head_major=True)
    pb_p = _project(xp2, w_b, PROJ_TM, PB_W, "proj_b_prompt")
    pg_p = _project(xp2, w_g, PROJ_TM, 1024, "proj_g_prompt")
    oas, lses = [], []
    for g in range(N_GROUPS):
        o, lse = _attn_heads(pa_p, _prompt_table(rel_bias, g), g, batch, seq)
        oas.append(o)
        lses.append(lse)
    ob_p, st_p = _gla_heads(pb_p, wup, gb, ng, jnp.zeros((batch, H_B, DK_B, DV_B), F32), batch, seq, GLA_CHUNK, 256)
    x1_p, x1p_p, ti_p, gt_p, rk_p, cnt_p = _merge(xp2, oas + lses, ob_p, pg_p, wpa, wpb, wo, g1, b1, rwh, rwl, rbp,
                                                  jnp.zeros((1, LANE), F32), 256)

    pa_s = _project(xs2, w_a, ts, A_QKV_WIDTH, "proj_a_sample", head_major=True)
    pb_s = _project(xs2, w_b, ts, PB_W, "proj_b_sample")
    pg_s = _project(xs2, w_g, ts, 1024, "proj_g_sample")
    tabc, tabn, combo_base = _sample_tables(rel_bias, dseq)
    new_rows = [_kv_pack(pa_s, g, dbatch, dseq, dseq, dseq) for g in range(N_GROUPS)]
    oa_s = _attn_sample(pa_s, new_rows, caches8, tabc, tabn, combo_base, dbatch, dseq)
    chunk_s = int(np.gcd(dseq, GLA_CHUNK))
    ob_s, st_s = _gla_heads(pb_s, wup, gb, ng, state, dbatch, dseq, chunk_s, dseq)
    x1_s, x1p_s, ti_s, gt_s, rk_s, cnt_all = _merge(xs2, [oa_s], ob_s, pg_s, wpa, wpb, wo, g1, b1, rwh, rwl, rbp, cnt_p, ts)

    top_i = jnp.concatenate([ti_p, ti_s], axis=0)
    gate = jnp.concatenate([gt_p, gt_s], axis=0)
    t_all = tp + ts
    n_blocks = -(-(t_all * TOP_K) // MOE_BM) + N_EXPERTS
    rank = jnp.concatenate([rk_p, rk_s], axis=0)
    counts = cnt_all[0, :N_EXPERTS].astype(jnp.int32)
    dest, be, rows, n_used = _route(top_i, rank, counts, n_blocks)
    xsorted = _sc_scatter_rows([x1p_p, x1p_s], dest, n_blocks * MOE_BM)
    keeps = [min(window, seq) for window, _ in DILATED_GROUPS]
    packed_p = {g: _kv_pack(pa_p, g, batch, seq, keeps[g], Q_BLOCK) for g in range(N_GROUPS - 1)}
    eo = _experts(xsorted, be, rows, n_used, w_gate_up, b_gate_up, w_down, b_down)
    ge = _sc_gather(eo, dest.reshape(TOP_K * t_all)).reshape(TOP_K, t_all, D_MODEL // 2)
    packed_p[N_GROUPS - 1] = _kv_pack(pa_p, N_GROUPS - 1, batch, seq, keeps[N_GROUPS - 1], Q_BLOCK)
    y_p = _combine(x1_p, ge, gate, g2, b2, 0, 256)
    y_s = _combine(x1_s, ge, gate, g2, b2, tp, 256)

    shifted = _sc_cache_shift(caches8, dseq)
    bufs_p, bufs_s = [], []
    for g, (window, _) in enumerate(DILATED_GROUPS):
        keep = keeps[g]
        bufs_p.append(packed_p[g].reshape(batch, keep, 2, H_A, HEAD_DIM))
        clen = caches[g].shape[1]
        assert clen == window and dseq <= clen
        buf = lax.dynamic_update_slice(shifted[g], new_rows[g], (0, clen - dseq, 0, 0))
        bufs_s.append(buf.reshape(dbatch, clen, 2, H_A, HEAD_DIM))
    return y_p.reshape(batch, seq, D_MODEL), y_s.reshape(dbatch, dseq, D_MODEL), bufs_p, st_p, bufs_s, st_s


def kernel(x_prompt, x_sample, cache_a1_kv, cache_a2_kv, cache_a3_kv, state_b_s, w_in, rel_bias, gla_w_up, gla_b,
           gla_norm_g, w_pa, w_pb, w_o, ln1_g, ln1_b, router_w, router_b, w_gate_up, b_gate_up, w_down, b_down,
           ln2_g, ln2_b):
    assert w_in.shape[0] == DEPTH
    yp, ys, bufs_p, st_p, bufs_s, st_s = _layer(
        x_prompt, x_sample, (cache_a1_kv[0], cache_a2_kv[0], cache_a3_kv[0]), state_b_s[0], w_in[0], rel_bias,
        gla_w_up[0], gla_b[0], gla_norm_g[0], w_pa[0], w_pb[0], w_o[0], ln1_g[0], ln1_b[0], router_w[0], router_b[0],
        w_gate_up[0], b_gate_up[0], w_down[0], b_down[0], ln2_g[0], ln2_b[0])
    return (yp, ys, bufs_p[0][None], bufs_p[1][None], bufs_p[2][None], st_p[None],
            bufs_s[0][None], bufs_s[1][None], bufs_s[2][None], st_s[None].astype(state_b_s.dtype))
```

```python
import functools

import numpy as np
import jax
import jax.numpy as jnp
from jax import lax
from jax.experimental import pallas as pl
from jax.experimental.pallas import tpu as pltpu
from jax.experimental.pallas import tpu_sc as plsc

F32 = jnp.float32
BF16 = jnp.bfloat16

D_MODEL = 2048
HEAD_DIM = 128
DILATED_GROUPS = ((128, 1), (512, 4), (2048, 16))
N_GROUPS = 3
H_A = 4
A_WIDTH = H_A * HEAD_DIM
A_QKV_WIDTH = N_GROUPS * A_WIDTH
Q_BLOCK = 128
N_BUCKETS = 32
REL_MAX_DIST = 2048
H_B = 4
DK_B = 64
DV_B = 128
GLA_RANK = 16
GLA_TAU = 16.0
GLA_CHUNK = 64
GLA_SUB = 16
GLA_EXP_CLAMP = 80.0
N_EXPERTS = 32
TOP_K = 4
D_FF = 2048
SWIGLU_LIMIT = 7.0
SWIGLU_ALPHA = 1.702
LN_EPS = 1e-5
RMS_EPS = 1e-6
DEPTH = 1
DEEPNORM_ALPHA = (2.0 * DEPTH) ** 0.25
ATT_SCALE = HEAD_DIM ** -0.5
NEG = float(np.finfo(np.float32).min)

VMEM_LIMIT = 56 * 1024 * 1024
LANE = 128

PA_W = 3 * A_QKV_WIDTH
PB_LR = H_B * DK_B * 2 + H_B * DV_B * 2
PB_W = PB_LR + LANE
PG_W = 2 * D_MODEL

PROJ_TM = 1024

MOE_BM = 2560
MOE_SB = 320
MOE_TF = 512
MOE_NF = D_FF // MOE_TF
MOE_TN = 512
MERGE_TC = 512
SC_CHUNK_BYTES = 128 * 1024
SC_SCATTER_BYTES = 160 * 1024


def _cparams(sem):
    return pltpu.CompilerParams(dimension_semantics=sem, vmem_limit_bytes=VMEM_LIMIT)


def _dot(a, b):
    return jnp.dot(a, b, preferred_element_type=F32)


def _dot_nt(a, b):
    return lax.dot_general(a, b, (((1,), (1,)), ((), ())), preferred_element_type=F32)


def _dot_tn(a, b):
    return lax.dot_general(a, b, (((0,), (0,)), ((), ())), preferred_element_type=F32)


def _proj_kernel(x_ref, w_ref, o_ref, xb_ref):
    @pl.when(pl.program_id(1) == 0)
    def _():
        xb_ref[...] = x_ref[...].astype(BF16)

    o_ref[...] = _dot(xb_ref[...], w_ref[...])


def _proj_heads_kernel(x_ref, w_ref, o_ref, xb_ref):
    @pl.when(pl.program_id(1) == 0)
    def _():
        xb_ref[...] = x_ref[...].astype(BF16)

    acc = _dot(xb_ref[...], w_ref[...])
    for c in range(o_ref.shape[0]):
        o_ref[c] = acc[:, c * HEAD_DIM:(c + 1) * HEAD_DIM]


def _project(x, w, tm, tn, name, head_major=False):
    T, D = x.shape
    N = w.shape[1]
    if head_major:
        nh = tn // HEAD_DIM
        out_spec = pl.BlockSpec((nh, tm, HEAD_DIM), lambda i, j: (j, i, 0))
        out_shape = jax.ShapeDtypeStruct((N // HEAD_DIM, T, HEAD_DIM), F32)
    else:
        out_spec = pl.BlockSpec((tm, tn), lambda i, j: (i, j))
        out_shape = jax.ShapeDtypeStruct((T, N), F32)
    return pl.pallas_call(
        _proj_heads_kernel if head_major else _proj_kernel,
        grid=(T // tm, N // tn),
        in_specs=[pl.BlockSpec((tm, D), lambda i, j: (i, 0)), pl.BlockSpec((D, tn), lambda i, j: (0, j))],
        out_specs=out_spec,
        out_shape=out_shape,
        scratch_shapes=[pltpu.VMEM((tm, D), BF16)],
        compiler_params=_cparams(("parallel", "arbitrary")),
        name=name,
    )(x, w)


def _t5_bucket(dist):
    max_exact = N_BUCKETS // 2
    d = np.maximum(dist, 1).astype(np.float32)
    large = max_exact + (np.log(d / max_exact) / np.log(REL_MAX_DIST / max_exact) * (N_BUCKETS - max_exact)).astype(np.int32)
    large = np.minimum(large, N_BUCKETS - 1)
    return np.where(dist < max_exact, dist, large).astype(np.int32)


def _bias_lookup(rel_bias, g, j, valid):
    _, dil = DILATED_GROUPS[g]
    bucket = _t5_bucket(dil * np.clip(j, 0, Q_BLOCK))
    onehot = bucket[None] == np.arange(N_BUCKETS).reshape((N_BUCKETS,) + (1,) * j.ndim)
    rb = rel_bias[:, g * H_A:(g + 1) * H_A].astype(F32).T.reshape((H_A, N_BUCKETS) + (1,) * j.ndim)
    vals = jnp.sum(jnp.where(onehot[None], rb, 0.0), axis=1)
    return jnp.where(valid[None], vals, NEG)


def _prompt_table(rel_bias, g):
    qi = np.arange(Q_BLOCK)[:, None]
    kj = np.arange(2 * Q_BLOCK)[None, :]
    j = Q_BLOCK + qi - kj
    return _bias_lookup(rel_bias, g, j, (j >= 0) & (j <= Q_BLOCK))


def _sample_tables(rel_bias, dec_seq):
    m = np.arange(Q_BLOCK)
    tabc, combo_base = [], []
    for g, (_, dil) in enumerate(DILATED_GROUPS):
        combo_base.append(len(tabc))
        for fl in range((dec_seq - 1) // dil + 1):
            j = Q_BLOCK + fl - m
            col = _bias_lookup(rel_bias, g, j, j <= Q_BLOCK).T
            col = jnp.concatenate([col, jnp.zeros_like(col)], axis=1)
            tabc.append(jnp.broadcast_to(col[:, :, None], (Q_BLOCK, 2 * H_A, LANE)))
    tabn = []
    s = np.arange(dec_seq)[:, None]
    sp = np.arange(dec_seq)[None, :]
    for g, (_, dil) in enumerate(DILATED_GROUPS):
        diff = s - sp
        t = _bias_lookup(rel_bias, g, diff // dil, (diff >= 0) & (diff % dil == 0))
        t = jnp.transpose(t, (1, 2, 0))
        t = jnp.concatenate([t, jnp.zeros_like(t)], axis=2)
        tabn.append(jnp.broadcast_to(t[..., None], (dec_seq, dec_seq, 2 * H_A, LANE)))
    return jnp.stack(tabc), jnp.stack(tabn), tuple(combo_base)


def _attn_heads_kernel(*refs, dil, nb, with_prev):
    if with_prev:
        q_ref, kc_ref, kp_ref, vc_ref, vp_ref, tab_ref, o_ref, lse_ref = refs
    else:
        q_ref, kc_ref, vc_ref, tab_ref, o_ref, lse_ref = refs
    has_prev = pl.program_id(1) > 0
    zero = jnp.zeros((Q_BLOCK, HEAD_DIM), BF16)
    tab_c = tab_ref[:, Q_BLOCK:]
    if with_prev:
        tab = jnp.concatenate([jnp.where(has_prev, tab_ref[:, :Q_BLOCK], NEG), tab_c], axis=1)
    else:
        tab = tab_c

    def heads(ref, bi, idx):
        return jnp.concatenate([ref[h, bi, idx, :].astype(BF16) for h in range(H_A)], axis=1)

    def body(u, carry):
        bi, r = u // dil, u % dil
        idx = pl.ds(r, Q_BLOCK, stride=dil)
        qs = [q_ref[h, bi, idx, :].astype(BF16) for h in range(H_A)]
        q_stack = jnp.concatenate(
            [jnp.concatenate([qs[h] if c == h else zero for c in range(H_A)], axis=1) for h in range(H_A)], axis=0)
        if with_prev:
            k_all = jnp.concatenate([heads(kp_ref, bi, idx), heads(kc_ref, bi, idx)], axis=0)
            v_all = jnp.concatenate([heads(vp_ref, bi, idx), heads(vc_ref, bi, idx)], axis=0)
        else:
            k_all, v_all = heads(kc_ref, bi, idx), heads(vc_ref, bi, idx)
        s = _dot_nt(q_stack, k_all) * ATT_SCALE + tab
        m = jnp.max(s, axis=-1, keepdims=True)
        p = jnp.exp(s - m)
        l = jnp.sum(p, axis=-1, keepdims=True)
        o = _dot((p * (1.0 / l)).astype(BF16), v_all)
        lse = m + jnp.log(l)
        for h in range(H_A):
            rows = slice(h * Q_BLOCK, (h + 1) * Q_BLOCK)
            o_ref[h, bi, idx, :] = o[rows, h * HEAD_DIM:(h + 1) * HEAD_DIM]
            lse_ref[h, bi, idx, :] = jnp.broadcast_to(lse[rows], (Q_BLOCK, HEAD_DIM))
        return carry

    lax.fori_loop(0, nb * dil, body, 0, unroll=min(nb * dil, 4))


def _attn_heads(pa_hm, table, g, batch, seq):
    _, dil = DILATED_GROUPS[g]
    rows = dil * Q_BLOCK
    nblk = seq // rows
    with_prev = nblk > 1
    nb = 4 if (dil == 1 and batch % 4 == 0) else 1
    pv = pa_hm.reshape(pa_hm.shape[0], batch, seq, HEAD_DIM)

    def spec(sec, prev):
        if prev:
            return pl.BlockSpec((H_A, nb, rows, HEAD_DIM), lambda b, i: (sec * N_GROUPS + g, b, jnp.maximum(i - 1, 0), 0))
        return pl.BlockSpec((H_A, nb, rows, HEAD_DIM), lambda b, i: (sec * N_GROUPS + g, b, i, 0))

    in_specs = [spec(0, False), spec(1, False)] + ([spec(1, True)] if with_prev else []) + [spec(2, False)] + (
        [spec(2, True)] if with_prev else []) + [pl.BlockSpec((H_A * Q_BLOCK, 2 * Q_BLOCK), lambda b, i: (0, 0))]
    out_spec = pl.BlockSpec((H_A, nb, rows, HEAD_DIM), lambda b, i: (0, b, i, 0))
    o, lse = pl.pallas_call(
        functools.partial(_attn_heads_kernel, dil=dil, nb=nb, with_prev=with_prev),
        grid=(batch // nb, nblk),
        in_specs=in_specs,
        out_specs=[out_spec, out_spec],
        out_shape=[jax.ShapeDtypeStruct((H_A, batch, seq, HEAD_DIM), F32)] * 2,
        compiler_params=_cparams(("parallel", "arbitrary")),
        name=f"attn_prompt_g{g}",
    )(*([pv] * (len(in_specs) - 1)), table.reshape(H_A * Q_BLOCK, 2 * Q_BLOCK))
    return o.reshape(H_A, batch * seq, HEAD_DIM), lse.reshape(H_A, batch * seq, HEAD_DIM)


KV_ROWS = 2 * H_A


def _kv_pack_kernel(k_ref, v_ref, o_ref):
    o_ref[...] = jnp.stack([k_ref[h] for h in range(H_A)] + [v_ref[h] for h in range(H_A)], axis=1)


def _kv_pack(pa_hm, g, batch, seq, keep, tm):
    nblk, blk0, per_b = keep // tm, (seq - keep) // tm, seq // tm
    out = pl.pallas_call(
        _kv_pack_kernel,
        grid=(batch, nblk),
        in_specs=[pl.BlockSpec((H_A, tm, HEAD_DIM), lambda b, i: (N_GROUPS + g, b * per_b + blk0 + i, 0)),
                  pl.BlockSpec((H_A, tm, HEAD_DIM), lambda b, i: (2 * N_GROUPS + g, b * per_b + blk0 + i, 0))],
        out_specs=pl.BlockSpec((tm, KV_ROWS, HEAD_DIM), lambda b, i: (b * nblk + i, 0, 0)),
        out_shape=jax.ShapeDtypeStruct((batch * keep, KV_ROWS, HEAD_DIM), F32),
        compiler_params=_cparams(("parallel", "parallel")),
        name=f"kv_pack_g{g}_{keep}",
    )(pa_hm, pa_hm)
    return out.reshape(batch, keep, KV_ROWS, HEAD_DIM)


def _attn_sample_kernel(qkv_ref, c1_ref, c2_ref, c3_ref, tabc_ref, tabn_ref, o_ref, n1_ref, n2_ref, n3_ref, *,
                        dec_seq, combo_base):
    caches = (c1_ref, c2_ref, c3_ref)
    news = (n1_ref, n2_ref, n3_ref)
    nh = N_GROUPS * H_A
    for g in range(N_GROUPS):
        news[g][...] = jnp.stack([qkv_ref[nh + g * H_A + h] for h in range(H_A)]
                                 + [qkv_ref[2 * nh + g * H_A + h] for h in range(H_A)], axis=1)
    zeros = jnp.zeros((H_A, HEAD_DIM), F32)
    for s in range(dec_seq):
        outs, lses = [], []
        for g, (_, dil) in enumerate(DILATED_GROUPS):
            rho, fl = s % dil, s // dil
            qm = jnp.concatenate([qkv_ref[g * H_A + h, s:s + 1, :] for h in range(H_A)] + [zeros], axis=0)
            qm = qm * ATT_SCALE
            kc = caches[g][:, rho]
            kn = news[g][...]
            sc = jnp.sum(kc * qm[None], axis=-1, keepdims=True) + tabc_ref[combo_base[g] + fl]
            sn = jnp.sum(kn * qm[None], axis=-1, keepdims=True) + tabn_ref[g, s]
            m = jnp.maximum(jnp.max(sc, axis=0), jnp.max(sn, axis=0))
            pc = jnp.exp(sc - m[None])
            pn = jnp.exp(sn - m[None])
            l = jnp.sum(pc, axis=0) + jnp.sum(pn, axis=0)
            acc = jnp.sum(pltpu.roll(pc, H_A, 1) * kc, axis=0) + jnp.sum(pltpu.roll(pn, H_A, 1) * kn, axis=0)
            outs.append(acc / pltpu.roll(l, H_A, 0))
            lses.append(pltpu.roll(m + jnp.log(l), H_A, 0))
        mm = jnp.maximum(jnp.maximum(lses[0], lses[1]), lses[2])
        ws = [jnp.exp(x - mm) for x in lses]
        o_ref[s] = (ws[0] * outs[0] + ws[1] * outs[1] + ws[2] * outs[2]) / (ws[0] + ws[1] + ws[2])


def _attn_sample(pa_hm, caches, tabc, tabn, combo_base, batch, dec_seq):
    views, specs = [], []
    for g, (window, dil) in enumerate(DILATED_GROUPS):
        assert caches[g].shape[1] == window and dec_seq <= Q_BLOCK
        views.append(caches[g].reshape(batch, Q_BLOCK, dil, KV_ROWS, HEAD_DIM))
        used = min(dil, dec_seq)
        specs.append(pl.BlockSpec((None, Q_BLOCK, used, KV_ROWS, HEAD_DIM), lambda b: (b, 0, 0, 0, 0)))
    row_spec = pl.BlockSpec((None, dec_seq, KV_ROWS, HEAD_DIM), lambda b: (b, 0, 0, 0))
    row_shape = jax.ShapeDtypeStruct((batch, dec_seq, KV_ROWS, HEAD_DIM), F32)
    out, *new_rows = pl.pallas_call(
        functools.partial(_attn_sample_kernel, dec_seq=dec_seq, combo_base=combo_base),
        grid=(batch,),
        in_specs=[pl.BlockSpec((3 * N_GROUPS * H_A, dec_seq, HEAD_DIM), lambda b: (0, b, 0))] + specs + [
            pl.BlockSpec(tabc.shape, lambda b: (0, 0, 0, 0)), pl.BlockSpec(tabn.shape, lambda b: (0, 0, 0, 0, 0))],
        out_specs=[row_spec] * (1 + N_GROUPS),
        out_shape=[row_shape] * (1 + N_GROUPS),
        compiler_params=_cparams(("parallel",)),
        name="attn_sample",
    )(pa_hm, *views, tabc, tabn)
    return out[:, :, H_A:, :].reshape(batch * dec_seq, A_WIDTH), new_rows


def _sc_cache_shift(caches, drop):
    info = plsc.get_sparse_core_info()
    n_workers = info.num_cores * info.num_subcores
    batch = caches[0].shape[0]
    assert batch % n_workers == 0
    mesh = plsc.VectorSubcoreMesh(core_axis_name="c", subcore_axis_name="s")

    row_bytes = KV_ROWS * HEAD_DIM * 4
    chunks = []
    for c in caches:
        keep = c.shape[1] - drop
        ch = max(d for d in range(1, SC_CHUNK_BYTES // row_bytes + 1) if keep % d == 0)
        chunks.append(ch)
    buf_rows = max(chunks)

    @functools.partial(pl.kernel, mesh=mesh, out_type=[jax.ShapeDtypeStruct(c.shape, c.dtype) for c in caches],
                       scratch_types=[pltpu.VMEM((buf_rows, KV_ROWS, HEAD_DIM), caches[0].dtype)])
    def shift(*refs):
        srcs, dsts, buf = refs[:len(caches)], refs[len(caches):2 * len(caches)], refs[-1]
        wid = lax.axis_index("s") * info.num_cores + lax.axis_index("c")
        for j in range(batch // n_workers):
            b = wid * (batch // n_workers) + j
            for src, dst, ch in zip(srcs, dsts, chunks):
                stage = buf.at[pl.ds(0, ch)]

                @pl.loop(0, (src.shape[1] - drop) // ch)
                def _(i):
                    pltpu.sync_copy(src.at[b, pl.ds(drop + i * ch, ch)], stage)
                    pltpu.sync_copy(stage, dst.at[b, pl.ds(i * ch, ch)])

    return shift(*caches)


def _split3(x):
    hi = x.astype(BF16)
    r = x - hi.astype(F32)
    mid = r.astype(BF16)
    lo = (r - mid.astype(F32)).astype(BF16)
    return hi, mid, lo


def _gla_heads_kernel(p_ref, wup_ref, gb_ref, ng_ref, s0_ref, o_ref, st_ref, *, chunk, tb):
    @pl.when(pl.program_id(1) == 0)
    def _():
        st_ref[...] = s0_ref[...]

    sub = min(GLA_SUB, chunk)
    n_sub = chunk // sub
    kq, vw = H_B * DK_B, H_B * DV_B

    def rb(x):
        xb = x.astype(BF16)
        return xb if chunk >= 16 else xb.astype(F32)

    row = lax.broadcasted_iota(jnp.int32, (chunk, chunk), 0)
    colm = lax.broadcasted_iota(jnp.int32, (chunk, chunk), 1)
    tri = rb(jnp.where(row >= colm, 1.0, 0.0))
    lane_head = lax.broadcasted_iota(jnp.int32, (1, kq), 1) // DK_B
    diag_blocks = (lax.broadcasted_iota(jnp.int32, (vw, kq), 0) // DV_B
                   == lax.broadcasted_iota(jnp.int32, (vw, kq), 1) // DK_B)
    a_cols = LANE if n_sub > 1 else chunk
    t_in = lax.broadcasted_iota(jnp.int32, (H_B * chunk, a_cols), 0) % chunk
    s_in = lax.broadcasted_iota(jnp.int32, (H_B * chunk, a_cols), 1)
    for c in range(tb // chunk):
        rows = slice(c * chunk, (c + 1) * chunk)
        z = _dot(rb(p_ref[rows, PB_LR:PB_W]), rb(wup_ref[...])) + gb_ref[...]
        la = -(jnp.maximum(-z, 0.0) + jnp.log1p(jnp.exp(-jnp.abs(z)))) * (1.0 / GLA_TAU)
        b3 = _dot(tri, rb(jnp.concatenate(_split3(la), axis=1)))
        b = b3[:, :kq] + b3[:, kq:2 * kq] + b3[:, 2 * kq:]
        blast = b[chunk - 1:chunk, :]
        q = p_ref[rows, 0:kq] * (DK_B ** -0.5)
        k = p_ref[rows, kq:2 * kq]
        v = rb(p_ref[rows, 2 * kq:2 * kq + vw])
        st = st_ref[...]
        o_inter = _dot_nt(rb(q * jnp.exp(b)), rb(st))
        refs = [jnp.zeros((1, kq), F32)] + [b[i * sub - 1:i * sub, :] for i in range(1, n_sub)]
        ref_rows = jnp.concatenate([jnp.broadcast_to(r, (sub, kq)) for r in refs], axis=0)
        qs = q * jnp.exp(b - ref_rows)
        q_stack = rb(jnp.concatenate([jnp.where(lane_head == h, qs, 0.0) for h in range(H_B)], axis=0))
        k_parts = []
        for r in refs:
            k_parts.append(k * jnp.exp(jnp.minimum(r - b, GLA_EXP_CLAMP)))
            if a_cols > chunk:
                k_parts.append(jnp.zeros((a_cols - chunk, kq), F32))
        raw = _dot_nt(q_stack, rb(jnp.concatenate(k_parts, axis=0)))
        a = jnp.zeros((H_B * chunk, a_cols), F32)
        for i in range(n_sub):
            a = a + jnp.where(t_in // sub == i, raw[:, i * a_cols:(i + 1) * a_cols], 0.0)
        a = jnp.where(s_in <= t_in, a, 0.0)[:, :chunk]
        o_all = _dot(rb(a), v)
        o = o_inter + jnp.concatenate(
            [o_all[h * chunk:(h + 1) * chunk, h * DV_B:(h + 1) * DV_B] for h in range(H_B)], axis=1)
        st_ref[...] = st * jnp.exp(blast) + jnp.where(diag_blocks, _dot_tn(v, rb(k * jnp.exp(blast - b))), 0.0)
        normed = []
        for h in range(H_B):
            oh = o[:, h * DV_B:(h + 1) * DV_B]
            normed.append(oh * lax.rsqrt(jnp.mean(oh * oh, axis=-1, keepdims=True) + RMS_EPS) * ng_ref[...])
        rg = p_ref[rows, 2 * kq + vw:2 * kq + 2 * vw]
        o_ref[rows, :] = (jnp.concatenate(normed, axis=1) * (rg * jax.nn.sigmoid(rg))).astype(BF16)


def _gla_heads(pb, wup, gb, ng, s0, batch, seq, chunk, tb):
    kq, vw = H_B * DK_B, H_B * DV_B
    eye = jnp.eye(H_B, dtype=F32)
    s0t = jnp.swapaxes(s0, -1, -2)
    s0_bd = (s0t[:, :, :, None, :] * eye[None, :, None, :, None]).reshape(batch, vw, kq)
    p3 = pb.reshape(batch, seq, PB_W)
    o, st = pl.pallas_call(
        functools.partial(_gla_heads_kernel, chunk=chunk, tb=tb),
        grid=(batch, seq // tb),
        in_specs=[pl.BlockSpec((None, tb, PB_W), lambda b, i: (b, i, 0)),
                  pl.BlockSpec(wup.shape, lambda b, i: (0, 0)),
                  pl.BlockSpec(gb.shape, lambda b, i: (0, 0)),
                  pl.BlockSpec(ng.shape, lambda b, i: (0, 0)),
                  pl.BlockSpec((None, vw, kq), lambda b, i: (b, 0, 0))],
        out_specs=[pl.BlockSpec((None, tb, vw), lambda b, i: (b, i, 0)),
                   pl.BlockSpec((None, vw, kq), lambda b, i: (b, 0, 0))],
        out_shape=[jax.ShapeDtypeStruct((batch, seq, vw), BF16), jax.ShapeDtypeStruct((batch, vw, kq), F32)],
        compiler_params=_cparams(("parallel", "arbitrary")),
        name=f"gla_c{chunk}",
    )(p3, wup, gb, ng, s0_bd)
    st5 = st.reshape(batch, H_B, DV_B, H_B, DK_B)
    st_heads = jnp.stack([st5[:, h, :, h, :] for h in range(H_B)], axis=1)
    return o.reshape(batch * seq, vw), jnp.swapaxes(st_heads, -1, -2)


def _layer_norm(u, g, b):
    mu = jnp.mean(u, axis=-1, keepdims=True)
    d = u - mu
    var = jnp.mean(d * d, axis=-1, keepdims=True)
    return d * lax.rsqrt(var + LN_EPS) * g + b


def _merge_kernel(*refs, n_groups):
    x_ref = refs[0]
    oa_refs = refs[1:1 + 2 * n_groups] if n_groups > 1 else refs[1:2]
    rest = refs[1 + (2 * n_groups if n_groups > 1 else 1):]
    (ob_ref, pg_a_ref, pg_b_ref, wpa_ref, wpb_ref, wo_ref, g1_ref, b1_ref, rwh_ref, rwl_ref, rb_ref, cnt0_ref,
     x1_ref, x1p_ref, ti_ref, gt_ref, rank_ref, cnt_ref, br_ref, u_ref) = rest
    tm = x_ref.shape[0]
    half = D_MODEL // 2

    def rows_part(rs):
        if n_groups > 1:
            def lanes(r):
                return jnp.concatenate([r[h, rs, :] for h in range(H_A)], axis=1)

            os_ = [lanes(r) for r in oa_refs[:n_groups]]
            ls = [lanes(r) for r in oa_refs[n_groups:]]
            mm = functools.reduce(jnp.maximum, ls)
            ws = [jnp.exp(x - mm) for x in ls]
            oa = sum(w * o for w, o in zip(ws, os_)) / sum(ws)
        else:
            oa = oa_refs[0][rs, :]
        oab, obb = oa.astype(BF16), ob_ref[rs, :]
        for c in range(D_MODEL // MERGE_TC):
            cs = slice(c * MERGE_TC, (c + 1) * MERGE_TC)
            br_ref[rs, cs] = (jax.nn.sigmoid(pg_a_ref[rs, cs]) * _dot(oab, wpa_ref[:, cs])
                              + jax.nn.sigmoid(pg_b_ref[rs, cs]) * _dot(obb, wpb_ref[:, cs])).astype(BF16)
        for c in range(D_MODEL // MERGE_TC):
            cs = slice(c * MERGE_TC, (c + 1) * MERGE_TC)
            u_ref[rs, cs] = DEEPNORM_ALPHA * x_ref[rs, cs] + _dot(br_ref[rs, :], wo_ref[:, cs])
        x1 = _layer_norm(u_ref[rs, :], g1_ref[...], b1_ref[...])
        x1_ref[rs, :] = x1
        xh = x1.astype(BF16)
        xhf = xh.astype(F32)
        bits = lax.bitcast_convert_type(xhf, jnp.int32)
        x1p_ref[rs, :] = lax.shift_right_logical(bits[:, :half], 16) | bits[:, half:]
        xl = (x1 - xhf).astype(BF16)
        logits = _dot(xh, rwh_ref[...]) + _dot(xl, rwh_ref[...]) + _dot(xh, rwl_ref[...]) + rb_ref[...]
        lane = lax.broadcasted_iota(jnp.int32, logits.shape, 1)
        vals = logits
        top_v, top_i = [], []
        for _ in range(TOP_K):
            m = jnp.max(vals, axis=-1, keepdims=True)
            ik = jnp.min(jnp.where(vals == m, lane, LANE), axis=-1, keepdims=True)
            vals = jnp.where(lane == ik, -jnp.inf, vals)
            top_v.append(m)
            top_i.append(ik)
        es = [jnp.exp(v - top_v[0]) for v in top_v]
        tot = functools.reduce(lambda a, b: a + b, es)
        ti_ref[rs, :] = jnp.concatenate(top_i, axis=1)
        gt_ref[rs, :] = jnp.concatenate([e / tot for e in es], axis=1)
        return [jnp.where(lane == ik, 1.0, 0.0) for ik in top_i]

    picks = rows_part(slice(0, tm))
    @pl.when(pl.program_id(0) == 0)
    def _():
        cnt_ref[...] = cnt0_ref[...]

    onehot = picks[0] + picks[1] + picks[2] + picks[3]
    tri = jnp.where(lax.broadcasted_iota(jnp.int32, (tm, tm), 0) > lax.broadcasted_iota(jnp.int32, (tm, tm), 1), 1.0, 0.0)
    before = _dot(tri.astype(BF16), onehot.astype(BF16)) + cnt_ref[...]
    rank_ref[...] = jnp.concatenate([jnp.sum(pk * before, axis=-1, keepdims=True) for pk in picks], axis=1).astype(jnp.int32)
    cnt_ref[...] = cnt_ref[...] + jnp.sum(onehot, axis=0, keepdims=True)


def _merge(x, oas, ob, pg, wpa, wpb, wo, g1, b1, rwh, rwl, rbp, cnt0, tm):
    T = x.shape[0]
    n_groups = len(oas) // 2 if len(oas) > 1 else 1

    def row(w):
        return pl.BlockSpec((tm, w), lambda i: (i, 0))

    def const(a):
        return pl.BlockSpec(a.shape, lambda i: (0,) * a.ndim, pipeline_mode=pl.Buffered(1))

    def oa_spec(a):
        return pl.BlockSpec((H_A, tm, HEAD_DIM), lambda i: (0, i, 0)) if a.ndim == 3 else row(A_WIDTH)

    in_specs = ([row(D_MODEL)] + [oa_spec(a) for a in oas] + [row(H_B * DV_B), row(D_MODEL),
                pl.BlockSpec((tm, D_MODEL), lambda i: (i, 1))]
                + [const(a) for a in (wpa, wpb, wo, g1, b1, rwh, rwl, rbp, cnt0)])
    return pl.pallas_call(
        functools.partial(_merge_kernel, n_groups=n_groups),
        grid=(T // tm,),
        in_specs=in_specs,
        out_specs=[row(D_MODEL), row(D_MODEL // 2), row(TOP_K), row(TOP_K), row(TOP_K),
                   pl.BlockSpec((1, LANE), lambda i: (0, 0))],
        out_shape=[jax.ShapeDtypeStruct((T, D_MODEL), F32), jax.ShapeDtypeStruct((T, D_MODEL // 2), jnp.int32),
                   jax.ShapeDtypeStruct((T, TOP_K), jnp.int32), jax.ShapeDtypeStruct((T, TOP_K), F32),
                   jax.ShapeDtypeStruct((T, TOP_K), jnp.int32), jax.ShapeDtypeStruct((1, LANE), F32)],
        scratch_shapes=[pltpu.VMEM((tm, D_MODEL), BF16), pltpu.VMEM((tm, D_MODEL), F32)],
        compiler_params=_cparams(("arbitrary",)),
        name=f"merge_g{n_groups}",
    )(x, *oas, ob, pg, pg, wpa, wpb, wo, g1, b1, rwh, rwl, rbp, cnt0)


def _sc_gather(table, idx):
    info = plsc.get_sparse_core_info()
    n_workers = info.num_cores * info.num_subcores
    n, width = idx.shape[0], table.shape[1]
    per_worker = n // n_workers
    chunk = SC_CHUNK_BYTES // (width * table.dtype.itemsize)
    assert per_worker * n_workers == n and per_worker % chunk == 0 and chunk % 8 == 0
    mesh = plsc.VectorSubcoreMesh(core_axis_name="c", subcore_axis_name="s")

    @functools.partial(
        pl.kernel, mesh=mesh,
        out_type=jax.ShapeDtypeStruct((n, width), table.dtype),
        scratch_types=[pltpu.VMEM((chunk,), jnp.int32), pltpu.VMEM((chunk, width), table.dtype),
                       pltpu.SemaphoreType.DMA],
    )
    def gather(table_hbm, idx_hbm, out_hbm, idx_v, rows_v, sem):
        wid = lax.axis_index("s") * info.num_cores + lax.axis_index("c")
        base = wid * per_worker

        @pl.loop(0, per_worker // chunk)
        def _(c):
            off = pl.multiple_of(base + c * chunk, chunk)
            pltpu.sync_copy(idx_hbm.at[pl.ds(off, chunk)], idx_v)
            pltpu.async_copy(table_hbm.at[idx_v], rows_v, sem).wait()
            pltpu.sync_copy(rows_v, out_hbm.at[pl.ds(off, chunk)])

    return gather(table, idx)


def _sc_scatter_rows(tables, idx, n_out):
    info = plsc.get_sparse_core_info()
    n_workers = info.num_cores * info.num_subcores
    n_idx, n = idx.shape
    width, dtype = tables[0].shape[1], tables[0].dtype
    max_rows = SC_SCATTER_BYTES // (width * dtype.itemsize)
    plan, tok0 = [], 0
    for t in tables:
        per_worker = t.shape[0] // n_workers
        assert per_worker * n_workers == t.shape[0] and per_worker % 8 == 0 and tok0 % 8 == 0
        plan.append((per_worker, max(d for d in range(8, max_rows + 1, 8) if per_worker % d == 0), tok0))
        tok0 += t.shape[0]
    assert tok0 == n
    mesh = plsc.VectorSubcoreMesh(core_axis_name="c", subcore_axis_name="s")
    scratch_types = []
    for _, chunk, _ in plan:
        scratch_types += [pltpu.VMEM((chunk,), jnp.int32)] * n_idx + [pltpu.VMEM((chunk, width), dtype)]

    @functools.partial(pl.kernel, mesh=mesh, out_type=jax.ShapeDtypeStruct((n_out, width), dtype),
                       scratch_types=scratch_types)
    def scatter(*refs):
        table_refs, idx_hbm, out_hbm = refs[:len(tables)], refs[len(tables)], refs[len(tables) + 1]
        scratch = refs[len(tables) + 2:]
        wid = lax.axis_index("s") * info.num_cores + lax.axis_index("c")
        for j, (per_worker, chunk, first) in enumerate(plan):
            idx_vs = scratch[j * (n_idx + 1):j * (n_idx + 1) + n_idx]
            rows_v = scratch[j * (n_idx + 1) + n_idx]
            table_hbm = table_refs[j]

            @pl.loop(0, per_worker // chunk)
            def _(c):
                off = pl.multiple_of(wid * per_worker + c * chunk, 8)
                pltpu.sync_copy(table_hbm.at[pl.ds(off, chunk)], rows_v)
                for k in range(n_idx):
                    pltpu.sync_copy(idx_hbm.at[pl.ds(pl.multiple_of(k * n + first + off, 8), chunk)], idx_vs[k])
                for k in range(n_idx):
                    pltpu.sync_copy(rows_v, out_hbm.at[idx_vs[k]])

    return scatter(*tables, idx.reshape(n_idx * n))


def _expert_kernel(be_ref, rows_ref, nu_ref, xs_hbm, wg_ref, wu_ref, bg_ref, bu_ref, wd_ref, bd_ref, o_ref,
                   hid_ref, xs_ref, xs_sem):
    del be_ref
    i = pl.program_id(0)
    p = pl.program_id(1)
    nrows = rows_ref[i]
    n_live = (nrows + (MOE_SB - 1)) // MOE_SB
    half = D_MODEL // 2

    def xs_copy(blk):
        return pltpu.make_async_copy(xs_hbm.at[pl.ds(pl.multiple_of(blk * MOE_BM, MOE_BM), MOE_BM), :], xs_ref, xs_sem)

    @pl.when(jnp.logical_and(i == 0, p == 0))
    def _():
        xs_copy(0).start()

    @pl.when(p == 0)
    def _():
        xs_copy(i).wait()

    @pl.when(jnp.logical_and(p == MOE_NF, i + 1 < nu_ref[0]))
    def _():
        xs_copy(i + 1).start()

    def paired(one):
        def pair(j, carry):
            one(2 * j)
            one(2 * j + 1)
            return carry

        lax.fori_loop(0, n_live // 2, pair, 0)

        @pl.when(n_live % 2 == 1)
        def _():
            one(n_live - 1)

    @pl.when(p < MOE_NF)
    def _():
        def gate_up(s):
            r0 = pl.multiple_of(s * MOE_SB, MOE_SB)
            rid = r0 + lax.broadcasted_iota(jnp.int32, (MOE_SB, half), 0)
            packed = jnp.where(rid < nrows, xs_ref[pl.ds(r0, MOE_SB), :], 0)
            lo = lax.bitcast_convert_type(lax.shift_left(packed, 16), F32)
            hi = lax.bitcast_convert_type(packed & jnp.int32(-65536), F32)
            x = jnp.concatenate([lo.astype(BF16), hi.astype(BF16)], axis=1)
            g = jnp.minimum(_dot(x, wg_ref[...].astype(BF16)) + bg_ref[...], SWIGLU_LIMIT)
            u = jnp.clip(_dot(x, wu_ref[...].astype(BF16)) + bu_ref[...], -SWIGLU_LIMIT, SWIGLU_LIMIT)
            hid_ref[p, pl.ds(r0, MOE_SB), :] = ((u + 1.0) * g * jax.nn.sigmoid(SWIGLU_ALPHA * g)).astype(BF16)

        paired(gate_up)

    @pl.when(p >= MOE_NF)
    def _():
        def down(s):
            r0 = pl.multiple_of(s * MOE_SB, MOE_SB)
            y = bd_ref[...]
            for f in range(MOE_NF):
                y = y + _dot(hid_ref[f, pl.ds(r0, MOE_SB), :], wd_ref[f * MOE_TF:(f + 1) * MOE_TF, :].astype(BF16))
            bits = lax.bitcast_convert_type(y.astype(BF16).astype(F32), jnp.int32)
            o_ref[pl.ds(r0, MOE_SB), :] = lax.shift_right_logical(bits[:, :MOE_TN // 2], 16) | bits[:, MOE_TN // 2:]

        def zero_body(s, carry):
            o_ref[pl.ds(pl.multiple_of(s * MOE_SB, MOE_SB), MOE_SB), :] = jnp.zeros((MOE_SB, MOE_TN // 2), jnp.int32)
            return carry

        paired(down)
        lax.fori_loop(n_live, MOE_BM // MOE_SB, zero_body, 0)


def _experts(xs, block_expert, block_rows, n_used, w_gate_up, b_gate_up, w_down, b_down):
    nf, nn = MOE_NF, D_MODEL // MOE_TN

    def gate_map(col0, lead):
        def index_map(i, p, be, rw, nu):
            ahead = p >= nf + nn - lead
            e = jnp.where(ahead, be[jnp.minimum(i + 1, nu[0] - 1)], be[i])
            return (e, 0, col0 + jnp.where(ahead, 0, jnp.minimum(p, nf - 1)))
        return index_map

    def down_map(i, p, be, rw, nu):
        parked = p < nf
        e = jnp.where(parked, be[jnp.maximum(i - 1, 0)], be[i])
        return (e, 0, jnp.where(parked, jnp.where(i > 0, nn - 1, 0), p - nf))

    grid_spec = pltpu.PrefetchScalarGridSpec(
        num_scalar_prefetch=3,
        grid=(n_used[0], nf + nn),
        in_specs=[
            pl.BlockSpec(memory_space=pl.ANY),
            pl.BlockSpec((None, D_MODEL, MOE_TF), gate_map(0, nn // 2)),
            pl.BlockSpec((None, D_MODEL, MOE_TF), gate_map(nf, nn // 4)),
            pl.BlockSpec((None, 1, MOE_TF), gate_map(0, nn // 2)),
            pl.BlockSpec((None, 1, MOE_TF), gate_map(nf, nn // 4)),
            pl.BlockSpec((None, D_FF, MOE_TN), down_map),
            pl.BlockSpec((None, 1, MOE_TN), down_map),
        ],
        out_specs=pl.BlockSpec((MOE_BM, MOE_TN // 2), lambda i, p, be, rw, nu: (i, jnp.maximum(p - nf, 0))),
        scratch_shapes=[pltpu.VMEM((nf, MOE_BM, MOE_TF), BF16), pltpu.VMEM((MOE_BM, D_MODEL // 2), jnp.int32),
                        pltpu.SemaphoreType.DMA],
    )
    bgu = b_gate_up.reshape(N_EXPERTS, 1, 2 * D_FF)
    bd = b_down.reshape(N_EXPERTS, 1, D_MODEL)
    return pl.pallas_call(
        _expert_kernel,
        grid_spec=grid_spec,
        out_shape=jax.ShapeDtypeStruct((xs.shape[0], D_MODEL // 2), jnp.int32),
        compiler_params=_cparams(("arbitrary", "arbitrary")),
        name="moe_experts",
    )(block_expert, block_rows, n_used, xs, w_gate_up, w_gate_up, bgu, bgu, w_down, bd)


def _route(top_i, rank, counts, n_blocks):
    T = top_i.shape[0]
    bpe = (counts + MOE_BM - 1) // MOE_BM
    bend = jnp.cumsum(bpe)
    bstart = bend - bpe
    experts = jnp.arange(N_EXPERTS, dtype=jnp.int32)
    start_of = jnp.sum(jnp.where(top_i[:, :, None] == experts, bstart * MOE_BM, 0), axis=-1)
    dest = (start_of + rank).astype(jnp.int32)
    n_used = bend[-1]
    blk = jnp.arange(n_blocks, dtype=jnp.int32)
    be = jnp.minimum(jnp.searchsorted(bend, jnp.minimum(blk, n_used - 1), side="right"), N_EXPERTS - 1).astype(jnp.int32)
    rows = jnp.clip(counts[be] - (blk - bstart[be]) * MOE_BM, 0, MOE_BM)
    rows = jnp.where(blk < n_used, rows, 0).astype(jnp.int32)
    return dest.reshape(T, TOP_K).T, be, rows, n_used.reshape(1).astype(jnp.int32)


def _combine_kernel(x1_ref, ge_ref, gt_ref, g2_ref, b2_ref, o_ref):
    hw, nn = MOE_TN // 2, D_MODEL // MOE_TN

    def unpack(w):
        lo = lax.bitcast_convert_type(lax.shift_left(w, 16), F32)
        hi = lax.bitcast_convert_type(w & jnp.int32(-65536), F32)
        return jnp.concatenate([part[:, n * hw:(n + 1) * hw] for n in range(nn) for part in (lo, hi)], axis=1)

    gt = gt_ref[...]
    m = gt[:, 0:1] * unpack(ge_ref[0])
    for k in range(1, TOP_K):
        m = m + gt[:, k:k + 1] * unpack(ge_ref[k])
    o_ref[...] = _layer_norm(DEEPNORM_ALPHA * x1_ref[...] + m, g2_ref[...], b2_ref[...])


def _combine(x1, ge, gate, g2, b2, row0, tm):
    n_rows = x1.shape[0]
    b0 = row0 // tm
    return pl.pallas_call(
        _combine_kernel,
        grid=(n_rows // tm,),
        in_specs=[pl.BlockSpec((tm, D_MODEL), lambda i: (i, 0)),
                  pl.BlockSpec((TOP_K, tm, D_MODEL // 2), lambda i: (0, b0 + i, 0)),
                  pl.BlockSpec((tm, TOP_K), lambda i: (b0 + i, 0)),
                  pl.BlockSpec(g2.shape, lambda i: (0, 0)), pl.BlockSpec(b2.shape, lambda i: (0, 0))],
        out_specs=pl.BlockSpec((tm, D_MODEL), lambda i: (i, 0)),
        out_shape=jax.ShapeDtypeStruct((n_rows, D_MODEL), F32),
        compiler_params=_cparams(("parallel",)),
        name="moe_combine",
    )(x1, ge, gate, g2, b2)


def _layer(xp, xs, caches, state, w_in, rel_bias, gla_w_up, gla_b, gla_norm_g, w_pa, w_pb, w_o, ln1_g, ln1_b,
           router_w, router_b, w_gate_up, b_gate_up, w_down, b_down, ln2_g, ln2_b):
    batch, seq, _ = xp.shape
    dbatch, dseq, _ = xs.shape
    tp, ts = batch * seq, dbatch * dseq
    xp2, xs2 = xp.reshape(tp, D_MODEL), xs.reshape(ts, D_MODEL)

    o_b0, o_lr, o_g = PA_W, PA_W + PB_LR, PA_W + PB_LR + GLA_RANK
    w_a = w_in[:, :PA_W].astype(BF16)
    w_b = jnp.concatenate([w_in[:, o_b0:o_g], jnp.zeros((D_MODEL, LANE - GLA_RANK), F32)], axis=1).astype(BF16)
    w_g = w_in[:, o_g:].astype(BF16)
    wup = jnp.concatenate([gla_w_up, jnp.zeros((LANE - GLA_RANK, H_B * DK_B), F32)], axis=0).astype(BF16)
    gb = gla_b.reshape(1, H_B * DK_B)
    ng = gla_norm_g.reshape(1, DV_B)
    wpa, wpb, wo = w_pa.astype(BF16), w_pb.astype(BF16), w_o.astype(BF16)
    g1, b1 = ln1_g.reshape(1, D_MODEL), ln1_b.reshape(1, D_MODEL)
    g2, b2 = ln2_g.reshape(1, D_MODEL), ln2_b.reshape(1, D_MODEL)
    rw = jnp.concatenate([router_w, jnp.zeros((D_MODEL, LANE - N_EXPERTS), F32)], axis=1)
    rwh = rw.astype(BF16)
    rwl = (rw - rwh.astype(F32)).astype(BF16)
    rbp = jnp.concatenate([router_b, jnp.full((LANE - N_EXPERTS,), NEG, F32)]).reshape(1, LANE)
    caches8 = [c.reshape(dbatch, c.shape[1], KV_ROWS, HEAD_DIM) for c in caches]

    pa_p = _project(xp2, w_a, PROJ_TM, A_QKV_WIDTH, "proj_a_prompt", head_major=True)
    pb_p = _project(xp2, w_b, PROJ_TM, PB_W, "proj_b_prompt")
    pg_p = _project(xp2, w_g, PROJ_TM, 1024, "proj_g_prompt")
    oas, lses = [], []
    for g in range(N_GROUPS):
        o, lse = _attn_heads(pa_p, _prompt_table(rel_bias, g), g, batch, seq)
        oas.append(o)
        lses.append(lse)
    ob_p, st_p = _gla_heads(pb_p, wup, gb, ng, jnp.zeros((batch, H_B, DK_B, DV_B), F32), batch, seq, GLA_CHUNK, 256)
    x1_p, x1p_p, ti_p, gt_p, rk_p, cnt_p = _merge(xp2, oas + lses, ob_p, pg_p, wpa, wpb, wo, g1, b1, rwh, rwl, rbp,
                                                  jnp.zeros((1, LANE), F32), 256)

    pa_s = _project(xs2, w_a, ts, A_QKV_WIDTH, "proj_a_sample", head_major=True)
    pb_s = _project(xs2, w_b, ts, PB_W, "proj_b_sample")
    pg_s = _project(xs2, w_g, ts, 1024, "proj_g_sample")
    tabc, tabn, combo_base = _sample_tables(rel_bias, dseq)
    oa_s, new_rows = _attn_sample(pa_s, caches8, tabc, tabn, combo_base, dbatch, dseq)
    chunk_s = int(np.gcd(dseq, GLA_CHUNK))
    ob_s, st_s = _gla_heads(pb_s, wup, gb, ng, state, dbatch, dseq, chunk_s, dseq)
    x1_s, x1p_s, ti_s, gt_s, rk_s, cnt_all = _merge(xs2, [oa_s], ob_s, pg_s, wpa, wpb, wo, g1, b1, rwh, rwl, rbp, cnt_p, ts)

    top_i = jnp.concatenate([ti_p, ti_s], axis=0)
    gate = jnp.concatenate([gt_p, gt_s], axis=0)
    t_all = tp + ts
    n_blocks = -(-(t_all * TOP_K) // MOE_BM) + N_EXPERTS
    rank = jnp.concatenate([rk_p, rk_s], axis=0)
    counts = cnt_all[0, :N_EXPERTS].astype(jnp.int32)
    dest, be, rows, n_used = _route(top_i, rank, counts, n_blocks)
    xsorted = _sc_scatter_rows([x1p_p, x1p_s], dest, n_blocks * MOE_BM)
    keeps = [min(window, seq) for window, _ in DILATED_GROUPS]
    packed_p = {g: _kv_pack(pa_p, g, batch, seq, keeps[g], Q_BLOCK) for g in range(N_GROUPS - 1)}
    eo = _experts(xsorted, be, rows, n_used, w_gate_up, b_gate_up, w_down, b_down)
    ge = _sc_gather(eo, dest.reshape(TOP_K * t_all)).reshape(TOP_K, t_all, D_MODEL // 2)
    packed_p[N_GROUPS - 1] = _kv_pack(pa_p, N_GROUPS - 1, batch, seq, keeps[N_GROUPS - 1], Q_BLOCK)
    y_p = _combine(x1_p, ge, gate, g2, b2, 0, 256)
    y_s = _combine(x1_s, ge, gate, g2, b2, tp, 256)

    shifted = _sc_cache_shift(caches8, dseq)
    bufs_p, bufs_s = [], []
    for g, (window, _) in enumerate(DILATED_GROUPS):
        keep = keeps[g]
        bufs_p.append(packed_p[g].reshape(batch, keep, 2, H_A, HEAD_DIM))
        clen = caches[g].shape[1]
        assert clen == window and dseq <= clen
        buf = lax.dynamic_update_slice(shifted[g], new_rows[g], (0, clen - dseq, 0, 0))
        bufs_s.append(buf.reshape(dbatch, clen, 2, H_A, HEAD_DIM))
    return y_p.reshape(batch, seq, D_MODEL), y_s.reshape(dbatch, dseq, D_MODEL), bufs_p, st_p, bufs_s, st_s


def kernel(x_prompt, x_sample, cache_a1_kv, cache_a2_kv, cache_a3_kv, state_b_s, w_in, rel_bias, gla_w_up, gla_b,
           gla_norm_g, w_pa, w_pb, w_o, ln1_g, ln1_b, router_w, router_b, w_gate_up, b_gate_up, w_down, b_down,
           ln2_g, ln2_b):
    assert w_in.shape[0] == DEPTH
    yp, ys, bufs_p, st_p, bufs_s, st_s = _layer(
        x_prompt, x_sample, (cache_a1_kv[0], cache_a2_kv[0], cache_a3_kv[0]), state_b_s[0], w_in[0], rel_bias,
        gla_w_up[0], gla_b[0], gla_norm_g[0], w_pa[0], w_pb[0], w_o[0], ln1_g[0], ln1_b[0], router_w[0], router_b[0],
        w_gate_up[0], b_gate_up[0], w_down[0], b_down[0], ln2_g[0], ln2_b[0])
    return (yp, ys, bufs_p[0][None], bufs_p[1][None], bufs_p[2][None], st_p[None],
            bufs_s[0][None], bufs_s[1][None], bufs_s[2][None], st_s[None].astype(state_b_s.dtype))
```

```python
import functools

import numpy as np
import jax
import jax.numpy as jnp
from jax import lax
from jax.experimental import pallas as pl
from jax.experimental.pallas import tpu as pltpu
from jax.experimental.pallas import tpu_sc as plsc

F32 = jnp.float32
BF16 = jnp.bfloat16

D_MODEL = 2048
HEAD_DIM = 128
DILATED_GROUPS = ((128, 1), (512, 4), (2048, 16))
N_GROUPS = 3
H_A = 4
A_WIDTH = H_A * HEAD_DIM
A_QKV_WIDTH = N_GROUPS * A_WIDTH
Q_BLOCK = 128
N_BUCKETS = 32
REL_MAX_DIST = 2048
H_B = 4
DK_B = 64
DV_B = 128
GLA_RANK = 16
GLA_TAU = 16.0
GLA_CHUNK = 64
GLA_SUB = 16
GLA_EXP_CLAMP = 80.0
N_EXPERTS = 32
TOP_K = 4
D_FF = 2048
SWIGLU_LIMIT = 7.0
SWIGLU_ALPHA = 1.702
LN_EPS = 1e-5
RMS_EPS = 1e-6
DEPTH = 1
DEEPNORM_ALPHA = (2.0 * DEPTH) ** 0.25
ATT_SCALE = HEAD_DIM ** -0.5
NEG = float(np.finfo(np.float32).min)

VMEM_LIMIT = 56 * 1024 * 1024
LANE = 128

PA_W = 3 * A_QKV_WIDTH
PB_LR = H_B * DK_B * 2 + H_B * DV_B * 2
PB_W = PB_LR + LANE
PG_W = 2 * D_MODEL

PROJ_TM = 1024

MOE_BM = 2560
MOE_SB = 320
MOE_TF = 512
MOE_NF = D_FF // MOE_TF
MOE_TN = 512
MERGE_TC = 512
SC_CHUNK_BYTES = 128 * 1024
SC_SCATTER_BYTES = 160 * 1024


def _cparams(sem):
    return pltpu.CompilerParams(dimension_semantics=sem, vmem_limit_bytes=VMEM_LIMIT)


def _dot(a, b):
    return jnp.dot(a, b, preferred_element_type=F32)


def _dot_nt(a, b):
    return lax.dot_general(a, b, (((1,), (1,)), ((), ())), preferred_element_type=F32)


def _dot_tn(a, b):
    return lax.dot_general(a, b, (((0,), (0,)), ((), ())), preferred_element_type=F32)


def _proj_kernel(x_ref, w_ref, o_ref, xb_ref):
    @pl.when(pl.program_id(1) == 0)
    def _():
        xb_ref[...] = x_ref[...].astype(BF16)

    o_ref[...] = _dot(xb_ref[...], w_ref[...])


def _proj_heads_kernel(x_ref, w_ref, o_ref, xb_ref):
    @pl.when(pl.program_id(1) == 0)
    def _():
        xb_ref[...] = x_ref[...].astype(BF16)

    acc = _dot(xb_ref[...], w_ref[...])
    for c in range(o_ref.shape[0]):
        o_ref[c] = acc[:, c * HEAD_DIM:(c + 1) * HEAD_DIM]


def _project(x, w, tm, tn, name, head_major=False):
    T, D = x.shape
    N = w.shape[1]
    if head_major:
        nh = tn // HEAD_DIM
        out_spec = pl.BlockSpec((nh, tm, HEAD_DIM), lambda i, j: (j, i, 0))
        out_shape = jax.ShapeDtypeStruct((N // HEAD_DIM, T, HEAD_DIM), F32)
    else:
        out_spec = pl.BlockSpec((tm, tn), lambda i, j: (i, j))
        out_shape = jax.ShapeDtypeStruct((T, N), F32)
    return pl.pallas_call(
        _proj_heads_kernel if head_major else _proj_kernel,
        grid=(T // tm, N // tn),
        in_specs=[pl.BlockSpec((tm, D), lambda i, j: (i, 0)), pl.BlockSpec((D, tn), lambda i, j: (0, j))],
        out_specs=out_spec,
        out_shape=out_shape,
        scratch_shapes=[pltpu.VMEM((tm, D), BF16)],
        compiler_params=_cparams(("parallel", "arbitrary")),
        name=name,
    )(x, w)


def _t5_bucket(dist):
    max_exact = N_BUCKETS // 2
    d = np.maximum(dist, 1).astype(np.float32)
    large = max_exact + (np.log(d / max_exact) / np.log(REL_MAX_DIST / max_exact) * (N_BUCKETS - max_exact)).astype(np.int32)
    large = np.minimum(large, N_BUCKETS - 1)
    return np.where(dist < max_exact, dist, large).astype(np.int32)


def _bias_lookup(rel_bias, g, j, valid):
    _, dil = DILATED_GROUPS[g]
    bucket = _t5_bucket(dil * np.clip(j, 0, Q_BLOCK))
    onehot = bucket[None] == np.arange(N_BUCKETS).reshape((N_BUCKETS,) + (1,) * j.ndim)
    rb = rel_bias[:, g * H_A:(g + 1) * H_A].astype(F32).T.reshape((H_A, N_BUCKETS) + (1,) * j.ndim)
    vals = jnp.sum(jnp.where(onehot[None], rb, 0.0), axis=1)
    return jnp.where(valid[None], vals, NEG)


def _prompt_table(rel_bias, g):
    qi = np.arange(Q_BLOCK)[:, None]
    kj = np.arange(2 * Q_BLOCK)[None, :]
    j = Q_BLOCK + qi - kj
    return _bias_lookup(rel_bias, g, j, (j >= 0) & (j <= Q_BLOCK))


def _sample_tables(rel_bias, dec_seq):
    m = np.arange(Q_BLOCK)
    tabc, combo_base = [], []
    for g, (_, dil) in enumerate(DILATED_GROUPS):
        combo_base.append(len(tabc))
        for fl in range((dec_seq - 1) // dil + 1):
            j = Q_BLOCK + fl - m
            col = _bias_lookup(rel_bias, g, j, j <= Q_BLOCK).T
            col = jnp.concatenate([col, jnp.zeros_like(col)], axis=1)
            tabc.append(jnp.broadcast_to(col[:, :, None], (Q_BLOCK, 2 * H_A, LANE)))
    tabn = []
    s = np.arange(dec_seq)[:, None]
    sp = np.arange(dec_seq)[None, :]
    for g, (_, dil) in enumerate(DILATED_GROUPS):
        diff = s - sp
        t = _bias_lookup(rel_bias, g, diff // dil, (diff >= 0) & (diff % dil == 0))
        t = jnp.transpose(t, (1, 2, 0))
        t = jnp.concatenate([t, jnp.zeros_like(t)], axis=2)
        tabn.append(jnp.broadcast_to(t[..., None], (dec_seq, dec_seq, 2 * H_A, LANE)))
    return jnp.stack(tabc), jnp.stack(tabn), tuple(combo_base)


def _attn_heads_kernel(*refs, dil, nb, with_prev):
    if with_prev:
        q_ref, kc_ref, kp_ref, vc_ref, vp_ref, tab_ref, o_ref, lse_ref = refs
    else:
        q_ref, kc_ref, vc_ref, tab_ref, o_ref, lse_ref = refs
    has_prev = pl.program_id(1) > 0
    zero = jnp.zeros((Q_BLOCK, HEAD_DIM), BF16)
    tab_c = tab_ref[:, Q_BLOCK:]
    if with_prev:
        tab = jnp.concatenate([jnp.where(has_prev, tab_ref[:, :Q_BLOCK], NEG), tab_c], axis=1)
    else:
        tab = tab_c

    def heads(ref, bi, idx):
        return jnp.concatenate([ref[h, bi, idx, :].astype(BF16) for h in range(H_A)], axis=1)

    def body(u, carry):
        bi, r = u // dil, u % dil
        idx = pl.ds(r, Q_BLOCK, stride=dil)
        qs = [q_ref[h, bi, idx, :].astype(BF16) for h in range(H_A)]
        q_stack = jnp.concatenate(
            [jnp.concatenate([qs[h] if c == h else zero for c in range(H_A)], axis=1) for h in range(H_A)], axis=0)
        if with_prev:
            k_all = jnp.concatenate([heads(kp_ref, bi, idx), heads(kc_ref, bi, idx)], axis=0)
            v_all = jnp.concatenate([heads(vp_ref, bi, idx), heads(vc_ref, bi, idx)], axis=0)
        else:
            k_all, v_all = heads(kc_ref, bi, idx), heads(vc_ref, bi, idx)
        s = _dot_nt(q_stack, k_all) * ATT_SCALE + tab
        m = jnp.max(s, axis=-1, keepdims=True)
        p = jnp.exp(s - m)
        l = jnp.sum(p, axis=-1, keepdims=True)
        o = _dot((p * (1.0 / l)).astype(BF16), v_all)
        lse = m + jnp.log(l)
        for h in range(H_A):
            rows = slice(h * Q_BLOCK, (h + 1) * Q_BLOCK)
            o_ref[h, bi, idx, :] = o[rows, h * HEAD_DIM:(h + 1) * HEAD_DIM]
            lse_ref[h, bi, idx, :] = jnp.broadcast_to(lse[rows], (Q_BLOCK, HEAD_DIM))
        return carry

    lax.fori_loop(0, nb * dil, body, 0, unroll=min(nb * dil, 4))


def _attn_heads(pa_hm, table, g, batch, seq):
    _, dil = DILATED_GROUPS[g]
    rows = dil * Q_BLOCK
    nblk = seq // rows
    with_prev = nblk > 1
    nb = 4 if (dil == 1 and batch % 4 == 0) else 1
    pv = pa_hm.reshape(pa_hm.shape[0], batch, seq, HEAD_DIM)

    def spec(sec, prev):
        if prev:
            return pl.BlockSpec((H_A, nb, rows, HEAD_DIM), lambda b, i: (sec * N_GROUPS + g, b, jnp.maximum(i - 1, 0), 0))
        return pl.BlockSpec((H_A, nb, rows, HEAD_DIM), lambda b, i: (sec * N_GROUPS + g, b, i, 0))

    in_specs = [spec(0, False), spec(1, False)] + ([spec(1, True)] if with_prev else []) + [spec(2, False)] + (
        [spec(2, True)] if with_prev else []) + [pl.BlockSpec((H_A * Q_BLOCK, 2 * Q_BLOCK), lambda b, i: (0, 0))]
    out_spec = pl.BlockSpec((H_A, nb, rows, HEAD_DIM), lambda b, i: (0, b, i, 0))
    o, lse = pl.pallas_call(
        functools.partial(_attn_heads_kernel, dil=dil, nb=nb, with_prev=with_prev),
        grid=(batch // nb, nblk),
        in_specs=in_specs,
        out_specs=[out_spec, out_spec],
        out_shape=[jax.ShapeDtypeStruct((H_A, batch, seq, HEAD_DIM), F32)] * 2,
        compiler_params=_cparams(("parallel", "arbitrary")),
        name=f"attn_prompt_g{g}",
    )(*([pv] * (len(in_specs) - 1)), table.reshape(H_A * Q_BLOCK, 2 * Q_BLOCK))
    return o.reshape(H_A, batch * seq, HEAD_DIM), lse.reshape(H_A, batch * seq, HEAD_DIM)


KV_ROWS = 2 * H_A


def _kv_pack_kernel(k_ref, v_ref, o_ref):
    o_ref[...] = jnp.stack([k_ref[h] for h in range(H_A)] + [v_ref[h] for h in range(H_A)], axis=1)


def _kv_pack(pa_hm, g, batch, seq, keep, tm):
    nblk, blk0, per_b = keep // tm, (seq - keep) // tm, seq // tm
    out = pl.pallas_call(
        _kv_pack_kernel,
        grid=(batch, nblk),
        in_specs=[pl.BlockSpec((H_A, tm, HEAD_DIM), lambda b, i: (N_GROUPS + g, b * per_b + blk0 + i, 0)),
                  pl.BlockSpec((H_A, tm, HEAD_DIM), lambda b, i: (2 * N_GROUPS + g, b * per_b + blk0 + i, 0))],
        out_specs=pl.BlockSpec((tm, KV_ROWS, HEAD_DIM), lambda b, i: (b * nblk + i, 0, 0)),
        out_shape=jax.ShapeDtypeStruct((batch * keep, KV_ROWS, HEAD_DIM), F32),
        compiler_params=_cparams(("parallel", "parallel")),
        name=f"kv_pack_g{g}_{keep}",
    )(pa_hm, pa_hm)
    return out.reshape(batch, keep, KV_ROWS, HEAD_DIM)


def _attn_sample_kernel(qkv_ref, c1_ref, c2_ref, c3_ref, tabc_ref, tabn_ref, o_ref, n1_ref, n2_ref, n3_ref, *,
                        dec_seq, combo_base):
    caches = (c1_ref, c2_ref, c3_ref)
    news = (n1_ref, n2_ref, n3_ref)
    nh = N_GROUPS * H_A
    for g in range(N_GROUPS):
        news[g][...] = jnp.stack([qkv_ref[nh + g * H_A + h] for h in range(H_A)]
                                 + [qkv_ref[2 * nh + g * H_A + h] for h in range(H_A)], axis=1)
    zeros = jnp.zeros((H_A, HEAD_DIM), F32)
    for s in range(dec_seq):
        outs, lses = [], []
        for g, (_, dil) in enumerate(DILATED_GROUPS):
            rho, fl = s % dil, s // dil
            qm = jnp.concatenate([qkv_ref[g * H_A + h, s:s + 1, :] for h in range(H_A)] + [zeros], axis=0)
            qm = qm * ATT_SCALE
            kc = caches[g][:, rho]
            kn = news[g][...]
            sc = jnp.sum(kc * qm[None], axis=-1, keepdims=True) + tabc_ref[combo_base[g] + fl]
            sn = jnp.sum(kn * qm[None], axis=-1, keepdims=True) + tabn_ref[g, s]
            m = jnp.maximum(jnp.max(sc, axis=0), jnp.max(sn, axis=0))
            pc = jnp.exp(sc - m[None])
            pn = jnp.exp(sn - m[None])
            l = jnp.sum(pc, axis=0) + jnp.sum(pn, axis=0)
            acc = jnp.sum(pltpu.roll(pc, H_A, 1) * kc, axis=0) + jnp.sum(pltpu.roll(pn, H_A, 1) * kn, axis=0)
            outs.append(acc / pltpu.roll(l, H_A, 0))
            lses.append(pltpu.roll(m + jnp.log(l), H_A, 0))
        mm = jnp.maximum(jnp.maximum(lses[0], lses[1]), lses[2])
        ws = [jnp.exp(x - mm) for x in lses]
        o_ref[s] = (ws[0] * outs[0] + ws[1] * outs[1] + ws[2] * outs[2]) / (ws[0] + ws[1] + ws[2])


def _attn_sample(pa_hm, caches, tabc, tabn, combo_base, batch, dec_seq):
    views, specs = [], []
    for g, (window, dil) in enumerate(DILATED_GROUPS):
        assert caches[g].shape[1] == window and dec_seq <= Q_BLOCK
        views.append(caches[g].reshape(batch, Q_BLOCK, dil, KV_ROWS, HEAD_DIM))
        used = min(dil, dec_seq)
        specs.append(pl.BlockSpec((None, Q_BLOCK, used, KV_ROWS, HEAD_DIM), lambda b: (b, 0, 0, 0, 0)))
    row_spec = pl.BlockSpec((None, dec_seq, KV_ROWS, HEAD_DIM), lambda b: (b, 0, 0, 0))
    row_shape = jax.ShapeDtypeStruct((batch, dec_seq, KV_ROWS, HEAD_DIM), F32)
    out, *new_rows = pl.pallas_call(
        functools.partial(_attn_sample_kernel, dec_seq=dec_seq, combo_base=combo_base),
        grid=(batch,),
        in_specs=[pl.BlockSpec((3 * N_GROUPS * H_A, dec_seq, HEAD_DIM), lambda b: (0, b, 0))] + specs + [
            pl.BlockSpec(tabc.shape, lambda b: (0, 0, 0, 0)), pl.BlockSpec(tabn.shape, lambda b: (0, 0, 0, 0, 0))],
        out_specs=[row_spec] * (1 + N_GROUPS),
        out_shape=[row_shape] * (1 + N_GROUPS),
        compiler_params=_cparams(("parallel",)),
        name="attn_sample",
    )(pa_hm, *views, tabc, tabn)
    return out[:, :, H_A:, :].reshape(batch * dec_seq, A_WIDTH), new_rows


def _sc_cache_shift(caches, drop):
    info = plsc.get_sparse_core_info()
    n_workers = info.num_cores * info.num_subcores
    batch = caches[0].shape[0]
    assert batch % n_workers == 0
    mesh = plsc.VectorSubcoreMesh(core_axis_name="c", subcore_axis_name="s")

    row_bytes = KV_ROWS * HEAD_DIM * 4
    chunks = []
    for c in caches:
        keep = c.shape[1] - drop
        ch = max(d for d in range(1, SC_CHUNK_BYTES // row_bytes + 1) if keep % d == 0)
        chunks.append(ch)
    buf_rows = max(chunks)

    @functools.partial(pl.kernel, mesh=mesh, out_type=[jax.ShapeDtypeStruct(c.shape, c.dtype) for c in caches],
                       scratch_types=[pltpu.VMEM((buf_rows, KV_ROWS, HEAD_DIM), caches[0].dtype)])
    def shift(*refs):
        srcs, dsts, buf = refs[:len(caches)], refs[len(caches):2 * len(caches)], refs[-1]
        wid = lax.axis_index("s") * info.num_cores + lax.axis_index("c")
        for j in range(batch // n_workers):
            b = wid * (batch // n_workers) + j
            for src, dst, ch in zip(srcs, dsts, chunks):
                stage = buf.at[pl.ds(0, ch)]

                @pl.loop(0, (src.shape[1] - drop) // ch)
                def _(i):
                    pltpu.sync_copy(src.at[b, pl.ds(drop + i * ch, ch)], stage)
                    pltpu.sync_copy(stage, dst.at[b, pl.ds(i * ch, ch)])

    return shift(*caches)


def _split3(x):
    hi = x.astype(BF16)
    r = x - hi.astype(F32)
    mid = r.astype(BF16)
    lo = (r - mid.astype(F32)).astype(BF16)
    return hi, mid, lo


def _gla_heads_kernel(p_ref, wup_ref, gb_ref, ng_ref, s0_ref, o_ref, st_ref, *, chunk, tb):
    @pl.when(pl.program_id(1) == 0)
    def _():
        st_ref[...] = s0_ref[...]

    sub = min(GLA_SUB, chunk)
    n_sub = chunk // sub
    kq, vw = H_B * DK_B, H_B * DV_B

    def rb(x):
        xb = x.astype(BF16)
        return xb if chunk >= 16 else xb.astype(F32)

    row = lax.broadcasted_iota(jnp.int32, (chunk, chunk), 0)
    colm = lax.broadcasted_iota(jnp.int32, (chunk, chunk), 1)
    tri = rb(jnp.where(row >= colm, 1.0, 0.0))
    lane_head = lax.broadcasted_iota(jnp.int32, (1, kq), 1) // DK_B
    diag_blocks = (lax.broadcasted_iota(jnp.int32, (vw, kq), 0) // DV_B
                   == lax.broadcasted_iota(jnp.int32, (vw, kq), 1) // DK_B)
    a_cols = LANE if n_sub > 1 else chunk
    t_in = lax.broadcasted_iota(jnp.int32, (H_B * chunk, a_cols), 0) % chunk
    s_in = lax.broadcasted_iota(jnp.int32, (H_B * chunk, a_cols), 1)
    for c in range(tb // chunk):
        rows = slice(c * chunk, (c + 1) * chunk)
        z = _dot(rb(p_ref[rows, PB_LR:PB_W]), rb(wup_ref[...])) + gb_ref[...]
        la = -(jnp.maximum(-z, 0.0) + jnp.log1p(jnp.exp(-jnp.abs(z)))) * (1.0 / GLA_TAU)
        b3 = _dot(tri, rb(jnp.concatenate(_split3(la), axis=1)))
        b = b3[:, :kq] + b3[:, kq:2 * kq] + b3[:, 2 * kq:]
        blast = b[chunk - 1:chunk, :]
        q = p_ref[rows, 0:kq] * (DK_B ** -0.5)
        k = p_ref[rows, kq:2 * kq]
        v = rb(p_ref[rows, 2 * kq:2 * kq + vw])
        st = st_ref[...]
        o_inter = _dot_nt(rb(q * jnp.exp(b)), rb(st))
        refs = [jnp.zeros((1, kq), F32)] + [b[i * sub - 1:i * sub, :] for i in range(1, n_sub)]
        ref_rows = jnp.concatenate([jnp.broadcast_to(r, (sub, kq)) for r in refs], axis=0)
        qs = q * jnp.exp(b - ref_rows)
        q_stack = rb(jnp.concatenate([jnp.where(lane_head == h, qs, 0.0) for h in range(H_B)], axis=0))
        k_parts = []
        for r in refs:
            k_parts.append(k * jnp.exp(jnp.minimum(r - b, GLA_EXP_CLAMP)))
            if a_cols > chunk:
                k_parts.append(jnp.zeros((a_cols - chunk, kq), F32))
        raw = _dot_nt(q_stack, rb(jnp.concatenate(k_parts, axis=0)))
        a = jnp.zeros((H_B * chunk, a_cols), F32)
        for i in range(n_sub):
            a = a + jnp.where(t_in // sub == i, raw[:, i * a_cols:(i + 1) * a_cols], 0.0)
        a = jnp.where(s_in <= t_in, a, 0.0)[:, :chunk]
        o_all = _dot(rb(a), v)
        o = o_inter + jnp.concatenate(
            [o_all[h * chunk:(h + 1) * chunk, h * DV_B:(h + 1) * DV_B] for h in range(H_B)], axis=1)
        st_ref[...] = st * jnp.exp(blast) + jnp.where(diag_blocks, _dot_tn(v, rb(k * jnp.exp(blast - b))), 0.0)
        normed = []
        for h in range(H_B):
            oh = o[:, h * DV_B:(h + 1) * DV_B]
            normed.append(oh * lax.rsqrt(jnp.mean(oh * oh, axis=-1, keepdims=True) + RMS_EPS) * ng_ref[...])
        rg = p_ref[rows, 2 * kq + vw:2 * kq + 2 * vw]
        o_ref[rows, :] = (jnp.concatenate(normed, axis=1) * (rg * jax.nn.sigmoid(rg))).astype(BF16)


def _gla_heads(pb, wup, gb, ng, s0, batch, seq, chunk, tb):
    kq, vw = H_B * DK_B, H_B * DV_B
    eye = jnp.eye(H_B, dtype=F32)
    s0t = jnp.swapaxes(s0, -1, -2)
    s0_bd = (s0t[:, :, :, None, :] * eye[None, :, None, :, None]).reshape(batch, vw, kq)
    p3 = pb.reshape(batch, seq, PB_W)
    o, st = pl.pallas_call(
        functools.partial(_gla_heads_kernel, chunk=chunk, tb=tb),
        grid=(batch, seq // tb),
        in_specs=[pl.BlockSpec((None, tb, PB_W), lambda b, i: (b, i, 0)),
                  pl.BlockSpec(wup.shape, lambda b, i: (0, 0)),
                  pl.BlockSpec(gb.shape, lambda b, i: (0, 0)),
                  pl.BlockSpec(ng.shape, lambda b, i: (0, 0)),
                  pl.BlockSpec((None, vw, kq), lambda b, i: (b, 0, 0))],
        out_specs=[pl.BlockSpec((None, tb, vw), lambda b, i: (b, i, 0)),
                   pl.BlockSpec((None, vw, kq), lambda b, i: (b, 0, 0))],
        out_shape=[jax.ShapeDtypeStruct((batch, seq, vw), BF16), jax.ShapeDtypeStruct((batch, vw, kq), F32)],
        compiler_params=_cparams(("parallel", "arbitrary")),
        name=f"gla_c{chunk}",
    )(p3, wup, gb, ng, s0_bd)
    st5 = st.reshape(batch, H_B, DV_B, H_B, DK_B)
    st_heads = jnp.stack([st5[:, h, :, h, :] for h in range(H_B)], axis=1)
    return o.reshape(batch * seq, vw), jnp.swapaxes(st_heads, -1, -2)


def _layer_norm(u, g, b):
    mu = jnp.mean(u, axis=-1, keepdims=True)
    d = u - mu
    var = jnp.mean(d * d, axis=-1, keepdims=True)
    return d * lax.rsqrt(var + LN_EPS) * g + b


def _merge_kernel(*refs, n_groups):
    x_ref = refs[0]
    oa_refs = refs[1:1 + 2 * n_groups] if n_groups > 1 else refs[1:2]
    rest = refs[1 + (2 * n_groups if n_groups > 1 else 1):]
    (ob_ref, pg_a_ref, pg_b_ref, wpa_ref, wpb_ref, wo_ref, g1_ref, b1_ref, rwh_ref, rwl_ref, rb_ref, cnt0_ref,
     x1_ref, x1p_ref, ti_ref, gt_ref, rank_ref, cnt_ref, br_ref, u_ref) = rest
    tm = x_ref.shape[0]
    half = D_MODEL // 2

    def rows_part(rs):
        if n_groups > 1:
            def lanes(r):
                return jnp.concatenate([r[h, rs, :] for h in range(H_A)], axis=1)

            os_ = [lanes(r) for r in oa_refs[:n_groups]]
            ls = [lanes(r) for r in oa_refs[n_groups:]]
            mm = functools.reduce(jnp.maximum, ls)
            ws = [jnp.exp(x - mm) for x in ls]
            oa = sum(w * o for w, o in zip(ws, os_)) / sum(ws)
        else:
            oa = oa_refs[0][rs, :]
        oab, obb = oa.astype(BF16), ob_ref[rs, :]
        for c in range(D_MODEL // MERGE_TC):
            cs = slice(c * MERGE_TC, (c + 1) * MERGE_TC)
            br_ref[rs, cs] = (jax.nn.sigmoid(pg_a_ref[rs, cs]) * _dot(oab, wpa_ref[:, cs])
                              + jax.nn.sigmoid(pg_b_ref[rs, cs]) * _dot(obb, wpb_ref[:, cs])).astype(BF16)
        for c in range(D_MODEL // MERGE_TC):
            cs = slice(c * MERGE_TC, (c + 1) * MERGE_TC)
            u_ref[rs, cs] = DEEPNORM_ALPHA * x_ref[rs, cs] + _dot(br_ref[rs, :], wo_ref[:, cs])
        x1 = _layer_norm(u_ref[rs, :], g1_ref[...], b1_ref[...])
        x1_ref[rs, :] = x1
        xh = x1.astype(BF16)
        xhf = xh.astype(F32)
        bits = lax.bitcast_convert_type(xhf, jnp.int32)
        x1p_ref[rs, :] = lax.shift_right_logical(bits[:, :half], 16) | bits[:, half:]
        xl = (x1 - xhf).astype(BF16)
        logits = _dot(xh, rwh_ref[...]) + _dot(xl, rwh_ref[...]) + _dot(xh, rwl_ref[...]) + rb_ref[...]
        lane = lax.broadcasted_iota(jnp.int32, logits.shape, 1)
        vals = logits
        top_v, top_i = [], []
        for _ in range(TOP_K):
            m = jnp.max(vals, axis=-1, keepdims=True)
            ik = jnp.min(jnp.where(vals == m, lane, LANE), axis=-1, keepdims=True)
            vals = jnp.where(lane == ik, -jnp.inf, vals)
            top_v.append(m)
            top_i.append(ik)
        es = [jnp.exp(v - top_v[0]) for v in top_v]
        tot = functools.reduce(lambda a, b: a + b, es)
        ti_ref[rs, :] = jnp.concatenate(top_i, axis=1)
        gt_ref[rs, :] = jnp.concatenate([e / tot for e in es], axis=1)
        return [jnp.where(lane == ik, 1.0, 0.0) for ik in top_i]

    picks = rows_part(slice(0, tm))
    @pl.when(pl.program_id(0) == 0)
    def _():
        cnt_ref[...] = cnt0_ref[...]

    onehot = picks[0] + picks[1] + picks[2] + picks[3]
    tri = jnp.where(lax.broadcasted_iota(jnp.int32, (tm, tm), 0) > lax.broadcasted_iota(jnp.int32, (tm, tm), 1), 1.0, 0.0)
    before = _dot(tri.astype(BF16), onehot.astype(BF16)) + cnt_ref[...]
    rank_ref[...] = jnp.concatenate([jnp.sum(pk * before, axis=-1, keepdims=True) for pk in picks], axis=1).astype(jnp.int32)
    cnt_ref[...] = cnt_ref[...] + jnp.sum(onehot, axis=0, keepdims=True)


def _merge(x, oas, ob, pg, wpa, wpb, wo, g1, b1, rwh, rwl, rbp, cnt0, tm):
    T = x.shape[0]
    n_groups = len(oas) // 2 if len(oas) > 1 else 1

    def row(w):
        return pl.BlockSpec((tm, w), lambda i: (i, 0))

    def const(a):
        return pl.BlockSpec(a.shape, lambda i: (0,) * a.ndim, pipeline_mode=pl.Buffered(1))

    def oa_spec(a):
        return pl.BlockSpec((H_A, tm, HEAD_DIM), lambda i: (0, i, 0)) if a.ndim == 3 else row(A_WIDTH)

    in_specs = ([row(D_MODEL)] + [oa_spec(a) for a in oas] + [row(H_B * DV_B), row(D_MODEL),
                pl.BlockSpec((tm, D_MODEL), lambda i: (i, 1))]
                + [const(a) for a in (wpa, wpb, wo, g1, b1, rwh, rwl, rbp, cnt0)])
    return pl.pallas_call(
        functools.partial(_merge_kernel, n_groups=n_groups),
        grid=(T // tm,),
        in_specs=in_specs,
        out_specs=[row(D_MODEL), row(D_MODEL // 2), row(TOP_K), row(TOP_K), row(TOP_K),
                   pl.BlockSpec((1, LANE), lambda i: (0, 0))],
        out_shape=[jax.ShapeDtypeStruct((T, D_MODEL), F32), jax.ShapeDtypeStruct((T, D_MODEL // 2), jnp.int32),
                   jax.ShapeDtypeStruct((T, TOP_K), jnp.int32), jax.ShapeDtypeStruct((T, TOP_K), F32),
                   jax.ShapeDtypeStruct((T, TOP_K), jnp.int32), jax.ShapeDtypeStruct((1, LANE), F32)],
        scratch_shapes=[pltpu.VMEM((tm, D_MODEL), BF16), pltpu.VMEM((tm, D_MODEL), F32)],
        compiler_params=_cparams(("arbitrary",)),
        name=f"merge_g{n_groups}",
    )(x, *oas, ob, pg, pg, wpa, wpb, wo, g1, b1, rwh, rwl, rbp, cnt0)


def _sc_gather(table, idx):
    info = plsc.get_sparse_core_info()
    n_workers = info.num_cores * info.num_subcores
    n, width = idx.shape[0], table.shape[1]
    per_worker = n // n_workers
    chunk = SC_CHUNK_BYTES // (width * table.dtype.itemsize)
    assert per_worker * n_workers == n and per_worker % chunk == 0 and chunk % 8 == 0
    mesh = plsc.VectorSubcoreMesh(core_axis_name="c", subcore_axis_name="s")

    @functools.partial(
        pl.kernel, mesh=mesh,
        out_type=jax.ShapeDtypeStruct((n, width), table.dtype),
        scratch_types=[pltpu.VMEM((chunk,), jnp.int32), pltpu.VMEM((chunk, width), table.dtype),
                       pltpu.SemaphoreType.DMA],
    )
    def gather(table_hbm, idx_hbm, out_hbm, idx_v, rows_v, sem):
        wid = lax.axis_index("s") * info.num_cores + lax.axis_index("c")
        base = wid * per_worker

        @pl.loop(0, per_worker // chunk)
        def _(c):
            off = pl.multiple_of(base + c * chunk, chunk)
            pltpu.sync_copy(idx_hbm.at[pl.ds(off, chunk)], idx_v)
            pltpu.async_copy(table_hbm.at[idx_v], rows_v, sem).wait()
            pltpu.sync_copy(rows_v, out_hbm.at[pl.ds(off, chunk)])

    return gather(table, idx)


def _sc_scatter_rows(tables, idx, n_out):
    info = plsc.get_sparse_core_info()
    n_workers = info.num_cores * info.num_subcores
    n_idx, n = idx.shape
    width, dtype = tables[0].shape[1], tables[0].dtype
    max_rows = SC_SCATTER_BYTES // (width * dtype.itemsize)
    plan, tok0 = [], 0
    for t in tables:
        per_worker = t.shape[0] // n_workers
        assert per_worker * n_workers == t.shape[0] and per_worker % 8 == 0 and tok0 % 8 == 0
        plan.append((per_worker, max(d for d in range(8, max_rows + 1, 8) if per_worker % d == 0), tok0))
        tok0 += t.shape[0]
    assert tok0 == n
    mesh = plsc.VectorSubcoreMesh(core_axis_name="c", subcore_axis_name="s")
    scratch_types = []
    for _, chunk, _ in plan:
        scratch_types += [pltpu.VMEM((chunk,), jnp.int32)] * n_idx + [pltpu.VMEM((chunk, width), dtype)]

    @functools.partial(pl.kernel, mesh=mesh, out_type=jax.ShapeDtypeStruct((n_out, width), dtype),
                       scratch_types=scratch_types)
    def scatter(*refs):
        table_refs, idx_hbm, out_hbm = refs[:len(tables)], refs[len(tables)], refs[len(tables) + 1]
        scratch = refs[len(tables) + 2:]
        wid = lax.axis_index("s") * info.num_cores + lax.axis_index("c")
        for j, (per_worker, chunk, first) in enumerate(plan):
            idx_vs = scratch[j * (n_idx + 1):j * (n_idx + 1) + n_idx]
            rows_v = scratch[j * (n_idx + 1) + n_idx]
            table_hbm = table_refs[j]

            @pl.loop(0, per_worker // chunk)
            def _(c):
                off = pl.multiple_of(wid * per_worker + c * chunk, 8)
                pltpu.sync_copy(table_hbm.at[pl.ds(off, chunk)], rows_v)
                for k in range(n_idx):
                    pltpu.sync_copy(idx_hbm.at[pl.ds(pl.multiple_of(k * n + first + off, 8), chunk)], idx_vs[k])
                for k in range(n_idx):
                    pltpu.sync_copy(rows_v, out_hbm.at[idx_vs[k]])

    return scatter(*tables, idx.reshape(n_idx * n))


def _expert_kernel(be_ref, rows_ref, nu_ref, xs_hbm, wg_ref, wu_ref, bg_ref, bu_ref, wd_ref, bd_ref, o_ref,
                   hid_ref, xs_ref, xs_sem):
    del be_ref
    i = pl.program_id(0)
    p = pl.program_id(1)
    nrows = rows_ref[i]
    n_live = (nrows + (MOE_SB - 1)) // MOE_SB
    half = D_MODEL // 2

    def xs_copy(blk):
        return pltpu.make_async_copy(xs_hbm.at[pl.ds(pl.multiple_of(blk * MOE_BM, MOE_BM), MOE_BM), :], xs_ref, xs_sem)

    @pl.when(jnp.logical_and(i == 0, p == 0))
    def _():
        xs_copy(0).start()

    @pl.when(p == 0)
    def _():
        xs_copy(i).wait()

    @pl.when(jnp.logical_and(p == MOE_NF, i + 1 < nu_ref[0]))
    def _():
        xs_copy(i + 1).start()

    def paired(one):
        def pair(j, carry):
            one(2 * j)
            one(2 * j + 1)
            return carry

        lax.fori_loop(0, n_live // 2, pair, 0)

        @pl.when(n_live % 2 == 1)
        def _():
            one(n_live - 1)

    @pl.when(p < MOE_NF)
    def _():
        def gate_up(s):
            r0 = pl.multiple_of(s * MOE_SB, MOE_SB)
            rid = r0 + lax.broadcasted_iota(jnp.int32, (MOE_SB, half), 0)
            packed = jnp.where(rid < nrows, xs_ref[pl.ds(r0, MOE_SB), :], 0)
            lo = lax.bitcast_convert_type(lax.shift_left(packed, 16), F32)
            hi = lax.bitcast_convert_type(packed & jnp.int32(-65536), F32)
            x = jnp.concatenate([lo.astype(BF16), hi.astype(BF16)], axis=1)
            g = jnp.minimum(_dot(x, wg_ref[...].astype(BF16)) + bg_ref[...], SWIGLU_LIMIT)
            u = jnp.clip(_dot(x, wu_ref[...].astype(BF16)) + bu_ref[...], -SWIGLU_LIMIT, SWIGLU_LIMIT)
            hid_ref[p, pl.ds(r0, MOE_SB), :] = ((u + 1.0) * g * jax.nn.sigmoid(SWIGLU_ALPHA * g)).astype(BF16)

        paired(gate_up)

    @pl.when(p >= MOE_NF)
    def _():
        def down(s):
            r0 = pl.multiple_of(s * MOE_SB, MOE_SB)
            y = bd_ref[...]
            for f in range(MOE_NF):
                y = y + _dot(hid_ref[f, pl.ds(r0, MOE_SB), :], wd_ref[f * MOE_TF:(f + 1) * MOE_TF, :].astype(BF16))
            bits = lax.bitcast_convert_type(y.astype(BF16).astype(F32), jnp.int32)
            o_ref[pl.ds(r0, MOE_SB), :] = lax.shift_right_logical(bits[:, :MOE_TN // 2], 16) | bits[:, MOE_TN // 2:]

        def zero_body(s, carry):
            o_ref[pl.ds(pl.multiple_of(s * MOE_SB, MOE_SB), MOE_SB), :] = jnp.zeros((MOE_SB, MOE_TN // 2), jnp.int32)
            return carry

        paired(down)
        lax.fori_loop(n_live, MOE_BM // MOE_SB, zero_body, 0)


def _experts(xs, block_expert, block_rows, n_used, w_gate_up, b_gate_up, w_down, b_down):
    nf, nn = MOE_NF, D_MODEL // MOE_TN

    def gate_map(col0, lead):
        def index_map(i, p, be, rw, nu):
            ahead = p >= nf + nn - lead
            e = jnp.where(ahead, be[jnp.minimum(i + 1, nu[0] - 1)], be[i])
            return (e, 0, col0 + jnp.where(ahead, 0, jnp.minimum(p, nf - 1)))
        return index_map

    def down_map(i, p, be, rw, nu):
        parked = p < nf
        e = jnp.where(parked, be[jnp.maximum(i - 1, 0)], be[i])
        return (e, 0, jnp.where(parked, jnp.where(i > 0, nn - 1, 0), p - nf))

    grid_spec = pltpu.PrefetchScalarGridSpec(
        num_scalar_prefetch=3,
        grid=(n_used[0], nf + nn),
        in_specs=[
            pl.BlockSpec(memory_space=pl.ANY),
            pl.BlockSpec((None, D_MODEL, MOE_TF), gate_map(0, nn // 2)),
            pl.BlockSpec((None, D_MODEL, MOE_TF), gate_map(nf, nn // 4)),
            pl.BlockSpec((None, 1, MOE_TF), gate_map(0, nn // 2)),
            pl.BlockSpec((None, 1, MOE_TF), gate_map(nf, nn // 4)),
            pl.BlockSpec((None, D_FF, MOE_TN), down_map),
            pl.BlockSpec((None, 1, MOE_TN), down_map),
        ],
        out_specs=pl.BlockSpec((MOE_BM, MOE_TN // 2), lambda i, p, be, rw, nu: (i, jnp.maximum(p - nf, 0))),
        scratch_shapes=[pltpu.VMEM((nf, MOE_BM, MOE_TF), BF16), pltpu.VMEM((MOE_BM, D_MODEL // 2), jnp.int32),
                        pltpu.SemaphoreType.DMA],
    )
    bgu = b_gate_up.reshape(N_EXPERTS, 1, 2 * D_FF)
    bd = b_down.reshape(N_EXPERTS, 1, D_MODEL)
    return pl.pallas_call(
        _expert_kernel,
        grid_spec=grid_spec,
        out_shape=jax.ShapeDtypeStruct((xs.shape[0], D_MODEL // 2), jnp.int32),
        compiler_params=_cparams(("arbitrary", "arbitrary")),
        name="moe_experts",
    )(block_expert, block_rows, n_used, xs, w_gate_up, w_gate_up, bgu, bgu, w_down, bd)


def _route(top_i, rank, counts, n_blocks):
    T = top_i.shape[0]
    bpe = (counts + MOE_BM - 1) // MOE_BM
    bend = jnp.cumsum(bpe)
    bstart = bend - bpe
    experts = jnp.arange(N_EXPERTS, dtype=jnp.int32)
    start_of = jnp.sum(jnp.where(top_i[:, :, None] == experts, bstart * MOE_BM, 0), axis=-1)
    dest = (start_of + rank).astype(jnp.int32)
    n_used = bend[-1]
    blk = jnp.arange(n_blocks, dtype=jnp.int32)
    be = jnp.minimum(jnp.searchsorted(bend, jnp.minimum(blk, n_used - 1), side="right"), N_EXPERTS - 1).astype(jnp.int32)
    rows = jnp.clip(counts[be] - (blk - bstart[be]) * MOE_BM, 0, MOE_BM)
    rows = jnp.where(blk < n_used, rows, 0).astype(jnp.int32)
    return dest.reshape(T, TOP_K).T, be, rows, n_used.reshape(1).astype(jnp.int32)


def _combine_kernel(x1_ref, ge_ref, gt_ref, g2_ref, b2_ref, o_ref):
    hw, nn = MOE_TN // 2, D_MODEL // MOE_TN

    def unpack(w):
        lo = lax.bitcast_convert_type(lax.shift_left(w, 16), F32)
        hi = lax.bitcast_convert_type(w & jnp.int32(-65536), F32)
        return jnp.concatenate([part[:, n * hw:(n + 1) * hw] for n in range(nn) for part in (lo, hi)], axis=1)

    gt = gt_ref[...]
    m = gt[:, 0:1] * unpack(ge_ref[0])
    for k in range(1, TOP_K):
        m = m + gt[:, k:k + 1] * unpack(ge_ref[k])
    o_ref[...] = _layer_norm(DEEPNORM_ALPHA * x1_ref[...] + m, g2_ref[...], b2_ref[...])


def _combine(x1, ge, gate, g2, b2, row0, tm):
    n_rows = x1.shape[0]
    b0 = row0 // tm
    return pl.pallas_call(
        _combine_kernel,
        grid=(n_rows // tm,),
        in_specs=[pl.BlockSpec((tm, D_MODEL), lambda i: (i, 0)),
                  pl.BlockSpec((TOP_K, tm, D_MODEL // 2), lambda i: (0, b0 + i, 0)),
                  pl.BlockSpec((tm, TOP_K), lambda i: (b0 + i, 0)),
                  pl.BlockSpec(g2.shape, lambda i: (0, 0)), pl.BlockSpec(b2.shape, lambda i: (0, 0))],
        out_specs=pl.BlockSpec((tm, D_MODEL), lambda i: (i, 0)),
        out_shape=jax.ShapeDtypeStruct((n_rows, D_MODEL), F32),
        compiler_params=_cparams(("parallel",)),
        name="moe_combine",
    )(x1, ge, gate, g2, b2)


def _layer(xp, xs, caches, state, w_in, rel_bias, gla_w_up, gla_b, gla_norm_g, w_pa, w_pb, w_o, ln1_g, ln1_b,
           router_w, router_b, w_gate_up, b_gate_up, w_down, b_down, ln2_g, ln2_b):
    batch, seq, _ = xp.shape
    dbatch, dseq, _ = xs.shape
    tp, ts = batch * seq, dbatch * dseq
    xp2, xs2 = xp.reshape(tp, D_MODEL), xs.reshape(ts, D_MODEL)

    o_b0, o_lr, o_g = PA_W, PA_W + PB_LR, PA_W + PB_LR + GLA_RANK
    w_a = w_in[:, :PA_W].astype(BF16)
    w_b = jnp.concatenate([w_in[:, o_b0:o_g], jnp.zeros((D_MODEL, LANE - GLA_RANK), F32)], axis=1).astype(BF16)
    w_g = w_in[:, o_g:].astype(BF16)
    wup = jnp.concatenate([gla_w_up, jnp.zeros((LANE - GLA_RANK, H_B * DK_B), F32)], axis=0).astype(BF16)
    gb = gla_b.reshape(1, H_B * DK_B)
    ng = gla_norm_g.reshape(1, DV_B)
    wpa, wpb, wo = w_pa.astype(BF16), w_pb.astype(BF16), w_o.astype(BF16)
    g1, b1 = ln1_g.reshape(1, D_MODEL), ln1_b.reshape(1, D_MODEL)
    g2, b2 = ln2_g.reshape(1, D_MODEL), ln2_b.reshape(1, D_MODEL)
    rw = jnp.concatenate([router_w, jnp.zeros((D_MODEL, LANE - N_EXPERTS), F32)], axis=1)
    rwh = rw.astype(BF16)
    rwl = (rw - rwh.astype(F32)).astype(BF16)
    rbp = jnp.concatenate([router_b, jnp.full((LANE - N_EXPERTS,), NEG, F32)]).reshape(1, LANE)
    caches8 = [c.reshape(dbatch, c.shape[1], KV_ROWS, HEAD_DIM) for c in caches]

    pa_p = _project(xp2, w_a, PROJ_TM, A_QKV_WIDTH, "proj_a_prompt", head_major=True)
    pb_p = _project(xp2, w_b, PROJ_TM, PB_W, "proj_b_prompt")
    pg_p = _project(xp2, w_g, PROJ_TM, 2048, "proj_g_prompt")
    oas, lses = [], []
    for g in range(N_GROUPS):
        o, lse = _attn_heads(pa_p, _prompt_table(rel_bias, g), g, batch, seq)
        oas.append(o)
        lses.append(lse)
    ob_p, st_p = _gla_heads(pb_p, wup, gb, ng, jnp.zeros((batch, H_B, DK_B, DV_B), F32), batch, seq, GLA_CHUNK, 256)
    x1_p, x1p_p, ti_p, gt_p, rk_p, cnt_p = _merge(xp2, oas + lses, ob_p, pg_p, wpa, wpb, wo, g1, b1, rwh, rwl, rbp,
                                                  jnp.zeros((1, LANE), F32), 256)

    pa_s = _project(xs2, w_a, ts, A_QKV_WIDTH, "proj_a_sample", head_major=True)
    pb_s = _project(xs2, w_b, ts, PB_W, "proj_b_sample")
    pg_s = _project(xs2, w_g, ts, 1024, "proj_g_sample")
    tabc, tabn, combo_base = _sample_tables(rel_bias, dseq)
    oa_s, new_rows = _attn_sample(pa_s, caches8, tabc, tabn, combo_base, dbatch, dseq)
    chunk_s = int(np.gcd(dseq, GLA_CHUNK))
    ob_s, st_s = _gla_heads(pb_s, wup, gb, ng, state, dbatch, dseq, chunk_s, dseq)
    x1_s, x1p_s, ti_s, gt_s, rk_s, cnt_all = _merge(xs2, [oa_s], ob_s, pg_s, wpa, wpb, wo, g1, b1, rwh, rwl, rbp, cnt_p, ts)

    top_i = jnp.concatenate([ti_p, ti_s], axis=0)
    gate = jnp.concatenate([gt_p, gt_s], axis=0)
    t_all = tp + ts
    n_blocks = -(-(t_all * TOP_K) // MOE_BM) + N_EXPERTS
    rank = jnp.concatenate([rk_p, rk_s], axis=0)
    counts = cnt_all[0, :N_EXPERTS].astype(jnp.int32)
    dest, be, rows, n_used = _route(top_i, rank, counts, n_blocks)
    xsorted = _sc_scatter_rows([x1p_p, x1p_s], dest, n_blocks * MOE_BM)
    keeps = [min(window, seq) for window, _ in DILATED_GROUPS]
    packed_p = {g: _kv_pack(pa_p, g, batch, seq, keeps[g], Q_BLOCK) for g in range(N_GROUPS - 1)}
    eo = _experts(xsorted, be, rows, n_used, w_gate_up, b_gate_up, w_down, b_down)
    ge = _sc_gather(eo, dest.reshape(TOP_K * t_all)).reshape(TOP_K, t_all, D_MODEL // 2)
    packed_p[N_GROUPS - 1] = _kv_pack(pa_p, N_GROUPS - 1, batch, seq, keeps[N_GROUPS - 1], Q_BLOCK)
    y_p = _combine(x1_p, ge, gate, g2, b2, 0, 512)
    y_s = _combine(x1_s, ge, gate, g2, b2, tp, 256)

    shifted = _sc_cache_shift(caches8, dseq)
    bufs_p, bufs_s = [], []
    for g, (window, _) in enumerate(DILATED_GROUPS):
        keep = keeps[g]
        bufs_p.append(packed_p[g].reshape(batch, keep, 2, H_A, HEAD_DIM))
        clen = caches[g].shape[1]
        assert clen == window and dseq <= clen
        buf = lax.dynamic_update_slice(shifted[g], new_rows[g], (0, clen - dseq, 0, 0))
        bufs_s.append(buf.reshape(dbatch, clen, 2, H_A, HEAD_DIM))
    return y_p.reshape(batch, seq, D_MODEL), y_s.reshape(dbatch, dseq, D_MODEL), bufs_p, st_p, bufs_s, st_s


def kernel(x_prompt, x_sample, cache_a1_kv, cache_a2_kv, cache_a3_kv, state_b_s, w_in, rel_bias, gla_w_up, gla_b,
           gla_norm_g, w_pa, w_pb, w_o, ln1_g, ln1_b, router_w, router_b, w_gate_up, b_gate_up, w_down, b_down,
           ln2_g, ln2_b):
    assert w_in.shape[0] == DEPTH
    yp, ys, bufs_p, st_p, bufs_s, st_s = _layer(
        x_prompt, x_sample, (cache_a1_kv[0], cache_a2_kv[0], cache_a3_kv[0]), state_b_s[0], w_in[0], rel_bias,
        gla_w_up[0], gla_b[0], gla_norm_g[0], w_pa[0], w_pb[0], w_o[0], ln1_g[0], ln1_b[0], router_w[0], router_b[0],
        w_gate_up[0], b_gate_up[0], w_down[0], b_down[0], ln2_g[0], ln2_b[0])
    return (yp, ys, bufs_p[0][None], bufs_p[1][None], bufs_p[2][None], st_p[None],
            bufs_s[0][None], bufs_s[1][None], bufs_s[2][None], st_s[None].astype(state_b_s.dtype))
```
